```python
import math
import jax, jax.numpy as jnp
from jax import lax
import numpy as np

D_MODEL = 2048
BATCH = 4
SEQ = 2048
DEPTH = 1
DEC_BATCH = 32
DEC_SEQ = 8
PAST_LEN = 8192
PAGE_SIZE = 128

HEAD_DIM = 64
ATT_HEADS = (D_MODEL // 2) // HEAD_DIM
ATT_GROUPS = ATT_HEADS // 4
ATT_HPG = ATT_HEADS // ATT_GROUPS
ATT_WIDTH = ATT_HEADS * HEAD_DIM
KV_WIDTH = ATT_GROUPS * HEAD_DIM
N_BRANCH = 3
CMP_BLOCK = 32
CMP_STRIDE = 16
CMP_RATIO = CMP_BLOCK // CMP_STRIDE
CMP_HIDDEN = 256
SEL_BLOCK = 64
N_SEL = 8
WINDOW = 512
Q_BLOCK = 128
HG_KEY = 128
HG_VAL = 128
HG_HEADS = (D_MODEL // 2) // HG_VAL
HG_WIDTH = HG_HEADS * HG_VAL
HG_CHUNK = 64
MIX_WIDTH = ATT_WIDTH + HG_WIDTH
D_FF = 256 * ((8 * D_MODEL // 3 + 255) // 256)
N_BUCKETS = 32
MAX_DISTANCE = 1024
EPS = 1e-6
IN_SPLITS = (ATT_WIDTH, 2 * N_BRANCH * KV_WIDTH, N_BRANCH * ATT_HEADS,
             HG_HEADS * HG_KEY, HG_HEADS * HG_KEY, HG_WIDTH, HG_WIDTH)
D_IN = sum(IN_SPLITS)

kernel_name = 'nsa_hgrn2_macaron_decode_step'


def rms_norm(x, gain):
    xf = x.astype(jnp.float32)
    y = xf * lax.rsqrt(jnp.mean(xf * xf, axis=-1, keepdims=True) + EPS)
    return (y * gain.astype(jnp.float32)).astype(x.dtype)


def swiglu_half(x, gain, w_gu, w_down):
    g, u = jnp.split(rms_norm(x, gain) @ w_gu, 2, axis=-1)
    return x + 0.5 * ((jax.nn.silu(g) * u) @ w_down)


def t5_bucket(dist):
    n = jnp.maximum(dist, 0)
    exact = N_BUCKETS // 2
    logn = jnp.log(jnp.maximum(n, 1).astype(jnp.float32) / exact)
    large = exact + (logn / math.log(MAX_DISTANCE / exact) * (N_BUCKETS - exact)).astype(jnp.int32)
    return jnp.where(n < exact, n, jnp.minimum(large, N_BUCKETS - 1))


def masked_softmax(s, mask):
    s = jnp.where(mask, s.astype(jnp.float32), -jnp.inf)
    m = jnp.max(s, axis=-1, keepdims=True)
    m = jnp.where(jnp.isfinite(m), m, 0.0)
    e = jnp.where(mask, jnp.exp(s - m), 0.0)
    d = jnp.sum(e, axis=-1, keepdims=True)
    return e / jnp.where(d > 0, d, 1.0)


def block_overlap(n_cmp, n_blk):
    cs = np.arange(n_cmp)[:, None] * CMP_STRIDE
    bs = np.arange(n_blk)[None, :] * SEL_BLOCK
    ov = np.minimum(cs + CMP_BLOCK, bs + SEL_BLOCK) - np.maximum(cs, bs)
    return (np.clip(ov, 0, None) / CMP_BLOCK).astype(np.float32)


def compress(rows, w1, b1, w2, b2):
    B, T, G, Dh = rows.shape
    n_cmp = (T - CMP_BLOCK) // CMP_STRIDE + 1
    n_chunk = n_cmp + CMP_RATIO - 1
    ch = rows[:, :n_chunk * CMP_STRIDE].reshape(B, n_chunk, CMP_STRIDE, G, Dh)
    ch = ch.transpose(0, 1, 3, 2, 4).reshape(B, n_chunk, G, CMP_STRIDE * Dh)
    h = b1
    for r in range(CMP_RATIO):
        h = h + jnp.einsum('bngf,fe->bnge', ch[:, r:r + n_cmp], w1[r])
    return jnp.einsum('bnge,ed->bngd', jax.nn.silu(h), w2) + b2


def compressed_kv(rows, w1, b1, w2, b2, k_gain):
    kc = rms_norm(compress(rows[:, :, 0], w1[0], b1[0], w2[0], b2[0]), k_gain)
    vc = compress(rows[:, :, 1], w1[1], b1[1], w2[1], b2[1])
    c_end = jnp.asarray(np.arange(kc.shape[1]) * CMP_STRIDE + CMP_BLOCK - 1, dtype=jnp.int32)
    return kc, vc, c_end


def sel_blocks(rows):
    B, T = rows.shape[:2]
    nb = -(-T // SEL_BLOCK)
    rows = jnp.pad(rows, ((0, 0), (0, nb * SEL_BLOCK - T), (0, 0), (0, 0), (0, 0)))
    return rows.reshape(B, nb, SEL_BLOCK, 2, ATT_GROUPS, HEAD_DIM).transpose(0, 3, 4, 1, 2, 5)


def project(xn, w_in, q_gain, k_gain, lb):
    B, T, _ = xn.shape
    pts = [int(v) for v in np.cumsum(IN_SPLITS)[:-1]]
    a_q, a_kv, a_g, h_q, h_f, h_i, h_g = jnp.split(xn @ w_in, pts, axis=-1)
    q = rms_norm(a_q.reshape(B, T, ATT_GROUPS, ATT_HPG, HEAD_DIM), q_gain)
    kv = a_kv.reshape(B, T, N_BRANCH, 2, ATT_GROUPS, HEAD_DIM)
    cmp_rows = kv[:, :, 0]
    k_nrm = rms_norm(kv[:, :, 1:, 0], k_gain[1:])
    slc_rows = jnp.stack([k_nrm[:, :, 0], kv[:, :, 1, 1]], axis=2)
    win_rows = jnp.stack([k_nrm[:, :, 1], kv[:, :, 2, 1]], axis=2)
    gates = jax.nn.sigmoid(a_g.reshape(B, T, N_BRANCH, ATT_GROUPS, ATT_HPG))
    z = h_f.reshape(B, T, HG_HEADS, HG_KEY).astype(jnp.float32)
    lbh = lb.reshape(HG_HEADS, HG_KEY)
    log_f = jnp.log(lbh + (1.0 - lbh) * jax.nn.sigmoid(z))
    k_hg = (1.0 - lbh) * jax.nn.sigmoid(-z)
    q_hg = h_q.reshape(B, T, HG_HEADS, HG_KEY)
    v_hg = h_i.reshape(B, T, HG_HEADS, HG_VAL)
    return q, gates, cmp_rows, slc_rows, win_rows, q_hg, k_hg, v_hg, log_f, h_g


def nsa_core(q, qpos, gates, kc, vc, c_end, sel, wkv, wpos, bias_table):
    B, T = q.shape[:2]
    scale = HEAD_DIM ** -0.5
    tbl = bias_table.astype(jnp.float32).reshape(N_BUCKETS, ATT_GROUPS, ATT_HPG)
    dist_c = qpos[:, None] - c_end[None, :]
    s = jnp.einsum('btgpd,bcgd->bgptc', q, kc).astype(jnp.float32) * scale
    s = s + tbl[t5_bucket(dist_c)].transpose(2, 3, 0, 1)
    p_cmp = masked_softmax(s, dist_c >= 0)
    o_cmp = jnp.einsum('bgptc,bcgd->btgpd', p_cmp, vc.astype(jnp.float32))
    n_blk = sel.shape[3]
    n_pick = min(N_SEL, n_blk)
    imp = jnp.einsum('bgptc,cn->bgtn', p_cmp, jnp.asarray(block_overlap(kc.shape[1], n_blk)))
    blk = jnp.arange(n_blk)
    cur = (qpos // SEL_BLOCK)[:, None]
    forced = (blk == 0) | (blk == cur) | (blk == cur - 1)
    valid = blk * SEL_BLOCK <= qpos[:, None]
    score = jnp.where(forced, ATT_HPG + 1.0, jnp.where(valid, imp, -1.0))
    idx = lax.top_k(score, n_pick)[1]
    bi = jnp.arange(B)[:, None, None, None]
    gi = jnp.arange(ATT_GROUPS)[None, :, None, None]
    k_sel = sel[:, 0][bi, gi, idx]
    v_sel = sel[:, 1][bi, gi, idx]
    dist_s = qpos[:, None, None] - (idx[..., None] * SEL_BLOCK + jnp.arange(SEL_BLOCK))
    bias_s = tbl[t5_bucket(dist_s), gi[..., None]].transpose(0, 1, 5, 2, 3, 4)
    s = jnp.einsum('btgpd,bgtnld->bgptnl', q, k_sel).astype(jnp.float32) * scale + bias_s
    p = masked_softmax(s.reshape(B, ATT_GROUPS, ATT_HPG, T, -1),
                       (dist_s >= 0).reshape(B, ATT_GROUPS, 1, T, -1))
    o_slc = jnp.einsum('bgptnl,bgtnld->btgpd', p.reshape(s.shape), v_sel.astype(jnp.float32))
    dist_w = qpos[:, None] - wpos[None, :]
    s = jnp.einsum('btgpd,bsgd->bgpts', q, wkv[:, :, 0]).astype(jnp.float32) * scale
    s = s + tbl[t5_bucket(dist_w)].transpose(2, 3, 0, 1)
    p_win = masked_softmax(s, (dist_w >= 0) & (dist_w < WINDOW) & (wpos[None, :] >= 0))
    o_win = jnp.einsum('bgpts,bsgd->btgpd', p_win, wkv[:, :, 1].astype(jnp.float32))
    g = gates.astype(jnp.float32)[..., None]
    return g[:, :, 0] * o_cmp + g[:, :, 1] * o_slc + g[:, :, 2] * o_win


def hgrn2_chunked(q, k, v, log_f, s0):
    B, T, H, _ = q.shape
    c = min(HG_CHUNK, T)
    pad = (-T) % c

    def prep(a):
        a = jnp.pad(a.astype(jnp.float32), ((0, 0), (0, pad), (0, 0), (0, 0)))
        return a.reshape(B, a.shape[1] // c, c, H, a.shape[-1]).transpose(1, 0, 3, 2, 4)

    causal = jnp.tril(jnp.ones((c, c), dtype=bool))

    def step(S, inp):
        qc, kc, vc, lc = inp
        b = jnp.cumsum(lc, axis=2)
        o = jnp.einsum('bhtd,bhde->bhte', qc * jnp.exp(b), S)
        diff = b[:, :, :, None, :] - b[:, :, None, :, :]
        dec = jnp.exp(jnp.where(causal[:, :, None], diff, -jnp.inf))
        a = jnp.einsum('bhtd,bhsd,bhtsd->bhts', qc, kc, dec)
        o = o + jnp.einsum('bhts,bhse->bhte', a, vc)
        b_last = b[:, :, -1]
        S = jnp.exp(b_last)[..., None] * S + jnp.einsum(
            'bhsd,bhse->bhde', kc * jnp.exp(b_last[:, :, None] - b), vc)
        return S, o

    s_T, o = lax.scan(step, s0.astype(jnp.float32), (prep(q), prep(k), prep(v), prep(log_f)))
    o = o.transpose(1, 0, 3, 2, 4).reshape(B, -1, H, v.shape[-1])[:, :T]
    return o, s_T


def mixer_out(o_att, o_hg, og, attn_gain, hg_gain, w_out, dtype):
    B, T = og.shape[:2]
    a = rms_norm(o_att.reshape(B, T, ATT_WIDTH), attn_gain).astype(dtype)
    h = (rms_norm(o_hg, hg_gain).reshape(B, T, HG_WIDTH)
         * jax.nn.silu(og.astype(jnp.float32))).astype(dtype)
    return jnp.concatenate([a, h], axis=-1) @ w_out


def prompt_mixer(xn, lb, bias_table, w_in, q_gain, k_gain, w_cmp1, b_cmp1, w_cmp2, b_cmp2,
                 attn_gain, hg_gain, w_out):
    B, T, _ = xn.shape
    q, gates, cmp_rows, slc_rows, win_rows, hq, hk, hv, hlf, hog = project(xn, w_in, q_gain, k_gain, lb)
    kc, vc, c_end = compressed_kv(cmp_rows, w_cmp1, b_cmp1, w_cmp2, b_cmp2, k_gain[0])
    sel = sel_blocks(slc_rows)
    win_pad = jnp.pad(win_rows, ((0, 0), (WINDOW, 0), (0, 0), (0, 0), (0, 0)))
    nqb = T // Q_BLOCK

    def qblock(args):
        i, qb, gb = args
        start = i * Q_BLOCK
        qpos = start + jnp.arange(Q_BLOCK)
        wkv = lax.dynamic_slice_in_dim(win_pad, start, WINDOW + Q_BLOCK, axis=1)
        wpos = start - WINDOW + jnp.arange(WINDOW + Q_BLOCK)
        return nsa_core(qb, qpos, gb, kc, vc, c_end, sel, wkv, wpos, bias_table)

    def to_blocks(a):
        return a.reshape(B, nqb, Q_BLOCK, *a.shape[2:]).swapaxes(0, 1)

    o = lax.map(qblock, (jnp.arange(nqb), to_blocks(q), to_blocks(gates)))
    o_att = o.swapaxes(0, 1).reshape(B, T, ATT_GROUPS, ATT_HPG, HEAD_DIM)
    s0 = jnp.zeros((B, HG_HEADS, HG_KEY, HG_VAL), jnp.float32)
    o_hg, s_T = hgrn2_chunked(hq, hk, hv, hlf, s0)
    y = mixer_out(o_att, o_hg, hog, attn_gain, hg_gain, w_out, xn.dtype)
    wb = min(WINDOW, T)
    return y, cmp_rows, slc_rows, win_rows[:, T - wb:], s_T.astype(xn.dtype)


def sample_mixer(xn, cache_cmp, cache_slc, win_buf, h_state, page_table, lb, bias_table,
                 w_in, q_gain, k_gain, w_cmp1, b_cmp1, w_cmp2, b_cmp2, attn_gain, hg_gain, w_out):
    B, T, _ = xn.shape
    past_len = page_table.shape[1] * cache_cmp.shape[1]
    q, gates, cmp_rows, slc_rows, win_rows, hq, hk, hv, hlf, hog = project(xn, w_in, q_gain, k_gain, lb)

    def gather(cache):
        pages = cache[page_table]
        return pages.reshape(B, past_len, *cache.shape[2:]).astype(xn.dtype)

    full_cmp = jnp.concatenate([gather(cache_cmp), cmp_rows], axis=1)
    full_slc = jnp.concatenate([gather(cache_slc), slc_rows], axis=1)
    kc, vc, c_end = compressed_kv(full_cmp, w_cmp1, b_cmp1, w_cmp2, b_cmp2, k_gain[0])
    sel = sel_blocks(full_slc)
    wb = win_buf.shape[1]
    wkv = jnp.concatenate([win_buf.astype(xn.dtype), win_rows], axis=1)
    wpos = past_len - wb + jnp.arange(wb + T)
    qpos = past_len + jnp.arange(T)
    o_att = nsa_core(q, qpos, gates, kc, vc, c_end, sel, wkv, wpos, bias_table)
    o_hg, s_T = hgrn2_chunked(hq, hk, hv, hlf, h_state)
    y = mixer_out(o_att, o_hg, hog, attn_gain, hg_gain, w_out, xn.dtype)
    return y, cmp_rows, slc_rows, wkv[:, T:], s_T.astype(h_state.dtype)


def setup_inputs(seed: int = 0) -> dict:
    key = jax.random.key(seed)
    ks = iter(jax.random.split(key, 32))

    def rnd(shape, scale):
        return jax.random.normal(next(ks), shape, jnp.float32) * scale

    def gain(shape):
        return 1.0 + rnd(shape, 0.01)

    n_pages = PAST_LEN // PAGE_SIZE
    n_phys = (DEC_BATCH * n_pages * 5) // 4
    wb = min(WINDOW, PAST_LEN)
    kv_row = (2, ATT_GROUPS, HEAD_DIM)
    perm = jax.random.permutation(next(ks), n_phys)
    page_table = perm[:DEC_BATCH * n_pages].reshape(DEC_BATCH, n_pages).astype(jnp.int32)
    return {
        'x_prompt': rnd((BATCH, SEQ, D_MODEL), 1.0),
        'x_sample': rnd((DEC_BATCH, DEC_SEQ, D_MODEL), 1.0),
        'cache_cmp_kv': rnd((DEPTH, n_phys, PAGE_SIZE) + kv_row, 1.0),
        'cache_slc_kv': rnd((DEPTH, n_phys, PAGE_SIZE) + kv_row, 1.0),
        'state_win_kv': rnd((DEPTH, DEC_BATCH, wb) + kv_row, 1.0),
        'state_hgrn': rnd((DEPTH, DEC_BATCH, HG_HEADS, HG_KEY, HG_VAL), 0.5),
        'page_table': page_table,
        'rel_bias_table': rnd((N_BUCKETS, ATT_HEADS), 0.5),
        'hgrn_lower_bound': rnd((DEPTH + 1, HG_HEADS * HG_KEY), 1.0),
        'norm_ffn1': gain((DEPTH, D_MODEL)),
        'w_ffn1_gate_up': rnd((DEPTH, D_MODEL, 2 * D_FF), D_MODEL ** -0.5),
        'w_ffn1_down': rnd((DEPTH, D_FF, D_MODEL), D_FF ** -0.5),
        'norm_mix': gain((DEPTH, D_MODEL)),
        'w_in': rnd((DEPTH, D_MODEL, D_IN), D_MODEL ** -0.5),
        'q_norm': gain((DEPTH, ATT_GROUPS, ATT_HPG, HEAD_DIM)),
        'k_norm': gain((DEPTH, N_BRANCH, ATT_GROUPS, HEAD_DIM)),
        'w_cmp1': rnd((DEPTH, 2, CMP_RATIO, CMP_STRIDE * HEAD_DIM, CMP_HIDDEN), (CMP_BLOCK * HEAD_DIM) ** -0.5),
        'b_cmp1': rnd((DEPTH, 2, CMP_HIDDEN), 0.01),
        'w_cmp2': rnd((DEPTH, 2, CMP_HIDDEN, HEAD_DIM), CMP_HIDDEN ** -0.5),
        'b_cmp2': rnd((DEPTH, 2, HEAD_DIM), 0.01),
        'attn_out_norm': gain((DEPTH, ATT_WIDTH)),
        'hgrn_out_norm': gain((DEPTH, HG_HEADS, HG_VAL)),
        'w_out': rnd((DEPTH, MIX_WIDTH, D_MODEL), MIX_WIDTH ** -0.5),
        'norm_ffn2': gain((DEPTH, D_MODEL)),
        'w_ffn2_gate_up': rnd((DEPTH, D_MODEL, 2 * D_FF), D_MODEL ** -0.5),
        'w_ffn2_down': rnd((DEPTH, D_FF, D_MODEL), D_FF ** -0.5),
    }


def reference(x_prompt, x_sample, cache_cmp_kv, cache_slc_kv, state_win_kv, state_hgrn, page_table,
              rel_bias_table, hgrn_lower_bound, norm_ffn1, w_ffn1_gate_up, w_ffn1_down, norm_mix,
              w_in, q_norm, k_norm, w_cmp1, b_cmp1, w_cmp2, b_cmp2, attn_out_norm, hgrn_out_norm,
              w_out, norm_ffn2, w_ffn2_gate_up, w_ffn2_down):
    lower = jnp.cumsum(jax.nn.softmax(hgrn_lower_bound.astype(jnp.float32), axis=0), axis=0)
    yp, ys = x_prompt, x_sample
    outs = [[] for _ in range(8)]
    for l in range(DEPTH):
        lp = (w_in[l], q_norm[l], k_norm[l], w_cmp1[l], b_cmp1[l], w_cmp2[l], b_cmp2[l],
              attn_out_norm[l], hgrn_out_norm[l], w_out[l])
        yp = swiglu_half(yp, norm_ffn1[l], w_ffn1_gate_up[l], w_ffn1_down[l])
        ys = swiglu_half(ys, norm_ffn1[l], w_ffn1_gate_up[l], w_ffn1_down[l])
        mp, cp, sp, wp, hp = prompt_mixer(rms_norm(yp, norm_mix[l]), lower[l], rel_bias_table, *lp)
        ms, cs, ss, ws, hs = sample_mixer(rms_norm(ys, norm_mix[l]), cache_cmp_kv[l], cache_slc_kv[l],
                                          state_win_kv[l], state_hgrn[l], page_table, lower[l],
                                          rel_bias_table, *lp)
        yp = yp + mp
        ys = ys + ms
        yp = swiglu_half(yp, norm_ffn2[l], w_ffn2_gate_up[l], w_ffn2_down[l])
        ys = swiglu_half(ys, norm_ffn2[l], w_ffn2_gate_up[l], w_ffn2_down[l])
        for lst, a in zip(outs, (cp, sp, wp, hp, cs, ss, ws, hs)):
            lst.append(a)
    cmp_p, slc_p, win_p, hg_p, cmp_s, slc_s, win_s, hg_s = [jnp.stack(a) for a in outs]
    return (yp, ys, cmp_p, slc_p, win_p, hg_p, cmp_s, slc_s, win_s, hg_s)
```

```python
import functools
import math

import jax
import jax.numpy as jnp
import numpy as np
from jax import lax
from jax.experimental import pallas as pl
from jax.experimental.pallas import tpu as pltpu

D_MODEL = 2048
BATCH = 4
SEQ = 2048
DEPTH = 1
DEC_BATCH = 32
DEC_SEQ = 8
PAST_LEN = 8192
PAGE_SIZE = 128
HEAD_DIM = 64
ATT_HEADS = (D_MODEL // 2) // HEAD_DIM
ATT_GROUPS = ATT_HEADS // 4
ATT_HPG = ATT_HEADS // ATT_GROUPS
ATT_WIDTH = ATT_HEADS * HEAD_DIM
KV_WIDTH = ATT_GROUPS * HEAD_DIM
N_BRANCH = 3
CMP_BLOCK = 32
CMP_STRIDE = 16
CMP_RATIO = CMP_BLOCK // CMP_STRIDE
CMP_HIDDEN = 256
SEL_BLOCK = 64
N_SEL = 8
WINDOW = 512
Q_BLOCK = 128
HG_KEY = 128
HG_VAL = 128
HG_HEADS = (D_MODEL // 2) // HG_VAL
HG_WIDTH = HG_HEADS * HG_VAL
HG_CHUNK = 64
MIX_WIDTH = ATT_WIDTH + HG_WIDTH
D_FF = 256 * ((8 * D_MODEL // 3 + 255) // 256)
N_BUCKETS = 32
MAX_DISTANCE = 1024
EPS = 1e-6
IN_SPLITS = (ATT_WIDTH, 2 * N_BRANCH * KV_WIDTH, N_BRANCH * ATT_HEADS,
             HG_HEADS * HG_KEY, HG_HEADS * HG_KEY, HG_WIDTH, HG_WIDTH)
D_IN = sum(IN_SPLITS)

N_TOKENS = BATCH * SEQ + DEC_BATCH * DEC_SEQ

V7X_VMEM_BYTES = 64 * 1024 * 1024
VMEM_LIMIT_BYTES = 56 * 1024 * 1024

FFN_ROW_TILE = 768
FFN_FF_TILE = 512


def _ffn_kernel(x_ref, gain_ref, wg_ref, wu_ref, wd_ref, o_ref, xn_ref, acc_ref):
    j = pl.program_id(1)

    @pl.when(j == 0)
    def _():
        x = x_ref[...]
        y = x * lax.rsqrt(jnp.mean(x * x, axis=-1, keepdims=True) + EPS)
        xn_ref[...] = (y * gain_ref[...]).astype(jnp.bfloat16)
        acc_ref[...] = jnp.zeros_like(acc_ref)

    xn = xn_ref[...]
    g = jnp.dot(xn, wg_ref[...], preferred_element_type=jnp.float32)
    u = jnp.dot(xn, wu_ref[...], preferred_element_type=jnp.float32)
    a = (g * jax.nn.sigmoid(g) * u).astype(jnp.bfloat16)
    acc_ref[...] += jnp.dot(a, wd_ref[...], preferred_element_type=jnp.float32)

    @pl.when(j == pl.num_programs(1) - 1)
    def _():
        o_ref[...] = x_ref[...] + 0.5 * acc_ref[...]


def _ffn(x, gain, w_gu, w_down):
    n, d = x.shape
    tm, tf = FFN_ROW_TILE, FFN_FF_TILE
    assert n % tm == 0 and D_FF % tf == 0
    nj = D_FF // tf
    return pl.pallas_call(
        _ffn_kernel,
        grid=(n // tm, nj),
        in_specs=[
            pl.BlockSpec((tm, d), lambda i, j: (i, 0)),
            pl.BlockSpec((1, d), lambda i, j: (0, 0)),
            pl.BlockSpec((d, tf), lambda i, j: (0, j)),
            pl.BlockSpec((d, tf), lambda i, j: (0, j + nj)),
            pl.BlockSpec((tf, d), lambda i, j: (j, 0)),
        ],
        out_specs=pl.BlockSpec((tm, d), lambda i, j: (i, 0)),
        out_shape=jax.ShapeDtypeStruct((n, d), jnp.float32),
        scratch_shapes=[pltpu.VMEM((tm, d), jnp.bfloat16), pltpu.VMEM((tm, d), jnp.float32)],
        compiler_params=pltpu.CompilerParams(
            dimension_semantics=("arbitrary", "arbitrary"), vmem_limit_bytes=VMEM_LIMIT_BYTES),
        name="ffn",
    )(x, gain.reshape(1, d), w_gu, w_gu, w_down)


def _rms_norm(x, gain):
    xf = x.astype(jnp.float32)
    y = xf * lax.rsqrt(jnp.mean(xf * xf, axis=-1, keepdims=True) + EPS)
    return (y * gain.astype(jnp.float32)).astype(x.dtype)


def _t5_bucket(dist):
    n = jnp.maximum(dist, 0)
    exact = N_BUCKETS // 2
    logn = jnp.log(jnp.maximum(n, 1).astype(jnp.float32) / exact)
    large = exact + (logn / math.log(MAX_DISTANCE / exact) * (N_BUCKETS - exact)).astype(jnp.int32)
    return jnp.where(n < exact, n, jnp.minimum(large, N_BUCKETS - 1))


def _masked_softmax(s, mask):
    s = jnp.where(mask, s.astype(jnp.float32), -jnp.inf)
    m = jnp.max(s, axis=-1, keepdims=True)
    m = jnp.where(jnp.isfinite(m), m, 0.0)
    e = jnp.where(mask, jnp.exp(s - m), 0.0)
    d = jnp.sum(e, axis=-1, keepdims=True)
    return e / jnp.where(d > 0, d, 1.0)


def _block_overlap(n_cmp, n_blk):
    cs = np.arange(n_cmp)[:, None] * CMP_STRIDE
    bs = np.arange(n_blk)[None, :] * SEL_BLOCK
    ov = np.minimum(cs + CMP_BLOCK, bs + SEL_BLOCK) - np.maximum(cs, bs)
    return (np.clip(ov, 0, None) / CMP_BLOCK).astype(np.float32)


def _compress(rows, w1, b1, w2, b2):
    B, T, G, Dh = rows.shape
    n_cmp = (T - CMP_BLOCK) // CMP_STRIDE + 1
    n_chunk = n_cmp + CMP_RATIO - 1
    ch = rows[:, :n_chunk * CMP_STRIDE].reshape(B, n_chunk, CMP_STRIDE, G, Dh)
    ch = ch.transpose(0, 1, 3, 2, 4).reshape(B, n_chunk, G, CMP_STRIDE * Dh)
    h = b1
    for r in range(CMP_RATIO):
        h = h + jnp.einsum('bngf,fe->bnge', ch[:, r:r + n_cmp], w1[r])
    return jnp.einsum('bnge,ed->bngd', jax.nn.silu(h), w2) + b2


def _compressed_kv(rows, w1, b1, w2, b2, k_gain):
    kc = _rms_norm(_compress(rows[:, :, 0], w1[0], b1[0], w2[0], b2[0]), k_gain)
    vc = _compress(rows[:, :, 1], w1[1], b1[1], w2[1], b2[1])
    c_end = jnp.asarray(np.arange(kc.shape[1]) * CMP_STRIDE + CMP_BLOCK - 1, dtype=jnp.int32)
    return kc, vc, c_end


def _sel_blocks(rows):
    B, T = rows.shape[:2]
    nb = -(-T // SEL_BLOCK)
    rows = jnp.pad(rows, ((0, 0), (0, nb * SEL_BLOCK - T), (0, 0), (0, 0), (0, 0)))
    return rows.reshape(B, nb, SEL_BLOCK, 2, ATT_GROUPS, HEAD_DIM).transpose(0, 3, 4, 1, 2, 5)


def _project(xn, w_in, q_gain, k_gain, lb):
    B, T, _ = xn.shape
    pts = [int(v) for v in np.cumsum(IN_SPLITS)[:-1]]
    a_q, a_kv, a_g, h_q, h_f, h_i, h_g = jnp.split(xn @ w_in, pts, axis=-1)
    q = _rms_norm(a_q.reshape(B, T, ATT_GROUPS, ATT_HPG, HEAD_DIM), q_gain)
    kv = a_kv.reshape(B, T, N_BRANCH, 2, ATT_GROUPS, HEAD_DIM)
    cmp_rows = kv[:, :, 0]
    k_nrm = _rms_norm(kv[:, :, 1:, 0], k_gain[1:])
    slc_rows = jnp.stack([k_nrm[:, :, 0], kv[:, :, 1, 1]], axis=2)
    win_rows = jnp.stack([k_nrm[:, :, 1], kv[:, :, 2, 1]], axis=2)
    gates = jax.nn.sigmoid(a_g.reshape(B, T, N_BRANCH, ATT_GROUPS, ATT_HPG))
    z = h_f.reshape(B, T, HG_HEADS, HG_KEY).astype(jnp.float32)
    lbh = lb.reshape(HG_HEADS, HG_KEY)
    log_f = jnp.log(lbh + (1.0 - lbh) * jax.nn.sigmoid(z))
    k_hg = (1.0 - lbh) * jax.nn.sigmoid(-z)
    q_hg = h_q.reshape(B, T, HG_HEADS, HG_KEY)
    v_hg = h_i.reshape(B, T, HG_HEADS, HG_VAL)
    return q, gates, cmp_rows, slc_rows, win_rows, q_hg, k_hg, v_hg, log_f, h_g


def _nsa_core(q, qpos, gates, kc, vc, c_end, sel, wkv, wpos, bias_table):
    B, T = q.shape[:2]
    scale = HEAD_DIM ** -0.5
    tbl = bias_table.astype(jnp.float32).reshape(N_BUCKETS, ATT_GROUPS, ATT_HPG)
    dist_c = qpos[:, None] - c_end[None, :]
    s = jnp.einsum('btgpd,bcgd->bgptc', q, kc).astype(jnp.float32) * scale
    s = s + tbl[_t5_bucket(dist_c)].transpose(2, 3, 0, 1)
    p_cmp = _masked_softmax(s, dist_c >= 0)
    o_cmp = jnp.einsum('bgptc,bcgd->btgpd', p_cmp, vc.astype(jnp.float32))
    n_blk = sel.shape[3]
    n_pick = min(N_SEL, n_blk)
    imp = jnp.einsum('bgptc,cn->bgtn', p_cmp, jnp.asarray(_block_overlap(kc.shape[1], n_blk)))
    blk = jnp.arange(n_blk)
    cur = (qpos // SEL_BLOCK)[:, None]
    forced = (blk == 0) | (blk == cur) | (blk == cur - 1)
    valid = blk * SEL_BLOCK <= qpos[:, None]
    score = jnp.where(forced, ATT_HPG + 1.0, jnp.where(valid, imp, -1.0))
    idx = lax.top_k(score, n_pick)[1]
    bi = jnp.arange(B)[:, None, None, None]
    gi = jnp.arange(ATT_GROUPS)[None, :, None, None]
    k_sel = sel[:, 0][bi, gi, idx]
    v_sel = sel[:, 1][bi, gi, idx]
    dist_s = qpos[:, None, None] - (idx[..., None] * SEL_BLOCK + jnp.arange(SEL_BLOCK))
    bias_s = tbl[_t5_bucket(dist_s), gi[..., None]].transpose(0, 1, 5, 2, 3, 4)
    s = jnp.einsum('btgpd,bgtnld->bgptnl', q, k_sel).astype(jnp.float32) * scale + bias_s
    p = _masked_softmax(s.reshape(B, ATT_GROUPS, ATT_HPG, T, -1),
                        (dist_s >= 0).reshape(B, ATT_GROUPS, 1, T, -1))
    o_slc = jnp.einsum('bgptnl,bgtnld->btgpd', p.reshape(s.shape), v_sel.astype(jnp.float32))
    dist_w = qpos[:, None] - wpos[None, :]
    s = jnp.einsum('btgpd,bsgd->bgpts', q, wkv[:, :, 0]).astype(jnp.float32) * scale
    s = s + tbl[_t5_bucket(dist_w)].transpose(2, 3, 0, 1)
    p_win = _masked_softmax(s, (dist_w >= 0) & (dist_w < WINDOW) & (wpos[None, :] >= 0))
    o_win = jnp.einsum('bgpts,bsgd->btgpd', p_win, wkv[:, :, 1].astype(jnp.float32))
    g = gates.astype(jnp.float32)[..., None]
    return g[:, :, 0] * o_cmp + g[:, :, 1] * o_slc + g[:, :, 2] * o_win


def _hgrn2_chunked(q, k, v, log_f, s0):
    B, T, H, _ = q.shape
    c = min(HG_CHUNK, T)
    pad = (-T) % c

    def prep(a):
        a = jnp.pad(a.astype(jnp.float32), ((0, 0), (0, pad), (0, 0), (0, 0)))
        return a.reshape(B, a.shape[1] // c, c, H, a.shape[-1]).transpose(1, 0, 3, 2, 4)

    causal = jnp.tril(jnp.ones((c, c), dtype=bool))

    def step(S, inp):
        qc, kc, vc, lc = inp
        b = jnp.cumsum(lc, axis=2)
        o = jnp.einsum('bhtd,bhde->bhte', qc * jnp.exp(b), S)
        diff = b[:, :, :, None, :] - b[:, :, None, :, :]
        dec = jnp.exp(jnp.where(causal[:, :, None], diff, -jnp.inf))
        a = jnp.einsum('bhtd,bhsd,bhtsd->bhts', qc, kc, dec)
        o = o + jnp.einsum('bhts,bhse->bhte', a, vc)
        b_last = b[:, :, -1]
        S = jnp.exp(b_last)[..., None] * S + jnp.einsum(
            'bhsd,bhse->bhde', kc * jnp.exp(b_last[:, :, None] - b), vc)
        return S, o

    s_T, o = lax.scan(step, s0.astype(jnp.float32), (prep(q), prep(k), prep(v), prep(log_f)))
    o = o.transpose(1, 0, 3, 2, 4).reshape(B, -1, H, v.shape[-1])[:, :T]
    return o, s_T


def _mixer_out(o_att, o_hg, og, attn_gain, hg_gain, w_out, dtype):
    B, T = og.shape[:2]
    a = _rms_norm(o_att.reshape(B, T, ATT_WIDTH), attn_gain).astype(dtype)
    h = (_rms_norm(o_hg, hg_gain).reshape(B, T, HG_WIDTH)
         * jax.nn.silu(og.astype(jnp.float32))).astype(dtype)
    return jnp.concatenate([a, h], axis=-1) @ w_out


def _prompt_mixer(xn, lb, bias_table, w_in, q_gain, k_gain, w_cmp1, b_cmp1, w_cmp2, b_cmp2,
                  attn_gain, hg_gain, w_out):
    B, T, _ = xn.shape
    q, gates, cmp_rows, slc_rows, win_rows, hq, hk, hv, hlf, hog = _project(xn, w_in, q_gain, k_gain, lb)
    kc, vc, c_end = _compressed_kv(cmp_rows, w_cmp1, b_cmp1, w_cmp2, b_cmp2, k_gain[0])
    sel = _sel_blocks(slc_rows)
    win_pad = jnp.pad(win_rows, ((0, 0), (WINDOW, 0), (0, 0), (0, 0), (0, 0)))
    nqb = T // Q_BLOCK

    def qblock(args):
        i, qb, gb = args
        start = i * Q_BLOCK
        qpos = start + jnp.arange(Q_BLOCK)
        wkv = lax.dynamic_slice_in_dim(win_pad, start, WINDOW + Q_BLOCK, axis=1)
        wpos = start - WINDOW + jnp.arange(WINDOW + Q_BLOCK)
        return _nsa_core(qb, qpos, gb, kc, vc, c_end, sel, wkv, wpos, bias_table)

    def to_blocks(a):
        return a.reshape(B, nqb, Q_BLOCK, *a.shape[2:]).swapaxes(0, 1)

    o = lax.map(qblock, (jnp.arange(nqb), to_blocks(q), to_blocks(gates)))
    o_att = o.swapaxes(0, 1).reshape(B, T, ATT_GROUPS, ATT_HPG, HEAD_DIM)
    s0 = jnp.zeros((B, HG_HEADS, HG_KEY, HG_VAL), jnp.float32)
    o_hg, s_T = _hgrn2_chunked(hq, hk, hv, hlf, s0)
    y = _mixer_out(o_att, o_hg, hog, attn_gain, hg_gain, w_out, xn.dtype)
    wb = min(WINDOW, T)
    return y, cmp_rows, slc_rows, win_rows[:, T - wb:], s_T.astype(xn.dtype)


def _sample_mixer(xn, cache_cmp, cache_slc, win_buf, h_state, page_table, lb, bias_table,
                  w_in, q_gain, k_gain, w_cmp1, b_cmp1, w_cmp2, b_cmp2, attn_gain, hg_gain, w_out):
    B, T, _ = xn.shape
    past_len = page_table.shape[1] * cache_cmp.shape[1]
    q, gates, cmp_rows, slc_rows, win_rows, hq, hk, hv, hlf, hog = _project(xn, w_in, q_gain, k_gain, lb)

    def gather(cache):
        pages = cache[page_table]
        return pages.reshape(B, past_len, *cache.shape[2:]).astype(xn.dtype)

    full_cmp = jnp.concatenate([gather(cache_cmp), cmp_rows], axis=1)
    full_slc = jnp.concatenate([gather(cache_slc), slc_rows], axis=1)
    kc, vc, c_end = _compressed_kv(full_cmp, w_cmp1, b_cmp1, w_cmp2, b_cmp2, k_gain[0])
    sel = _sel_blocks(full_slc)
    wb = win_buf.shape[1]
    wkv = jnp.concatenate([win_buf.astype(xn.dtype), win_rows], axis=1)
    wpos = past_len - wb + jnp.arange(wb + T)
    qpos = past_len + jnp.arange(T)
    o_att = _nsa_core(q, qpos, gates, kc, vc, c_end, sel, wkv, wpos, bias_table)
    o_hg, s_T = _hgrn2_chunked(hq, hk, hv, hlf, h_state)
    y = _mixer_out(o_att, o_hg, hog, attn_gain, hg_gain, w_out, xn.dtype)
    return y, cmp_rows, slc_rows, wkv[:, T:], s_T.astype(h_state.dtype)


def kernel(x_prompt, x_sample, cache_cmp_kv, cache_slc_kv, state_win_kv, state_hgrn, page_table,
           rel_bias_table, hgrn_lower_bound, norm_ffn1, w_ffn1_gate_up, w_ffn1_down, norm_mix,
           w_in, q_norm, k_norm, w_cmp1, b_cmp1, w_cmp2, b_cmp2, attn_out_norm, hgrn_out_norm,
           w_out, norm_ffn2, w_ffn2_gate_up, w_ffn2_down):
    bf16 = jnp.bfloat16
    lower = jnp.cumsum(jax.nn.softmax(hgrn_lower_bound.astype(jnp.float32), axis=0), axis=0)
    np_tok = BATCH * SEQ
    x_all = jnp.concatenate([x_prompt.reshape(np_tok, D_MODEL),
                             x_sample.reshape(DEC_BATCH * DEC_SEQ, D_MODEL)], axis=0)
    l = 0
    lp = (w_in[l], q_norm[l], k_norm[l], w_cmp1[l], b_cmp1[l], w_cmp2[l], b_cmp2[l],
          attn_out_norm[l], hgrn_out_norm[l], w_out[l])
    y1 = _ffn(x_all, norm_ffn1[l], w_ffn1_gate_up[l].astype(bf16), w_ffn1_down[l].astype(bf16))
    yp = y1[:np_tok].reshape(BATCH, SEQ, D_MODEL)
    ys = y1[np_tok:].reshape(DEC_BATCH, DEC_SEQ, D_MODEL)
    mp, cp, sp, wp, hp = _prompt_mixer(_rms_norm(yp, norm_mix[l]), lower[l], rel_bias_table, *lp)
    ms, cs, ss, ws, hs = _sample_mixer(_rms_norm(ys, norm_mix[l]), cache_cmp_kv[l], cache_slc_kv[l],
                                       state_win_kv[l], state_hgrn[l], page_table, lower[l],
                                       rel_bias_table, *lp)
    y2 = jnp.concatenate([(yp + mp).reshape(np_tok, D_MODEL),
                          (ys + ms).reshape(DEC_BATCH * DEC_SEQ, D_MODEL)], axis=0)
    y3 = _ffn(y2, norm_ffn2[l], w_ffn2_gate_up[l].astype(bf16), w_ffn2_down[l].astype(bf16))
    yp3 = y3[:np_tok].reshape(BATCH, SEQ, D_MODEL)
    ys3 = y3[np_tok:].reshape(DEC_BATCH, DEC_SEQ, D_MODEL)
    return (yp3, ys3, cp[None], sp[None], wp[None], hp[None], cs[None], ss[None], ws[None], hs[None])
```

```python
import functools
import math

import jax
import jax.numpy as jnp
import numpy as np
from jax import lax
from jax.experimental import pallas as pl
from jax.experimental.pallas import tpu as pltpu

D_MODEL = 2048
BATCH = 4
SEQ = 2048
DEPTH = 1
DEC_BATCH = 32
DEC_SEQ = 8
PAST_LEN = 8192
PAGE_SIZE = 128
HEAD_DIM = 64
ATT_HEADS = (D_MODEL // 2) // HEAD_DIM
ATT_GROUPS = ATT_HEADS // 4
ATT_HPG = ATT_HEADS // ATT_GROUPS
ATT_WIDTH = ATT_HEADS * HEAD_DIM
KV_WIDTH = ATT_GROUPS * HEAD_DIM
N_BRANCH = 3
CMP_BLOCK = 32
CMP_STRIDE = 16
CMP_RATIO = CMP_BLOCK // CMP_STRIDE
CMP_HIDDEN = 256
SEL_BLOCK = 64
N_SEL = 8
WINDOW = 512
Q_BLOCK = 128
HG_KEY = 128
HG_VAL = 128
HG_HEADS = (D_MODEL // 2) // HG_VAL
HG_WIDTH = HG_HEADS * HG_VAL
MIX_WIDTH = ATT_WIDTH + HG_WIDTH
D_FF = 256 * ((8 * D_MODEL // 3 + 255) // 256)
N_BUCKETS = 32
MAX_DISTANCE = 1024
EPS = 1e-6
IN_SPLITS = (ATT_WIDTH, 2 * N_BRANCH * KV_WIDTH, N_BRANCH * ATT_HEADS,
             HG_HEADS * HG_KEY, HG_HEADS * HG_KEY, HG_WIDTH, HG_WIDTH)
D_IN = sum(IN_SPLITS)

N_PROMPT = BATCH * SEQ
N_SAMPLE = DEC_BATCH * DEC_SEQ
N_TOKENS = N_PROMPT + N_SAMPLE
N_PAGES = PAST_LEN // PAGE_SIZE
KV_ROW = 2 * KV_WIDTH

LANES = 128
V7X_VMEM_BYTES = 64 * 1024 * 1024
VMEM_LIMIT_BYTES = 56 * 1024 * 1024

NEG = -1e30
F32 = jnp.float32
BF16 = jnp.bfloat16

FFN_ROW_TILE = 768
FFN_FF_TILE = 512
PROJ_ROW_TILE = 384
PROJ_COL_TILE = 512
OUT_ROW_TILE = 256
HG_CHUNK_PROMPT = 16
HG_CHUNK_SAMPLE = DEC_SEQ
CMP_UNIT_ROWS = 2048
CMP_UNITS = 4


def _cparams(*sem):
    return pltpu.CompilerParams(dimension_semantics=sem, vmem_limit_bytes=VMEM_LIMIT_BYTES)


def _t5_thresholds():
    n = np.arange(0, 2 * MAX_DISTANCE + 2)
    exact = N_BUCKETS // 2
    logn = np.log(np.maximum(n, 1).astype(np.float64) / exact)
    large = exact + (logn / math.log(MAX_DISTANCE / exact) * (N_BUCKETS - exact)).astype(np.int32)
    b = np.where(n < exact, n, np.minimum(large, N_BUCKETS - 1))
    return [int(n[b >= k][0]) for k in range(N_BUCKETS)]


T5_THRESHOLDS = _t5_thresholds()


def _block_diag_ones(n, blk):
    i = np.arange(n)
    return (i[:, None] // blk == i[None, :] // blk).astype(np.float32)


def _split2(x):
    hi = x.astype(BF16)
    lo = (x - hi.astype(F32)).astype(BF16)
    return hi, lo


def _group_rms(t, bd, gain, width):
    hi, lo = _split2(t * t)
    ss = (jnp.dot(hi, bd, preferred_element_type=F32) + jnp.dot(lo, bd, preferred_element_type=F32))
    return t * lax.rsqrt(ss * (1.0 / width) + EPS) * gain


def _ffn_kernel(x_ref, gain_ref, wg_ref, wu_ref, wd_ref, o_ref, xn_ref, acc_ref):
    j = pl.program_id(1)

    @pl.when(j == 0)
    def _():
        x = x_ref[...]
        y = x * lax.rsqrt(jnp.mean(x * x, axis=-1, keepdims=True) + EPS)
        xn_ref[...] = (y * gain_ref[...]).astype(BF16)
        acc_ref[...] = jnp.zeros_like(acc_ref)

    xn = xn_ref[...]
    g = jnp.dot(xn, wg_ref[...], preferred_element_type=F32)
    u = jnp.dot(xn, wu_ref[...], preferred_element_type=F32)
    a = (g * jax.nn.sigmoid(g) * u).astype(BF16)
    acc_ref[...] += jnp.dot(a, wd_ref[...], preferred_element_type=F32)

    @pl.when(j == pl.num_programs(1) - 1)
    def _():
        o_ref[...] = x_ref[...] + 0.5 * acc_ref[...]


def _ffn(x, gain, w_gu, w_down):
    n, d = x.shape
    tm, tf = FFN_ROW_TILE, FFN_FF_TILE
    assert n % tm == 0 and D_FF % tf == 0
    nj = D_FF // tf
    return pl.pallas_call(
        _ffn_kernel,
        grid=(n // tm, nj),
        in_specs=[
            pl.BlockSpec((tm, d), lambda i, j: (i, 0)),
            pl.BlockSpec((1, d), lambda i, j: (0, 0)),
            pl.BlockSpec((d, tf), lambda i, j: (0, j)),
            pl.BlockSpec((d, tf), lambda i, j: (0, j + nj)),
            pl.BlockSpec((tf, d), lambda i, j: (j, 0)),
        ],
        out_specs=pl.BlockSpec((tm, d), lambda i, j: (i, 0)),
        out_shape=jax.ShapeDtypeStruct((n, d), F32),
        scratch_shapes=[pltpu.VMEM((tm, d), BF16), pltpu.VMEM((tm, d), F32)],
        compiler_params=_cparams("arbitrary", "arbitrary"),
        name="ffn",
    )(x, gain.reshape(1, d), w_gu, w_gu, w_down)


PB_Q, PB_CMP, PB_SLC, PB_WIN, PB_HQ, PB_HF, PB_HI, PB_HG, PB_GATE = 0, 2, 3, 4, 5, 7, 9, 11, 13
PROJ_N_BLOCKS = 14


def _permute_w_in(w_in):
    p = [int(v) for v in np.cumsum(IN_SPLITS)]
    a_g = w_in[:, p[1]:p[2]]
    src = np.zeros((PROJ_COL_TILE,), np.int32)
    valid = np.zeros((PROJ_COL_TILE,), bool)
    for g in range(ATT_GROUPS):
        for br in range(N_BRANCH):
            for hp in range(ATT_HPG):
                src[g * LANES + br * ATT_HPG + hp] = br * ATT_HEADS + g * ATT_HPG + hp
                valid[g * LANES + br * ATT_HPG + hp] = True
    gate = jnp.where(jnp.asarray(valid)[None, :], a_g[:, src], 0.0)
    return jnp.concatenate([w_in[:, :p[1]], w_in[:, p[2]:], gate], axis=1).astype(BF16)


def _pack_kv(k, v):
    parts = []
    for g in range(ATT_GROUPS):
        parts.append(k[:, g * HEAD_DIM:(g + 1) * HEAD_DIM])
        parts.append(v[:, g * HEAD_DIM:(g + 1) * HEAD_DIM])
    return jnp.concatenate(parts, axis=1).astype(BF16)


def _proj_kernel(x_ref, gain_ref, w_ref, bd_ref, qg_ref, kg_ref, lbp_ref,
                 q_ref, cmp_ref, slc_ref, slcp_ref, win_ref, winp_ref,
                 hq_ref, lf_ref, hk_ref, hv_ref, hog_ref, gate_ref, xn_ref):
    c = pl.program_id(1)

    @pl.when(c == 0)
    def _():
        x = x_ref[...]
        y = x * lax.rsqrt(jnp.mean(x * x, axis=-1, keepdims=True) + EPS)
        xn_ref[...] = (y * gain_ref[...]).astype(BF16)

    acc = jnp.dot(xn_ref[...], w_ref[...], preferred_element_type=F32)
    half = KV_WIDTH

    @pl.when(c < PB_CMP)
    def _():
        q_ref[...] = _group_rms(acc, bd_ref[...], qg_ref[...], HEAD_DIM)

    @pl.when(c == PB_CMP)
    def _():
        cmp_ref[...] = acc

    def kv_branch(rows_ref, pack_ref, br):
        k = _group_rms(acc[:, :half], bd_ref[:half, :half], kg_ref[br - 1:br, :], HEAD_DIM)
        v = acc[:, half:]
        rows_ref[:, :half] = k
        rows_ref[:, half:] = v
        pack_ref[...] = _pack_kv(k, v)

    @pl.when(c == PB_SLC)
    def _():
        kv_branch(slc_ref, slcp_ref, 1)

    @pl.when(c == PB_WIN)
    def _():
        kv_branch(win_ref, winp_ref, 2)

    @pl.when((c >= PB_HQ) & (c < PB_HF))
    def _():
        hq_ref[...] = acc

    @pl.when((c >= PB_HF) & (c < PB_HI))
    def _():
        p = lbp_ref[...]
        e = jnp.exp(p - jnp.max(p, axis=0, keepdims=True))
        lb = e[0:1, :] / jnp.sum(e, axis=0, keepdims=True)
        lf_ref[...] = jnp.log(lb + (1.0 - lb) * jax.nn.sigmoid(acc))
        hk_ref[...] = (1.0 - lb) * jax.nn.sigmoid(-acc)

    @pl.when((c >= PB_HI) & (c < PB_HG))
    def _():
        hv_ref[...] = acc

    @pl.when((c >= PB_HG) & (c < PB_GATE))
    def _():
        hog_ref[...] = acc

    @pl.when(c == PB_GATE)
    def _():
        gate_ref[...] = jax.nn.sigmoid(acc)


def _project_all(y, gain, w_perm, q_gain, k_gain, lb_logits):
    n, d = y.shape
    tm, tc = PROJ_ROW_TILE, PROJ_COL_TILE
    assert n % tm == 0 and DEPTH == 1
    bd = jnp.asarray(_block_diag_ones(tc, HEAD_DIM), BF16)

    def two(first):
        return lambda i, c: (i, jnp.clip(c - first, 0, 1))

    one = lambda i, c: (i, 0)
    wide = lambda dt: jax.ShapeDtypeStruct((n, 2 * tc), dt)
    narrow = lambda dt: jax.ShapeDtypeStruct((n, tc), dt)
    out_shape = (wide(F32), narrow(F32), narrow(F32), narrow(BF16), narrow(F32), narrow(BF16),
                 wide(F32), wide(F32), wide(F32), wide(F32), wide(F32), narrow(F32))
    blk = lambda f: pl.BlockSpec((tm, tc), f)
    out_specs = (blk(two(PB_Q)), blk(one), blk(one), blk(one), blk(one), blk(one),
                 blk(two(PB_HQ)), blk(two(PB_HF)), blk(two(PB_HF)), blk(two(PB_HI)), blk(two(PB_HG)), blk(one))
    return pl.pallas_call(
        _proj_kernel,
        grid=(n // tm, PROJ_N_BLOCKS),
        in_specs=[
            pl.BlockSpec((tm, d), lambda i, c: (i, 0)),
            pl.BlockSpec((1, d), lambda i, c: (0, 0)),
            pl.BlockSpec((d, tc), lambda i, c: (0, c)),
            pl.BlockSpec((tc, tc), lambda i, c: (0, 0)),
            pl.BlockSpec((1, tc), lambda i, c: (0, jnp.clip(c, 0, 1))),
            pl.BlockSpec((2, KV_WIDTH), lambda i, c: (0, 0)),
            pl.BlockSpec((DEPTH + 1, tc), lambda i, c: (0, jnp.clip(c - PB_HF, 0, 1))),
        ],
        out_specs=out_specs,
        out_shape=out_shape,
        scratch_shapes=[pltpu.VMEM((tm, d), BF16)],
        compiler_params=_cparams("arbitrary", "arbitrary"),
        name="proj",
    )(y, gain.reshape(1, d), w_perm, bd, q_gain.reshape(1, ATT_WIDTH),
      k_gain[1:].reshape(2, KV_WIDTH), lb_logits)


def _hgrn_kernel(q_ref, k_ref, v_ref, lf_ref, tri_ref, ones_ref, s0_ref, o_ref, st_ref,
                 qe_scr, kd_scr, dec_scr, *, t_len, ch, has_state):
    nj = t_len // ch
    shape3 = (nj, ch, HG_KEY)
    q3, k3, v3, lf3 = q_ref[...], k_ref[...], v_ref[...], lf_ref[...]
    tl = lax.broadcasted_iota(jnp.int32, shape3, 1)

    def row(x3, s):
        return jnp.broadcast_to(x3[:, s:s + 1, :], shape3)

    if t_len >= 256:
        lf2 = lf3.reshape(t_len, HG_KEY)
        parts = []
        for r0 in range(0, t_len, 256):
            x = lf2[r0:r0 + 256]
            hi = x.astype(BF16)
            r1 = x - hi.astype(F32)
            mid = r1.astype(BF16)
            lo = (r1 - mid.astype(F32)).astype(BF16)
            tri = tri_ref[...]
            parts.append(jnp.dot(tri, hi, preferred_element_type=F32)
                         + jnp.dot(tri, mid, preferred_element_type=F32)
                         + jnp.dot(tri, lo, preferred_element_type=F32))
        b3 = jnp.concatenate(parts, axis=0).reshape(shape3)
    else:
        b3 = jnp.zeros(shape3, F32)
        for s in range(ch):
            b3 = b3 + jnp.where(tl >= s, row(lf3, s), 0.0)

    bl3 = row(b3, ch - 1)
    qe_scr[...] = (q3 * jnp.exp(b3)).astype(BF16)
    kd_scr[...] = (k3 * jnp.exp(bl3 - b3)).astype(BF16)
    dec_scr[...] = jnp.exp(b3[:, ch - 1:ch, :])

    ones = ones_ref[...]
    od = jnp.zeros(shape3, F32)
    for s in range(ch):
        w = q3 * row(k3, s) * jnp.exp(jnp.where(tl >= s, b3 - row(b3, s), NEG))
        a = jnp.dot(w.reshape(t_len, HG_KEY).astype(BF16), ones, preferred_element_type=F32)
        od = od + a.reshape(shape3) * row(v3, s)
    o_ref[...] = od

    if has_state:
        st0 = s0_ref[0, 0].T
    else:
        st0 = jnp.zeros((HG_VAL, HG_KEY), F32)

    def body(j, st):
        oj = lax.dot_general(qe_scr[j], st.astype(BF16), (((1,), (1,)), ((), ())),
                             preferred_element_type=F32)
        o_ref[j] = o_ref[j] + oj
        ut = lax.dot_general(v_ref[j].astype(BF16), kd_scr[j], (((0,), (0,)), ((), ())),
                             preferred_element_type=F32)
        return st * dec_scr[j] + ut

    st = lax.fori_loop(0, nj, body, st0, unroll=min(nj, 8))
    st_ref[0, 0] = st.T


def _hgrn(hq, hk, hv, lf, s0, *, n_seq, t_len, ch, row0):
    n = hq.shape[0]
    nj = t_len // ch
    assert row0 % t_len == 0 and t_len % ch == 0 and ch % 8 == 0
    blk0 = row0 // t_len
    r3 = lambda a: a.reshape(n // ch, ch, HG_WIDTH)
    has_state = s0 is not None
    if not has_state:
        s0 = jnp.zeros((1, 1, HG_KEY, HG_VAL), F32)
    tri_n = 256 if t_len >= 256 else 8
    i = np.arange(tri_n)
    tri = jnp.asarray(((i[:, None] // ch == i[None, :] // ch) & (i[:, None] >= i[None, :])).astype(np.float32), BF16)
    ones = jnp.ones((HG_KEY, HG_KEY), BF16)
    seq = pl.BlockSpec((nj, ch, HG_KEY), lambda b, h: (blk0 + b, 0, h))
    state_in = pl.BlockSpec((1, 1, HG_KEY, HG_VAL), (lambda b, h: (b, h, 0, 0)) if has_state else (lambda b, h: (0, 0, 0, 0)))
    o, st = pl.pallas_call(
        functools.partial(_hgrn_kernel, t_len=t_len, ch=ch, has_state=has_state),
        grid=(n_seq, HG_HEADS),
        in_specs=[seq, seq, seq, seq,
                  pl.BlockSpec((tri_n, tri_n), lambda b, h: (0, 0)),
                  pl.BlockSpec((HG_KEY, HG_KEY), lambda b, h: (0, 0)),
                  state_in],
        out_specs=(pl.BlockSpec((nj, ch, HG_VAL), lambda b, h: (b, 0, h)),
                   pl.BlockSpec((1, 1, HG_KEY, HG_VAL), lambda b, h: (b, h, 0, 0))),
        out_shape=(jax.ShapeDtypeStruct((n_seq * nj, ch, HG_WIDTH), F32),
                   jax.ShapeDtypeStruct((n_seq, HG_HEADS, HG_KEY, HG_VAL), F32)),
        scratch_shapes=[pltpu.VMEM((nj, ch, HG_KEY), BF16), pltpu.VMEM((nj, ch, HG_KEY), BF16),
                        pltpu.VMEM((nj, 1, HG_KEY), F32)],
        compiler_params=_cparams("arbitrary", "arbitrary"),
        name="hgrn",
    )(r3(hq), r3(hk), r3(hv), r3(lf), tri, ones, s0)
    return o.reshape(n_seq * t_len, HG_WIDTH), st


def _mixout_kernel(y_ref, oa_ref, oh_ref, og_ref, ag_ref, hgain_ref, w_ref, o_ref):
    oa = oa_ref[...]
    a = oa * lax.rsqrt(jnp.mean(oa * oa, axis=-1, keepdims=True) + EPS) * ag_ref[...]
    oh = oh_ref[...]
    hs = []
    for h in range(HG_HEADS):
        x = oh[:, h * HG_VAL:(h + 1) * HG_VAL]
        hs.append(x * lax.rsqrt(jnp.mean(x * x, axis=-1, keepdims=True) + EPS))
    og = og_ref[...]
    hh = jnp.concatenate(hs, axis=1) * hgain_ref[...] * (og * jax.nn.sigmoid(og))
    m = (jnp.dot(a.astype(BF16), w_ref[:ATT_WIDTH, :], preferred_element_type=F32)
         + jnp.dot(hh.astype(BF16), w_ref[ATT_WIDTH:, :], preferred_element_type=F32))
    o_ref[...] = y_ref[...] + m


def _mixer_out(y, o_att, o_hg, og, attn_gain, hg_gain, w_out):
    n, d = y.shape
    tm = OUT_ROW_TILE
    assert n % tm == 0
    row = lambda w: pl.BlockSpec((tm, w), lambda i: (i, 0))
    const = lambda s: pl.BlockSpec(s, lambda i: (0, 0))
    return pl.pallas_call(
        _mixout_kernel,
        grid=(n // tm,),
        in_specs=[row(d), row(ATT_WIDTH), row(HG_WIDTH), row(HG_WIDTH),
                  const((1, ATT_WIDTH)), const((1, HG_WIDTH)), const((MIX_WIDTH, d))],
        out_specs=row(d),
        out_shape=jax.ShapeDtypeStruct((n, d), F32),
        compiler_params=_cparams("arbitrary"),
        name="mixout",
    )(y, o_att, o_hg, og, attn_gain.reshape(1, ATT_WIDTH), hg_gain.reshape(1, HG_WIDTH), w_out)


def _bias_kernel(tbl_ref, o_ref, *, a0, ar, ac, rows_blk):
    h = pl.program_id(0)
    rb = pl.program_id(1)
    shape = o_ref.shape[1:]
    r = lax.broadcasted_iota(jnp.int32, shape, 0) + rb * rows_blk
    c = lax.broadcasted_iota(jnp.int32, shape, 1)
    n = a0 + ar * r + ac * c
    out = jnp.full(shape, tbl_ref[0, h], F32)
    for k in range(1, N_BUCKETS):
        out = jnp.where(n >= T5_THRESHOLDS[k], tbl_ref[k, h], out)
    o_ref[0] = out


def _bias_table(tbl, rows, cols, a0, ar, ac):
    rows_blk = min(rows, 512)
    assert rows % rows_blk == 0 and cols % LANES == 0
    return pl.pallas_call(
        functools.partial(_bias_kernel, a0=a0, ar=ar, ac=ac, rows_blk=rows_blk),
        grid=(ATT_HEADS, rows // rows_blk),
        in_specs=[pl.BlockSpec(memory_space=pltpu.SMEM)],
        out_specs=pl.BlockSpec((1, rows_blk, cols), lambda h, rb: (h, rb, 0)),
        out_shape=jax.ShapeDtypeStruct((ATT_HEADS, rows, cols), F32),
        compiler_params=_cparams("arbitrary", "arbitrary"),
        name="t5_bias",
    )(tbl)


def _compress_weights(w1, b1, w2, b2):
    w = w1.reshape(2, CMP_RATIO, CMP_STRIDE, HEAD_DIM, CMP_HIDDEN)
    z = jnp.zeros_like(w)
    top = jnp.concatenate([w, z], axis=-1)
    bot = jnp.concatenate([z, w], axis=-1)
    wpad = jnp.stack([top, bot], axis=3)
    wpad = wpad.reshape(2 * CMP_RATIO, CMP_STRIDE * LANES, 2 * CMP_HIDDEN).astype(BF16)
    z2 = jnp.zeros_like(w2)
    w2pad = jnp.concatenate([jnp.concatenate([w2, z2], axis=-1), jnp.concatenate([z2, w2], axis=-1)], axis=1)
    return wpad, jnp.concatenate([b1, b1], axis=-1), w2pad.astype(BF16), jnp.concatenate([b2, b2], axis=-1)


def _compress_kernel(*refs, n_in, paged, packed):
    if paged:
        refs = refs[1:]
    in_refs = refs[:n_in]
    perm_ref, wpad_ref, b1_ref, w2pad_ref, b2_ref, kg_ref, bd_ref, out_ref, xs_ref = refs[n_in:]
    u = pl.program_id(1)
    n_chunk = CMP_UNITS * CMP_UNIT_ROWS // CMP_STRIDE
    perm = perm_ref[...]

    for t in range(CMP_UNIT_ROWS // 256):
        if paged:
            x = jnp.concatenate([in_refs[2 * t][0], in_refs[2 * t + 1][0]], axis=0)
        else:
            x = in_refs[0][t * 256:(t + 1) * 256, :]
        y = jnp.dot(perm, x.astype(BF16), preferred_element_type=F32).astype(BF16)
        base = pl.multiple_of(u * (CMP_UNIT_ROWS // CMP_STRIDE) + t * 16, 16)
        for s in range(CMP_STRIDE):
            xs_ref[s, pl.ds(base, 16), :] = y[s * 16:(s + 1) * 16, :]

    @pl.when(u == CMP_UNITS - 1)
    def _():
        halves = []
        for kv in range(2):
            outs = []
            for j in range(2):
                col = (kv * 2 + j) * LANES
                lhs = jnp.concatenate([xs_ref[s, :, col:col + LANES] for s in range(CMP_STRIDE)], axis=1)
                h0 = jnp.dot(lhs, wpad_ref[kv * CMP_RATIO + 0], preferred_element_type=F32)
                h1 = jnp.dot(lhs, wpad_ref[kv * CMP_RATIO + 1], preferred_element_type=F32)
                h = b1_ref[kv:kv + 1, :] + h0 + pltpu.roll(h1, n_chunk - 1, axis=0)
                a = (h * jax.nn.sigmoid(h)).astype(BF16)
                outs.append(jnp.dot(a, w2pad_ref[kv], preferred_element_type=F32) + b2_ref[kv:kv + 1, :])
            halves.append(jnp.concatenate(outs, axis=1))
        kc = _group_rms(halves[0], bd_ref[...], kg_ref[...], HEAD_DIM)
        vc = halves[1]
        if packed:
            pk = _pack_kv(kc, vc)
            for g in range(ATT_GROUPS):
                out_ref[0, g] = pk[:, g * LANES:(g + 1) * LANES]
        else:
            out_ref[0, :, :KV_WIDTH] = kc.astype(BF16)
            out_ref[0, :, KV_WIDTH:] = vc.astype(BF16)


def _compress(rows, page_table, cw, k_gain0, *, paged):
    wpad, b1, w2pad, b2 = cw
    n_chunk = CMP_UNITS * CMP_UNIT_ROWS // CMP_STRIDE
    i = np.arange(256)
    perm = np.zeros((256, 256), np.float32)
    perm[(i % 16) * 16 + i // 16, i] = 1.0
    consts = (jnp.asarray(perm, BF16), wpad, b1, w2pad, b2, k_gain0.reshape(1, KV_WIDTH),
              jnp.asarray(_block_diag_ones(KV_WIDTH, HEAD_DIM), BF16))
    if paged:
        n_grp = page_table.shape[0]
        pages_per_unit = CMP_UNIT_ROWS // PAGE_SIZE
        n_in = pages_per_unit
        const = lambda a: pl.BlockSpec(a.shape, lambda b, u, pt, nd=a.ndim: (0,) * nd)
        in_specs = [pl.BlockSpec((1, PAGE_SIZE, KV_ROW), lambda b, u, pt, k=k: (pt[b, u * pages_per_unit + k], 0, 0))
                    for k in range(n_in)]
        out_spec = pl.BlockSpec((1, n_chunk, KV_ROW), lambda b, u, pt: (b, 0, 0))
        out_shape = jax.ShapeDtypeStruct((n_grp, n_chunk, KV_ROW), BF16)
        args = (page_table,) + (rows,) * n_in + consts
        nsp = 1
    else:
        n_grp, n_in = 1, 1
        const = lambda a: pl.BlockSpec(a.shape, lambda b, u, nd=a.ndim: (0,) * nd)
        in_specs = [pl.BlockSpec((CMP_UNIT_ROWS, KV_ROW), lambda b, u: (u, 0))]
        out_spec = pl.BlockSpec((1, ATT_GROUPS, n_chunk, LANES), lambda b, u: (0, 0, 0, 0))
        out_shape = jax.ShapeDtypeStruct((1, ATT_GROUPS, n_chunk, LANES), BF16)
        args = (rows,) + consts
        nsp = 0
    grid_spec = pltpu.PrefetchScalarGridSpec(
        num_scalar_prefetch=nsp, grid=(n_grp, CMP_UNITS),
        in_specs=in_specs + [const(a) for a in consts],
        out_specs=out_spec,
        scratch_shapes=[pltpu.VMEM((CMP_STRIDE, n_chunk, KV_ROW), BF16)])
    return pl.pallas_call(
        functools.partial(_compress_kernel, n_in=n_in, paged=paged, packed=not paged),
        grid_spec=grid_spec, out_shape=out_shape,
        compiler_params=_cparams("arbitrary", "arbitrary"),
        name="compress_paged" if paged else "compress",
    )(*args)


def _block_overlap(n_cmp, n_blk):
    cs = np.arange(n_cmp)[:, None] * CMP_STRIDE
    bs = np.arange(n_blk)[None, :] * SEL_BLOCK
    ov = np.minimum(cs + CMP_BLOCK, bs + SEL_BLOCK) - np.maximum(cs, bs)
    return (np.clip(ov, 0, None) / CMP_BLOCK).astype(np.float32)


def _overlap_padded(n_cmp, n_blk, rows, cols):
    ov = np.zeros((rows, cols), np.float32)
    ov[:n_cmp, :n_blk] = _block_overlap(n_cmp, n_blk)
    return jnp.asarray(ov, BF16)


def _softmax_rows(s, mask):
    s = jnp.where(mask, s, NEG)
    m = jnp.max(s, axis=1, keepdims=True)
    e = jnp.where(mask, jnp.exp(s - m), 0.0)
    d = jnp.sum(e, axis=1, keepdims=True)
    return e / jnp.where(d > 0, d, 1.0)


def _select_blocks(imp, qpos, n_blk):
    blk = lax.broadcasted_iota(jnp.int32, imp.shape, 1)
    cur = qpos >> SEL_SHIFT
    forced = (blk == 0) | (blk == cur) | (blk == cur - 1)
    valid = blk * SEL_BLOCK <= qpos
    score = jnp.where(forced, ATT_HPG + 1.0, jnp.where(valid, imp, -1.0))
    score = jnp.where(blk < n_blk, score, NEG)
    blk_f = blk.astype(F32)
    sel = jnp.zeros(imp.shape, F32)
    for _ in range(min(N_SEL, n_blk)):
        m = jnp.max(score, axis=1, keepdims=True)
        first = jnp.min(jnp.where(score == m, blk_f, 1e9), axis=1, keepdims=True)
        hit = blk_f == first
        sel = jnp.where(hit, 1.0, sel)
        score = jnp.where(hit, NEG, score)
    return sel


def _online_update(s, mask, v, m_ref, l_ref, acc_ref, idx=None):
    at = (lambda r: r) if idx is None else (lambda r: r.at[idx])
    m_r, l_r, acc_r = at(m_ref), at(l_ref), at(acc_ref)
    s = jnp.where(mask, s, NEG)
    m_prev = m_r[...]
    m_new = jnp.maximum(m_prev, jnp.max(s, axis=1, keepdims=True))
    alpha = jnp.exp(m_prev - m_new)
    e = jnp.where(mask, jnp.exp(s - m_new), 0.0)
    l_r[...] = alpha * l_r[...] + jnp.sum(e, axis=1, keepdims=True)
    pv = jnp.dot(e.astype(BF16), v, preferred_element_type=F32)
    reps = acc_r.shape[-1] // LANES
    a_w = alpha if reps == 1 else jnp.concatenate([alpha] * reps, axis=1)
    acc_r[...] = a_w * acc_r[...] + pv
    m_r[...] = m_new


SEL_SHIFT = SEL_BLOCK.bit_length() - 1
NT_DIMS = (((1,), (1,)), ((), ()))


def _nsa_prompt_kernel(q_ref, gate_ref, kvc_ref, slc_ref, win_ref, bcmp_ref, toep_ref, ov_ref, o_ref,
                       m_scr, l_scr, acc_scr):
    i = pl.program_id(2)
    tile = (Q_BLOCK, LANES)
    lane = lax.broadcasted_iota(jnp.int32, tile, 1)
    sub = lax.broadcasted_iota(jnp.int32, tile, 0)
    low = lane < HEAD_DIM
    qpos = i * Q_BLOCK + sub

    qa = q_ref[...] * (HEAD_DIM ** -0.5)
    qp = []
    for pair in range(ATT_HPG // 2):
        x = qa[:, pair * LANES:(pair + 1) * LANES]
        qp.append(jnp.where(low, x, 0.0).astype(BF16))
        qp.append(jnp.where(low, pltpu.roll(x, HEAD_DIM, axis=1), 0.0).astype(BF16))

    kvc = kvc_ref[0, 0]
    cmask = (lane * CMP_STRIDE + (CMP_BLOCK - 1)) <= qpos
    psum = jnp.zeros(tile, F32)
    o_cmp = []
    for p in range(ATT_HPG):
        s = lax.dot_general(qp[p], kvc, NT_DIMS, preferred_element_type=F32) + bcmp_ref[p]
        pr = _softmax_rows(s, cmask)
        o_cmp.append(jnp.dot(pr.astype(BF16), kvc, preferred_element_type=F32))
        psum = psum + pr
    hi, lo = _split2(psum)
    ov = ov_ref[...]
    imp = jnp.dot(hi, ov, preferred_element_type=F32) + jnp.dot(lo, ov, preferred_element_type=F32)
    sel = _select_blocks(imp, qpos, SEQ // SEL_BLOCK).astype(BF16)

    def reset():
        m_scr[...] = jnp.full(m_scr.shape, NEG, F32)
        l_scr[...] = jnp.zeros(l_scr.shape, F32)
        acc_scr[...] = jnp.zeros(acc_scr.shape, F32)

    def finish():
        return [acc_scr[p] / jnp.where(l_scr[p] > 0, l_scr[p], 1.0) for p in range(ATT_HPG)]

    def key_tile(j, kv_ref, use_sel):
        kv = kv_ref[pl.ds(pl.multiple_of(j * LANES, LANES), LANES), :]
        delta = i - j
        dist = delta * LANES + sub - lane
        if use_sel:
            expand = jnp.where(sub == 2 * j + (lane >> SEL_SHIFT), 1.0, 0.0).astype(BF16)
            picked = jnp.dot(sel, expand, preferred_element_type=F32) > 0.5
            mask = picked & (dist >= 0)
        else:
            mask = (dist >= 0) & (dist < WINDOW)
        for p in range(ATT_HPG):
            s = lax.dot_general(qp[p], kv, NT_DIMS, preferred_element_type=F32)
            s = s + toep_ref[p, pl.ds(pl.multiple_of(delta * LANES, LANES), LANES), :]
            _online_update(s, mask, kv, m_scr, l_scr, acc_scr, idx=p)

    reset()

    def slc_body(j, carry):
        key_tile(j, slc_ref, True)
        return carry

    lax.fori_loop(0, i + 1, slc_body, 0)
    o_slc = finish()

    reset()

    def win_body(j, carry):
        key_tile(j, win_ref, False)
        return carry

    lax.fori_loop(jnp.maximum(i - WINDOW // LANES, 0), i + 1, win_body, 0)
    o_win = finish()

    g = gate_ref[...]
    comb = []
    for p in range(ATT_HPG):
        col = lambda br: jnp.broadcast_to(g[:, br * ATT_HPG + p:br * ATT_HPG + p + 1], tile)
        comb.append(col(0) * o_cmp[p] + col(1) * o_slc[p] + col(2) * o_win[p])
    for pair in range(ATT_HPG // 2):
        o_ref[:, pair * LANES:(pair + 1) * LANES] = jnp.where(
            low, pltpu.roll(comb[2 * pair], HEAD_DIM, axis=1), comb[2 * pair + 1])


def _nsa_prompt(q, gates, kvc, slc_pack, win_pack, bias_cmp, bias_toep, n_batch=BATCH):
    nqb = SEQ // Q_BLOCK
    ov = _overlap_padded(SEQ // CMP_STRIDE - 1, SEQ // SEL_BLOCK, LANES, LANES)
    gw = ATT_HPG * HEAD_DIM
    return pl.pallas_call(
        _nsa_prompt_kernel,
        grid=(n_batch, ATT_GROUPS, nqb),
        in_specs=[
            pl.BlockSpec((Q_BLOCK, gw), lambda b, g, i: (b * nqb + i, g)),
            pl.BlockSpec((Q_BLOCK, LANES), lambda b, g, i: (b * nqb + i, g)),
            pl.BlockSpec((1, 1, SEQ // CMP_STRIDE, LANES), lambda b, g, i: (0, g, b, 0)),
            pl.BlockSpec((SEQ, LANES), lambda b, g, i: (b, g)),
            pl.BlockSpec((SEQ, LANES), lambda b, g, i: (b, g)),
            pl.BlockSpec((ATT_HPG, Q_BLOCK, LANES), lambda b, g, i: (g, i, 0)),
            pl.BlockSpec((ATT_HPG, SEQ, LANES), lambda b, g, i: (g, 0, 0)),
            pl.BlockSpec((LANES, LANES), lambda b, g, i: (0, 0)),
        ],
        out_specs=pl.BlockSpec((Q_BLOCK, gw), lambda b, g, i: (b * nqb + i, g)),
        out_shape=jax.ShapeDtypeStruct((n_batch * SEQ, ATT_WIDTH), F32),
        scratch_shapes=[pltpu.VMEM((ATT_HPG, Q_BLOCK, LANES), F32)] * 3,
        compiler_params=_cparams("arbitrary", "arbitrary", "arbitrary"),
        name="nsa_prompt",
    )(q, gates, kvc, slc_pack, win_pack, bias_cmp, bias_toep, ov)


SAMPLE_ROWS = ATT_HEADS * DEC_SEQ
SAMPLE_PAGES_PER_STEP = 16
SAMPLE_STEPS = N_PAGES // SAMPLE_PAGES_PER_STEP
SAMPLE_N_CMP = (PAST_LEN + DEC_SEQ - CMP_BLOCK) // CMP_STRIDE + 1
SAMPLE_N_BLK = -(-(PAST_LEN + DEC_SEQ) // SEL_BLOCK)
SAMPLE_BLK_LANES = 2 * LANES


def _nsa_sample_kernel(pt_ref, q_ref, gate_ref, kvc_ref, *rest):
    page_refs = rest[:SAMPLE_PAGES_PER_STEP]
    (slc_new_ref, win_state_ref, win_new_ref, bcmp_ref, bslc_ref, bwin_ref, perm_ref, ov_ref,
     o_ref, qbd_scr, sel_scr, m_scr, l_scr, acc_scr, ocmp_scr, owin_scr) = rest[SAMPLE_PAGES_PER_STEP:]
    del pt_ref
    u = pl.program_id(1)
    tile = (SAMPLE_ROWS, LANES)
    wide = (SAMPLE_ROWS, KV_WIDTH)
    lane = lax.broadcasted_iota(jnp.int32, tile, 1)
    t_row = lax.broadcasted_iota(jnp.int32, tile, 0) & (DEC_SEQ - 1)
    rows_per_group = ATT_HPG * DEC_SEQ
    own = ((lax.broadcasted_iota(jnp.int32, wide, 1) >> (HEAD_DIM.bit_length() - 1))
           == (lax.broadcasted_iota(jnp.int32, wide, 0) >> (rows_per_group.bit_length() - 1)))

    def reset():
        m_scr[...] = jnp.full(m_scr.shape, NEG, F32)
        l_scr[...] = jnp.zeros(l_scr.shape, F32)
        acc_scr[...] = jnp.zeros(acc_scr.shape, F32)

    def finish():
        l = l_scr[...]
        inv = 1.0 / jnp.where(l > 0, l, 1.0)
        return jnp.concatenate([inv, inv], axis=1) * acc_scr[...]

    def key_tile(rows, bias, mask):
        s = lax.dot_general(qbd_scr[...], rows[:, :KV_WIDTH].astype(BF16), NT_DIMS, preferred_element_type=F32) + bias
        _online_update(s, mask, rows[:, KV_WIDTH:].astype(BF16), m_scr, l_scr, acc_scr)

    def new_rows(ref):
        return jnp.concatenate([ref[...], jnp.zeros((LANES - DEC_SEQ, KV_ROW), F32)], axis=0)

    @pl.when(u == 0)
    def _():
        q = (q_ref[...] * (HEAD_DIM ** -0.5)).astype(BF16)
        qperm = jnp.dot(q, perm_ref[...], preferred_element_type=F32)
        qfull = jnp.concatenate([qperm[:, p * KV_WIDTH:(p + 1) * KV_WIDTH]
                                 for g in range(ATT_GROUPS) for p in range(ATT_HPG)], axis=0)
        qbd = jnp.where(own, qfull, 0.0).astype(BF16)
        qbd_scr[...] = qbd

        kvc = kvc_ref[0]
        s = lax.dot_general(qbd, kvc[:, :KV_WIDTH], NT_DIMS, preferred_element_type=F32) + bcmp_ref[...]
        cmask = lax.broadcasted_iota(jnp.int32, s.shape, 1) < SAMPLE_N_CMP
        pr = _softmax_rows(s, cmask)
        ocmp_scr[...] = jnp.dot(pr.astype(BF16), kvc[:, KV_WIDTH:], preferred_element_type=F32)
        ps = []
        for g in range(ATT_GROUPS):
            r0 = g * rows_per_group
            ps.append(sum(pr[r0 + p * DEC_SEQ:r0 + (p + 1) * DEC_SEQ, :] for p in range(ATT_HPG)))
        hi, lo = _split2(jnp.concatenate(ps, axis=0))
        ov = ov_ref[...]
        imp = jnp.dot(hi, ov, preferred_element_type=F32) + jnp.dot(lo, ov, preferred_element_type=F32)
        qpos = PAST_LEN + (lax.broadcasted_iota(jnp.int32, imp.shape, 0) & (DEC_SEQ - 1))
        sel = _select_blocks(imp, qpos, SAMPLE_N_BLK)
        sel_scr[...] = jnp.concatenate([sel[g * DEC_SEQ:(g + 1) * DEC_SEQ, :]
                                        for g in range(ATT_GROUPS) for p in range(ATT_HPG)], axis=0).astype(BF16)

        reset()
        for w in range(WINDOW // LANES):
            rows = win_state_ref[0, w * LANES:(w + 1) * LANES, :]
            key_tile(rows, bwin_ref[:, w * LANES:(w + 1) * LANES], (w * LANES + lane) > t_row)
        key_tile(new_rows(win_new_ref), bwin_ref[:, WINDOW:WINDOW + LANES], lane <= t_row)
        owin_scr[...] = finish()
        reset()

    blk_of_key = lax.broadcasted_iota(jnp.int32, (SAMPLE_BLK_LANES, LANES), 1) >> SEL_SHIFT
    blk_row = lax.broadcasted_iota(jnp.int32, (SAMPLE_BLK_LANES, LANES), 0)
    for k in range(SAMPLE_PAGES_PER_STEP):
        pg = u * SAMPLE_PAGES_PER_STEP + k
        expand = jnp.where(blk_row == 2 * pg + blk_of_key, 1.0, 0.0).astype(BF16)
        picked = jnp.dot(sel_scr[...], expand, preferred_element_type=F32) > 0.5
        bias = bslc_ref[:, pl.ds(pl.multiple_of(pg * LANES, LANES), LANES)]
        key_tile(page_refs[k][0], bias, picked)

    @pl.when(u == SAMPLE_STEPS - 1)
    def _():
        key_tile(new_rows(slc_new_ref), bslc_ref[:, PAST_LEN:PAST_LEN + LANES], lane <= t_row)
        o_slc = finish()
        gt = gate_ref[...]

        def gate_rows(br):
            cols = []
            for g in range(ATT_GROUPS):
                for p in range(ATT_HPG):
                    c = g * LANES + br * ATT_HPG + p
                    cols.append(jnp.broadcast_to(gt[:, c:c + 1], (DEC_SEQ, KV_WIDTH)))
            return jnp.concatenate(cols, axis=0)

        comb = gate_rows(0) * ocmp_scr[...] + gate_rows(1) * o_slc + gate_rows(2) * owin_scr[...]
        comb = jnp.where(own, comb, 0.0)
        per_head = []
        for p in range(ATT_HPG):
            per_head.append(sum(comb[(g * ATT_HPG + p) * DEC_SEQ:(g * ATT_HPG + p + 1) * DEC_SEQ, :]
                                for g in range(ATT_GROUPS)))
        hi, lo = _split2(jnp.concatenate(per_head, axis=1))
        perm = perm_ref[...]
        o_ref[...] = (lax.dot_general(hi, perm, NT_DIMS, preferred_element_type=F32)
                      + lax.dot_general(lo, perm, NT_DIMS, preferred_element_type=F32))


def _nsa_sample(q, gates, kvc, cache_slc, page_table, slc_rows, state_win, win_rows, bias_cmp, bias_slc, bias_win):
    n_seq = page_table.shape[0]
    row0 = N_PROMPT // DEC_SEQ
    src = np.arange(ATT_WIDTH)
    g, p, d = src // (ATT_HPG * HEAD_DIM), (src // HEAD_DIM) % ATT_HPG, src % HEAD_DIM
    perm = np.zeros((ATT_WIDTH, ATT_WIDTH), np.float32)
    perm[src, p * KV_WIDTH + g * HEAD_DIM + d] = 1.0
    ov = _overlap_padded(SAMPLE_N_CMP, SAMPLE_N_BLK, PAST_LEN // CMP_STRIDE, SAMPLE_BLK_LANES)
    tok = lambda w: pl.BlockSpec((DEC_SEQ, w), lambda b, u, pt: (row0 + b, 0))
    const = lambda a: pl.BlockSpec(a.shape, lambda b, u, pt, nd=a.ndim: (0,) * nd)
    per_seq = lambda s: pl.BlockSpec((1,) + s, lambda b, u, pt: (b, 0, 0))
    pages = [pl.BlockSpec((1, PAGE_SIZE, KV_ROW),
                          lambda b, u, pt, k=k: (pt[b, u * SAMPLE_PAGES_PER_STEP + k], 0, 0))
             for k in range(SAMPLE_PAGES_PER_STEP)]
    consts = (bias_cmp, bias_slc, bias_win, jnp.asarray(perm, BF16), ov)
    grid_spec = pltpu.PrefetchScalarGridSpec(
        num_scalar_prefetch=1, grid=(n_seq, SAMPLE_STEPS),
        in_specs=[tok(ATT_WIDTH), tok(ATT_GROUPS * LANES), per_seq((PAST_LEN // CMP_STRIDE, KV_ROW))] + pages
        + [tok(KV_ROW), per_seq((WINDOW, KV_ROW)), tok(KV_ROW)] + [const(a) for a in consts],
        out_specs=pl.BlockSpec((DEC_SEQ, ATT_WIDTH), lambda b, u, pt: (b, 0)),
        scratch_shapes=[pltpu.VMEM((SAMPLE_ROWS, KV_WIDTH), BF16), pltpu.VMEM((SAMPLE_ROWS, SAMPLE_BLK_LANES), BF16),
                        pltpu.VMEM((SAMPLE_ROWS, LANES), F32), pltpu.VMEM((SAMPLE_ROWS, LANES), F32),
                        pltpu.VMEM((SAMPLE_ROWS, KV_WIDTH), F32), pltpu.VMEM((SAMPLE_ROWS, KV_WIDTH), F32),
                        pltpu.VMEM((SAMPLE_ROWS, KV_WIDTH), F32)])
    return pl.pallas_call(
        _nsa_sample_kernel, grid_spec=grid_spec,
        out_shape=jax.ShapeDtypeStruct((n_seq * DEC_SEQ, ATT_WIDTH), F32),
        compiler_params=_cparams("arbitrary", "arbitrary"),
        name="nsa_sample",
    )(page_table, q, gates, kvc, *([cache_slc] * SAMPLE_PAGES_PER_STEP), slc_rows, state_win, win_rows, *consts)


def kernel(x_prompt, x_sample, cache_cmp_kv, cache_slc_kv, state_win_kv, state_hgrn, page_table,
           rel_bias_table, hgrn_lower_bound, norm_ffn1, w_ffn1_gate_up, w_ffn1_down, norm_mix,
           w_in, q_norm, k_norm, w_cmp1, b_cmp1, w_cmp2, b_cmp2, attn_out_norm, hgrn_out_norm,
           w_out, norm_ffn2, w_ffn2_gate_up, w_ffn2_down):
    assert DEPTH == 1
    l = 0
    n_phys = cache_cmp_kv.shape[1]
    kv_shape = (2, ATT_GROUPS, HEAD_DIM)
    x_all = jnp.concatenate([x_prompt.reshape(N_PROMPT, D_MODEL), x_sample.reshape(N_SAMPLE, D_MODEL)], axis=0)

    y1 = _ffn(x_all, norm_ffn1[l], w_ffn1_gate_up[l].astype(BF16), w_ffn1_down[l].astype(BF16))

    (q, cmp_rows, slc_rows, slc_pack, win_rows, win_pack, hq, lf, hk, hv, hog, gates) = _project_all(
        y1, norm_mix[l], _permute_w_in(w_in[l]), q_norm[l], k_norm[l], hgrn_lower_bound)

    tbl = rel_bias_table.astype(F32)
    first_end = CMP_BLOCK - 1
    bias_cmp_p = _bias_table(tbl, SEQ, LANES, -first_end, 1, -CMP_STRIDE)
    bias_toep = _bias_table(tbl, SEQ, LANES, 0, 1, -1)
    bias_cmp_s = _bias_table(tbl, DEC_SEQ, PAST_LEN // CMP_STRIDE, PAST_LEN - first_end, 1, -CMP_STRIDE)
    bias_slc_s = _bias_table(tbl, DEC_SEQ, PAST_LEN + LANES, PAST_LEN, 1, -1)
    bias_win_s = _bias_table(tbl, DEC_SEQ, WINDOW + LANES, WINDOW, 1, -1)
    rows_ht = lambda a: a.reshape(SAMPLE_ROWS, a.shape[-1])

    cw = _compress_weights(w_cmp1[l], b_cmp1[l], w_cmp2[l], b_cmp2[l])
    kvc_p = _compress(cmp_rows, None, cw, k_norm[l][0], paged=False)
    kvc_s = _compress(cache_cmp_kv[l].reshape(n_phys, PAGE_SIZE, KV_ROW), page_table, cw, k_norm[l][0], paged=True)

    o_att_p = _nsa_prompt(q, gates, kvc_p, slc_pack, win_pack, bias_cmp_p, bias_toep)
    o_att_s = _nsa_sample(q, gates, kvc_s, cache_slc_kv[l].reshape(n_phys, PAGE_SIZE, KV_ROW), page_table, slc_rows,
                          state_win_kv[l].reshape(DEC_BATCH, WINDOW, KV_ROW), win_rows,
                          rows_ht(bias_cmp_s), rows_ht(bias_slc_s), rows_ht(bias_win_s))

    o_hg_p, hg_p = _hgrn(hq, hk, hv, lf, None, n_seq=BATCH, t_len=SEQ, ch=HG_CHUNK_PROMPT, row0=0)
    o_hg_s, hg_s = _hgrn(hq, hk, hv, lf, state_hgrn[l].astype(F32), n_seq=DEC_BATCH, t_len=DEC_SEQ,
                         ch=HG_CHUNK_SAMPLE, row0=N_PROMPT)

    y2 = _mixer_out(y1, jnp.concatenate([o_att_p, o_att_s], axis=0), jnp.concatenate([o_hg_p, o_hg_s], axis=0),
                    hog, attn_out_norm[l], hgrn_out_norm[l], w_out[l].astype(BF16))
    y3 = _ffn(y2, norm_ffn2[l], w_ffn2_gate_up[l].astype(BF16), w_ffn2_down[l].astype(BF16))

    prompt_rows = lambda a: a[:N_PROMPT].reshape((1, BATCH, SEQ) + kv_shape)
    sample_rows = lambda a: a[N_PROMPT:].reshape((1, DEC_BATCH, DEC_SEQ) + kv_shape)
    win_p = prompt_rows(win_rows)[:, :, SEQ - min(WINDOW, SEQ):]
    win_s = jnp.concatenate([state_win_kv[l][:, DEC_SEQ:], sample_rows(win_rows)[0]], axis=1)[None]
    return (y3[:N_PROMPT].reshape(BATCH, SEQ, D_MODEL), y3[N_PROMPT:].reshape(DEC_BATCH, DEC_SEQ, D_MODEL),
            prompt_rows(cmp_rows), prompt_rows(slc_rows), win_p, hg_p[None],
            sample_rows(cmp_rows), sample_rows(slc_rows), win_s, hg_s[None].astype(state_hgrn.dtype))
```

```python
import functools
import math

import jax
import jax.numpy as jnp
import numpy as np
from jax import lax
from jax.experimental import pallas as pl
from jax.experimental.pallas import tpu as pltpu

D_MODEL = 2048
BATCH = 4
SEQ = 2048
DEPTH = 1
DEC_BATCH = 32
DEC_SEQ = 8
PAST_LEN = 8192
PAGE_SIZE = 128
HEAD_DIM = 64
ATT_HEADS = (D_MODEL // 2) // HEAD_DIM
ATT_GROUPS = ATT_HEADS // 4
ATT_HPG = ATT_HEADS // ATT_GROUPS
ATT_WIDTH = ATT_HEADS * HEAD_DIM
KV_WIDTH = ATT_GROUPS * HEAD_DIM
N_BRANCH = 3
CMP_BLOCK = 32
CMP_STRIDE = 16
CMP_RATIO = CMP_BLOCK // CMP_STRIDE
CMP_HIDDEN = 256
SEL_BLOCK = 64
N_SEL = 8
WINDOW = 512
Q_BLOCK = 128
HG_KEY = 128
HG_VAL = 128
HG_HEADS = (D_MODEL // 2) // HG_VAL
HG_WIDTH = HG_HEADS * HG_VAL
MIX_WIDTH = ATT_WIDTH + HG_WIDTH
D_FF = 256 * ((8 * D_MODEL // 3 + 255) // 256)
N_BUCKETS = 32
MAX_DISTANCE = 1024
EPS = 1e-6
IN_SPLITS = (ATT_WIDTH, 2 * N_BRANCH * KV_WIDTH, N_BRANCH * ATT_HEADS,
             HG_HEADS * HG_KEY, HG_HEADS * HG_KEY, HG_WIDTH, HG_WIDTH)
D_IN = sum(IN_SPLITS)

N_PROMPT = BATCH * SEQ
N_SAMPLE = DEC_BATCH * DEC_SEQ
N_TOKENS = N_PROMPT + N_SAMPLE
N_PAGES = PAST_LEN // PAGE_SIZE
KV_ROW = 2 * KV_WIDTH

LANES = 128
V7X_VMEM_BYTES = 64 * 1024 * 1024
VMEM_LIMIT_BYTES = 56 * 1024 * 1024

NEG = -1e30
F32 = jnp.float32
BF16 = jnp.bfloat16

FFN_ROW_TILE = 768
FFN_FF_TILE = 512
PROJ_ROW_TILE = 384
PROJ_COL_TILE = 512
OUT_ROW_TILE = 256
HG_CHUNK_PROMPT = 16
HG_CHUNK_SAMPLE = DEC_SEQ
CMP_UNIT_ROWS = 2048
CMP_UNITS = 4


def _cparams(*sem):
    return pltpu.CompilerParams(dimension_semantics=sem, vmem_limit_bytes=VMEM_LIMIT_BYTES)


def _t5_thresholds():
    n = np.arange(0, 2 * MAX_DISTANCE + 2)
    exact = N_BUCKETS // 2
    logn = np.log(np.maximum(n, 1).astype(np.float64) / exact)
    large = exact + (logn / math.log(MAX_DISTANCE / exact) * (N_BUCKETS - exact)).astype(np.int32)
    b = np.where(n < exact, n, np.minimum(large, N_BUCKETS - 1))
    return [int(n[b >= k][0]) for k in range(N_BUCKETS)]


T5_THRESHOLDS = _t5_thresholds()


def _block_diag_ones(n, blk):
    i = np.arange(n)
    return (i[:, None] // blk == i[None, :] // blk).astype(np.float32)


def _split2(x):
    hi = x.astype(BF16)
    lo = (x - hi.astype(F32)).astype(BF16)
    return hi, lo


def _group_rms(t, bd, gain, width):
    hi, lo = _split2(t * t)
    ss = (jnp.dot(hi, bd, preferred_element_type=F32) + jnp.dot(lo, bd, preferred_element_type=F32))
    return t * lax.rsqrt(ss * (1.0 / width) + EPS) * gain


def _ffn_kernel(x_ref, gain_ref, wg_ref, wu_ref, wd_ref, o_ref, xn_ref, acc_ref):
    j = pl.program_id(1)

    @pl.when(j == 0)
    def _():
        x = x_ref[...]
        y = x * lax.rsqrt(jnp.mean(x * x, axis=-1, keepdims=True) + EPS)
        xn_ref[...] = (y * gain_ref[...]).astype(BF16)
        acc_ref[...] = jnp.zeros_like(acc_ref)

    xn = xn_ref[...]
    g = jnp.dot(xn, wg_ref[...], preferred_element_type=F32)
    u = jnp.dot(xn, wu_ref[...], preferred_element_type=F32)
    a = (g * jax.nn.sigmoid(g) * u).astype(BF16)
    acc_ref[...] += jnp.dot(a, wd_ref[...], preferred_element_type=F32)

    @pl.when(j == pl.num_programs(1) - 1)
    def _():
        o_ref[...] = x_ref[...] + 0.5 * acc_ref[...]


def _ffn(x, gain, w_gu, w_down):
    n, d = x.shape
    tm, tf = FFN_ROW_TILE, FFN_FF_TILE
    assert n % tm == 0 and D_FF % tf == 0
    nj = D_FF // tf
    return pl.pallas_call(
        _ffn_kernel,
        grid=(n // tm, nj),
        in_specs=[
            pl.BlockSpec((tm, d), lambda i, j: (i, 0)),
            pl.BlockSpec((1, d), lambda i, j: (0, 0)),
            pl.BlockSpec((d, tf), lambda i, j: (0, j)),
            pl.BlockSpec((d, tf), lambda i, j: (0, j + nj)),
            pl.BlockSpec((tf, d), lambda i, j: (j, 0)),
        ],
        out_specs=pl.BlockSpec((tm, d), lambda i, j: (i, 0)),
        out_shape=jax.ShapeDtypeStruct((n, d), F32),
        scratch_shapes=[pltpu.VMEM((tm, d), BF16), pltpu.VMEM((tm, d), F32)],
        compiler_params=_cparams("arbitrary", "arbitrary"),
        name="ffn",
    )(x, gain.reshape(1, d), w_gu, w_gu, w_down)


PB_Q, PB_CMP, PB_SLC, PB_WIN, PB_HQ, PB_HF, PB_HI, PB_HG, PB_GATE = 0, 2, 3, 4, 5, 7, 9, 11, 13
PROJ_N_BLOCKS = 14


def _permute_w_in(w_in):
    p = [int(v) for v in np.cumsum(IN_SPLITS)]
    a_g = w_in[:, p[1]:p[2]]
    src = np.zeros((PROJ_COL_TILE,), np.int32)
    valid = np.zeros((PROJ_COL_TILE,), bool)
    for g in range(ATT_GROUPS):
        for br in range(N_BRANCH):
            for hp in range(ATT_HPG):
                src[g * LANES + br * ATT_HPG + hp] = br * ATT_HEADS + g * ATT_HPG + hp
                valid[g * LANES + br * ATT_HPG + hp] = True
    gate = jnp.where(jnp.asarray(valid)[None, :], a_g[:, src], 0.0)
    return jnp.concatenate([w_in[:, :p[1]], w_in[:, p[2]:], gate], axis=1).astype(BF16)


def _pack_kv(k, v):
    parts = []
    for g in range(ATT_GROUPS):
        parts.append(k[:, g * HEAD_DIM:(g + 1) * HEAD_DIM])
        parts.append(v[:, g * HEAD_DIM:(g + 1) * HEAD_DIM])
    return jnp.concatenate(parts, axis=1).astype(BF16)


def _proj_kernel(x_ref, gain_ref, w_ref, bd_ref, qg_ref, kg_ref, lbp_ref,
                 q_ref, cmp_ref, slc_ref, slcp_ref, win_ref, winp_ref,
                 hq_ref, lf_ref, hk_ref, hv_ref, hog_ref, gate_ref, xn_ref):
    c = pl.program_id(1)

    @pl.when(c == 0)
    def _():
        x = x_ref[...]
        y = x * lax.rsqrt(jnp.mean(x * x, axis=-1, keepdims=True) + EPS)
        xn_ref[...] = (y * gain_ref[...]).astype(BF16)

    acc = jnp.dot(xn_ref[...], w_ref[...], preferred_element_type=F32)
    half = KV_WIDTH

    @pl.when(c < PB_CMP)
    def _():
        q_ref[...] = _group_rms(acc, bd_ref[...], qg_ref[...], HEAD_DIM)

    @pl.when(c == PB_CMP)
    def _():
        cmp_ref[...] = acc

    def kv_branch(rows_ref, pack_ref, br):
        k = _group_rms(acc[:, :half], bd_ref[:half, :half], kg_ref[br - 1:br, :], HEAD_DIM)
        v = acc[:, half:]
        rows_ref[:, :half] = k
        rows_ref[:, half:] = v
        pack_ref[...] = _pack_kv(k, v)

    @pl.when(c == PB_SLC)
    def _():
        kv_branch(slc_ref, slcp_ref, 1)

    @pl.when(c == PB_WIN)
    def _():
        kv_branch(win_ref, winp_ref, 2)

    @pl.when((c >= PB_HQ) & (c < PB_HF))
    def _():
        hq_ref[...] = acc

    @pl.when((c >= PB_HF) & (c < PB_HI))
    def _():
        p = lbp_ref[...]
        e = jnp.exp(p - jnp.max(p, axis=0, keepdims=True))
        lb = e[0:1, :] / jnp.sum(e, axis=0, keepdims=True)
        lf_ref[...] = jnp.log(lb + (1.0 - lb) * jax.nn.sigmoid(acc))
        hk_ref[...] = (1.0 - lb) * jax.nn.sigmoid(-acc)

    @pl.when((c >= PB_HI) & (c < PB_HG))
    def _():
        hv_ref[...] = acc

    @pl.when((c >= PB_HG) & (c < PB_GATE))
    def _():
        hog_ref[...] = acc

    @pl.when(c == PB_GATE)
    def _():
        gate_ref[...] = jax.nn.sigmoid(acc)


def _project_all(y, gain, w_perm, q_gain, k_gain, lb_logits):
    n, d = y.shape
    tm, tc = PROJ_ROW_TILE, PROJ_COL_TILE
    assert n % tm == 0 and DEPTH == 1
    bd = jnp.asarray(_block_diag_ones(tc, HEAD_DIM), BF16)

    def two(first):
        return lambda i, c: (i, jnp.clip(c - first, 0, 1))

    one = lambda i, c: (i, 0)
    wide = lambda dt: jax.ShapeDtypeStruct((n, 2 * tc), dt)
    narrow = lambda dt: jax.ShapeDtypeStruct((n, tc), dt)
    out_shape = (wide(F32), narrow(F32), narrow(F32), narrow(BF16), narrow(F32), narrow(BF16),
                 wide(F32), wide(F32), wide(F32), wide(F32), wide(F32), narrow(F32))
    blk = lambda f: pl.BlockSpec((tm, tc), f)
    out_specs = (blk(two(PB_Q)), blk(one), blk(one), blk(one), blk(one), blk(one),
                 blk(two(PB_HQ)), blk(two(PB_HF)), blk(two(PB_HF)), blk(two(PB_HI)), blk(two(PB_HG)), blk(one))
    return pl.pallas_call(
        _proj_kernel,
        grid=(n // tm, PROJ_N_BLOCKS),
        in_specs=[
            pl.BlockSpec((tm, d), lambda i, c: (i, 0)),
            pl.BlockSpec((1, d), lambda i, c: (0, 0)),
            pl.BlockSpec((d, tc), lambda i, c: (0, c)),
            pl.BlockSpec((tc, tc), lambda i, c: (0, 0)),
            pl.BlockSpec((1, tc), lambda i, c: (0, jnp.clip(c, 0, 1))),
            pl.BlockSpec((2, KV_WIDTH), lambda i, c: (0, 0)),
            pl.BlockSpec((DEPTH + 1, tc), lambda i, c: (0, jnp.clip(c - PB_HF, 0, 1))),
        ],
        out_specs=out_specs,
        out_shape=out_shape,
        scratch_shapes=[pltpu.VMEM((tm, d), BF16)],
        compiler_params=_cparams("arbitrary", "arbitrary"),
        name="proj",
    )(y, gain.reshape(1, d), w_perm, bd, q_gain.reshape(1, ATT_WIDTH),
      k_gain[1:].reshape(2, KV_WIDTH), lb_logits)


def _hgrn_kernel(q_ref, k_ref, v_ref, lf_ref, tri_ref, ones_ref, s0_ref, o_ref, st_ref,
                 qe_scr, kd_scr, dec_scr, oi_scr, *, t_len, ch, has_state):
    nj = t_len // ch
    shape3 = (nj, ch, HG_KEY)
    q3, k3, v3, lf3 = q_ref[...], k_ref[...], v_ref[...], lf_ref[...]
    tl = lax.broadcasted_iota(jnp.int32, shape3, 1)

    def row(x3, s):
        return jnp.broadcast_to(x3[:, s:s + 1, :], shape3)

    if t_len >= 256:
        lf2 = lf3.reshape(t_len, HG_KEY)
        parts = []
        for r0 in range(0, t_len, 256):
            x = lf2[r0:r0 + 256]
            hi = x.astype(BF16)
            r1 = x - hi.astype(F32)
            mid = r1.astype(BF16)
            lo = (r1 - mid.astype(F32)).astype(BF16)
            tri = tri_ref[...]
            parts.append(jnp.dot(tri, hi, preferred_element_type=F32)
                         + jnp.dot(tri, mid, preferred_element_type=F32)
                         + jnp.dot(tri, lo, preferred_element_type=F32))
        b3 = jnp.concatenate(parts, axis=0).reshape(shape3)
    else:
        b3 = jnp.zeros(shape3, F32)
        for s in range(ch):
            b3 = b3 + jnp.where(tl >= s, row(lf3, s), 0.0)

    bl3 = row(b3, ch - 1)
    qe_scr[...] = (q3 * jnp.exp(b3)).astype(BF16)
    kd_scr[...] = (k3 * jnp.exp(bl3 - b3)).astype(BF16)
    dec_scr[...] = jnp.exp(b3[:, ch - 1:ch, :])

    ones = ones_ref[...]
    od = jnp.zeros(shape3, F32)
    for s in range(ch):
        w = q3 * row(k3, s) * jnp.exp(jnp.where(tl >= s, b3 - row(b3, s), NEG))
        a = jnp.dot(w.reshape(t_len, HG_KEY).astype(BF16), ones, preferred_element_type=F32)
        od = od + a.reshape(shape3) * row(v3, s)
    o_ref[...] = od

    if has_state:
        st0 = s0_ref[0, 0].T
    else:
        st0 = jnp.zeros((HG_VAL, HG_KEY), F32)

    def body(j, st):
        oi_scr[j] = lax.dot_general(qe_scr[j], st.astype(BF16), (((1,), (1,)), ((), ())),
                                    preferred_element_type=F32)
        ut = lax.dot_general(v_ref[j].astype(BF16), kd_scr[j], (((0,), (0,)), ((), ())),
                             preferred_element_type=F32)
        return st * dec_scr[j] + ut

    st = lax.fori_loop(0, nj, body, st0, unroll=min(nj, 8))
    o_ref[...] = o_ref[...] + oi_scr[...]
    st_ref[0, 0] = st.T


def _hgrn(hq, hk, hv, lf, s0, *, n_seq, t_len, ch, row0):
    n = hq.shape[0]
    nj = t_len // ch
    assert row0 % t_len == 0 and t_len % ch == 0 and ch % 8 == 0
    blk0 = row0 // t_len
    r3 = lambda a: a.reshape(n // ch, ch, HG_WIDTH)
    has_state = s0 is not None
    if not has_state:
        s0 = jnp.zeros((1, 1, HG_KEY, HG_VAL), F32)
    tri_n = 256 if t_len >= 256 else 8
    i = np.arange(tri_n)
    tri = jnp.asarray(((i[:, None] // ch == i[None, :] // ch) & (i[:, None] >= i[None, :])).astype(np.float32), BF16)
    ones = jnp.ones((HG_KEY, HG_KEY), BF16)
    seq = pl.BlockSpec((nj, ch, HG_KEY), lambda b, h: (blk0 + b, 0, h))
    state_in = pl.BlockSpec((1, 1, HG_KEY, HG_VAL), (lambda b, h: (b, h, 0, 0)) if has_state else (lambda b, h: (0, 0, 0, 0)))
    o, st = pl.pallas_call(
        functools.partial(_hgrn_kernel, t_len=t_len, ch=ch, has_state=has_state),
        grid=(n_seq, HG_HEADS),
        in_specs=[seq, seq, seq, seq,
                  pl.BlockSpec((tri_n, tri_n), lambda b, h: (0, 0)),
                  pl.BlockSpec((HG_KEY, HG_KEY), lambda b, h: (0, 0)),
                  state_in],
        out_specs=(pl.BlockSpec((nj, ch, HG_VAL), lambda b, h: (b, 0, h)),
                   pl.BlockSpec((1, 1, HG_KEY, HG_VAL), lambda b, h: (b, h, 0, 0))),
        out_shape=(jax.ShapeDtypeStruct((n_seq * nj, ch, HG_WIDTH), F32),
                   jax.ShapeDtypeStruct((n_seq, HG_HEADS, HG_KEY, HG_VAL), F32)),
        scratch_shapes=[pltpu.VMEM((nj, ch, HG_KEY), BF16), pltpu.VMEM((nj, ch, HG_KEY), BF16),
                        pltpu.VMEM((nj, 1, HG_KEY), F32), pltpu.VMEM((nj, ch, HG_VAL), F32)],
        compiler_params=_cparams("arbitrary", "arbitrary"),
        name="hgrn",
    )(r3(hq), r3(hk), r3(hv), r3(lf), tri, ones, s0)
    return o.reshape(n_seq * t_len, HG_WIDTH), st


def _mixout_kernel(y_ref, oa_ref, oh_ref, og_ref, ag_ref, hgain_ref, w_ref, o_ref):
    oa = oa_ref[...]
    a = oa * lax.rsqrt(jnp.mean(oa * oa, axis=-1, keepdims=True) + EPS) * ag_ref[...]
    oh = oh_ref[...]
    hs = []
    for h in range(HG_HEADS):
        x = oh[:, h * HG_VAL:(h + 1) * HG_VAL]
        hs.append(x * lax.rsqrt(jnp.mean(x * x, axis=-1, keepdims=True) + EPS))
    og = og_ref[...]
    hh = jnp.concatenate(hs, axis=1) * hgain_ref[...] * (og * jax.nn.sigmoid(og))
    m = (jnp.dot(a.astype(BF16), w_ref[:ATT_WIDTH, :], preferred_element_type=F32)
         + jnp.dot(hh.astype(BF16), w_ref[ATT_WIDTH:, :], preferred_element_type=F32))
    o_ref[...] = y_ref[...] + m


def _mixer_out(y, o_att, o_hg, og, attn_gain, hg_gain, w_out):
    n, d = y.shape
    tm = OUT_ROW_TILE
    assert n % tm == 0
    row = lambda w: pl.BlockSpec((tm, w), lambda i: (i, 0))
    const = lambda s: pl.BlockSpec(s, lambda i: (0, 0))
    return pl.pallas_call(
        _mixout_kernel,
        grid=(n // tm,),
        in_specs=[row(d), row(ATT_WIDTH), row(HG_WIDTH), row(HG_WIDTH),
                  const((1, ATT_WIDTH)), const((1, HG_WIDTH)), const((MIX_WIDTH, d))],
        out_specs=row(d),
        out_shape=jax.ShapeDtypeStruct((n, d), F32),
        compiler_params=_cparams("arbitrary"),
        name="mixout",
    )(y, o_att, o_hg, og, attn_gain.reshape(1, ATT_WIDTH), hg_gain.reshape(1, HG_WIDTH), w_out)


def _bias_kernel(tbl_ref, o_ref, *, a0, ar, ac, rows_blk):
    h = pl.program_id(0)
    rb = pl.program_id(1)
    shape = o_ref.shape[1:]
    r = lax.broadcasted_iota(jnp.int32, shape, 0) + rb * rows_blk
    c = lax.broadcasted_iota(jnp.int32, shape, 1)
    n = a0 + ar * r + ac * c
    out = jnp.full(shape, tbl_ref[0, h], F32)
    for k in range(1, N_BUCKETS):
        out = jnp.where(n >= T5_THRESHOLDS[k], tbl_ref[k, h], out)
    o_ref[0] = out


def _bias_table(tbl, rows, cols, a0, ar, ac):
    rows_blk = min(rows, 512)
    assert rows % rows_blk == 0 and cols % LANES == 0
    return pl.pallas_call(
        functools.partial(_bias_kernel, a0=a0, ar=ar, ac=ac, rows_blk=rows_blk),
        grid=(ATT_HEADS, rows // rows_blk),
        in_specs=[pl.BlockSpec(memory_space=pltpu.SMEM)],
        out_specs=pl.BlockSpec((1, rows_blk, cols), lambda h, rb: (h, rb, 0)),
        out_shape=jax.ShapeDtypeStruct((ATT_HEADS, rows, cols), F32),
        compiler_params=_cparams("arbitrary", "arbitrary"),
        name="t5_bias",
    )(tbl)


def _compress_weights(w1, b1, w2, b2):
    w = w1.reshape(2, CMP_RATIO, CMP_STRIDE, HEAD_DIM, CMP_HIDDEN)
    z = jnp.zeros_like(w)
    top = jnp.concatenate([w, z], axis=-1)
    bot = jnp.concatenate([z, w], axis=-1)
    wpad = jnp.stack([top, bot], axis=3)
    wpad = wpad.reshape(2 * CMP_RATIO, CMP_STRIDE * LANES, 2 * CMP_HIDDEN).astype(BF16)
    z2 = jnp.zeros_like(w2)
    w2pad = jnp.concatenate([jnp.concatenate([w2, z2], axis=-1), jnp.concatenate([z2, w2], axis=-1)], axis=1)
    return wpad, jnp.concatenate([b1, b1], axis=-1), w2pad.astype(BF16), jnp.concatenate([b2, b2], axis=-1)


def _compress_kernel(*refs, n_in, paged, packed):
    if paged:
        refs = refs[1:]
    in_refs = refs[:n_in]
    perm_ref, wpad_ref, b1_ref, w2pad_ref, b2_ref, kg_ref, bd_ref, out_ref, xs_ref = refs[n_in:]
    u = pl.program_id(1)
    n_chunk = CMP_UNITS * CMP_UNIT_ROWS // CMP_STRIDE
    perm = perm_ref[...]

    for t in range(CMP_UNIT_ROWS // 256):
        if paged:
            x = jnp.concatenate([in_refs[2 * t][0], in_refs[2 * t + 1][0]], axis=0)
        else:
            x = in_refs[0][t * 256:(t + 1) * 256, :]
        y = jnp.dot(perm, x.astype(BF16), preferred_element_type=F32).astype(BF16)
        base = pl.multiple_of(u * (CMP_UNIT_ROWS // CMP_STRIDE) + t * 16, 16)
        for s in range(CMP_STRIDE):
            xs_ref[s, pl.ds(base, 16), :] = y[s * 16:(s + 1) * 16, :]

    @pl.when(u == CMP_UNITS - 1)
    def _():
        halves = []
        for kv in range(2):
            outs = []
            for j in range(2):
                col = (kv * 2 + j) * LANES
                lhs = jnp.concatenate([xs_ref[s, :, col:col + LANES] for s in range(CMP_STRIDE)], axis=1)
                h0 = jnp.dot(lhs, wpad_ref[kv * CMP_RATIO + 0], preferred_element_type=F32)
                h1 = jnp.dot(lhs, wpad_ref[kv * CMP_RATIO + 1], preferred_element_type=F32)
                h = b1_ref[kv:kv + 1, :] + h0 + pltpu.roll(h1, n_chunk - 1, axis=0)
                a = (h * jax.nn.sigmoid(h)).astype(BF16)
                outs.append(jnp.dot(a, w2pad_ref[kv], preferred_element_type=F32) + b2_ref[kv:kv + 1, :])
            halves.append(jnp.concatenate(outs, axis=1))
        kc = _group_rms(halves[0], bd_ref[...], kg_ref[...], HEAD_DIM)
        vc = halves[1]
        if packed:
            pk = _pack_kv(kc, vc)
            for g in range(ATT_GROUPS):
                out_ref[0, g] = pk[:, g * LANES:(g + 1) * LANES]
        else:
            out_ref[0, :, :KV_WIDTH] = kc.astype(BF16)
            out_ref[0, :, KV_WIDTH:] = vc.astype(BF16)


def _compress(rows, page_table, cw, k_gain0, *, paged):
    wpad, b1, w2pad, b2 = cw
    n_chunk = CMP_UNITS * CMP_UNIT_ROWS // CMP_STRIDE
    i = np.arange(256)
    perm = np.zeros((256, 256), np.float32)
    perm[(i % 16) * 16 + i // 16, i] = 1.0
    consts = (jnp.asarray(perm, BF16), wpad, b1, w2pad, b2, k_gain0.reshape(1, KV_WIDTH),
              jnp.asarray(_block_diag_ones(KV_WIDTH, HEAD_DIM), BF16))
    if paged:
        n_grp = page_table.shape[0]
        pages_per_unit = CMP_UNIT_ROWS // PAGE_SIZE
        n_in = pages_per_unit
        const = lambda a: pl.BlockSpec(a.shape, lambda b, u, pt, nd=a.ndim: (0,) * nd)
        in_specs = [pl.BlockSpec((1, PAGE_SIZE, KV_ROW), lambda b, u, pt, k=k: (pt[b, u * pages_per_unit + k], 0, 0))
                    for k in range(n_in)]
        out_spec = pl.BlockSpec((1, n_chunk, KV_ROW), lambda b, u, pt: (b, 0, 0))
        out_shape = jax.ShapeDtypeStruct((n_grp, n_chunk, KV_ROW), BF16)
        args = (page_table,) + (rows,) * n_in + consts
        nsp = 1
    else:
        n_grp, n_in = 1, 1
        const = lambda a: pl.BlockSpec(a.shape, lambda b, u, nd=a.ndim: (0,) * nd)
        in_specs = [pl.BlockSpec((CMP_UNIT_ROWS, KV_ROW), lambda b, u: (u, 0))]
        out_spec = pl.BlockSpec((1, ATT_GROUPS, n_chunk, LANES), lambda b, u: (0, 0, 0, 0))
        out_shape = jax.ShapeDtypeStruct((1, ATT_GROUPS, n_chunk, LANES), BF16)
        args = (rows,) + consts
        nsp = 0
    grid_spec = pltpu.PrefetchScalarGridSpec(
        num_scalar_prefetch=nsp, grid=(n_grp, CMP_UNITS),
        in_specs=in_specs + [const(a) for a in consts],
        out_specs=out_spec,
        scratch_shapes=[pltpu.VMEM((CMP_STRIDE, n_chunk, KV_ROW), BF16)])
    return pl.pallas_call(
        functools.partial(_compress_kernel, n_in=n_in, paged=paged, packed=not paged),
        grid_spec=grid_spec, out_shape=out_shape,
        compiler_params=_cparams("arbitrary", "arbitrary"),
        name="compress_paged" if paged else "compress",
    )(*args)


PAGES_PER_STEP = CMP_UNIT_ROWS // PAGE_SIZE


def _cache_view(cache):
    return jnp.transpose(cache, (0, 2, 3, 4, 1))


def _compress_paged_kernel(pt_ref, *refs):
    del pt_ref
    page_refs = refs[:PAGES_PER_STEP]
    pick_ref, w1_ref, b1_ref, w2_ref, b2_ref, kg_ref, bd_ref, out_ref, xs_ref = refs[PAGES_PER_STEP:]
    u = pl.program_id(1)
    n_chunk = CMP_UNITS * CMP_UNIT_ROWS // CMP_STRIDE
    pick = pick_ref[...]
    zero = jnp.zeros((HEAD_DIM, 2 * PAGE_SIZE), BF16)

    for t in range(PAGES_PER_STEP // 2):
        base = pl.multiple_of(u * (CMP_UNIT_ROWS // CMP_STRIDE) + t * 16, 16)
        for kv in range(2):
            for g in range(ATT_GROUPS):
                kt = jnp.concatenate([page_refs[2 * t][0, kv, g], page_refs[2 * t + 1][0, kv, g]], axis=1).astype(BF16)
                rhs = jnp.concatenate([jnp.concatenate([kt, zero], axis=1),
                                       jnp.concatenate([zero, kt], axis=1)], axis=0)
                y = lax.dot_general(pick, rhs, NT_DIMS, preferred_element_type=F32).astype(BF16)
                for s2 in range(CMP_STRIDE // 2):
                    xs_ref[kv * ATT_GROUPS + g, s2, pl.ds(base, 16), :] = y[s2 * 16:(s2 + 1) * 16, :]

    @pl.when(u == CMP_UNITS - 1)
    def _():
        halves = []
        for kv in range(2):
            lhs = jnp.concatenate(
                [jnp.concatenate([xs_ref[kv * ATT_GROUPS + g, s2] for s2 in range(CMP_STRIDE // 2)], axis=1)
                 for g in range(ATT_GROUPS)], axis=0)
            h0 = jnp.dot(lhs, w1_ref[kv * CMP_RATIO + 0], preferred_element_type=F32)
            h1 = jnp.dot(lhs, w1_ref[kv * CMP_RATIO + 1], preferred_element_type=F32)
            h = b1_ref[kv:kv + 1, :] + h0 + pltpu.roll(h1, ATT_GROUPS * n_chunk - 1, axis=0)
            a = (h * jax.nn.sigmoid(h)).astype(BF16)
            o = b2_ref[kv:kv + 1, :]
            for g in range(ATT_GROUPS):
                o = o + jnp.dot(a[g * n_chunk:(g + 1) * n_chunk], w2_ref[kv * ATT_GROUPS + g], preferred_element_type=F32)
            halves.append(o)
        out_ref[0, :, :KV_WIDTH] = _group_rms(halves[0], bd_ref[...], kg_ref[...], HEAD_DIM).astype(BF16)
        out_ref[0, :, KV_WIDTH:] = halves[1].astype(BF16)


def _compress_paged(cache_t, page_table, w1, b1, w2, b2, k_gain0):
    n_seq = page_table.shape[0]
    n_chunk = CMP_UNITS * CMP_UNIT_ROWS // CMP_STRIDE
    r = np.arange(LANES)
    s2, c = r // 16, r % 16
    pick = np.zeros((LANES, 2, 2 * PAGE_SIZE), np.float32)
    for half in range(2):
        pick[r, half, CMP_STRIDE * c + 2 * s2 + half] = 1.0
    w2p = jnp.zeros((2, ATT_GROUPS, CMP_HIDDEN, KV_WIDTH), F32)
    for g in range(ATT_GROUPS):
        w2p = w2p.at[:, g, :, g * HEAD_DIM:(g + 1) * HEAD_DIM].set(w2)
    consts = (jnp.asarray(pick.reshape(LANES, 4 * PAGE_SIZE), BF16),
              w1.reshape(2 * CMP_RATIO, CMP_STRIDE * HEAD_DIM, CMP_HIDDEN).astype(BF16), b1,
              w2p.reshape(2 * ATT_GROUPS, CMP_HIDDEN, KV_WIDTH).astype(BF16), jnp.tile(b2, (1, ATT_GROUPS)),
              k_gain0.reshape(1, KV_WIDTH), jnp.asarray(_block_diag_ones(KV_WIDTH, HEAD_DIM), BF16))
    const = lambda a: pl.BlockSpec(a.shape, lambda b, u, pt, nd=a.ndim: (0,) * nd)
    pages = [pl.BlockSpec((1, 2, ATT_GROUPS, HEAD_DIM, PAGE_SIZE),
                          lambda b, u, pt, k=k: (pt[b, u * PAGES_PER_STEP + k], 0, 0, 0, 0))
             for k in range(PAGES_PER_STEP)]
    grid_spec = pltpu.PrefetchScalarGridSpec(
        num_scalar_prefetch=1, grid=(n_seq, CMP_UNITS),
        in_specs=pages + [const(a) for a in consts],
        out_specs=pl.BlockSpec((1, n_chunk, KV_ROW), lambda b, u, pt: (b, 0, 0)),
        scratch_shapes=[pltpu.VMEM((2 * ATT_GROUPS, CMP_STRIDE // 2, n_chunk, LANES), BF16)])
    return pl.pallas_call(
        _compress_paged_kernel, grid_spec=grid_spec,
        out_shape=jax.ShapeDtypeStruct((n_seq, n_chunk, KV_ROW), BF16),
        compiler_params=_cparams("arbitrary", "arbitrary"),
        name="compress_paged",
    )(page_table, *([cache_t] * PAGES_PER_STEP), *consts)


def _block_overlap(n_cmp, n_blk):
    cs = np.arange(n_cmp)[:, None] * CMP_STRIDE
    bs = np.arange(n_blk)[None, :] * SEL_BLOCK
    ov = np.minimum(cs + CMP_BLOCK, bs + SEL_BLOCK) - np.maximum(cs, bs)
    return (np.clip(ov, 0, None) / CMP_BLOCK).astype(np.float32)


def _overlap_padded(n_cmp, n_blk, rows, cols):
    ov = np.zeros((rows, cols), np.float32)
    ov[:n_cmp, :n_blk] = _block_overlap(n_cmp, n_blk)
    return jnp.asarray(ov, BF16)


def _softmax_rows(s, mask):
    s = jnp.where(mask, s, NEG)
    m = jnp.max(s, axis=1, keepdims=True)
    e = jnp.where(mask, jnp.exp(s - m), 0.0)
    d = jnp.sum(e, axis=1, keepdims=True)
    return e / jnp.where(d > 0, d, 1.0)


def _select_blocks(imp, qpos, n_blk, axis):
    blk = lax.broadcasted_iota(jnp.int32, imp.shape, axis)
    cur = qpos >> SEL_SHIFT
    forced = (blk == 0) | (blk == cur) | (blk == cur - 1)
    valid = blk * SEL_BLOCK <= qpos
    score = jnp.where(forced, ATT_HPG + 1.0, jnp.where(valid, imp, -1.0))
    ahead = jnp.zeros(imp.shape, F32)
    for m in range(n_blk):
        sm = jnp.broadcast_to(score[m:m + 1, :] if axis == 0 else score[:, m:m + 1], imp.shape)
        tie = jnp.where(blk > m, 1.0, 0.0)
        ahead = ahead + jnp.where(sm > score, 1.0, jnp.where(sm == score, tie, 0.0))
    return jnp.where(blk < n_blk, jnp.where(ahead < N_SEL, 1.0, 0.0), 0.0)


def _online_update(s, mask, v, m_ref, l_ref, acc_ref, idx=None):
    at = (lambda r: r) if idx is None else (lambda r: r.at[idx])
    m_r, l_r, acc_r = at(m_ref), at(l_ref), at(acc_ref)
    s = jnp.where(mask, s, NEG)
    m_prev = m_r[...]
    m_new = jnp.maximum(m_prev, jnp.max(s, axis=1, keepdims=True))
    alpha = jnp.exp(m_prev - m_new)
    e = jnp.where(mask, jnp.exp(s - m_new), 0.0)
    l_r[...] = alpha * l_r[...] + jnp.sum(e, axis=1, keepdims=True)
    pv = jnp.dot(e.astype(BF16), v, preferred_element_type=F32)
    reps = acc_r.shape[-1] // LANES
    a_w = alpha if reps == 1 else jnp.concatenate([alpha] * reps, axis=1)
    acc_r[...] = a_w * acc_r[...] + pv
    m_r[...] = m_new


SEL_SHIFT = SEL_BLOCK.bit_length() - 1
NT_DIMS = (((1,), (1,)), ((), ()))


def _nsa_prompt_kernel(q_ref, gate_ref, kvc_ref, slc_ref, win_ref, bcmp_ref, toep_ref, ov_ref, o_ref,
                       s_scr, m_scr, l_scr, acc_scr):
    i = pl.program_id(2)
    tile = (Q_BLOCK, LANES)
    rows = ATT_HPG * Q_BLOCK
    lane = lax.broadcasted_iota(jnp.int32, tile, 1)
    sub = lax.broadcasted_iota(jnp.int32, tile, 0)
    low = lane < HEAD_DIM
    qpos = i * Q_BLOCK + sub
    head = lambda x, p: x[p * Q_BLOCK:(p + 1) * Q_BLOCK]

    qa = q_ref[...] * (HEAD_DIM ** -0.5)
    qp = []
    for pair in range(ATT_HPG // 2):
        x = qa[:, pair * LANES:(pair + 1) * LANES]
        qp.append(jnp.where(low, x, 0.0))
        qp.append(jnp.where(low, pltpu.roll(x, HEAD_DIM, axis=1), 0.0))
    qs = jnp.concatenate(qp, axis=0).astype(BF16)

    kvc = kvc_ref[0, 0]
    s = lax.dot_general(qs, kvc, NT_DIMS, preferred_element_type=F32) + bcmp_ref[...].reshape(rows, LANES)
    cmask1 = (lane * CMP_STRIDE + (CMP_BLOCK - 1)) <= qpos
    pr = _softmax_rows(s, jnp.concatenate([cmask1] * ATT_HPG, axis=0))
    o_cmp = jnp.dot(pr.astype(BF16), kvc, preferred_element_type=F32)
    hi, lo = _split2(sum(head(pr, p) for p in range(ATT_HPG)))
    ov = ov_ref[...]
    imp = jnp.dot(hi, ov, preferred_element_type=F32) + jnp.dot(lo, ov, preferred_element_type=F32)
    n_blk = SEQ // SEL_BLOCK
    imp_t = imp.T[:n_blk, :]
    qpos_t = i * Q_BLOCK + lax.broadcasted_iota(jnp.int32, (n_blk, Q_BLOCK), 1)
    sel_t = _select_blocks(imp_t, qpos_t, n_blk, axis=0).astype(BF16)
    blk_row = lax.broadcasted_iota(jnp.int32, (n_blk, LANES), 0)
    blk_of_key = lax.broadcasted_iota(jnp.int32, (n_blk, LANES), 1) >> SEL_SHIFT
    last_tile = SEQ // LANES - 1

    def attend(kv_ref, j_lo, n_tiles, use_sel):
        def tile_kv(slot):
            j = j_lo + slot
            jc = jnp.minimum(j, last_tile)
            return j, kv_ref[pl.ds(pl.multiple_of(jc * LANES, LANES), LANES), :]

        m_scr[...] = jnp.full(m_scr.shape, NEG, F32)

        def score_tile(slot):
            j, kv = tile_kv(slot)
            delta = i - j
            dist = delta * LANES + sub - lane
            if use_sel:
                expand = jnp.where(blk_row == 2 * j + blk_of_key, 1.0, 0.0).astype(BF16)
                picked = lax.dot_general(sel_t, expand, (((0,), (0,)), ((), ())), preferred_element_type=F32)
                mask = jnp.where(dist >= 0, picked, 0.0) > 0.5
            else:
                mask = jnp.where(dist >= 0, jnp.where(dist < WINDOW, 1.0, 0.0), 0.0) > 0.5
            sc = lax.dot_general(qs, kv, NT_DIMS, preferred_element_type=F32)
            row0 = pl.multiple_of(jnp.maximum(delta, 0) * LANES, LANES)
            for p in range(ATT_HPG):
                sp = jnp.where(mask, head(sc, p) + toep_ref[p, pl.ds(row0, LANES), :], NEG)
                s_scr[slot, p] = sp
                m_scr[p] = jnp.maximum(m_scr[p], sp)

        def scores(jj, carry):
            score_tile(2 * jj)
            score_tile(2 * jj + 1)
            return carry

        n_pairs = (n_tiles + 1) // 2
        lax.fori_loop(0, n_pairs, scores, 0)
        for p in range(ATT_HPG):
            m_scr[p] = jnp.broadcast_to(jnp.max(m_scr[p], axis=1, keepdims=True), tile)
        l_scr[...] = jnp.zeros(l_scr.shape, F32)
        acc_scr[...] = jnp.zeros(acc_scr.shape, F32)

        def weight_tile(slot):
            _, kv = tile_kv(slot)
            es = []
            for p in range(ATT_HPG):
                e = jnp.exp(s_scr[slot, p] - m_scr[p])
                l_scr[p] = l_scr[p] + e
                es.append(e)
            return jnp.dot(jnp.concatenate(es, axis=0).astype(BF16), kv, preferred_element_type=F32)

        def weights(jj, carry):
            acc_scr[...] += weight_tile(2 * jj) + weight_tile(2 * jj + 1)
            return carry

        lax.fori_loop(0, n_pairs, weights, 0)
        out = []
        for p in range(ATT_HPG):
            out.append(head(acc_scr[...], p) / jnp.sum(l_scr[p], axis=1, keepdims=True))
        return out

    o_slc = attend(slc_ref, 0, i + 1, True)
    w_lo = jnp.maximum(i - WINDOW // LANES, 0)
    o_win = attend(win_ref, w_lo, i + 1 - w_lo, False)

    g = gate_ref[...]
    comb = []
    for p in range(ATT_HPG):
        col = lambda br: jnp.broadcast_to(g[:, br * ATT_HPG + p:br * ATT_HPG + p + 1], tile)
        comb.append(col(0) * head(o_cmp, p) + col(1) * o_slc[p] + col(2) * o_win[p])
    for pair in range(ATT_HPG // 2):
        o_ref[:, pair * LANES:(pair + 1) * LANES] = jnp.where(
            low, pltpu.roll(comb[2 * pair], HEAD_DIM, axis=1), comb[2 * pair + 1])


def _nsa_prompt(q, gates, kvc, slc_pack, win_pack, bias_cmp, bias_toep, n_batch=BATCH):
    nqb = SEQ // Q_BLOCK
    ov = _overlap_padded(SEQ // CMP_STRIDE - 1, SEQ // SEL_BLOCK, LANES, LANES)
    gw = ATT_HPG * HEAD_DIM
    return pl.pallas_call(
        _nsa_prompt_kernel,
        grid=(n_batch, ATT_GROUPS, nqb),
        in_specs=[
            pl.BlockSpec((Q_BLOCK, gw), lambda b, g, i: (b * nqb + i, g)),
            pl.BlockSpec((Q_BLOCK, LANES), lambda b, g, i: (b * nqb + i, g)),
            pl.BlockSpec((1, 1, SEQ // CMP_STRIDE, LANES), lambda b, g, i: (0, g, b, 0)),
            pl.BlockSpec((SEQ, LANES), lambda b, g, i: (b, g)),
            pl.BlockSpec((SEQ, LANES), lambda b, g, i: (b, g)),
            pl.BlockSpec((ATT_HPG, Q_BLOCK, LANES), lambda b, g, i: (g, i, 0)),
            pl.BlockSpec((ATT_HPG, SEQ, LANES), lambda b, g, i: (g, 0, 0)),
            pl.BlockSpec((LANES, LANES), lambda b, g, i: (0, 0)),
        ],
        out_specs=pl.BlockSpec((Q_BLOCK, gw), lambda b, g, i: (b * nqb + i, g)),
        out_shape=jax.ShapeDtypeStruct((n_batch * SEQ, ATT_WIDTH), F32),
        scratch_shapes=[pltpu.VMEM((nqb, ATT_HPG, Q_BLOCK, LANES), F32),
                        pltpu.VMEM((ATT_HPG, Q_BLOCK, LANES), F32), pltpu.VMEM((ATT_HPG, Q_BLOCK, LANES), F32),
                        pltpu.VMEM((ATT_HPG * Q_BLOCK, LANES), F32)],
        compiler_params=_cparams("arbitrary", "arbitrary", "arbitrary"),
        name="nsa_prompt",
    )(q, gates, kvc, slc_pack, win_pack, bias_cmp, bias_toep, ov)


SAMPLE_ROWS = ATT_HEADS * DEC_SEQ
SAMPLE_PAGES_PER_STEP = 16
SAMPLE_STEPS = N_PAGES // SAMPLE_PAGES_PER_STEP
SAMPLE_N_CMP = (PAST_LEN + DEC_SEQ - CMP_BLOCK) // CMP_STRIDE + 1
SAMPLE_N_BLK = -(-(PAST_LEN + DEC_SEQ) // SEL_BLOCK)
SAMPLE_BLK_LANES = 2 * LANES


def _nsa_sample_kernel(pt_ref, q_ref, gate_ref, kvc_ref, *rest):
    page_refs = rest[:SAMPLE_PAGES_PER_STEP]
    (slc_new_ref, win_state_ref, win_new_ref, bcmp_ref, bslc_ref, bwin_ref, perm_ref, ov_ref,
     o_ref, qbd_scr, sel_scr, m_scr, l_scr, acc_scr, ocmp_scr, owin_scr) = rest[SAMPLE_PAGES_PER_STEP:]
    del pt_ref
    u = pl.program_id(1)
    tile = (SAMPLE_ROWS, LANES)
    wide = (SAMPLE_ROWS, KV_WIDTH)
    lane = lax.broadcasted_iota(jnp.int32, tile, 1)
    t_row = lax.broadcasted_iota(jnp.int32, tile, 0) & (DEC_SEQ - 1)
    rows_per_group = ATT_HPG * DEC_SEQ
    own = ((lax.broadcasted_iota(jnp.int32, wide, 1) >> (HEAD_DIM.bit_length() - 1))
           == (lax.broadcasted_iota(jnp.int32, wide, 0) >> (rows_per_group.bit_length() - 1)))

    def reset():
        m_scr[...] = jnp.full(m_scr.shape, NEG, F32)
        l_scr[...] = jnp.zeros(l_scr.shape, F32)
        acc_scr[...] = jnp.zeros(acc_scr.shape, F32)

    def finish():
        l = jnp.sum(l_scr[...], axis=1, keepdims=True)
        return acc_scr[...] / jnp.where(l > 0, l, 1.0)

    def attend(tiles):
        qbd = qbd_scr[...]
        scores = []
        m_el = None
        for kt, _, bias, mask in tiles:
            s = jnp.where(mask, jnp.dot(qbd, kt.astype(BF16), preferred_element_type=F32) + bias, NEG)
            scores.append(s)
            m_el = s if m_el is None else jnp.maximum(m_el, s)
        m_prev = m_scr[...]
        m_new = jnp.maximum(m_prev, jnp.max(m_el, axis=1, keepdims=True))
        alpha = jnp.exp(m_prev - m_new)
        l_el = alpha * l_scr[...]
        acc = jnp.concatenate([alpha, alpha], axis=1) * acc_scr[...]
        for (_, vt, _, _), s in zip(tiles, scores):
            e = jnp.exp(s - m_new)
            l_el = l_el + e
            acc = acc + lax.dot_general(e.astype(BF16), vt.astype(BF16), NT_DIMS, preferred_element_type=F32)
        m_scr[...] = m_new
        l_scr[...] = l_el
        acc_scr[...] = acc

    def page_tile(kv4, bias, mask):
        return (kv4[0].reshape(KV_WIDTH, LANES), kv4[1].reshape(KV_WIDTH, LANES), bias, mask)

    def new_tile(ref, bias, mask):
        rows = jnp.concatenate([ref[...], jnp.zeros((LANES - DEC_SEQ, KV_ROW), F32)], axis=0)
        return (rows[:, :KV_WIDTH].T, rows[:, KV_WIDTH:].T, bias, mask)

    @pl.when(u == 0)
    def _():
        q = (q_ref[...] * (HEAD_DIM ** -0.5)).astype(BF16)
        qperm = jnp.dot(q, perm_ref[...], preferred_element_type=F32)
        qfull = jnp.concatenate([qperm[:, p * KV_WIDTH:(p + 1) * KV_WIDTH]
                                 for g in range(ATT_GROUPS) for p in range(ATT_HPG)], axis=0)
        qbd = jnp.where(own, qfull, 0.0).astype(BF16)
        qbd_scr[...] = qbd

        kvc = kvc_ref[0]
        s = lax.dot_general(qbd, kvc[:, :KV_WIDTH], NT_DIMS, preferred_element_type=F32) + bcmp_ref[...]
        cmask = lax.broadcasted_iota(jnp.int32, s.shape, 1) < SAMPLE_N_CMP
        pr = _softmax_rows(s, cmask)
        ocmp_scr[...] = jnp.dot(pr.astype(BF16), kvc[:, KV_WIDTH:], preferred_element_type=F32)
        ps = []
        for g in range(ATT_GROUPS):
            r0 = g * rows_per_group
            ps.append(sum(pr[r0 + p * DEC_SEQ:r0 + (p + 1) * DEC_SEQ, :] for p in range(ATT_HPG)))
        hi, lo = _split2(jnp.concatenate(ps, axis=0))
        ov = ov_ref[...]
        imp = jnp.dot(hi, ov, preferred_element_type=F32) + jnp.dot(lo, ov, preferred_element_type=F32)
        qpos = PAST_LEN + (lax.broadcasted_iota(jnp.int32, imp.shape, 0) & (DEC_SEQ - 1))
        sel = _select_blocks(imp, qpos, SAMPLE_N_BLK, axis=1)
        sel_scr[...] = jnp.concatenate([sel[g * DEC_SEQ:(g + 1) * DEC_SEQ, :]
                                        for g in range(ATT_GROUPS) for p in range(ATT_HPG)], axis=0).astype(BF16)

        reset()
        tiles = [page_tile(win_state_ref[0, :, :, :, w * LANES:(w + 1) * LANES],
                           bwin_ref[:, w * LANES:(w + 1) * LANES], (w * LANES + lane) > t_row)
                 for w in range(WINDOW // LANES)]
        tiles.append(new_tile(win_new_ref, bwin_ref[:, WINDOW:WINDOW + LANES], lane <= t_row))
        attend(tiles)
        owin_scr[...] = finish()
        reset()

    blk_of_key = lax.broadcasted_iota(jnp.int32, (SAMPLE_BLK_LANES, LANES), 1) >> SEL_SHIFT
    blk_row = lax.broadcasted_iota(jnp.int32, (SAMPLE_BLK_LANES, LANES), 0)
    sel = sel_scr[...]
    tiles = []
    for k in range(SAMPLE_PAGES_PER_STEP):
        pg = u * SAMPLE_PAGES_PER_STEP + k
        expand = jnp.where(blk_row == 2 * pg + blk_of_key, 1.0, 0.0).astype(BF16)
        picked = jnp.dot(sel, expand, preferred_element_type=F32) > 0.5
        bias = bslc_ref[:, pl.ds(pl.multiple_of(pg * LANES, LANES), LANES)]
        tiles.append(page_tile(page_refs[k][0], bias, picked))
    attend(tiles)

    @pl.when(u == SAMPLE_STEPS - 1)
    def _():
        attend([new_tile(slc_new_ref, bslc_ref[:, PAST_LEN:PAST_LEN + LANES], lane <= t_row)])
        o_slc = finish()
        gt = gate_ref[...]

        def gate_rows(br):
            cols = []
            for g in range(ATT_GROUPS):
                for p in range(ATT_HPG):
                    c = g * LANES + br * ATT_HPG + p
                    cols.append(jnp.broadcast_to(gt[:, c:c + 1], (DEC_SEQ, KV_WIDTH)))
            return jnp.concatenate(cols, axis=0)

        comb = gate_rows(0) * ocmp_scr[...] + gate_rows(1) * o_slc + gate_rows(2) * owin_scr[...]
        comb = jnp.where(own, comb, 0.0)
        per_head = []
        for p in range(ATT_HPG):
            per_head.append(sum(comb[(g * ATT_HPG + p) * DEC_SEQ:(g * ATT_HPG + p + 1) * DEC_SEQ, :]
                                for g in range(ATT_GROUPS)))
        hi, lo = _split2(jnp.concatenate(per_head, axis=1))
        perm = perm_ref[...]
        o_ref[...] = (lax.dot_general(hi, perm, NT_DIMS, preferred_element_type=F32)
                      + lax.dot_general(lo, perm, NT_DIMS, preferred_element_type=F32))


def _nsa_sample(q, gates, kvc, cache_slc, page_table, slc_rows, state_win, win_rows, bias_cmp, bias_slc, bias_win):
    n_seq = page_table.shape[0]
    row0 = N_PROMPT // DEC_SEQ
    src = np.arange(ATT_WIDTH)
    g, p, d = src // (ATT_HPG * HEAD_DIM), (src // HEAD_DIM) % ATT_HPG, src % HEAD_DIM
    perm = np.zeros((ATT_WIDTH, ATT_WIDTH), np.float32)
    perm[src, p * KV_WIDTH + g * HEAD_DIM + d] = 1.0
    ov = _overlap_padded(SAMPLE_N_CMP, SAMPLE_N_BLK, PAST_LEN // CMP_STRIDE, SAMPLE_BLK_LANES)
    tok = lambda w: pl.BlockSpec((DEC_SEQ, w), lambda b, u, pt: (row0 + b, 0))
    const = lambda a: pl.BlockSpec(a.shape, lambda b, u, pt, nd=a.ndim: (0,) * nd)
    per_seq = lambda s: pl.BlockSpec((1,) + s, lambda b, u, pt, nd=len(s): (b,) + (0,) * nd)
    pages = [pl.BlockSpec((1, 2, ATT_GROUPS, HEAD_DIM, PAGE_SIZE),
                          lambda b, u, pt, k=k: (pt[b, u * SAMPLE_PAGES_PER_STEP + k], 0, 0, 0, 0))
             for k in range(SAMPLE_PAGES_PER_STEP)]
    consts = (bias_cmp, bias_slc, bias_win, jnp.asarray(perm, BF16), ov)
    grid_spec = pltpu.PrefetchScalarGridSpec(
        num_scalar_prefetch=1, grid=(n_seq, SAMPLE_STEPS),
        in_specs=[tok(ATT_WIDTH), tok(ATT_GROUPS * LANES), per_seq((PAST_LEN // CMP_STRIDE, KV_ROW))] + pages
        + [tok(KV_ROW), per_seq((2, ATT_GROUPS, HEAD_DIM, WINDOW)), tok(KV_ROW)] + [const(a) for a in consts],
        out_specs=pl.BlockSpec((DEC_SEQ, ATT_WIDTH), lambda b, u, pt: (b, 0)),
        scratch_shapes=[pltpu.VMEM((SAMPLE_ROWS, KV_WIDTH), BF16), pltpu.VMEM((SAMPLE_ROWS, SAMPLE_BLK_LANES), BF16),
                        pltpu.VMEM((SAMPLE_ROWS, LANES), F32), pltpu.VMEM((SAMPLE_ROWS, LANES), F32),
                        pltpu.VMEM((SAMPLE_ROWS, KV_WIDTH), F32), pltpu.VMEM((SAMPLE_ROWS, KV_WIDTH), F32),
                        pltpu.VMEM((SAMPLE_ROWS, KV_WIDTH), F32)])
    return pl.pallas_call(
        _nsa_sample_kernel, grid_spec=grid_spec,
        out_shape=jax.ShapeDtypeStruct((n_seq * DEC_SEQ, ATT_WIDTH), F32),
        compiler_params=_cparams("arbitrary", "arbitrary"),
        name="nsa_sample",
    )(page_table, q, gates, kvc, *([cache_slc] * SAMPLE_PAGES_PER_STEP), slc_rows, state_win, win_rows, *consts)


def kernel(x_prompt, x_sample, cache_cmp_kv, cache_slc_kv, state_win_kv, state_hgrn, page_table,
           rel_bias_table, hgrn_lower_bound, norm_ffn1, w_ffn1_gate_up, w_ffn1_down, norm_mix,
           w_in, q_norm, k_norm, w_cmp1, b_cmp1, w_cmp2, b_cmp2, attn_out_norm, hgrn_out_norm,
           w_out, norm_ffn2, w_ffn2_gate_up, w_ffn2_down):
    assert DEPTH == 1
    l = 0
    n_phys = cache_cmp_kv.shape[1]
    kv_shape = (2, ATT_GROUPS, HEAD_DIM)
    x_all = jnp.concatenate([x_prompt.reshape(N_PROMPT, D_MODEL), x_sample.reshape(N_SAMPLE, D_MODEL)], axis=0)

    y1 = _ffn(x_all, norm_ffn1[l], w_ffn1_gate_up[l].astype(BF16), w_ffn1_down[l].astype(BF16))

    (q, cmp_rows, slc_rows, slc_pack, win_rows, win_pack, hq, lf, hk, hv, hog, gates) = _project_all(
        y1, norm_mix[l], _permute_w_in(w_in[l]), q_norm[l], k_norm[l], hgrn_lower_bound)

    tbl = rel_bias_table.astype(F32)
    first_end = CMP_BLOCK - 1
    bias_cmp_p = _bias_table(tbl, SEQ, LANES, -first_end, 1, -CMP_STRIDE)
    bias_toep = _bias_table(tbl, SEQ, LANES, 0, 1, -1)
    bias_cmp_s = _bias_table(tbl, DEC_SEQ, PAST_LEN // CMP_STRIDE, PAST_LEN - first_end, 1, -CMP_STRIDE)
    bias_slc_s = _bias_table(tbl, DEC_SEQ, PAST_LEN + LANES, PAST_LEN, 1, -1)
    bias_win_s = _bias_table(tbl, DEC_SEQ, WINDOW + LANES, WINDOW, 1, -1)
    rows_ht = lambda a: a.reshape(SAMPLE_ROWS, a.shape[-1])

    cw = _compress_weights(w_cmp1[l], b_cmp1[l], w_cmp2[l], b_cmp2[l])
    kvc_p = _compress(cmp_rows, None, cw, k_norm[l][0], paged=False)
    kvc_s = _compress_paged(_cache_view(cache_cmp_kv[l]), page_table, w_cmp1[l], b_cmp1[l], w_cmp2[l], b_cmp2[l],
                            k_norm[l][0])

    o_att_p = _nsa_prompt(q, gates, kvc_p, slc_pack, win_pack, bias_cmp_p, bias_toep)
    o_att_s = _nsa_sample(q, gates, kvc_s, _cache_view(cache_slc_kv[l]), page_table, slc_rows,
                          _cache_view(state_win_kv[l]), win_rows,
                          rows_ht(bias_cmp_s), rows_ht(bias_slc_s), rows_ht(bias_win_s))

    o_hg_p, hg_p = _hgrn(hq, hk, hv, lf, None, n_seq=BATCH, t_len=SEQ, ch=HG_CHUNK_PROMPT, row0=0)
    o_hg_s, hg_s = _hgrn(hq, hk, hv, lf, state_hgrn[l].astype(F32), n_seq=DEC_BATCH, t_len=DEC_SEQ,
                         ch=HG_CHUNK_SAMPLE, row0=N_PROMPT)

    y2 = _mixer_out(y1, jnp.concatenate([o_att_p, o_att_s], axis=0), jnp.concatenate([o_hg_p, o_hg_s], axis=0),
                    hog, attn_out_norm[l], hgrn_out_norm[l], w_out[l].astype(BF16))
    y3 = _ffn(y2, norm_ffn2[l], w_ffn2_gate_up[l].astype(BF16), w_ffn2_down[l].astype(BF16))

    prompt_rows = lambda a: a[:N_PROMPT].reshape((1, BATCH, SEQ) + kv_shape)
    sample_rows = lambda a: a[N_PROMPT:].reshape((1, DEC_BATCH, DEC_SEQ) + kv_shape)
    win_p = prompt_rows(win_rows)[:, :, SEQ - min(WINDOW, SEQ):]
    win_s = jnp.concatenate([state_win_kv[l][:, DEC_SEQ:], sample_rows(win_rows)[0]], axis=1)[None]
    return (y3[:N_PROMPT].reshape(BATCH, SEQ, D_MODEL), y3[N_PROMPT:].reshape(DEC_BATCH, DEC_SEQ, D_MODEL),
            prompt_rows(cmp_rows), prompt_rows(slc_rows), win_p, hg_p[None],
            sample_rows(cmp_rows), sample_rows(slc_rows), win_s, hg_s[None].astype(state_hgrn.dtype))
```

```python
import functools
import math

import jax
import jax.numpy as jnp
import numpy as np
from jax import lax
from jax.experimental import pallas as pl
from jax.experimental.pallas import tpu as pltpu

D_MODEL = 2048
BATCH = 4
SEQ = 2048
DEPTH = 1
DEC_BATCH = 32
DEC_SEQ = 8
PAST_LEN = 8192
PAGE_SIZE = 128
HEAD_DIM = 64
ATT_HEADS = (D_MODEL // 2) // HEAD_DIM
ATT_GROUPS = ATT_HEADS // 4
ATT_HPG = ATT_HEADS // ATT_GROUPS
ATT_WIDTH = ATT_HEADS * HEAD_DIM
KV_WIDTH = ATT_GROUPS * HEAD_DIM
N_BRANCH = 3
CMP_BLOCK = 32
CMP_STRIDE = 16
CMP_RATIO = CMP_BLOCK // CMP_STRIDE
CMP_HIDDEN = 256
SEL_BLOCK = 64
N_SEL = 8
WINDOW = 512
Q_BLOCK = 128
HG_KEY = 128
HG_VAL = 128
HG_HEADS = (D_MODEL // 2) // HG_VAL
HG_WIDTH = HG_HEADS * HG_VAL
MIX_WIDTH = ATT_WIDTH + HG_WIDTH
D_FF = 256 * ((8 * D_MODEL // 3 + 255) // 256)
N_BUCKETS = 32
MAX_DISTANCE = 1024
EPS = 1e-6
IN_SPLITS = (ATT_WIDTH, 2 * N_BRANCH * KV_WIDTH, N_BRANCH * ATT_HEADS,
             HG_HEADS * HG_KEY, HG_HEADS * HG_KEY, HG_WIDTH, HG_WIDTH)
D_IN = sum(IN_SPLITS)

N_PROMPT = BATCH * SEQ
N_SAMPLE = DEC_BATCH * DEC_SEQ
N_TOKENS = N_PROMPT + N_SAMPLE
N_PAGES = PAST_LEN // PAGE_SIZE
KV_ROW = 2 * KV_WIDTH

LANES = 128
V7X_VMEM_BYTES = 64 * 1024 * 1024
VMEM_LIMIT_BYTES = 56 * 1024 * 1024

NEG = -1e30
F32 = jnp.float32
BF16 = jnp.bfloat16

FFN_ROW_TILE = 768
FFN_FF_TILE = 512
PROJ_ROW_TILE = 528
PROJ_COL_TILE = 512
OUT_ROW_TILE = 256
HG_CHUNK_PROMPT = 16
HG_CHUNK_SAMPLE = DEC_SEQ
CMP_UNIT_ROWS = 2048
CMP_UNITS = 4


def _cparams(*sem):
    return pltpu.CompilerParams(dimension_semantics=sem, vmem_limit_bytes=VMEM_LIMIT_BYTES)


def _t5_thresholds():
    n = np.arange(0, 2 * MAX_DISTANCE + 2)
    exact = N_BUCKETS // 2
    logn = np.log(np.maximum(n, 1).astype(np.float64) / exact)
    large = exact + (logn / math.log(MAX_DISTANCE / exact) * (N_BUCKETS - exact)).astype(np.int32)
    b = np.where(n < exact, n, np.minimum(large, N_BUCKETS - 1))
    return [int(n[b >= k][0]) for k in range(N_BUCKETS)]


T5_THRESHOLDS = _t5_thresholds()


def _block_diag_ones(n, blk):
    i = np.arange(n)
    return (i[:, None] // blk == i[None, :] // blk).astype(np.float32)


def _split2(x):
    hi = x.astype(BF16)
    lo = (x - hi.astype(F32)).astype(BF16)
    return hi, lo


def _group_rms(t, bd, gain, width):
    hi, lo = _split2(t * t)
    ss = (jnp.dot(hi, bd, preferred_element_type=F32) + jnp.dot(lo, bd, preferred_element_type=F32))
    return t * lax.rsqrt(ss * (1.0 / width) + EPS) * gain


def _ffn_kernel(*refs, split_in, split_out):
    n_in = 2 if split_in else 1
    x_ref = refs[0]
    gain_ref, wg_ref, wu_ref, wd_ref = refs[n_in:n_in + 4]
    o_ref = refs[n_in + 4]
    xn_ref, acc_ref = refs[-2:]
    j = pl.program_id(1)
    last_tile = pl.program_id(0) == pl.num_programs(0) - 1
    tm = x_ref.shape[0]

    def rows_in():
        x = x_ref[...]
        if split_in:
            xs = refs[1][...]
            x = jnp.where(last_tile, jnp.concatenate([x[:tm - xs.shape[0]], xs], axis=0), x)
        return x

    @pl.when(j == 0)
    def _():
        x = rows_in()
        y = x * lax.rsqrt(jnp.mean(x * x, axis=-1, keepdims=True) + EPS)
        xn_ref[...] = (y * gain_ref[...]).astype(BF16)
        acc_ref[...] = jnp.zeros_like(acc_ref)

    xn = xn_ref[...]
    g = jnp.dot(xn, wg_ref[...], preferred_element_type=F32)
    u = jnp.dot(xn, wu_ref[...], preferred_element_type=F32)
    a = (g * jax.nn.sigmoid(g) * u).astype(BF16)
    acc_ref[...] += jnp.dot(a, wd_ref[...], preferred_element_type=F32)

    @pl.when(j == pl.num_programs(1) - 1)
    def _():
        res = rows_in() + 0.5 * acc_ref[...]
        o_ref[...] = res
        if split_out:
            os_ref = refs[n_in + 5]

            @pl.when(last_tile)
            def _():
                os_ref[...] = res[tm - os_ref.shape[0]:]


def _ffn(x, gain, w_gu, w_down, split_out=False):
    split_in = isinstance(x, tuple)
    tm, tf = FFN_ROW_TILE, FFN_FF_TILE
    n_p, n_s = N_PROMPT, N_SAMPLE
    n, d = n_p + n_s, D_MODEL
    assert n % tm == 0 and D_FF % tf == 0 and n_p % tm == tm - n_s
    nj = D_FF // tf
    row = pl.BlockSpec((tm, d), lambda i, j: (i, 0))
    tail = pl.BlockSpec((n_s, d), lambda i, j: (0, 0))
    xs = x if split_in else (x,)
    return pl.pallas_call(
        functools.partial(_ffn_kernel, split_in=split_in, split_out=split_out),
        grid=(n // tm, nj),
        in_specs=([row, tail] if split_in else [row]) + [
            pl.BlockSpec((1, d), lambda i, j: (0, 0)),
            pl.BlockSpec((d, tf), lambda i, j: (0, j)),
            pl.BlockSpec((d, tf), lambda i, j: (0, j + nj)),
            pl.BlockSpec((tf, d), lambda i, j: (j, 0)),
        ],
        out_specs=(row, tail) if split_out else row,
        out_shape=((jax.ShapeDtypeStruct((n_p, d), F32), jax.ShapeDtypeStruct((n_s, d), F32)) if split_out
                   else jax.ShapeDtypeStruct((n, d), F32)),
        scratch_shapes=[pltpu.VMEM((tm, d), BF16), pltpu.VMEM((tm, d), F32)],
        compiler_params=_cparams("arbitrary", "arbitrary"),
        name="ffn",
    )(*xs, gain.reshape(1, d), w_gu, w_gu, w_down)


PB_Q, PB_CMP, PB_SLC, PB_WIN, PB_HQ, PB_HF, PB_HI, PB_HG, PB_GATE = 0, 2, 3, 4, 5, 7, 9, 11, 13
PROJ_N_BLOCKS = 14


def _permute_w_in(w_in):
    p = [int(v) for v in np.cumsum(IN_SPLITS)]
    a_g = w_in[:, p[1]:p[2]]
    src = np.zeros((PROJ_COL_TILE,), np.int32)
    valid = np.zeros((PROJ_COL_TILE,), bool)
    for g in range(ATT_GROUPS):
        for br in range(N_BRANCH):
            for hp in range(ATT_HPG):
                src[g * LANES + br * ATT_HPG + hp] = br * ATT_HEADS + g * ATT_HPG + hp
                valid[g * LANES + br * ATT_HPG + hp] = True
    gate = jnp.where(jnp.asarray(valid)[None, :], a_g[:, src], 0.0)
    return jnp.concatenate([w_in[:, :p[1]], w_in[:, p[2]:], gate], axis=1).astype(BF16)


def _pack_kv(k, v):
    parts = []
    for g in range(ATT_GROUPS):
        parts.append(k[:, g * HEAD_DIM:(g + 1) * HEAD_DIM])
        parts.append(v[:, g * HEAD_DIM:(g + 1) * HEAD_DIM])
    return jnp.concatenate(parts, axis=1).astype(BF16)


def _proj_kernel(x_ref, gain_ref, w_ref, bd_ref, qg_ref, kg_ref, lbp_ref,
                 q_ref, cmp_ref, slc_ref, slcp_ref, win_ref, winp_ref,
                 hq_ref, lf_ref, hk_ref, hv_ref, hog_ref, gate_ref, xn_ref):
    c = pl.program_id(1)

    @pl.when(c == 0)
    def _():
        x = x_ref[...]
        y = x * lax.rsqrt(jnp.mean(x * x, axis=-1, keepdims=True) + EPS)
        xn_ref[...] = (y * gain_ref[...]).astype(BF16)

    acc = jnp.dot(xn_ref[...], w_ref[...], preferred_element_type=F32)
    half = KV_WIDTH

    @pl.when(c < PB_CMP)
    def _():
        q_ref[...] = _group_rms(acc, bd_ref[...], qg_ref[...], HEAD_DIM)

    @pl.when(c == PB_CMP)
    def _():
        cmp_ref[...] = acc

    def kv_branch(rows_ref, pack_ref, br):
        k = _group_rms(acc[:, :half], bd_ref[:half, :half], kg_ref[br - 1:br, :], HEAD_DIM)
        v = acc[:, half:]
        rows_ref[:, :half] = k
        rows_ref[:, half:] = v
        pack_ref[...] = _pack_kv(k, v)

    @pl.when(c == PB_SLC)
    def _():
        kv_branch(slc_ref, slcp_ref, 1)

    @pl.when(c == PB_WIN)
    def _():
        kv_branch(win_ref, winp_ref, 2)

    @pl.when((c >= PB_HQ) & (c < PB_HF))
    def _():
        hq_ref[...] = acc

    @pl.when((c >= PB_HF) & (c < PB_HI))
    def _():
        p = lbp_ref[...]
        e = jnp.exp(p - jnp.max(p, axis=0, keepdims=True))
        lb = e[0:1, :] / jnp.sum(e, axis=0, keepdims=True)
        lf_ref[...] = jnp.log(lb + (1.0 - lb) * jax.nn.sigmoid(acc))
        hk_ref[...] = (1.0 - lb) * jax.nn.sigmoid(-acc)

    @pl.when((c >= PB_HI) & (c < PB_HG))
    def _():
        hv_ref[...] = acc

    @pl.when((c >= PB_HG) & (c < PB_GATE))
    def _():
        hog_ref[...] = acc

    @pl.when(c == PB_GATE)
    def _():
        gate_ref[...] = jax.nn.sigmoid(acc)


def _project_all(y, gain, w_perm, q_gain, k_gain, lb_logits):
    n, d = y.shape
    tm, tc = PROJ_ROW_TILE, PROJ_COL_TILE
    assert n % tm == 0 and DEPTH == 1
    bd = jnp.asarray(_block_diag_ones(tc, HEAD_DIM), BF16)

    def two(first):
        return lambda i, c: (i, jnp.clip(c - first, 0, 1))

    one = lambda i, c: (i, 0)
    wide = lambda dt: jax.ShapeDtypeStruct((n, 2 * tc), dt)
    narrow = lambda dt: jax.ShapeDtypeStruct((n, tc), dt)
    out_shape = (wide(F32), narrow(F32), narrow(F32), narrow(BF16), narrow(F32), narrow(BF16),
                 wide(F32), wide(F32), wide(F32), wide(F32), wide(F32), narrow(F32))
    blk = lambda f: pl.BlockSpec((tm, tc), f)
    out_specs = (blk(two(PB_Q)), blk(one), blk(one), blk(one), blk(one), blk(one),
                 blk(two(PB_HQ)), blk(two(PB_HF)), blk(two(PB_HF)), blk(two(PB_HI)), blk(two(PB_HG)), blk(one))
    return pl.pallas_call(
        _proj_kernel,
        grid=(n // tm, PROJ_N_BLOCKS),
        in_specs=[
            pl.BlockSpec((tm, d), lambda i, c: (i, 0)),
            pl.BlockSpec((1, d), lambda i, c: (0, 0)),
            pl.BlockSpec((d, tc), lambda i, c: (0, c)),
            pl.BlockSpec((tc, tc), lambda i, c: (0, 0)),
            pl.BlockSpec((1, tc), lambda i, c: (0, jnp.clip(c, 0, 1))),
            pl.BlockSpec((2, KV_WIDTH), lambda i, c: (0, 0)),
            pl.BlockSpec((DEPTH + 1, tc), lambda i, c: (0, jnp.clip(c - PB_HF, 0, 1))),
        ],
        out_specs=out_specs,
        out_shape=out_shape,
        scratch_shapes=[pltpu.VMEM((tm, d), BF16)],
        compiler_params=_cparams("arbitrary", "arbitrary"),
        name="proj",
    )(y, gain.reshape(1, d), w_perm, bd, q_gain.reshape(1, ATT_WIDTH),
      k_gain[1:].reshape(2, KV_WIDTH), lb_logits)


def _hgrn_kernel(q_ref, k_ref, v_ref, lf_ref, tri_ref, ones_ref, s0_ref, o_ref, st_ref,
                 qe_scr, kd_scr, vt_scr, dec_scr, oi_scr, *, t_len, ch, has_state):
    nj = t_len // ch
    shape3 = (nj, ch, HG_KEY)
    q3, k3, v3, lf3 = q_ref[...], k_ref[...], v_ref[...], lf_ref[...]
    tl = lax.broadcasted_iota(jnp.int32, shape3, 1)

    def row(x3, s):
        return jnp.broadcast_to(x3[:, s:s + 1, :], shape3)

    if t_len >= 256:
        lf2 = lf3.reshape(t_len, HG_KEY)
        parts = []
        for r0 in range(0, t_len, 256):
            x = lf2[r0:r0 + 256]
            hi = x.astype(BF16)
            r1 = x - hi.astype(F32)
            mid = r1.astype(BF16)
            lo = (r1 - mid.astype(F32)).astype(BF16)
            tri = tri_ref[...]
            parts.append(jnp.dot(tri, hi, preferred_element_type=F32)
                         + jnp.dot(tri, mid, preferred_element_type=F32)
                         + jnp.dot(tri, lo, preferred_element_type=F32))
        b3 = jnp.concatenate(parts, axis=0).reshape(shape3)
    else:
        b3 = jnp.zeros(shape3, F32)
        for s in range(ch):
            b3 = b3 + jnp.where(tl >= s, row(lf3, s), 0.0)

    bl3 = row(b3, ch - 1)
    qe3 = q3 * jnp.exp(b3)
    kd3 = k3 * jnp.exp(bl3 - b3)
    dec_scr[...] = jnp.exp(b3[:, ch - 1:ch, :])
    blocked = t_len % LANES == 0
    if blocked:
        nb = t_len // LANES
        qe2, kd2, v2 = (x.reshape(t_len, HG_KEY) for x in (qe3, kd3, v3))
        for m in range(nb):
            blk = slice(m * LANES, (m + 1) * LANES)
            qe_scr[m] = qe2[blk].T.astype(BF16)
            vt_scr[m] = v2[blk].T.astype(BF16)
            kd_scr[m] = kd2[blk].astype(BF16)
    else:
        qe_scr[...] = qe3.astype(BF16)
        kd_scr[...] = kd3.astype(BF16)

    ones = ones_ref[...]
    od = jnp.zeros(shape3, F32)
    for s in range(ch):
        w = q3 * row(k3, s) * jnp.exp(jnp.where(tl >= s, b3 - row(b3, s), NEG))
        a = jnp.dot(w.reshape(t_len, HG_KEY).astype(BF16), ones, preferred_element_type=F32)
        od = od + a.reshape(shape3) * row(v3, s)
    o_ref[...] = od

    if has_state:
        st0 = s0_ref[0, 0].T
    else:
        st0 = jnp.zeros((HG_VAL, HG_KEY), F32)

    if blocked:
        cpb = LANES // ch
        lane_chunk = lax.broadcasted_iota(jnp.int32, (HG_KEY, LANES), 1) >> (ch.bit_length() - 1)
        keep = [jnp.where(lane_chunk == r, 1.0, 0.0).astype(BF16) for r in range(cpb)]

        def body(m, st):
            qet, vt, kd = qe_scr[m], vt_scr[m], kd_scr[m]
            ot = jnp.zeros((HG_VAL, LANES), F32)
            for r in range(cpb):
                ot = ot + jnp.dot(st.astype(BF16), qet * keep[r], preferred_element_type=F32)
                ut = jnp.dot(vt * keep[r], kd, preferred_element_type=F32)
                st = st * dec_scr[m * cpb + r] + ut
            oi_scr[m] = ot
            return st

        st = lax.fori_loop(0, t_len // LANES, body, st0)
        for m in range(t_len // LANES):
            o_ref[m * cpb:(m + 1) * cpb] = o_ref[m * cpb:(m + 1) * cpb] + oi_scr[m].T.reshape(cpb, ch, HG_VAL)
    else:
        def body(j, st):
            oi_scr[j] = lax.dot_general(qe_scr[j], st.astype(BF16), NT_DIMS, preferred_element_type=F32)
            ut = lax.dot_general(v_ref[j].astype(BF16), kd_scr[j], (((0,), (0,)), ((), ())),
                                 preferred_element_type=F32)
            return st * dec_scr[j] + ut

        st = lax.fori_loop(0, nj, body, st0, unroll=True)
        o_ref[...] = o_ref[...] + oi_scr[...]
    st_ref[0, 0] = st.T


def _hgrn(hq, hk, hv, lf, s0, *, n_seq, t_len, ch, row0):
    n = hq.shape[0]
    nj = t_len // ch
    assert row0 % t_len == 0 and t_len % ch == 0 and ch % 8 == 0
    blk0 = row0 // t_len
    r3 = lambda a: a.reshape(n // ch, ch, HG_WIDTH)
    has_state = s0 is not None
    if not has_state:
        s0 = jnp.zeros((1, 1, HG_KEY, HG_VAL), F32)
    tri_n = 256 if t_len >= 256 else 8
    i = np.arange(tri_n)
    tri = jnp.asarray(((i[:, None] // ch == i[None, :] // ch) & (i[:, None] >= i[None, :])).astype(np.float32), BF16)
    ones = jnp.ones((HG_KEY, HG_KEY), BF16)
    blk_shape = (t_len // LANES, LANES, HG_KEY) if t_len % LANES == 0 else (nj, ch, HG_KEY)
    hps = HG_HEADS if nj == 1 else 1
    seq = pl.BlockSpec((nj, ch, hps * HG_KEY), lambda b, h: (blk0 + b, 0, h))
    state_in = pl.BlockSpec((1, hps, HG_KEY, HG_VAL), (lambda b, h: (b, h, 0, 0)) if has_state else (lambda b, h: (0, 0, 0, 0)))

    def body(q_ref, k_ref, v_ref, lf_ref, tri_ref, ones_ref, s0_ref, o_ref, st_ref, *scratch):
        for h in range(hps):
            ln = slice(h * HG_KEY, (h + 1) * HG_KEY)
            _hgrn_kernel(q_ref.at[:, :, ln], k_ref.at[:, :, ln], v_ref.at[:, :, ln], lf_ref.at[:, :, ln], tri_ref,
                         ones_ref, s0_ref.at[:, h:h + 1], o_ref.at[:, :, ln], st_ref.at[:, h:h + 1], *scratch,
                         t_len=t_len, ch=ch, has_state=has_state)

    o, st = pl.pallas_call(
        body,
        grid=(n_seq, HG_HEADS // hps),
        in_specs=[seq, seq, seq, seq,
                  pl.BlockSpec((tri_n, tri_n), lambda b, h: (0, 0)),
                  pl.BlockSpec((HG_KEY, HG_KEY), lambda b, h: (0, 0)),
                  state_in],
        out_specs=(pl.BlockSpec((nj, ch, hps * HG_VAL), lambda b, h: (b, 0, h)),
                   pl.BlockSpec((1, hps, HG_KEY, HG_VAL), lambda b, h: (b, h, 0, 0))),
        out_shape=(jax.ShapeDtypeStruct((n_seq * nj, ch, HG_WIDTH), F32),
                   jax.ShapeDtypeStruct((n_seq, HG_HEADS, HG_KEY, HG_VAL), F32)),
        scratch_shapes=[pltpu.VMEM(blk_shape, BF16), pltpu.VMEM(blk_shape, BF16), pltpu.VMEM(blk_shape, BF16),
                        pltpu.VMEM((nj, 1, HG_KEY), F32), pltpu.VMEM(blk_shape, F32)],
        compiler_params=_cparams("arbitrary", "arbitrary"),
        name="hgrn",
    )(r3(hq), r3(hk), r3(hv), r3(lf), tri, ones, s0)
    return o.reshape(n_seq * t_len, HG_WIDTH), st


def _mixout_kernel(y_ref, oap_ref, oas_ref, ohp_ref, ohs_ref, og_ref, ag_ref, hgain_ref, w_ref, o_ref):
    is_sample = pl.program_id(0) == pl.num_programs(0) - 1
    oa = jnp.where(is_sample, oas_ref[...], oap_ref[...])
    a = oa * lax.rsqrt(jnp.mean(oa * oa, axis=-1, keepdims=True) + EPS) * ag_ref[...]
    oh = jnp.where(is_sample, ohs_ref[...], ohp_ref[...])
    hs = []
    for h in range(HG_HEADS):
        x = oh[:, h * HG_VAL:(h + 1) * HG_VAL]
        hs.append(x * lax.rsqrt(jnp.mean(x * x, axis=-1, keepdims=True) + EPS))
    og = og_ref[...]
    hh = jnp.concatenate(hs, axis=1) * hgain_ref[...] * (og * jax.nn.sigmoid(og))
    m = (jnp.dot(a.astype(BF16), w_ref[:ATT_WIDTH, :], preferred_element_type=F32)
         + jnp.dot(hh.astype(BF16), w_ref[ATT_WIDTH:, :], preferred_element_type=F32))
    o_ref[...] = y_ref[...] + m


def _mixer_out(y, o_att_p, o_att_s, o_hg_p, o_hg_s, og, attn_gain, hg_gain, w_out):
    n, d = y.shape
    tm = OUT_ROW_TILE
    n_p = o_att_p.shape[0]
    assert n % tm == 0 and n_p % tm == 0 and o_att_s.shape[0] == tm and n == n_p + tm
    row = lambda w: pl.BlockSpec((tm, w), lambda i: (i, 0))
    prompt_row = lambda w: pl.BlockSpec((tm, w), lambda i: (jnp.minimum(i, n_p // tm - 1), 0))
    const = lambda s: pl.BlockSpec(s, lambda i: (0, 0))
    return pl.pallas_call(
        _mixout_kernel,
        grid=(n // tm,),
        in_specs=[row(d), prompt_row(ATT_WIDTH), const((tm, ATT_WIDTH)), prompt_row(HG_WIDTH), const((tm, HG_WIDTH)),
                  row(HG_WIDTH), const((1, ATT_WIDTH)), const((1, HG_WIDTH)), const((MIX_WIDTH, d))],
        out_specs=row(d),
        out_shape=jax.ShapeDtypeStruct((n, d), F32),
        compiler_params=_cparams("arbitrary"),
        name="mixout",
    )(y, o_att_p, o_att_s, o_hg_p, o_hg_s, og, attn_gain.reshape(1, ATT_WIDTH), hg_gain.reshape(1, HG_WIDTH), w_out)


def _bias_kernel(tbl_ref, o_ref, *, a0, ar, ac, rows_blk):
    h = pl.program_id(0)
    rb = pl.program_id(1)
    shape = o_ref.shape[1:]
    r = lax.broadcasted_iota(jnp.int32, shape, 0) + rb * rows_blk
    c = lax.broadcasted_iota(jnp.int32, shape, 1)
    n = a0 + ar * r + ac * c
    out = jnp.full(shape, tbl_ref[0, h], F32)
    for k in range(1, N_BUCKETS):
        out = jnp.where(n >= T5_THRESHOLDS[k], tbl_ref[k, h], out)
    o_ref[0] = out


def _bias_table(tbl, rows, cols, a0, ar, ac):
    rows_blk = math.gcd(rows, 512)
    assert rows_blk % 8 == 0 and cols % LANES == 0
    return pl.pallas_call(
        functools.partial(_bias_kernel, a0=a0, ar=ar, ac=ac, rows_blk=rows_blk),
        grid=(ATT_HEADS, rows // rows_blk),
        in_specs=[pl.BlockSpec(memory_space=pltpu.SMEM)],
        out_specs=pl.BlockSpec((1, rows_blk, cols), lambda h, rb: (h, rb, 0)),
        out_shape=jax.ShapeDtypeStruct((ATT_HEADS, rows, cols), F32),
        compiler_params=_cparams("arbitrary", "arbitrary"),
        name="t5_bias",
    )(tbl)


def _compress_weights(w1, b1, w2, b2):
    w = w1.reshape(2, CMP_RATIO, CMP_STRIDE, HEAD_DIM, CMP_HIDDEN)
    z = jnp.zeros_like(w)
    top = jnp.concatenate([w, z], axis=-1)
    bot = jnp.concatenate([z, w], axis=-1)
    wpad = jnp.stack([top, bot], axis=3)
    wpad = wpad.reshape(2 * CMP_RATIO, CMP_STRIDE * LANES, 2 * CMP_HIDDEN).astype(BF16)
    z2 = jnp.zeros_like(w2)
    w2pad = jnp.concatenate([jnp.concatenate([w2, z2], axis=-1), jnp.concatenate([z2, w2], axis=-1)], axis=1)
    return wpad, jnp.concatenate([b1, b1], axis=-1), w2pad.astype(BF16), jnp.concatenate([b2, b2], axis=-1)


def _compress_kernel(*refs, n_in, paged, packed):
    if paged:
        refs = refs[1:]
    in_refs = refs[:n_in]
    perm_ref, wpad_ref, b1_ref, w2pad_ref, b2_ref, kg_ref, bd_ref, out_ref, xs_ref = refs[n_in:]
    u = pl.program_id(1)
    n_chunk = CMP_UNITS * CMP_UNIT_ROWS // CMP_STRIDE
    perm = perm_ref[...]

    for t in range(CMP_UNIT_ROWS // 256):
        if paged:
            x = jnp.concatenate([in_refs[2 * t][0], in_refs[2 * t + 1][0]], axis=0)
        else:
            x = in_refs[0][t * 256:(t + 1) * 256, :]
        y = jnp.dot(perm, x.astype(BF16), preferred_element_type=F32).astype(BF16)
        base = pl.multiple_of(u * (CMP_UNIT_ROWS // CMP_STRIDE) + t * 16, 16)
        for s in range(CMP_STRIDE):
            xs_ref[s, pl.ds(base, 16), :] = y[s * 16:(s + 1) * 16, :]

    @pl.when(u == CMP_UNITS - 1)
    def _():
        halves = []
        for kv in range(2):
            outs = []
            for j in range(2):
                col = (kv * 2 + j) * LANES
                lhs = jnp.concatenate([xs_ref[s, :, col:col + LANES] for s in range(CMP_STRIDE)], axis=1)
                h0 = jnp.dot(lhs, wpad_ref[kv * CMP_RATIO + 0], preferred_element_type=F32)
                h1 = jnp.dot(lhs, wpad_ref[kv * CMP_RATIO + 1], preferred_element_type=F32)
                h = b1_ref[kv:kv + 1, :] + h0 + pltpu.roll(h1, n_chunk - 1, axis=0)
                a = (h * jax.nn.sigmoid(h)).astype(BF16)
                outs.append(jnp.dot(a, w2pad_ref[kv], preferred_element_type=F32) + b2_ref[kv:kv + 1, :])
            halves.append(jnp.concatenate(outs, axis=1))
        kc = _group_rms(halves[0], bd_ref[...], kg_ref[...], HEAD_DIM)
        vc = halves[1]
        if packed:
            pk = _pack_kv(kc, vc)
            for g in range(ATT_GROUPS):
                out_ref[0, g] = pk[:, g * LANES:(g + 1) * LANES]
        else:
            out_ref[0, :, :KV_WIDTH] = kc.astype(BF16)
            out_ref[0, :, KV_WIDTH:] = vc.astype(BF16)


def _compress(rows, page_table, cw, k_gain0, *, paged):
    wpad, b1, w2pad, b2 = cw
    n_chunk = CMP_UNITS * CMP_UNIT_ROWS // CMP_STRIDE
    i = np.arange(256)
    perm = np.zeros((256, 256), np.float32)
    perm[(i % 16) * 16 + i // 16, i] = 1.0
    consts = (jnp.asarray(perm, BF16), wpad, b1, w2pad, b2, k_gain0.reshape(1, KV_WIDTH),
              jnp.asarray(_block_diag_ones(KV_WIDTH, HEAD_DIM), BF16))
    if paged:
        n_grp = page_table.shape[0]
        pages_per_unit = CMP_UNIT_ROWS // PAGE_SIZE
        n_in = pages_per_unit
        const = lambda a: pl.BlockSpec(a.shape, lambda b, u, pt, nd=a.ndim: (0,) * nd)
        in_specs = [pl.BlockSpec((1, PAGE_SIZE, KV_ROW), lambda b, u, pt, k=k: (pt[b, u * pages_per_unit + k], 0, 0))
                    for k in range(n_in)]
        out_spec = pl.BlockSpec((1, n_chunk, KV_ROW), lambda b, u, pt: (b, 0, 0))
        out_shape = jax.ShapeDtypeStruct((n_grp, n_chunk, KV_ROW), BF16)
        args = (page_table,) + (rows,) * n_in + consts
        nsp = 1
    else:
        n_grp, n_in = 1, 1
        const = lambda a: pl.BlockSpec(a.shape, lambda b, u, nd=a.ndim: (0,) * nd)
        in_specs = [pl.BlockSpec((CMP_UNIT_ROWS, KV_ROW), lambda b, u: (u, 0))]
        out_spec = pl.BlockSpec((1, ATT_GROUPS, n_chunk, LANES), lambda b, u: (0, 0, 0, 0))
        out_shape = jax.ShapeDtypeStruct((1, ATT_GROUPS, n_chunk, LANES), BF16)
        args = (rows,) + consts
        nsp = 0
    grid_spec = pltpu.PrefetchScalarGridSpec(
        num_scalar_prefetch=nsp, grid=(n_grp, CMP_UNITS),
        in_specs=in_specs + [const(a) for a in consts],
        out_specs=out_spec,
        scratch_shapes=[pltpu.VMEM((CMP_STRIDE, n_chunk, KV_ROW), BF16)])
    return pl.pallas_call(
        functools.partial(_compress_kernel, n_in=n_in, paged=paged, packed=not paged),
        grid_spec=grid_spec, out_shape=out_shape,
        compiler_params=_cparams("arbitrary", "arbitrary"),
        name="compress_paged" if paged else "compress",
    )(*args)


PAGES_PER_STEP = CMP_UNIT_ROWS // PAGE_SIZE


def _cache_view(cache):
    return jnp.transpose(cache, (0, 2, 3, 4, 1))


def _compress_paged_kernel(pt_ref, *refs):
    del pt_ref
    page_refs = refs[:PAGES_PER_STEP]
    pick_ref, w1_ref, b1_ref, w2_ref, b2_ref, kg_ref, bd_ref, out_ref, xs_ref = refs[PAGES_PER_STEP:]
    u = pl.program_id(1)
    n_chunk = CMP_UNITS * CMP_UNIT_ROWS // CMP_STRIDE
    pick = pick_ref[...]
    zero = jnp.zeros((HEAD_DIM, 2 * PAGE_SIZE), BF16)

    for t in range(PAGES_PER_STEP // 2):
        base = pl.multiple_of(u * (CMP_UNIT_ROWS // CMP_STRIDE) + t * 16, 16)
        for kv in range(2):
            for g in range(ATT_GROUPS):
                kt = jnp.concatenate([page_refs[2 * t][0, kv, g], page_refs[2 * t + 1][0, kv, g]], axis=1).astype(BF16)
                rhs = jnp.concatenate([jnp.concatenate([kt, zero], axis=1),
                                       jnp.concatenate([zero, kt], axis=1)], axis=0)
                y = lax.dot_general(pick, rhs, NT_DIMS, preferred_element_type=F32).astype(BF16)
                for s2 in range(CMP_STRIDE // 2):
                    xs_ref[kv * ATT_GROUPS + g, s2, pl.ds(base, 16), :] = y[s2 * 16:(s2 + 1) * 16, :]

    @pl.when(u == CMP_UNITS - 1)
    def _():
        halves = []
        for kv in range(2):
            lhs = jnp.concatenate(
                [jnp.concatenate([xs_ref[kv * ATT_GROUPS + g, s2] for s2 in range(CMP_STRIDE // 2)], axis=1)
                 for g in range(ATT_GROUPS)], axis=0)
            h0 = jnp.dot(lhs, w1_ref[kv * CMP_RATIO + 0], preferred_element_type=F32)
            h1 = jnp.dot(lhs, w1_ref[kv * CMP_RATIO + 1], preferred_element_type=F32)
            h = b1_ref[kv:kv + 1, :] + h0 + pltpu.roll(h1, ATT_GROUPS * n_chunk - 1, axis=0)
            a = (h * jax.nn.sigmoid(h)).astype(BF16)
            o = b2_ref[kv:kv + 1, :]
            for g in range(ATT_GROUPS):
                o = o + jnp.dot(a[g * n_chunk:(g + 1) * n_chunk], w2_ref[kv * ATT_GROUPS + g], preferred_element_type=F32)
            halves.append(o)
        out_ref[0, :, :KV_WIDTH] = _group_rms(halves[0], bd_ref[...], kg_ref[...], HEAD_DIM).astype(BF16)
        out_ref[0, :, KV_WIDTH:] = halves[1].astype(BF16)


def _compress_paged(cache_t, page_table, w1, b1, w2, b2, k_gain0):
    n_seq = page_table.shape[0]
    n_chunk = CMP_UNITS * CMP_UNIT_ROWS // CMP_STRIDE
    r = np.arange(LANES)
    s2, c = r // 16, r % 16
    pick = np.zeros((LANES, 2, 2 * PAGE_SIZE), np.float32)
    for half in range(2):
        pick[r, half, CMP_STRIDE * c + 2 * s2 + half] = 1.0
    w2p = jnp.zeros((2, ATT_GROUPS, CMP_HIDDEN, KV_WIDTH), F32)
    for g in range(ATT_GROUPS):
        w2p = w2p.at[:, g, :, g * HEAD_DIM:(g + 1) * HEAD_DIM].set(w2)
    consts = (jnp.asarray(pick.reshape(LANES, 4 * PAGE_SIZE), BF16),
              w1.reshape(2 * CMP_RATIO, CMP_STRIDE * HEAD_DIM, CMP_HIDDEN).astype(BF16), b1,
              w2p.reshape(2 * ATT_GROUPS, CMP_HIDDEN, KV_WIDTH).astype(BF16), jnp.tile(b2, (1, ATT_GROUPS)),
              k_gain0.reshape(1, KV_WIDTH), jnp.asarray(_block_diag_ones(KV_WIDTH, HEAD_DIM), BF16))
    const = lambda a: pl.BlockSpec(a.shape, lambda b, u, pt, nd=a.ndim: (0,) * nd)
    pages = [pl.BlockSpec((1, 2, ATT_GROUPS, HEAD_DIM, PAGE_SIZE),
                          lambda b, u, pt, k=k: (pt[b, u * PAGES_PER_STEP + k], 0, 0, 0, 0))
             for k in range(PAGES_PER_STEP)]
    grid_spec = pltpu.PrefetchScalarGridSpec(
        num_scalar_prefetch=1, grid=(n_seq, CMP_UNITS),
        in_specs=pages + [const(a) for a in consts],
        out_specs=pl.BlockSpec((1, n_chunk, KV_ROW), lambda b, u, pt: (b, 0, 0)),
        scratch_shapes=[pltpu.VMEM((2 * ATT_GROUPS, CMP_STRIDE // 2, n_chunk, LANES), BF16)])
    return pl.pallas_call(
        _compress_paged_kernel, grid_spec=grid_spec,
        out_shape=jax.ShapeDtypeStruct((n_seq, n_chunk, KV_ROW), BF16),
        compiler_params=_cparams("arbitrary", "arbitrary"),
        name="compress_paged",
    )(page_table, *([cache_t] * PAGES_PER_STEP), *consts)


def _block_overlap(n_cmp, n_blk):
    cs = np.arange(n_cmp)[:, None] * CMP_STRIDE
    bs = np.arange(n_blk)[None, :] * SEL_BLOCK
    ov = np.minimum(cs + CMP_BLOCK, bs + SEL_BLOCK) - np.maximum(cs, bs)
    return (np.clip(ov, 0, None) / CMP_BLOCK).astype(np.float32)


def _overlap_padded(n_cmp, n_blk, rows, cols):
    ov = np.zeros((rows, cols), np.float32)
    ov[:n_cmp, :n_blk] = _block_overlap(n_cmp, n_blk)
    return jnp.asarray(ov, BF16)


def _softmax_rows(s, mask):
    s = jnp.where(mask, s, NEG)
    m = jnp.max(s, axis=1, keepdims=True)
    e = jnp.where(mask, jnp.exp(s - m), 0.0)
    d = jnp.sum(e, axis=1, keepdims=True)
    return e / jnp.where(d > 0, d, 1.0)


def _select_blocks(imp, qpos, n_blk, axis):
    blk = lax.broadcasted_iota(jnp.int32, imp.shape, axis)
    cur = qpos >> SEL_SHIFT
    forced = (blk == 0) | (blk == cur) | (blk == cur - 1)
    valid = blk * SEL_BLOCK <= qpos
    score = jnp.where(forced, ATT_HPG + 1.0, jnp.where(valid, imp, -1.0))
    ahead = jnp.zeros(imp.shape, F32)
    for m in range(n_blk):
        sm = jnp.broadcast_to(score[m:m + 1, :] if axis == 0 else score[:, m:m + 1], imp.shape)
        tie = jnp.where(blk > m, 1.0, 0.0)
        ahead = ahead + jnp.where(sm > score, 1.0, jnp.where(sm == score, tie, 0.0))
    return jnp.where(blk < n_blk, jnp.where(ahead < N_SEL, 1.0, 0.0), 0.0)


def _online_update(s, mask, v, m_ref, l_ref, acc_ref, idx=None):
    at = (lambda r: r) if idx is None else (lambda r: r.at[idx])
    m_r, l_r, acc_r = at(m_ref), at(l_ref), at(acc_ref)
    s = jnp.where(mask, s, NEG)
    m_prev = m_r[...]
    m_new = jnp.maximum(m_prev, jnp.max(s, axis=1, keepdims=True))
    alpha = jnp.exp(m_prev - m_new)
    e = jnp.where(mask, jnp.exp(s - m_new), 0.0)
    l_r[...] = alpha * l_r[...] + jnp.sum(e, axis=1, keepdims=True)
    pv = jnp.dot(e.astype(BF16), v, preferred_element_type=F32)
    reps = acc_r.shape[-1] // LANES
    a_w = alpha if reps == 1 else jnp.concatenate([alpha] * reps, axis=1)
    acc_r[...] = a_w * acc_r[...] + pv
    m_r[...] = m_new


SEL_SHIFT = SEL_BLOCK.bit_length() - 1
NT_DIMS = (((1,), (1,)), ((), ()))


SLC_UNROLL = 4
TOEP_TILES = min(SEQ // LANES, -(-(T5_THRESHOLDS[-1] + LANES - 1) // LANES) + 1)


def _nsa_prompt_kernel(q_ref, gate_ref, kvc_ref, slc_ref, win_ref, bcmp_ref, toep_ref, ov_ref, o_ref,
                       s_scr, m_scr, l_scr, acc_scr):
    i = pl.program_id(2)
    tile = (Q_BLOCK, LANES)
    rows = ATT_HPG * Q_BLOCK
    lane = lax.broadcasted_iota(jnp.int32, tile, 1)
    sub = lax.broadcasted_iota(jnp.int32, tile, 0)
    low = lane < HEAD_DIM
    qpos = i * Q_BLOCK + sub
    head = lambda x, p: x[p * Q_BLOCK:(p + 1) * Q_BLOCK]

    qa = q_ref[...] * (HEAD_DIM ** -0.5)
    qp = []
    for pair in range(ATT_HPG // 2):
        x = qa[:, pair * LANES:(pair + 1) * LANES]
        qp.append(jnp.where(low, x, 0.0))
        qp.append(jnp.where(low, pltpu.roll(x, HEAD_DIM, axis=1), 0.0))
    qs = jnp.concatenate(qp, axis=0).astype(BF16)

    kvc = kvc_ref[0, 0]
    s = lax.dot_general(qs, kvc, NT_DIMS, preferred_element_type=F32) + bcmp_ref[...].reshape(rows, LANES)
    cmask1 = (lane * CMP_STRIDE + (CMP_BLOCK - 1)) <= qpos
    pr = _softmax_rows(s, jnp.concatenate([cmask1] * ATT_HPG, axis=0))
    o_cmp = jnp.dot(pr.astype(BF16), kvc, preferred_element_type=F32)
    hi, lo = _split2(sum(head(pr, p) for p in range(ATT_HPG)))
    ov = ov_ref[...]
    imp = jnp.dot(hi, ov, preferred_element_type=F32) + jnp.dot(lo, ov, preferred_element_type=F32)
    n_blk = SEQ // SEL_BLOCK
    imp_t = imp.T[:n_blk, :]
    qpos_t = i * Q_BLOCK + lax.broadcasted_iota(jnp.int32, (n_blk, Q_BLOCK), 1)
    sel_t = _select_blocks(imp_t, qpos_t, n_blk, axis=0).astype(BF16)
    blk_row = lax.broadcasted_iota(jnp.int32, (n_blk, LANES), 0)
    blk_of_key = lax.broadcasted_iota(jnp.int32, (n_blk, LANES), 1) >> SEL_SHIFT
    last_tile = SEQ // LANES - 1

    def key_tile(kv_ref, j):
        jc = jnp.clip(j, 0, last_tile)
        return kv_ref[pl.ds(pl.multiple_of(jc * LANES, LANES), LANES), :]

    def masked_scores(kv, j, mask):
        sc = lax.dot_general(qs, kv, NT_DIMS, preferred_element_type=F32)
        row0 = pl.multiple_of(jnp.clip(i - j, 0, TOEP_TILES - 1) * LANES, LANES)
        return [jnp.where(mask, head(sc, p) + toep_ref[p, pl.ds(row0, LANES), :], NEG) for p in range(ATT_HPG)]

    def weights_times_v(scores, m, kv):
        es = [jnp.exp(scores[p] - m[p]) for p in range(ATT_HPG)]
        return es, jnp.dot(jnp.concatenate(es, axis=0).astype(BF16), kv, preferred_element_type=F32)

    n_win = WINDOW // LANES + 1
    win_kv, win_s = [], []
    for w in range(n_win):
        j = i - (n_win - 1) + w
        dist = (i - j) * LANES + sub - lane
        inside = jnp.where(dist >= 0, jnp.where(dist < WINDOW, 1.0, 0.0), 0.0)
        win_kv.append(key_tile(win_ref, j))
        win_s.append(masked_scores(win_kv[w], j, jnp.where(j >= 0, inside, 0.0) > 0.5))
    m_w = []
    for p in range(ATT_HPG):
        m_el = functools.reduce(jnp.maximum, [win_s[w][p] for w in range(n_win)])
        m_w.append(jnp.max(m_el, axis=1, keepdims=True))
    acc_w = jnp.zeros((rows, LANES), F32)
    l_w = [jnp.zeros(tile, F32) for _ in range(ATT_HPG)]
    for w in range(n_win):
        es, pv = weights_times_v(win_s[w], m_w, win_kv[w])
        acc_w = acc_w + pv
        l_w = [l_w[p] + es[p] for p in range(ATT_HPG)]
    o_win = [head(acc_w, p) / jnp.sum(l_w[p], axis=1, keepdims=True) for p in range(ATT_HPG)]

    n_steps = (i + SLC_UNROLL) >> (SLC_UNROLL.bit_length() - 1)
    m_scr[...] = jnp.full(m_scr.shape, NEG, F32)

    def slc_scores(jj, carry):
        sps = []
        for r in range(SLC_UNROLL):
            j = SLC_UNROLL * jj + r
            dist = (i - j) * LANES + sub - lane
            expand = jnp.where(blk_row == 2 * j + blk_of_key, 1.0, 0.0).astype(BF16)
            picked = lax.dot_general(sel_t, expand, (((0,), (0,)), ((), ())), preferred_element_type=F32)
            sp = masked_scores(key_tile(slc_ref, j), j, jnp.where(dist >= 0, picked, 0.0) > 0.5)
            for p in range(ATT_HPG):
                s_scr[j, p] = sp[p]
            sps.append(sp)
        for p in range(ATT_HPG):
            m_scr[p] = jnp.maximum(m_scr[p], functools.reduce(jnp.maximum, [sp[p] for sp in sps]))
        return carry

    lax.fori_loop(0, n_steps, slc_scores, 0)
    for p in range(ATT_HPG):
        m_scr[p] = jnp.broadcast_to(jnp.max(m_scr[p], axis=1, keepdims=True), tile)
    l_scr[...] = jnp.zeros(l_scr.shape, F32)
    acc_scr[...] = jnp.zeros(acc_scr.shape, F32)

    def slc_weights(jj, carry):
        m = [m_scr[p] for p in range(ATT_HPG)]
        pvs, ess = [], []
        for r in range(SLC_UNROLL):
            j = SLC_UNROLL * jj + r
            es, pv = weights_times_v([s_scr[j, p] for p in range(ATT_HPG)], m, key_tile(slc_ref, j))
            pvs.append(pv)
            ess.append(es)
        acc_scr[...] += functools.reduce(jnp.add, pvs)
        for p in range(ATT_HPG):
            l_scr[p] = l_scr[p] + functools.reduce(jnp.add, [es[p] for es in ess])
        return carry

    lax.fori_loop(0, n_steps, slc_weights, 0)
    o_slc = [head(acc_scr[...], p) / jnp.sum(l_scr[p], axis=1, keepdims=True) for p in range(ATT_HPG)]

    g = gate_ref[...]
    comb = []
    for p in range(ATT_HPG):
        col = lambda br: jnp.broadcast_to(g[:, br * ATT_HPG + p:br * ATT_HPG + p + 1], tile)
        comb.append(col(0) * head(o_cmp, p) + col(1) * o_slc[p] + col(2) * o_win[p])
    for pair in range(ATT_HPG // 2):
        o_ref[:, pair * LANES:(pair + 1) * LANES] = jnp.where(
            low, pltpu.roll(comb[2 * pair], HEAD_DIM, axis=1), comb[2 * pair + 1])


def _nsa_prompt(q, gates, kvc, slc_pack, win_pack, bias_cmp, bias_toep, n_batch=BATCH):
    nqb = SEQ // Q_BLOCK
    ov = _overlap_padded(SEQ // CMP_STRIDE - 1, SEQ // SEL_BLOCK, LANES, LANES)
    gw = ATT_HPG * HEAD_DIM
    return pl.pallas_call(
        _nsa_prompt_kernel,
        grid=(n_batch, ATT_GROUPS, nqb),
        in_specs=[
            pl.BlockSpec((Q_BLOCK, gw), lambda b, g, i: (b * nqb + i, g)),
            pl.BlockSpec((Q_BLOCK, LANES), lambda b, g, i: (b * nqb + i, g)),
            pl.BlockSpec((1, 1, SEQ // CMP_STRIDE, LANES), lambda b, g, i: (0, g, b, 0)),
            pl.BlockSpec((SEQ, LANES), lambda b, g, i: (b, g)),
            pl.BlockSpec((SEQ, LANES), lambda b, g, i: (b, g)),
            pl.BlockSpec((ATT_HPG, Q_BLOCK, LANES), lambda b, g, i: (g, i, 0)),
            pl.BlockSpec((ATT_HPG, TOEP_TILES * LANES, LANES), lambda b, g, i: (g, 0, 0)),
            pl.BlockSpec((LANES, LANES), lambda b, g, i: (0, 0)),
        ],
        out_specs=pl.BlockSpec((Q_BLOCK, gw), lambda b, g, i: (b * nqb + i, g)),
        out_shape=jax.ShapeDtypeStruct((n_batch * SEQ, ATT_WIDTH), F32),
        scratch_shapes=[pltpu.VMEM((nqb, ATT_HPG, Q_BLOCK, LANES), F32),
                        pltpu.VMEM((ATT_HPG, Q_BLOCK, LANES), F32), pltpu.VMEM((ATT_HPG, Q_BLOCK, LANES), F32),
                        pltpu.VMEM((ATT_HPG * Q_BLOCK, LANES), F32)],
        compiler_params=_cparams("arbitrary", "arbitrary", "arbitrary"),
        name="nsa_prompt",
    )(q, gates, kvc, slc_pack, win_pack, bias_cmp, bias_toep, ov)


SAMPLE_ROWS = ATT_HEADS * DEC_SEQ
SAMPLE_PAGES_PER_STEP = 16
SAMPLE_STEPS = N_PAGES // SAMPLE_PAGES_PER_STEP
SAMPLE_N_CMP = (PAST_LEN + DEC_SEQ - CMP_BLOCK) // CMP_STRIDE + 1
SAMPLE_N_BLK = -(-(PAST_LEN + DEC_SEQ) // SEL_BLOCK)
SAMPLE_BLK_LANES = 2 * LANES


def _nsa_sample_kernel(pt_ref, q_ref, gate_ref, kvc_ref, *rest):
    page_refs = rest[:SAMPLE_PAGES_PER_STEP]
    (slc_new_ref, win_state_ref, win_new_ref, bcmp_ref, bslc_ref, bwin_ref, perm_ref, ov_ref,
     o_ref, qbd_scr, sel_scr, m_scr, l_scr, acc_scr, ocmp_scr, owin_scr) = rest[SAMPLE_PAGES_PER_STEP:]
    del pt_ref
    u = pl.program_id(1)
    tile = (SAMPLE_ROWS, LANES)
    wide = (SAMPLE_ROWS, KV_WIDTH)
    lane = lax.broadcasted_iota(jnp.int32, tile, 1)
    t_row = lax.broadcasted_iota(jnp.int32, tile, 0) & (DEC_SEQ - 1)
    rows_per_group = ATT_HPG * DEC_SEQ
    own = ((lax.broadcasted_iota(jnp.int32, wide, 1) >> (HEAD_DIM.bit_length() - 1))
           == (lax.broadcasted_iota(jnp.int32, wide, 0) >> (rows_per_group.bit_length() - 1)))

    def reset():
        m_scr[...] = jnp.full(m_scr.shape, NEG, F32)
        l_scr[...] = jnp.zeros(l_scr.shape, F32)
        acc_scr[...] = jnp.zeros(acc_scr.shape, F32)

    def finish():
        l = jnp.sum(l_scr[...], axis=1, keepdims=True)
        return acc_scr[...] / jnp.where(l > 0, l, 1.0)

    def attend(tiles):
        qbd = qbd_scr[...]
        scores = []
        m_el = None
        for kt, _, bias, mask in tiles:
            s = jnp.where(mask, jnp.dot(qbd, kt.astype(BF16), preferred_element_type=F32) + bias, NEG)
            scores.append(s)
            m_el = s if m_el is None else jnp.maximum(m_el, s)
        m_prev = m_scr[...]
        m_new = jnp.maximum(m_prev, jnp.max(m_el, axis=1, keepdims=True))
        alpha = jnp.exp(m_prev - m_new)
        l_el = alpha * l_scr[...]
        acc = jnp.concatenate([alpha, alpha], axis=1) * acc_scr[...]
        for (_, vt, _, _), s in zip(tiles, scores):
            e = jnp.exp(s - m_new)
            l_el = l_el + e
            acc = acc + lax.dot_general(e.astype(BF16), vt.astype(BF16), NT_DIMS, preferred_element_type=F32)
        m_scr[...] = m_new
        l_scr[...] = l_el
        acc_scr[...] = acc

    def page_tile(kv4, bias, mask):
        return (kv4[0].reshape(KV_WIDTH, LANES), kv4[1].reshape(KV_WIDTH, LANES), bias, mask)

    def new_tile(ref, bias, mask):
        rows = jnp.concatenate([ref[...], jnp.zeros((LANES - DEC_SEQ, KV_ROW), F32)], axis=0)
        return (rows[:, :KV_WIDTH].T, rows[:, KV_WIDTH:].T, bias, mask)

    @pl.when(u == 0)
    def _():
        q = (q_ref[...] * (HEAD_DIM ** -0.5)).astype(BF16)
        qperm = jnp.dot(q, perm_ref[...], preferred_element_type=F32)
        qfull = jnp.concatenate([qperm[:, p * KV_WIDTH:(p + 1) * KV_WIDTH]
                                 for g in range(ATT_GROUPS) for p in range(ATT_HPG)], axis=0)
        qbd = jnp.where(own, qfull, 0.0).astype(BF16)
        qbd_scr[...] = qbd

        kvc = kvc_ref[0]
        s = lax.dot_general(qbd, kvc[:, :KV_WIDTH], NT_DIMS, preferred_element_type=F32) + bcmp_ref[...]
        cmask = lax.broadcasted_iota(jnp.int32, s.shape, 1) < SAMPLE_N_CMP
        pr = _softmax_rows(s, cmask)
        ocmp_scr[...] = jnp.dot(pr.astype(BF16), kvc[:, KV_WIDTH:], preferred_element_type=F32)
        ps = []
        for g in range(ATT_GROUPS):
            r0 = g * rows_per_group
            ps.append(sum(pr[r0 + p * DEC_SEQ:r0 + (p + 1) * DEC_SEQ, :] for p in range(ATT_HPG)))
        hi, lo = _split2(jnp.concatenate(ps, axis=0))
        ov = ov_ref[...]
        imp = jnp.dot(hi, ov, preferred_element_type=F32) + jnp.dot(lo, ov, preferred_element_type=F32)
        qpos = PAST_LEN + (lax.broadcasted_iota(jnp.int32, imp.shape, 0) & (DEC_SEQ - 1))
        sel = _select_blocks(imp, qpos, SAMPLE_N_BLK, axis=1)
        sel_scr[...] = jnp.concatenate([sel[g * DEC_SEQ:(g + 1) * DEC_SEQ, :]
                                        for g in range(ATT_GROUPS) for p in range(ATT_HPG)], axis=0).astype(BF16)

        reset()
        tiles = [page_tile(win_state_ref[0, :, :, :, w * LANES:(w + 1) * LANES],
                           bwin_ref[:, w * LANES:(w + 1) * LANES], (w * LANES + lane) > t_row)
                 for w in range(WINDOW // LANES)]
        tiles.append(new_tile(win_new_ref, bwin_ref[:, WINDOW:WINDOW + LANES], lane <= t_row))
        attend(tiles)
        owin_scr[...] = finish()
        reset()

    blk_of_key = lax.broadcasted_iota(jnp.int32, (SAMPLE_BLK_LANES, LANES), 1) >> SEL_SHIFT
    blk_row = lax.broadcasted_iota(jnp.int32, (SAMPLE_BLK_LANES, LANES), 0)
    sel = sel_scr[...]
    tiles = []
    for k in range(SAMPLE_PAGES_PER_STEP):
        pg = u * SAMPLE_PAGES_PER_STEP + k
        expand = jnp.where(blk_row == 2 * pg + blk_of_key, 1.0, 0.0).astype(BF16)
        picked = jnp.dot(sel, expand, preferred_element_type=F32) > 0.5
        bias = bslc_ref[:, pl.ds(pl.multiple_of(pg * LANES, LANES), LANES)]
        tiles.append(page_tile(page_refs[k][0], bias, picked))
    attend(tiles)

    @pl.when(u == SAMPLE_STEPS - 1)
    def _():
        attend([new_tile(slc_new_ref, bslc_ref[:, PAST_LEN:PAST_LEN + LANES], lane <= t_row)])
        o_slc = finish()
        gt = gate_ref[...]

        def gate_rows(br):
            cols = []
            for g in range(ATT_GROUPS):
                for p in range(ATT_HPG):
                    c = g * LANES + br * ATT_HPG + p
                    cols.append(jnp.broadcast_to(gt[:, c:c + 1], (DEC_SEQ, KV_WIDTH)))
            return jnp.concatenate(cols, axis=0)

        comb = gate_rows(0) * ocmp_scr[...] + gate_rows(1) * o_slc + gate_rows(2) * owin_scr[...]
        comb = jnp.where(own, comb, 0.0)
        per_head = []
        for p in range(ATT_HPG):
            per_head.append(sum(comb[(g * ATT_HPG + p) * DEC_SEQ:(g * ATT_HPG + p + 1) * DEC_SEQ, :]
                                for g in range(ATT_GROUPS)))
        hi, lo = _split2(jnp.concatenate(per_head, axis=1))
        perm = perm_ref[...]
        o_ref[...] = (lax.dot_general(hi, perm, NT_DIMS, preferred_element_type=F32)
                      + lax.dot_general(lo, perm, NT_DIMS, preferred_element_type=F32))


def _nsa_sample(q, gates, kvc, cache_slc, page_table, slc_rows, state_win, win_rows, bias_cmp, bias_slc, bias_win):
    n_seq = page_table.shape[0]
    row0 = N_PROMPT // DEC_SEQ
    src = np.arange(ATT_WIDTH)
    g, p, d = src // (ATT_HPG * HEAD_DIM), (src // HEAD_DIM) % ATT_HPG, src % HEAD_DIM
    perm = np.zeros((ATT_WIDTH, ATT_WIDTH), np.float32)
    perm[src, p * KV_WIDTH + g * HEAD_DIM + d] = 1.0
    ov = _overlap_padded(SAMPLE_N_CMP, SAMPLE_N_BLK, PAST_LEN // CMP_STRIDE, SAMPLE_BLK_LANES)
    tok = lambda w: pl.BlockSpec((DEC_SEQ, w), lambda b, u, pt: (row0 + b, 0))
    const = lambda a: pl.BlockSpec(a.shape, lambda b, u, pt, nd=a.ndim: (0,) * nd)
    per_seq = lambda s: pl.BlockSpec((1,) + s, lambda b, u, pt, nd=len(s): (b,) + (0,) * nd)
    pages = [pl.BlockSpec((1, 2, ATT_GROUPS, HEAD_DIM, PAGE_SIZE),
                          lambda b, u, pt, k=k: (pt[b, u * SAMPLE_PAGES_PER_STEP + k], 0, 0, 0, 0))
             for k in range(SAMPLE_PAGES_PER_STEP)]
    consts = (bias_cmp, bias_slc, bias_win, jnp.asarray(perm, BF16), ov)
    grid_spec = pltpu.PrefetchScalarGridSpec(
        num_scalar_prefetch=1, grid=(n_seq, SAMPLE_STEPS),
        in_specs=[tok(ATT_WIDTH), tok(ATT_GROUPS * LANES), per_seq((PAST_LEN // CMP_STRIDE, KV_ROW))] + pages
        + [tok(KV_ROW), per_seq((2, ATT_GROUPS, HEAD_DIM, WINDOW)), tok(KV_ROW)] + [const(a) for a in consts],
        out_specs=pl.BlockSpec((DEC_SEQ, ATT_WIDTH), lambda b, u, pt: (b, 0)),
        scratch_shapes=[pltpu.VMEM((SAMPLE_ROWS, KV_WIDTH), BF16), pltpu.VMEM((SAMPLE_ROWS, SAMPLE_BLK_LANES), BF16),
                        pltpu.VMEM((SAMPLE_ROWS, LANES), F32), pltpu.VMEM((SAMPLE_ROWS, LANES), F32),
                        pltpu.VMEM((SAMPLE_ROWS, KV_WIDTH), F32), pltpu.VMEM((SAMPLE_ROWS, KV_WIDTH), F32),
                        pltpu.VMEM((SAMPLE_ROWS, KV_WIDTH), F32)])
    return pl.pallas_call(
        _nsa_sample_kernel, grid_spec=grid_spec,
        out_shape=jax.ShapeDtypeStruct((n_seq * DEC_SEQ, ATT_WIDTH), F32),
        compiler_params=_cparams("arbitrary", "arbitrary"),
        name="nsa_sample",
    )(page_table, q, gates, kvc, *([cache_slc] * SAMPLE_PAGES_PER_STEP), slc_rows, state_win, win_rows, *consts)


def kernel(x_prompt, x_sample, cache_cmp_kv, cache_slc_kv, state_win_kv, state_hgrn, page_table,
           rel_bias_table, hgrn_lower_bound, norm_ffn1, w_ffn1_gate_up, w_ffn1_down, norm_mix,
           w_in, q_norm, k_norm, w_cmp1, b_cmp1, w_cmp2, b_cmp2, attn_out_norm, hgrn_out_norm,
           w_out, norm_ffn2, w_ffn2_gate_up, w_ffn2_down):
    assert DEPTH == 1
    l = 0
    kv_shape = (2, ATT_GROUPS, HEAD_DIM)

    y1 = _ffn((x_prompt.reshape(N_PROMPT, D_MODEL), x_sample.reshape(N_SAMPLE, D_MODEL)), norm_ffn1[l],
              w_ffn1_gate_up[l].astype(BF16), w_ffn1_down[l].astype(BF16))

    (q, cmp_rows, slc_rows, slc_pack, win_rows, win_pack, hq, lf, hk, hv, hog, gates) = _project_all(
        y1, norm_mix[l], _permute_w_in(w_in[l]), q_norm[l], k_norm[l], hgrn_lower_bound)

    tbl = rel_bias_table.astype(F32)
    first_end = CMP_BLOCK - 1
    bias_cmp_p = _bias_table(tbl, SEQ, LANES, -first_end, 1, -CMP_STRIDE)
    bias_toep = _bias_table(tbl, TOEP_TILES * LANES, LANES, 0, 1, -1)
    bias_cmp_s = _bias_table(tbl, DEC_SEQ, PAST_LEN // CMP_STRIDE, PAST_LEN - first_end, 1, -CMP_STRIDE)
    bias_slc_s = _bias_table(tbl, DEC_SEQ, PAST_LEN + LANES, PAST_LEN, 1, -1)
    bias_win_s = _bias_table(tbl, DEC_SEQ, WINDOW + LANES, WINDOW, 1, -1)
    rows_ht = lambda a: a.reshape(SAMPLE_ROWS, a.shape[-1])

    cw = _compress_weights(w_cmp1[l], b_cmp1[l], w_cmp2[l], b_cmp2[l])
    kvc_p = _compress(cmp_rows, None, cw, k_norm[l][0], paged=False)
    kvc_s = _compress_paged(_cache_view(cache_cmp_kv[l]), page_table, w_cmp1[l], b_cmp1[l], w_cmp2[l], b_cmp2[l],
                            k_norm[l][0])

    o_att_p = _nsa_prompt(q, gates, kvc_p, slc_pack, win_pack, bias_cmp_p, bias_toep)
    o_att_s = _nsa_sample(q, gates, kvc_s, _cache_view(cache_slc_kv[l]), page_table, slc_rows,
                          _cache_view(state_win_kv[l]), win_rows,
                          rows_ht(bias_cmp_s), rows_ht(bias_slc_s), rows_ht(bias_win_s))

    o_hg_p, hg_p = _hgrn(hq, hk, hv, lf, None, n_seq=BATCH, t_len=SEQ, ch=HG_CHUNK_PROMPT, row0=0)
    o_hg_s, hg_s = _hgrn(hq, hk, hv, lf, state_hgrn[l].astype(F32), n_seq=DEC_BATCH, t_len=DEC_SEQ,
                         ch=HG_CHUNK_SAMPLE, row0=N_PROMPT)

    y2 = _mixer_out(y1, o_att_p, o_att_s, o_hg_p, o_hg_s, hog, attn_out_norm[l], hgrn_out_norm[l],
                    w_out[l].astype(BF16))
    y3_p, y3_s = _ffn(y2, norm_ffn2[l], w_ffn2_gate_up[l].astype(BF16), w_ffn2_down[l].astype(BF16), split_out=True)

    prompt_rows = lambda a: a[:N_PROMPT].reshape((1, BATCH, SEQ) + kv_shape)
    sample_rows = lambda a: a[N_PROMPT:].reshape((1, DEC_BATCH, DEC_SEQ) + kv_shape)
    win_p = prompt_rows(win_rows)[:, :, SEQ - min(WINDOW, SEQ):]
    win_s = jnp.concatenate([state_win_kv[l][:, DEC_SEQ:], sample_rows(win_rows)[0]], axis=1)[None]
    return (y3_p.reshape(BATCH, SEQ, D_MODEL), y3_s.reshape(DEC_BATCH, DEC_SEQ, D_MODEL),
            prompt_rows(cmp_rows), prompt_rows(slc_rows), win_p, hg_p[None],
            sample_rows(cmp_rows), sample_rows(slc_rows), win_s, hg_s[None].astype(state_hgrn.dtype))
```

```python
import functools
import math

import jax
import jax.numpy as jnp
import numpy as np
from jax import lax
from jax.experimental import pallas as pl
from jax.experimental.pallas import tpu as pltpu

D_MODEL = 2048
BATCH = 4
SEQ = 2048
DEPTH = 1
DEC_BATCH = 32
DEC_SEQ = 8
PAST_LEN = 8192
PAGE_SIZE = 128
HEAD_DIM = 64
ATT_HEADS = (D_MODEL // 2) // HEAD_DIM
ATT_GROUPS = ATT_HEADS // 4
ATT_HPG = ATT_HEADS // ATT_GROUPS
ATT_WIDTH = ATT_HEADS * HEAD_DIM
KV_WIDTH = ATT_GROUPS * HEAD_DIM
N_BRANCH = 3
CMP_BLOCK = 32
CMP_STRIDE = 16
CMP_RATIO = CMP_BLOCK // CMP_STRIDE
CMP_HIDDEN = 256
SEL_BLOCK = 64
N_SEL = 8
WINDOW = 512
Q_BLOCK = 128
HG_KEY = 128
HG_VAL = 128
HG_HEADS = (D_MODEL // 2) // HG_VAL
HG_WIDTH = HG_HEADS * HG_VAL
MIX_WIDTH = ATT_WIDTH + HG_WIDTH
D_FF = 256 * ((8 * D_MODEL // 3 + 255) // 256)
N_BUCKETS = 32
MAX_DISTANCE = 1024
EPS = 1e-6
IN_SPLITS = (ATT_WIDTH, 2 * N_BRANCH * KV_WIDTH, N_BRANCH * ATT_HEADS,
             HG_HEADS * HG_KEY, HG_HEADS * HG_KEY, HG_WIDTH, HG_WIDTH)
D_IN = sum(IN_SPLITS)

N_PROMPT = BATCH * SEQ
N_SAMPLE = DEC_BATCH * DEC_SEQ
N_TOKENS = N_PROMPT + N_SAMPLE
N_PAGES = PAST_LEN // PAGE_SIZE
KV_ROW = 2 * KV_WIDTH

LANES = 128
V7X_VMEM_BYTES = 64 * 1024 * 1024
VMEM_LIMIT_BYTES = 56 * 1024 * 1024

NEG = -1e30
F32 = jnp.float32
BF16 = jnp.bfloat16

FFN_ROW_TILE = 768
FFN_FF_TILE = 512
PROJ_ROW_TILE = 528
PROJ_COL_TILE = 512
OUT_ROW_TILE = 256
HG_CHUNK_PROMPT = 16
HG_CHUNK_SAMPLE = DEC_SEQ
CMP_UNIT_ROWS = 2048
CMP_UNITS = 4


def _cparams(*sem):
    return pltpu.CompilerParams(dimension_semantics=sem, vmem_limit_bytes=VMEM_LIMIT_BYTES)


def _t5_thresholds():
    n = np.arange(0, 2 * MAX_DISTANCE + 2)
    exact = N_BUCKETS // 2
    logn = np.log(np.maximum(n, 1).astype(np.float64) / exact)
    large = exact + (logn / math.log(MAX_DISTANCE / exact) * (N_BUCKETS - exact)).astype(np.int32)
    b = np.where(n < exact, n, np.minimum(large, N_BUCKETS - 1))
    return [int(n[b >= k][0]) for k in range(N_BUCKETS)]


T5_THRESHOLDS = _t5_thresholds()


def _block_diag_ones(n, blk):
    i = np.arange(n)
    return (i[:, None] // blk == i[None, :] // blk).astype(np.float32)


def _split2(x):
    hi = x.astype(BF16)
    lo = (x - hi.astype(F32)).astype(BF16)
    return hi, lo


def _group_rms(t, bd, gain, width):
    hi, lo = _split2(t * t)
    ss = (jnp.dot(hi, bd, preferred_element_type=F32) + jnp.dot(lo, bd, preferred_element_type=F32))
    return t * lax.rsqrt(ss * (1.0 / width) + EPS) * gain


def _ffn_kernel(*refs, split_in, split_out):
    n_in = 2 if split_in else 1
    x_ref = refs[0]
    gain_ref, wg_ref, wu_ref, wd_ref = refs[n_in:n_in + 4]
    o_ref = refs[n_in + 4]
    xn_ref, acc_ref = refs[-2:]
    j = pl.program_id(1)
    last_tile = pl.program_id(0) == pl.num_programs(0) - 1
    tm = x_ref.shape[0]

    def rows_in():
        x = x_ref[...]
        if split_in:
            xs = refs[1][...]
            x = jnp.where(last_tile, jnp.concatenate([x[:tm - xs.shape[0]], xs], axis=0), x)
        return x

    @pl.when(j == 0)
    def _():
        x = rows_in()
        y = x * lax.rsqrt(jnp.mean(x * x, axis=-1, keepdims=True) + EPS)
        xn_ref[...] = (y * gain_ref[...]).astype(BF16)
        acc_ref[...] = jnp.zeros_like(acc_ref)

    xn = xn_ref[...]
    g = jnp.dot(xn, wg_ref[...], preferred_element_type=F32)
    u = jnp.dot(xn, wu_ref[...], preferred_element_type=F32)
    a = (g * jax.nn.sigmoid(g) * u).astype(BF16)
    acc_ref[...] += jnp.dot(a, wd_ref[...], preferred_element_type=F32)

    @pl.when(j == pl.num_programs(1) - 1)
    def _():
        res = rows_in() + 0.5 * acc_ref[...]
        o_ref[...] = res
        if split_out:
            os_ref = refs[n_in + 5]

            @pl.when(last_tile)
            def _():
                os_ref[...] = res[tm - os_ref.shape[0]:]


def _ffn(x, gain, w_gu, w_down, split_out=False):
    split_in = isinstance(x, tuple)
    tm, tf = FFN_ROW_TILE, FFN_FF_TILE
    n_p, n_s = N_PROMPT, N_SAMPLE
    n, d = n_p + n_s, D_MODEL
    assert n % tm == 0 and D_FF % tf == 0 and n_p % tm == tm - n_s
    nj = D_FF // tf
    row = pl.BlockSpec((tm, d), lambda i, j: (i, 0))
    tail = pl.BlockSpec((n_s, d), lambda i, j: (0, 0))
    xs = x if split_in else (x,)
    return pl.pallas_call(
        functools.partial(_ffn_kernel, split_in=split_in, split_out=split_out),
        grid=(n // tm, nj),
        in_specs=([row, tail] if split_in else [row]) + [
            pl.BlockSpec((1, d), lambda i, j: (0, 0)),
            pl.BlockSpec((d, tf), lambda i, j: (0, j)),
            pl.BlockSpec((d, tf), lambda i, j: (0, j + nj)),
            pl.BlockSpec((tf, d), lambda i, j: (j, 0)),
        ],
        out_specs=(row, tail) if split_out else row,
        out_shape=((jax.ShapeDtypeStruct((n_p, d), F32), jax.ShapeDtypeStruct((n_s, d), F32)) if split_out
                   else jax.ShapeDtypeStruct((n, d), F32)),
        scratch_shapes=[pltpu.VMEM((tm, d), BF16), pltpu.VMEM((tm, d), F32)],
        compiler_params=_cparams("arbitrary", "arbitrary"),
        name="ffn",
    )(*xs, gain.reshape(1, d), w_gu, w_gu, w_down)


PB_Q, PB_CMP, PB_SLC, PB_WIN, PB_HQ, PB_HF, PB_HI, PB_HG, PB_GATE = 0, 2, 3, 4, 5, 7, 9, 11, 13
PROJ_N_BLOCKS = 14


def _permute_w_in(w_in):
    p = [int(v) for v in np.cumsum(IN_SPLITS)]
    a_g = w_in[:, p[1]:p[2]]
    src = np.zeros((PROJ_COL_TILE,), np.int32)
    valid = np.zeros((PROJ_COL_TILE,), bool)
    for g in range(ATT_GROUPS):
        for br in range(N_BRANCH):
            for hp in range(ATT_HPG):
                src[g * LANES + br * ATT_HPG + hp] = br * ATT_HEADS + g * ATT_HPG + hp
                valid[g * LANES + br * ATT_HPG + hp] = True
    gate = jnp.where(jnp.asarray(valid)[None, :], a_g[:, src], 0.0)
    return jnp.concatenate([w_in[:, :p[1]], w_in[:, p[2]:], gate], axis=1).astype(BF16)


def _pack_kv(k, v):
    parts = []
    for g in range(ATT_GROUPS):
        parts.append(k[:, g * HEAD_DIM:(g + 1) * HEAD_DIM])
        parts.append(v[:, g * HEAD_DIM:(g + 1) * HEAD_DIM])
    return jnp.concatenate(parts, axis=1).astype(BF16)


def _proj_kernel(x_ref, gain_ref, w_ref, bd_ref, qg_ref, kg_ref, lbp_ref,
                 q_ref, cmp_ref, slc_ref, slcp_ref, win_ref, winp_ref,
                 hq_ref, lf_ref, hk_ref, hv_ref, hog_ref, gate_ref, xn_ref):
    c = pl.program_id(1)

    @pl.when(c == 0)
    def _():
        x = x_ref[...]
        y = x * lax.rsqrt(jnp.mean(x * x, axis=-1, keepdims=True) + EPS)
        xn_ref[...] = (y * gain_ref[...]).astype(BF16)

    acc = jnp.dot(xn_ref[...], w_ref[...], preferred_element_type=F32)
    half = KV_WIDTH

    @pl.when(c < PB_CMP)
    def _():
        q_ref[...] = _group_rms(acc, bd_ref[...], qg_ref[...], HEAD_DIM)

    @pl.when(c == PB_CMP)
    def _():
        cmp_ref[...] = acc

    def kv_branch(rows_ref, pack_ref, br):
        k = _group_rms(acc[:, :half], bd_ref[:half, :half], kg_ref[br - 1:br, :], HEAD_DIM)
        v = acc[:, half:]
        rows_ref[:, :half] = k
        rows_ref[:, half:] = v
        pack_ref[...] = _pack_kv(k, v)

    @pl.when(c == PB_SLC)
    def _():
        kv_branch(slc_ref, slcp_ref, 1)

    @pl.when(c == PB_WIN)
    def _():
        kv_branch(win_ref, winp_ref, 2)

    @pl.when((c >= PB_HQ) & (c < PB_HF))
    def _():
        hq_ref[...] = acc

    @pl.when((c >= PB_HF) & (c < PB_HI))
    def _():
        p = lbp_ref[...]
        e = jnp.exp(p - jnp.max(p, axis=0, keepdims=True))
        lb = e[0:1, :] / jnp.sum(e, axis=0, keepdims=True)
        lf_ref[...] = jnp.log(lb + (1.0 - lb) * jax.nn.sigmoid(acc))
        hk_ref[...] = (1.0 - lb) * jax.nn.sigmoid(-acc)

    @pl.when((c >= PB_HI) & (c < PB_HG))
    def _():
        hv_ref[...] = acc

    @pl.when((c >= PB_HG) & (c < PB_GATE))
    def _():
        hog_ref[...] = acc

    @pl.when(c == PB_GATE)
    def _():
        gate_ref[...] = jax.nn.sigmoid(acc)


def _project_all(y, gain, w_perm, q_gain, k_gain, lb_logits):
    n, d = y.shape
    tm, tc = PROJ_ROW_TILE, PROJ_COL_TILE
    assert n % tm == 0 and DEPTH == 1
    bd = jnp.asarray(_block_diag_ones(tc, HEAD_DIM), BF16)

    def two(first):
        return lambda i, c: (i, jnp.clip(c - first, 0, 1))

    one = lambda i, c: (i, 0)
    wide = lambda dt: jax.ShapeDtypeStruct((n, 2 * tc), dt)
    narrow = lambda dt: jax.ShapeDtypeStruct((n, tc), dt)
    out_shape = (wide(F32), narrow(F32), narrow(F32), narrow(BF16), narrow(F32), narrow(BF16),
                 wide(F32), wide(F32), wide(F32), wide(F32), wide(F32), narrow(F32))
    blk = lambda f: pl.BlockSpec((tm, tc), f)
    out_specs = (blk(two(PB_Q)), blk(one), blk(one), blk(one), blk(one), blk(one),
                 blk(two(PB_HQ)), blk(two(PB_HF)), blk(two(PB_HF)), blk(two(PB_HI)), blk(two(PB_HG)), blk(one))
    return pl.pallas_call(
        _proj_kernel,
        grid=(n // tm, PROJ_N_BLOCKS),
        in_specs=[
            pl.BlockSpec((tm, d), lambda i, c: (i, 0)),
            pl.BlockSpec((1, d), lambda i, c: (0, 0)),
            pl.BlockSpec((d, tc), lambda i, c: (0, c)),
            pl.BlockSpec((tc, tc), lambda i, c: (0, 0)),
            pl.BlockSpec((1, tc), lambda i, c: (0, jnp.clip(c, 0, 1))),
            pl.BlockSpec((2, KV_WIDTH), lambda i, c: (0, 0)),
            pl.BlockSpec((DEPTH + 1, tc), lambda i, c: (0, jnp.clip(c - PB_HF, 0, 1))),
        ],
        out_specs=out_specs,
        out_shape=out_shape,
        scratch_shapes=[pltpu.VMEM((tm, d), BF16)],
        compiler_params=_cparams("arbitrary", "arbitrary"),
        name="proj",
    )(y, gain.reshape(1, d), w_perm, bd, q_gain.reshape(1, ATT_WIDTH),
      k_gain[1:].reshape(2, KV_WIDTH), lb_logits)


def _hgrn_kernel(q_ref, k_ref, v_ref, lf_ref, tri_ref, ones_ref, s0_ref, o_ref, st_ref,
                 qe_scr, kd_scr, vt_scr, dec_scr, oi_scr, *, t_len, ch, has_state):
    nj = t_len // ch
    shape3 = (nj, ch, HG_KEY)
    q3, k3, v3, lf3 = q_ref[...], k_ref[...], v_ref[...], lf_ref[...]
    tl = lax.broadcasted_iota(jnp.int32, shape3, 1)

    def row(x3, s):
        return jnp.broadcast_to(x3[:, s:s + 1, :], shape3)

    if t_len >= 256:
        lf2 = lf3.reshape(t_len, HG_KEY)
        parts = []
        for r0 in range(0, t_len, 256):
            x = lf2[r0:r0 + 256]
            hi = x.astype(BF16)
            r1 = x - hi.astype(F32)
            mid = r1.astype(BF16)
            lo = (r1 - mid.astype(F32)).astype(BF16)
            tri = tri_ref[...]
            parts.append(jnp.dot(tri, hi, preferred_element_type=F32)
                         + jnp.dot(tri, mid, preferred_element_type=F32)
                         + jnp.dot(tri, lo, preferred_element_type=F32))
        b3 = jnp.concatenate(parts, axis=0).reshape(shape3)
    else:
        b3 = jnp.zeros(shape3, F32)
        for s in range(ch):
            b3 = b3 + jnp.where(tl >= s, row(lf3, s), 0.0)

    bl3 = row(b3, ch - 1)
    qe3 = q3 * jnp.exp(b3)
    kd3 = k3 * jnp.exp(bl3 - b3)
    dec_scr[...] = jnp.exp(b3[:, ch - 1:ch, :])
    blocked = t_len % LANES == 0
    if blocked:
        nb = t_len // LANES
        qe2, kd2, v2 = (x.reshape(t_len, HG_KEY) for x in (qe3, kd3, v3))
        for m in range(nb):
            blk = slice(m * LANES, (m + 1) * LANES)
            qe_scr[m] = qe2[blk].T.astype(BF16)
            vt_scr[m] = v2[blk].T.astype(BF16)
            kd_scr[m] = kd2[blk].astype(BF16)
    else:
        qe_scr[...] = qe3.astype(BF16)
        kd_scr[...] = kd3.astype(BF16)

    ones = ones_ref[...]
    od = jnp.zeros(shape3, F32)
    for s in range(ch):
        w = q3 * row(k3, s) * jnp.exp(jnp.where(tl >= s, b3 - row(b3, s), NEG))
        a = jnp.dot(w.reshape(t_len, HG_KEY).astype(BF16), ones, preferred_element_type=F32)
        od = od + a.reshape(shape3) * row(v3, s)
    o_ref[...] = od

    if has_state:
        st0 = s0_ref[0, 0].T
    else:
        st0 = jnp.zeros((HG_VAL, HG_KEY), F32)

    if blocked:
        cpb = LANES // ch
        lane_chunk = lax.broadcasted_iota(jnp.int32, (HG_KEY, LANES), 1) >> (ch.bit_length() - 1)
        keep = [jnp.where(lane_chunk == r, 1.0, 0.0).astype(BF16) for r in range(cpb)]

        def body(m, st):
            qet, vt, kd = qe_scr[m], vt_scr[m], kd_scr[m]
            ot = jnp.zeros((HG_VAL, LANES), F32)
            for r in range(cpb):
                ot = ot + jnp.dot(st.astype(BF16), qet * keep[r], preferred_element_type=F32)
                ut = jnp.dot(vt * keep[r], kd, preferred_element_type=F32)
                st = st * dec_scr[m * cpb + r] + ut
            oi_scr[m] = ot
            return st

        st = lax.fori_loop(0, t_len // LANES, body, st0)
        for m in range(t_len // LANES):
            o_ref[m * cpb:(m + 1) * cpb] = o_ref[m * cpb:(m + 1) * cpb] + oi_scr[m].T.reshape(cpb, ch, HG_VAL)
    else:
        def body(j, st):
            oi_scr[j] = lax.dot_general(qe_scr[j], st.astype(BF16), NT_DIMS, preferred_element_type=F32)
            ut = lax.dot_general(v_ref[j].astype(BF16), kd_scr[j], (((0,), (0,)), ((), ())),
                                 preferred_element_type=F32)
            return st * dec_scr[j] + ut

        st = lax.fori_loop(0, nj, body, st0, unroll=True)
        o_ref[...] = o_ref[...] + oi_scr[...]
    st_ref[0, 0] = st.T


def _hgrn(hq, hk, hv, lf, s0, *, n_seq, t_len, ch, row0):
    n = hq.shape[0]
    nj = t_len // ch
    assert row0 % t_len == 0 and t_len % ch == 0 and ch % 8 == 0
    blk0 = row0 // t_len
    r3 = lambda a: a.reshape(n // ch, ch, HG_WIDTH)
    has_state = s0 is not None
    if not has_state:
        s0 = jnp.zeros((1, 1, HG_KEY, HG_VAL), F32)
    tri_n = 256 if t_len >= 256 else 8
    i = np.arange(tri_n)
    tri = jnp.asarray(((i[:, None] // ch == i[None, :] // ch) & (i[:, None] >= i[None, :])).astype(np.float32), BF16)
    ones = jnp.ones((HG_KEY, HG_KEY), BF16)
    blk_shape = (t_len // LANES, LANES, HG_KEY) if t_len % LANES == 0 else (nj, ch, HG_KEY)
    hps = HG_HEADS if nj == 1 else 1
    seq = pl.BlockSpec((nj, ch, hps * HG_KEY), lambda b, h: (blk0 + b, 0, h))
    state_in = pl.BlockSpec((1, hps, HG_KEY, HG_VAL), (lambda b, h: (b, h, 0, 0)) if has_state else (lambda b, h: (0, 0, 0, 0)))

    def body(q_ref, k_ref, v_ref, lf_ref, tri_ref, ones_ref, s0_ref, o_ref, st_ref, *scratch):
        for h in range(hps):
            ln = slice(h * HG_KEY, (h + 1) * HG_KEY)
            _hgrn_kernel(q_ref.at[:, :, ln], k_ref.at[:, :, ln], v_ref.at[:, :, ln], lf_ref.at[:, :, ln], tri_ref,
                         ones_ref, s0_ref.at[:, h:h + 1], o_ref.at[:, :, ln], st_ref.at[:, h:h + 1], *scratch,
                         t_len=t_len, ch=ch, has_state=has_state)

    o, st = pl.pallas_call(
        body,
        grid=(n_seq, HG_HEADS // hps),
        in_specs=[seq, seq, seq, seq,
                  pl.BlockSpec((tri_n, tri_n), lambda b, h: (0, 0)),
                  pl.BlockSpec((HG_KEY, HG_KEY), lambda b, h: (0, 0)),
                  state_in],
        out_specs=(pl.BlockSpec((nj, ch, hps * HG_VAL), lambda b, h: (b, 0, h)),
                   pl.BlockSpec((1, hps, HG_KEY, HG_VAL), lambda b, h: (b, h, 0, 0))),
        out_shape=(jax.ShapeDtypeStruct((n_seq * nj, ch, HG_WIDTH), F32),
                   jax.ShapeDtypeStruct((n_seq, HG_HEADS, HG_KEY, HG_VAL), F32)),
        scratch_shapes=[pltpu.VMEM(blk_shape, BF16), pltpu.VMEM(blk_shape, BF16), pltpu.VMEM(blk_shape, BF16),
                        pltpu.VMEM((nj, 1, HG_KEY), F32), pltpu.VMEM(blk_shape, F32)],
        compiler_params=_cparams("arbitrary", "arbitrary"),
        name="hgrn",
    )(r3(hq), r3(hk), r3(hv), r3(lf), tri, ones, s0)
    return o.reshape(n_seq * t_len, HG_WIDTH), st


def _mixout_kernel(y_ref, oap_ref, oas_ref, ohp_ref, ohs_ref, og_ref, ag_ref, hgain_ref, w_ref, o_ref):
    is_sample = pl.program_id(0) == pl.num_programs(0) - 1
    oa = jnp.where(is_sample, oas_ref[...], oap_ref[...])
    a = oa * lax.rsqrt(jnp.mean(oa * oa, axis=-1, keepdims=True) + EPS) * ag_ref[...]
    oh = jnp.where(is_sample, ohs_ref[...], ohp_ref[...])
    hs = []
    for h in range(HG_HEADS):
        x = oh[:, h * HG_VAL:(h + 1) * HG_VAL]
        hs.append(x * lax.rsqrt(jnp.mean(x * x, axis=-1, keepdims=True) + EPS))
    og = og_ref[...]
    hh = jnp.concatenate(hs, axis=1) * hgain_ref[...] * (og * jax.nn.sigmoid(og))
    m = (jnp.dot(a.astype(BF16), w_ref[:ATT_WIDTH, :], preferred_element_type=F32)
         + jnp.dot(hh.astype(BF16), w_ref[ATT_WIDTH:, :], preferred_element_type=F32))
    o_ref[...] = y_ref[...] + m


def _mixer_out(y, o_att_p, o_att_s, o_hg_p, o_hg_s, og, attn_gain, hg_gain, w_out):
    n, d = y.shape
    tm = OUT_ROW_TILE
    n_p = o_att_p.shape[0]
    assert n % tm == 0 and n_p % tm == 0 and o_att_s.shape[0] == tm and n == n_p + tm
    row = lambda w: pl.BlockSpec((tm, w), lambda i: (i, 0))
    prompt_row = lambda w: pl.BlockSpec((tm, w), lambda i: (jnp.minimum(i, n_p // tm - 1), 0))
    const = lambda s: pl.BlockSpec(s, lambda i: (0, 0))
    return pl.pallas_call(
        _mixout_kernel,
        grid=(n // tm,),
        in_specs=[row(d), prompt_row(ATT_WIDTH), const((tm, ATT_WIDTH)), prompt_row(HG_WIDTH), const((tm, HG_WIDTH)),
                  row(HG_WIDTH), const((1, ATT_WIDTH)), const((1, HG_WIDTH)), const((MIX_WIDTH, d))],
        out_specs=row(d),
        out_shape=jax.ShapeDtypeStruct((n, d), F32),
        compiler_params=_cparams("arbitrary"),
        name="mixout",
    )(y, o_att_p, o_att_s, o_hg_p, o_hg_s, og, attn_gain.reshape(1, ATT_WIDTH), hg_gain.reshape(1, HG_WIDTH), w_out)


def _bias_kernel(tbl_ref, o_ref, *, a0, ar, ac, rows_blk):
    h = pl.program_id(0)
    rb = pl.program_id(1)
    shape = o_ref.shape[1:]
    r = lax.broadcasted_iota(jnp.int32, shape, 0) + rb * rows_blk
    c = lax.broadcasted_iota(jnp.int32, shape, 1)
    n = a0 + ar * r + ac * c
    out = jnp.full(shape, tbl_ref[0, h], F32)
    for k in range(1, N_BUCKETS):
        out = jnp.where(n >= T5_THRESHOLDS[k], tbl_ref[k, h], out)
    o_ref[0] = out


def _bias_table(tbl, rows, cols, a0, ar, ac):
    rows_blk = max(r for r in range(8, min(rows, 512) + 1, 8) if rows % r == 0)
    assert cols % LANES == 0
    return pl.pallas_call(
        functools.partial(_bias_kernel, a0=a0, ar=ar, ac=ac, rows_blk=rows_blk),
        grid=(ATT_HEADS, rows // rows_blk),
        in_specs=[pl.BlockSpec(memory_space=pltpu.SMEM)],
        out_specs=pl.BlockSpec((1, rows_blk, cols), lambda h, rb: (h, rb, 0)),
        out_shape=jax.ShapeDtypeStruct((ATT_HEADS, rows, cols), F32),
        compiler_params=_cparams("arbitrary", "arbitrary"),
        name="t5_bias",
    )(tbl)


def _compress_weights(w1, b1, w2, b2):
    w = w1.reshape(2, CMP_RATIO, CMP_STRIDE, HEAD_DIM, CMP_HIDDEN)
    z = jnp.zeros_like(w)
    top = jnp.concatenate([w, z], axis=-1)
    bot = jnp.concatenate([z, w], axis=-1)
    wpad = jnp.stack([top, bot], axis=3)
    wpad = wpad.reshape(2 * CMP_RATIO, CMP_STRIDE * LANES, 2 * CMP_HIDDEN).astype(BF16)
    z2 = jnp.zeros_like(w2)
    w2pad = jnp.concatenate([jnp.concatenate([w2, z2], axis=-1), jnp.concatenate([z2, w2], axis=-1)], axis=1)
    return wpad, jnp.concatenate([b1, b1], axis=-1), w2pad.astype(BF16), jnp.concatenate([b2, b2], axis=-1)


def _compress_kernel(*refs, n_in, paged, packed):
    if paged:
        refs = refs[1:]
    in_refs = refs[:n_in]
    perm_ref, wpad_ref, b1_ref, w2pad_ref, b2_ref, kg_ref, bd_ref, out_ref, xs_ref = refs[n_in:]
    u = pl.program_id(1)
    n_chunk = CMP_UNITS * CMP_UNIT_ROWS // CMP_STRIDE
    perm = perm_ref[...]

    for t in range(CMP_UNIT_ROWS // 256):
        if paged:
            x = jnp.concatenate([in_refs[2 * t][0], in_refs[2 * t + 1][0]], axis=0)
        else:
            x = in_refs[0][t * 256:(t + 1) * 256, :]
        y = jnp.dot(perm, x.astype(BF16), preferred_element_type=F32).astype(BF16)
        base = pl.multiple_of(u * (CMP_UNIT_ROWS // CMP_STRIDE) + t * 16, 16)
        for s in range(CMP_STRIDE):
            xs_ref[s, pl.ds(base, 16), :] = y[s * 16:(s + 1) * 16, :]

    @pl.when(u == CMP_UNITS - 1)
    def _():
        halves = []
        for kv in range(2):
            outs = []
            for j in range(2):
                col = (kv * 2 + j) * LANES
                lhs = jnp.concatenate([xs_ref[s, :, col:col + LANES] for s in range(CMP_STRIDE)], axis=1)
                h0 = jnp.dot(lhs, wpad_ref[kv * CMP_RATIO + 0], preferred_element_type=F32)
                h1 = jnp.dot(lhs, wpad_ref[kv * CMP_RATIO + 1], preferred_element_type=F32)
                h = b1_ref[kv:kv + 1, :] + h0 + pltpu.roll(h1, n_chunk - 1, axis=0)
                a = (h * jax.nn.sigmoid(h)).astype(BF16)
                outs.append(jnp.dot(a, w2pad_ref[kv], preferred_element_type=F32) + b2_ref[kv:kv + 1, :])
            halves.append(jnp.concatenate(outs, axis=1))
        kc = _group_rms(halves[0], bd_ref[...], kg_ref[...], HEAD_DIM)
        vc = halves[1]
        if packed:
            pk = _pack_kv(kc, vc)
            for g in range(ATT_GROUPS):
                out_ref[0, g] = pk[:, g * LANES:(g + 1) * LANES]
        else:
            out_ref[0, :, :KV_WIDTH] = kc.astype(BF16)
            out_ref[0, :, KV_WIDTH:] = vc.astype(BF16)


def _compress(rows, page_table, cw, k_gain0, *, paged):
    wpad, b1, w2pad, b2 = cw
    n_chunk = CMP_UNITS * CMP_UNIT_ROWS // CMP_STRIDE
    i = np.arange(256)
    perm = np.zeros((256, 256), np.float32)
    perm[(i % 16) * 16 + i // 16, i] = 1.0
    consts = (jnp.asarray(perm, BF16), wpad, b1, w2pad, b2, k_gain0.reshape(1, KV_WIDTH),
              jnp.asarray(_block_diag_ones(KV_WIDTH, HEAD_DIM), BF16))
    if paged:
        n_grp = page_table.shape[0]
        pages_per_unit = CMP_UNIT_ROWS // PAGE_SIZE
        n_in = pages_per_unit
        const = lambda a: pl.BlockSpec(a.shape, lambda b, u, pt, nd=a.ndim: (0,) * nd)
        in_specs = [pl.BlockSpec((1, PAGE_SIZE, KV_ROW), lambda b, u, pt, k=k: (pt[b, u * pages_per_unit + k], 0, 0))
                    for k in range(n_in)]
        out_spec = pl.BlockSpec((1, n_chunk, KV_ROW), lambda b, u, pt: (b, 0, 0))
        out_shape = jax.ShapeDtypeStruct((n_grp, n_chunk, KV_ROW), BF16)
        args = (page_table,) + (rows,) * n_in + consts
        nsp = 1
    else:
        n_grp, n_in = 1, 1
        const = lambda a: pl.BlockSpec(a.shape, lambda b, u, nd=a.ndim: (0,) * nd)
        in_specs = [pl.BlockSpec((CMP_UNIT_ROWS, KV_ROW), lambda b, u: (u, 0))]
        out_spec = pl.BlockSpec((1, ATT_GROUPS, n_chunk, LANES), lambda b, u: (0, 0, 0, 0))
        out_shape = jax.ShapeDtypeStruct((1, ATT_GROUPS, n_chunk, LANES), BF16)
        args = (rows,) + consts
        nsp = 0
    grid_spec = pltpu.PrefetchScalarGridSpec(
        num_scalar_prefetch=nsp, grid=(n_grp, CMP_UNITS),
        in_specs=in_specs + [const(a) for a in consts],
        out_specs=out_spec,
        scratch_shapes=[pltpu.VMEM((CMP_STRIDE, n_chunk, KV_ROW), BF16)])
    return pl.pallas_call(
        functools.partial(_compress_kernel, n_in=n_in, paged=paged, packed=not paged),
        grid_spec=grid_spec, out_shape=out_shape,
        compiler_params=_cparams("arbitrary", "arbitrary"),
        name="compress_paged" if paged else "compress",
    )(*args)


PAGES_PER_STEP = CMP_UNIT_ROWS // PAGE_SIZE


def _cache_view(cache):
    return jnp.transpose(cache, (0, 2, 3, 4, 1))


def _compress_paged_kernel(pt_ref, *refs):
    del pt_ref
    page_refs = refs[:PAGES_PER_STEP]
    pick_ref, w1_ref, b1_ref, w2_ref, b2_ref, kg_ref, bd_ref, out_ref, xs_ref = refs[PAGES_PER_STEP:]
    u = pl.program_id(1)
    n_chunk = CMP_UNITS * CMP_UNIT_ROWS // CMP_STRIDE
    pick = pick_ref[...]
    n_kvg = 2 * ATT_GROUPS

    for t in range(PAGES_PER_STEP // 2):
        base = pl.multiple_of(u * (CMP_UNIT_ROWS // CMP_STRIDE) + t * 16, 16)
        kt = jnp.concatenate([page_refs[2 * t][0].reshape(n_kvg * HEAD_DIM, PAGE_SIZE),
                              page_refs[2 * t + 1][0].reshape(n_kvg * HEAD_DIM, PAGE_SIZE)], axis=1).astype(BF16)
        z = jnp.dot(kt, pick, preferred_element_type=F32)
        for kvg in range(n_kvg):
            zk = z[kvg * HEAD_DIM:(kvg + 1) * HEAD_DIM]
            y = jnp.concatenate([zk[:, :LANES], zk[:, LANES:]], axis=0).T.astype(BF16)
            for s2 in range(CMP_STRIDE // 2):
                xs_ref[kvg, s2, pl.ds(base, 16), :] = y[s2 * 16:(s2 + 1) * 16, :]

    @pl.when(u == CMP_UNITS - 1)
    def _():
        halves = []
        for kv in range(2):
            lhs = jnp.concatenate(
                [jnp.concatenate([xs_ref[kv * ATT_GROUPS + g, s2] for s2 in range(CMP_STRIDE // 2)], axis=1)
                 for g in range(ATT_GROUPS)], axis=0)
            h0 = jnp.dot(lhs, w1_ref[kv * CMP_RATIO + 0], preferred_element_type=F32)
            h1 = jnp.dot(lhs, w1_ref[kv * CMP_RATIO + 1], preferred_element_type=F32)
            h = b1_ref[kv:kv + 1, :] + h0 + pltpu.roll(h1, ATT_GROUPS * n_chunk - 1, axis=0)
            a = (h * jax.nn.sigmoid(h)).astype(BF16)
            o = b2_ref[kv:kv + 1, :]
            for g in range(ATT_GROUPS):
                o = o + jnp.dot(a[g * n_chunk:(g + 1) * n_chunk], w2_ref[kv * ATT_GROUPS + g], preferred_element_type=F32)
            halves.append(o)
        out_ref[0, :, :KV_WIDTH] = _group_rms(halves[0], bd_ref[...], kg_ref[...], HEAD_DIM).astype(BF16)
        out_ref[0, :, KV_WIDTH:] = halves[1].astype(BF16)


def _compress_paged(cache_t, page_table, w1, b1, w2, b2, k_gain0):
    n_seq = page_table.shape[0]
    n_chunk = CMP_UNITS * CMP_UNIT_ROWS // CMP_STRIDE
    r = np.arange(LANES)
    s2, c = r // 16, r % 16
    pick = np.zeros((2 * PAGE_SIZE, 2, LANES), np.float32)
    for half in range(2):
        pick[CMP_STRIDE * c + 2 * s2 + half, half, r] = 1.0
    w2p = jnp.zeros((2, ATT_GROUPS, CMP_HIDDEN, KV_WIDTH), F32)
    for g in range(ATT_GROUPS):
        w2p = w2p.at[:, g, :, g * HEAD_DIM:(g + 1) * HEAD_DIM].set(w2)
    consts = (jnp.asarray(pick.reshape(2 * PAGE_SIZE, 2 * LANES), BF16),
              w1.reshape(2 * CMP_RATIO, CMP_STRIDE * HEAD_DIM, CMP_HIDDEN).astype(BF16), b1,
              w2p.reshape(2 * ATT_GROUPS, CMP_HIDDEN, KV_WIDTH).astype(BF16), jnp.tile(b2, (1, ATT_GROUPS)),
              k_gain0.reshape(1, KV_WIDTH), jnp.asarray(_block_diag_ones(KV_WIDTH, HEAD_DIM), BF16))
    const = lambda a: pl.BlockSpec(a.shape, lambda b, u, pt, nd=a.ndim: (0,) * nd)
    pages = [pl.BlockSpec((1, 2, ATT_GROUPS, HEAD_DIM, PAGE_SIZE),
                          lambda b, u, pt, k=k: (pt[b, u * PAGES_PER_STEP + k], 0, 0, 0, 0))
             for k in range(PAGES_PER_STEP)]
    grid_spec = pltpu.PrefetchScalarGridSpec(
        num_scalar_prefetch=1, grid=(n_seq, CMP_UNITS),
        in_specs=pages + [const(a) for a in consts],
        out_specs=pl.BlockSpec((1, n_chunk, KV_ROW), lambda b, u, pt: (b, 0, 0)),
        scratch_shapes=[pltpu.VMEM((2 * ATT_GROUPS, CMP_STRIDE // 2, n_chunk, LANES), BF16)])
    return pl.pallas_call(
        _compress_paged_kernel, grid_spec=grid_spec,
        out_shape=jax.ShapeDtypeStruct((n_seq, n_chunk, KV_ROW), BF16),
        compiler_params=_cparams("arbitrary", "arbitrary"),
        name="compress_paged",
    )(page_table, *([cache_t] * PAGES_PER_STEP), *consts)


def _block_overlap(n_cmp, n_blk):
    cs = np.arange(n_cmp)[:, None] * CMP_STRIDE
    bs = np.arange(n_blk)[None, :] * SEL_BLOCK
    ov = np.minimum(cs + CMP_BLOCK, bs + SEL_BLOCK) - np.maximum(cs, bs)
    return (np.clip(ov, 0, None) / CMP_BLOCK).astype(np.float32)


def _overlap_padded(n_cmp, n_blk, rows, cols):
    ov = np.zeros((rows, cols), np.float32)
    ov[:n_cmp, :n_blk] = _block_overlap(n_cmp, n_blk)
    return jnp.asarray(ov, BF16)


def _softmax_rows(s, mask):
    s = jnp.where(mask, s, NEG)
    m = jnp.max(s, axis=1, keepdims=True)
    e = jnp.where(mask, jnp.exp(s - m), 0.0)
    d = jnp.sum(e, axis=1, keepdims=True)
    return e / jnp.where(d > 0, d, 1.0)


def _select_blocks(imp, qpos, n_blk, axis):
    blk = lax.broadcasted_iota(jnp.int32, imp.shape, axis)
    cur = qpos >> SEL_SHIFT
    forced = (blk == 0) | (blk == cur) | (blk == cur - 1)
    valid = blk * SEL_BLOCK <= qpos
    score = jnp.where(forced, ATT_HPG + 1.0, jnp.where(valid, imp, -1.0))
    ahead = jnp.zeros(imp.shape, F32)
    for m in range(n_blk):
        sm = jnp.broadcast_to(score[m:m + 1, :] if axis == 0 else score[:, m:m + 1], imp.shape)
        tie = jnp.where(blk > m, 1.0, 0.0)
        ahead = ahead + jnp.where(sm > score, 1.0, jnp.where(sm == score, tie, 0.0))
    return jnp.where(blk < n_blk, jnp.where(ahead < N_SEL, 1.0, 0.0), 0.0)


def _online_update(s, mask, v, m_ref, l_ref, acc_ref, idx=None):
    at = (lambda r: r) if idx is None else (lambda r: r.at[idx])
    m_r, l_r, acc_r = at(m_ref), at(l_ref), at(acc_ref)
    s = jnp.where(mask, s, NEG)
    m_prev = m_r[...]
    m_new = jnp.maximum(m_prev, jnp.max(s, axis=1, keepdims=True))
    alpha = jnp.exp(m_prev - m_new)
    e = jnp.where(mask, jnp.exp(s - m_new), 0.0)
    l_r[...] = alpha * l_r[...] + jnp.sum(e, axis=1, keepdims=True)
    pv = jnp.dot(e.astype(BF16), v, preferred_element_type=F32)
    reps = acc_r.shape[-1] // LANES
    a_w = alpha if reps == 1 else jnp.concatenate([alpha] * reps, axis=1)
    acc_r[...] = a_w * acc_r[...] + pv
    m_r[...] = m_new


SEL_SHIFT = SEL_BLOCK.bit_length() - 1
NT_DIMS = (((1,), (1,)), ((), ()))


SLC_UNROLL = 4
TOEP_TILES = min(SEQ // LANES, -(-(T5_THRESHOLDS[-1] + LANES - 1) // LANES) + 1)


def _nsa_prompt_kernel(q_ref, gate_ref, kvc_ref, slc_ref, win_ref, bcmp_ref, toep_ref, ov_ref, o_ref,
                       s_scr, m_scr, l_scr, acc_scr):
    i = pl.program_id(2)
    tile = (Q_BLOCK, LANES)
    rows = ATT_HPG * Q_BLOCK
    lane = lax.broadcasted_iota(jnp.int32, tile, 1)
    sub = lax.broadcasted_iota(jnp.int32, tile, 0)
    low = lane < HEAD_DIM
    qpos = i * Q_BLOCK + sub
    head = lambda x, p: x[p * Q_BLOCK:(p + 1) * Q_BLOCK]

    qa = q_ref[...] * (HEAD_DIM ** -0.5)
    qp = []
    for pair in range(ATT_HPG // 2):
        x = qa[:, pair * LANES:(pair + 1) * LANES]
        qp.append(jnp.where(low, x, 0.0))
        qp.append(jnp.where(low, pltpu.roll(x, HEAD_DIM, axis=1), 0.0))
    qs = jnp.concatenate(qp, axis=0).astype(BF16)

    kvc = kvc_ref[0, 0]
    s = lax.dot_general(qs, kvc, NT_DIMS, preferred_element_type=F32) + bcmp_ref[...].reshape(rows, LANES)
    cmask1 = (lane * CMP_STRIDE + (CMP_BLOCK - 1)) <= qpos
    pr = _softmax_rows(s, jnp.concatenate([cmask1] * ATT_HPG, axis=0))
    o_cmp = jnp.dot(pr.astype(BF16), kvc, preferred_element_type=F32)
    hi, lo = _split2(sum(head(pr, p) for p in range(ATT_HPG)))
    ov = ov_ref[...]
    imp = jnp.dot(hi, ov, preferred_element_type=F32) + jnp.dot(lo, ov, preferred_element_type=F32)
    n_blk = SEQ // SEL_BLOCK
    imp_t = imp.T[:n_blk, :]
    qpos_t = i * Q_BLOCK + lax.broadcasted_iota(jnp.int32, (n_blk, Q_BLOCK), 1)
    sel_t = _select_blocks(imp_t, qpos_t, n_blk, axis=0).astype(BF16)
    blk_row = lax.broadcasted_iota(jnp.int32, (n_blk, LANES), 0)
    blk_of_key = lax.broadcasted_iota(jnp.int32, (n_blk, LANES), 1) >> SEL_SHIFT
    last_tile = SEQ // LANES - 1

    def key_tile(kv_ref, j):
        jc = jnp.clip(j, 0, last_tile)
        return kv_ref[pl.ds(pl.multiple_of(jc * LANES, LANES), LANES), :]

    def masked_scores(kv, j, mask):
        sc = lax.dot_general(qs, kv, NT_DIMS, preferred_element_type=F32)
        row0 = pl.multiple_of(jnp.clip(i - j, 0, TOEP_TILES - 1) * LANES, LANES)
        return [jnp.where(mask, head(sc, p) + toep_ref[p, pl.ds(row0, LANES), :], NEG) for p in range(ATT_HPG)]

    def weights_times_v(scores, m, kv):
        es = [jnp.exp(scores[p] - m[p]) for p in range(ATT_HPG)]
        return es, jnp.dot(jnp.concatenate(es, axis=0).astype(BF16), kv, preferred_element_type=F32)

    n_win = WINDOW // LANES + 1
    win_kv, win_s = [], []
    for w in range(n_win):
        j = i - (n_win - 1) + w
        dist = (i - j) * LANES + sub - lane
        inside = jnp.where(dist >= 0, jnp.where(dist < WINDOW, 1.0, 0.0), 0.0)
        win_kv.append(key_tile(win_ref, j))
        win_s.append(masked_scores(win_kv[w], j, jnp.where(j >= 0, inside, 0.0) > 0.5))
    m_w = []
    for p in range(ATT_HPG):
        m_el = functools.reduce(jnp.maximum, [win_s[w][p] for w in range(n_win)])
        m_w.append(jnp.max(m_el, axis=1, keepdims=True))
    acc_w = jnp.zeros((rows, LANES), F32)
    l_w = [jnp.zeros(tile, F32) for _ in range(ATT_HPG)]
    for w in range(n_win):
        es, pv = weights_times_v(win_s[w], m_w, win_kv[w])
        acc_w = acc_w + pv
        l_w = [l_w[p] + es[p] for p in range(ATT_HPG)]
    o_win = [head(acc_w, p) / jnp.sum(l_w[p], axis=1, keepdims=True) for p in range(ATT_HPG)]

    n_steps = (i + SLC_UNROLL) >> (SLC_UNROLL.bit_length() - 1)
    m_scr[...] = jnp.full(m_scr.shape, NEG, F32)

    def slc_scores(jj, carry):
        sps = []
        for r in range(SLC_UNROLL):
            j = SLC_UNROLL * jj + r
            dist = (i - j) * LANES + sub - lane
            expand = jnp.where(blk_row == 2 * j + blk_of_key, 1.0, 0.0).astype(BF16)
            picked = lax.dot_general(sel_t, expand, (((0,), (0,)), ((), ())), preferred_element_type=F32)
            sp = masked_scores(key_tile(slc_ref, j), j, jnp.where(dist >= 0, picked, 0.0) > 0.5)
            for p in range(ATT_HPG):
                s_scr[j, p] = sp[p]
            sps.append(sp)
        for p in range(ATT_HPG):
            m_scr[p] = jnp.maximum(m_scr[p], functools.reduce(jnp.maximum, [sp[p] for sp in sps]))
        return carry

    lax.fori_loop(0, n_steps, slc_scores, 0)
    for p in range(ATT_HPG):
        m_scr[p] = jnp.broadcast_to(jnp.max(m_scr[p], axis=1, keepdims=True), tile)
    l_scr[...] = jnp.zeros(l_scr.shape, F32)
    acc_scr[...] = jnp.zeros(acc_scr.shape, F32)

    def slc_weights(jj, carry):
        m = [m_scr[p] for p in range(ATT_HPG)]
        pvs, ess = [], []
        for r in range(SLC_UNROLL):
            j = SLC_UNROLL * jj + r
            es, pv = weights_times_v([s_scr[j, p] for p in range(ATT_HPG)], m, key_tile(slc_ref, j))
            pvs.append(pv)
            ess.append(es)
        acc_scr[...] += functools.reduce(jnp.add, pvs)
        for p in range(ATT_HPG):
            l_scr[p] = l_scr[p] + functools.reduce(jnp.add, [es[p] for es in ess])
        return carry

    lax.fori_loop(0, n_steps, slc_weights, 0)
    o_slc = [head(acc_scr[...], p) / jnp.sum(l_scr[p], axis=1, keepdims=True) for p in range(ATT_HPG)]

    g = gate_ref[...]
    comb = []
    for p in range(ATT_HPG):
        col = lambda br: jnp.broadcast_to(g[:, br * ATT_HPG + p:br * ATT_HPG + p + 1], tile)
        comb.append(col(0) * head(o_cmp, p) + col(1) * o_slc[p] + col(2) * o_win[p])
    for pair in range(ATT_HPG // 2):
        o_ref[:, pair * LANES:(pair + 1) * LANES] = jnp.where(
            low, pltpu.roll(comb[2 * pair], HEAD_DIM, axis=1), comb[2 * pair + 1])


NSA_GROUPS_PER_STEP = 2


def _nsa_prompt_multi_kernel(q_ref, gate_ref, kvc_ref, slc_ref, win_ref, bcmp_ref, toep_ref, ov_ref, o_ref,
                             s_scr, m_scr, l_scr, acc_scr):
    i = pl.program_id(2)
    ngs = NSA_GROUPS_PER_STEP
    tile = (Q_BLOCK, LANES)
    rows = ATT_HPG * Q_BLOCK
    lane = lax.broadcasted_iota(jnp.int32, tile, 1)
    sub = lax.broadcasted_iota(jnp.int32, tile, 0)
    low = lane < HEAD_DIM
    qpos = i * Q_BLOCK + sub
    head = lambda x, p: x[p * Q_BLOCK:(p + 1) * Q_BLOCK]
    heads = range(ATT_HPG)
    n_blk = SEQ // SEL_BLOCK
    blk_row = lax.broadcasted_iota(jnp.int32, (n_blk, LANES), 0)
    blk_of_key = lax.broadcasted_iota(jnp.int32, (n_blk, LANES), 1) >> SEL_SHIFT
    qpos_t = i * Q_BLOCK + lax.broadcasted_iota(jnp.int32, (n_blk, Q_BLOCK), 1)
    last_tile = SEQ // LANES - 1
    ov = ov_ref[...]
    glanes = lambda gg: slice(gg * LANES, (gg + 1) * LANES)

    def key_tile(kv_ref, gg, j):
        jc = jnp.clip(j, 0, last_tile)
        return kv_ref[pl.ds(pl.multiple_of(jc * LANES, LANES), LANES), glanes(gg)]

    def masked_scores(qs, gg, kv, j, mask):
        sc = lax.dot_general(qs, kv, NT_DIMS, preferred_element_type=F32)
        row0 = pl.multiple_of(jnp.clip(i - j, 0, TOEP_TILES - 1) * LANES, LANES)
        return [jnp.where(mask, head(sc, p) + toep_ref[gg * ATT_HPG + p, pl.ds(row0, LANES), :], NEG) for p in heads]

    def weights_times_v(scores, m, kv):
        es = [jnp.exp(scores[p] - m[p]) for p in heads]
        return es, jnp.dot(jnp.concatenate(es, axis=0).astype(BF16), kv, preferred_element_type=F32)

    def front(gg):
        qa = q_ref[:, gg * ATT_HPG * HEAD_DIM:(gg + 1) * ATT_HPG * HEAD_DIM] * (HEAD_DIM ** -0.5)
        qp = []
        for pair in range(ATT_HPG // 2):
            x = qa[:, pair * LANES:(pair + 1) * LANES]
            qp.append(jnp.where(low, x, 0.0))
            qp.append(jnp.where(low, pltpu.roll(x, HEAD_DIM, axis=1), 0.0))
        qs = jnp.concatenate(qp, axis=0).astype(BF16)

        kvc = kvc_ref[0, gg]
        s = lax.dot_general(qs, kvc, NT_DIMS, preferred_element_type=F32)
        s = s + bcmp_ref[gg * ATT_HPG:(gg + 1) * ATT_HPG].reshape(rows, LANES)
        cmask1 = (lane * CMP_STRIDE + (CMP_BLOCK - 1)) <= qpos
        pr = _softmax_rows(s, jnp.concatenate([cmask1] * ATT_HPG, axis=0))
        o_cmp = jnp.dot(pr.astype(BF16), kvc, preferred_element_type=F32)
        hi, lo = _split2(sum(head(pr, p) for p in heads))
        imp = jnp.dot(hi, ov, preferred_element_type=F32) + jnp.dot(lo, ov, preferred_element_type=F32)
        sel_t = _select_blocks(imp.T[:n_blk, :], qpos_t, n_blk, axis=0).astype(BF16)

        n_win = WINDOW // LANES + 1
        win_kv, win_s = [], []
        for w in range(n_win):
            j = i - (n_win - 1) + w
            dist = (i - j) * LANES + sub - lane
            inside = jnp.where(dist >= 0, jnp.where(dist < WINDOW, 1.0, 0.0), 0.0)
            win_kv.append(key_tile(win_ref, gg, j))
            win_s.append(masked_scores(qs, gg, win_kv[w], j, jnp.where(j >= 0, inside, 0.0) > 0.5))
        m_w = [jnp.max(functools.reduce(jnp.maximum, [win_s[w][p] for w in range(n_win)]), axis=1, keepdims=True)
               for p in heads]
        acc_w = jnp.zeros((rows, LANES), F32)
        l_w = [jnp.zeros(tile, F32) for _ in heads]
        for w in range(n_win):
            es, pv = weights_times_v(win_s[w], m_w, win_kv[w])
            acc_w = acc_w + pv
            l_w = [l_w[p] + es[p] for p in heads]
        o_win = [head(acc_w, p) / jnp.sum(l_w[p], axis=1, keepdims=True) for p in heads]
        return qs, o_cmp, sel_t, o_win

    fronts = [front(gg) for gg in range(ngs)]

    n_steps = (i + SLC_UNROLL) >> (SLC_UNROLL.bit_length() - 1)
    m_scr[...] = jnp.full(m_scr.shape, NEG, F32)

    def slc_scores(jj, carry):
        for gg in range(ngs):
            qs, _, sel_t, _ = fronts[gg]
            sps = []
            for r in range(SLC_UNROLL):
                j = SLC_UNROLL * jj + r
                dist = (i - j) * LANES + sub - lane
                expand = jnp.where(blk_row == 2 * j + blk_of_key, 1.0, 0.0).astype(BF16)
                picked = lax.dot_general(sel_t, expand, (((0,), (0,)), ((), ())), preferred_element_type=F32)
                sp = masked_scores(qs, gg, key_tile(slc_ref, gg, j), j, jnp.where(dist >= 0, picked, 0.0) > 0.5)
                for p in heads:
                    s_scr[j, gg * ATT_HPG + p] = sp[p]
                sps.append(sp)
            for p in heads:
                hp = gg * ATT_HPG + p
                m_scr[hp] = jnp.maximum(m_scr[hp], functools.reduce(jnp.maximum, [sp[p] for sp in sps]))
        return carry

    lax.fori_loop(0, n_steps, slc_scores, 0)
    for hp in range(ngs * ATT_HPG):
        m_scr[hp] = jnp.broadcast_to(jnp.max(m_scr[hp], axis=1, keepdims=True), tile)
    l_scr[...] = jnp.zeros(l_scr.shape, F32)
    acc_scr[...] = jnp.zeros(acc_scr.shape, F32)

    def slc_weights(jj, carry):
        for gg in range(ngs):
            m = [m_scr[gg * ATT_HPG + p] for p in heads]
            pvs, ess = [], []
            for r in range(SLC_UNROLL):
                j = SLC_UNROLL * jj + r
                es, pv = weights_times_v([s_scr[j, gg * ATT_HPG + p] for p in heads], m, key_tile(slc_ref, gg, j))
                pvs.append(pv)
                ess.append(es)
            acc_scr[gg] += functools.reduce(jnp.add, pvs)
            for p in heads:
                hp = gg * ATT_HPG + p
                l_scr[hp] = l_scr[hp] + functools.reduce(jnp.add, [es[p] for es in ess])
        return carry

    lax.fori_loop(0, n_steps, slc_weights, 0)

    for gg in range(ngs):
        _, o_cmp, _, o_win = fronts[gg]
        g = gate_ref[:, glanes(gg)]
        comb = []
        for p in heads:
            o_slc = head(acc_scr[gg], p) / jnp.sum(l_scr[gg * ATT_HPG + p], axis=1, keepdims=True)
            col = lambda br: jnp.broadcast_to(g[:, br * ATT_HPG + p:br * ATT_HPG + p + 1], tile)
            comb.append(col(0) * head(o_cmp, p) + col(1) * o_slc + col(2) * o_win[p])
        for pair in range(ATT_HPG // 2):
            c0 = gg * ATT_HPG * HEAD_DIM + pair * LANES
            o_ref[:, c0:c0 + LANES] = jnp.where(low, pltpu.roll(comb[2 * pair], HEAD_DIM, axis=1), comb[2 * pair + 1])


def _nsa_prompt(q, gates, kvc, slc_pack, win_pack, bias_cmp, bias_toep, n_batch=BATCH):
    nqb = SEQ // Q_BLOCK
    ngs = NSA_GROUPS_PER_STEP
    ov = _overlap_padded(SEQ // CMP_STRIDE - 1, SEQ // SEL_BLOCK, LANES, LANES)
    gw = ngs * ATT_HPG * HEAD_DIM
    nh = ngs * ATT_HPG
    return pl.pallas_call(
        _nsa_prompt_multi_kernel,
        grid=(n_batch, ATT_GROUPS // ngs, nqb),
        in_specs=[
            pl.BlockSpec((Q_BLOCK, gw), lambda b, g, i: (b * nqb + i, g)),
            pl.BlockSpec((Q_BLOCK, ngs * LANES), lambda b, g, i: (b * nqb + i, g)),
            pl.BlockSpec((1, ngs, SEQ // CMP_STRIDE, LANES), lambda b, g, i: (0, g, b, 0)),
            pl.BlockSpec((SEQ, ngs * LANES), lambda b, g, i: (b, g)),
            pl.BlockSpec((SEQ, ngs * LANES), lambda b, g, i: (b, g)),
            pl.BlockSpec((nh, Q_BLOCK, LANES), lambda b, g, i: (g, i, 0)),
            pl.BlockSpec((nh, TOEP_TILES * LANES, LANES), lambda b, g, i: (g, 0, 0)),
            pl.BlockSpec((LANES, LANES), lambda b, g, i: (0, 0)),
        ],
        out_specs=pl.BlockSpec((Q_BLOCK, gw), lambda b, g, i: (b * nqb + i, g)),
        out_shape=jax.ShapeDtypeStruct((n_batch * SEQ, ATT_WIDTH), F32),
        scratch_shapes=[pltpu.VMEM((nqb, nh, Q_BLOCK, LANES), F32),
                        pltpu.VMEM((nh, Q_BLOCK, LANES), F32), pltpu.VMEM((nh, Q_BLOCK, LANES), F32),
                        pltpu.VMEM((ngs, ATT_HPG * Q_BLOCK, LANES), F32)],
        compiler_params=_cparams("arbitrary", "arbitrary", "arbitrary"),
        name="nsa_prompt",
    )(q, gates, kvc, slc_pack, win_pack, bias_cmp, bias_toep, ov)


SAMPLE_ROWS = ATT_HEADS * DEC_SEQ
SAMPLE_PAGES_PER_STEP = 16
SAMPLE_STEPS = N_PAGES // SAMPLE_PAGES_PER_STEP
SAMPLE_N_CMP = (PAST_LEN + DEC_SEQ - CMP_BLOCK) // CMP_STRIDE + 1
SAMPLE_N_BLK = -(-(PAST_LEN + DEC_SEQ) // SEL_BLOCK)
SAMPLE_BLK_LANES = 2 * LANES


def _nsa_sample_kernel(pt_ref, q_ref, gate_ref, kvc_ref, *rest):
    page_refs = rest[:SAMPLE_PAGES_PER_STEP]
    (slc_new_ref, win_state_ref, win_new_ref, bcmp_ref, bslc_ref, bwin_ref, perm_ref, ov_ref,
     o_ref, qbd_scr, sel_scr, m_scr, l_scr, acc_scr, ocmp_scr, owin_scr) = rest[SAMPLE_PAGES_PER_STEP:]
    del pt_ref
    u = pl.program_id(1)
    tile = (SAMPLE_ROWS, LANES)
    wide = (SAMPLE_ROWS, KV_WIDTH)
    lane = lax.broadcasted_iota(jnp.int32, tile, 1)
    t_row = lax.broadcasted_iota(jnp.int32, tile, 0) & (DEC_SEQ - 1)
    rows_per_group = ATT_HPG * DEC_SEQ
    own = ((lax.broadcasted_iota(jnp.int32, wide, 1) >> (HEAD_DIM.bit_length() - 1))
           == (lax.broadcasted_iota(jnp.int32, wide, 0) >> (rows_per_group.bit_length() - 1)))

    def reset():
        m_scr[...] = jnp.full(m_scr.shape, NEG, F32)
        l_scr[...] = jnp.zeros(l_scr.shape, F32)
        acc_scr[...] = jnp.zeros(acc_scr.shape, F32)

    def finish():
        l = jnp.sum(l_scr[...], axis=1, keepdims=True)
        return acc_scr[...] / jnp.where(l > 0, l, 1.0)

    def attend(tiles):
        qbd = qbd_scr[...]
        scores = []
        m_el = None
        for kt, _, bias, mask in tiles:
            s = jnp.where(mask, jnp.dot(qbd, kt.astype(BF16), preferred_element_type=F32) + bias, NEG)
            scores.append(s)
            m_el = s if m_el is None else jnp.maximum(m_el, s)
        m_prev = m_scr[...]
        m_new = jnp.maximum(m_prev, jnp.max(m_el, axis=1, keepdims=True))
        alpha = jnp.exp(m_prev - m_new)
        l_el = alpha * l_scr[...]
        acc = jnp.concatenate([alpha, alpha], axis=1) * acc_scr[...]
        for (_, vt, _, _), s in zip(tiles, scores):
            e = jnp.exp(s - m_new)
            l_el = l_el + e
            acc = acc + lax.dot_general(e.astype(BF16), vt.astype(BF16), NT_DIMS, preferred_element_type=F32)
        m_scr[...] = m_new
        l_scr[...] = l_el
        acc_scr[...] = acc

    def page_tile(kv4, bias, mask):
        return (kv4[0].reshape(KV_WIDTH, LANES), kv4[1].reshape(KV_WIDTH, LANES), bias, mask)

    def new_tile(ref, bias, mask):
        rows = jnp.concatenate([ref[...], jnp.zeros((LANES - DEC_SEQ, KV_ROW), F32)], axis=0)
        return (rows[:, :KV_WIDTH].T, rows[:, KV_WIDTH:].T, bias, mask)

    @pl.when(u == 0)
    def _():
        q = (q_ref[...] * (HEAD_DIM ** -0.5)).astype(BF16)
        qperm = jnp.dot(q, perm_ref[...], preferred_element_type=F32)
        qfull = jnp.concatenate([qperm[:, p * KV_WIDTH:(p + 1) * KV_WIDTH]
                                 for g in range(ATT_GROUPS) for p in range(ATT_HPG)], axis=0)
        qbd = jnp.where(own, qfull, 0.0).astype(BF16)
        qbd_scr[...] = qbd

        kvc = kvc_ref[0]
        s = lax.dot_general(qbd, kvc[:, :KV_WIDTH], NT_DIMS, preferred_element_type=F32) + bcmp_ref[...]
        cmask = lax.broadcasted_iota(jnp.int32, s.shape, 1) < SAMPLE_N_CMP
        pr = _softmax_rows(s, cmask)
        ocmp_scr[...] = jnp.dot(pr.astype(BF16), kvc[:, KV_WIDTH:], preferred_element_type=F32)
        ps = []
        for g in range(ATT_GROUPS):
            r0 = g * rows_per_group
            ps.append(sum(pr[r0 + p * DEC_SEQ:r0 + (p + 1) * DEC_SEQ, :] for p in range(ATT_HPG)))
        hi, lo = _split2(jnp.concatenate(ps, axis=0))
        ov = ov_ref[...]
        imp = jnp.dot(hi, ov, preferred_element_type=F32) + jnp.dot(lo, ov, preferred_element_type=F32)
        qpos = PAST_LEN + (lax.broadcasted_iota(jnp.int32, imp.shape, 0) & (DEC_SEQ - 1))
        sel = _select_blocks(imp, qpos, SAMPLE_N_BLK, axis=1)
        sel_scr[...] = jnp.concatenate([sel[g * DEC_SEQ:(g + 1) * DEC_SEQ, :]
                                        for g in range(ATT_GROUPS) for p in range(ATT_HPG)], axis=0).astype(BF16)

        reset()
        tiles = [page_tile(win_state_ref[0, :, :, :, w * LANES:(w + 1) * LANES],
                           bwin_ref[:, w * LANES:(w + 1) * LANES], (w * LANES + lane) > t_row)
                 for w in range(WINDOW // LANES)]
        tiles.append(new_tile(win_new_ref, bwin_ref[:, WINDOW:WINDOW + LANES], lane <= t_row))
        attend(tiles)
        owin_scr[...] = finish()
        reset()

    blk_of_key = lax.broadcasted_iota(jnp.int32, (SAMPLE_BLK_LANES, LANES), 1) >> SEL_SHIFT
    blk_row = lax.broadcasted_iota(jnp.int32, (SAMPLE_BLK_LANES, LANES), 0)
    sel = sel_scr[...]
    tiles = []
    for k in range(SAMPLE_PAGES_PER_STEP):
        pg = u * SAMPLE_PAGES_PER_STEP + k
        expand = jnp.where(blk_row == 2 * pg + blk_of_key, 1.0, 0.0).astype(BF16)
        picked = jnp.dot(sel, expand, preferred_element_type=F32) > 0.5
        bias = bslc_ref[:, pl.ds(pl.multiple_of(pg * LANES, LANES), LANES)]
        tiles.append(page_tile(page_refs[k][0], bias, picked))
    attend(tiles)

    @pl.when(u == SAMPLE_STEPS - 1)
    def _():
        attend([new_tile(slc_new_ref, bslc_ref[:, PAST_LEN:PAST_LEN + LANES], lane <= t_row)])
        o_slc = finish()
        gt = gate_ref[...]

        def gate_rows(br):
            cols = []
            for g in range(ATT_GROUPS):
                for p in range(ATT_HPG):
                    c = g * LANES + br * ATT_HPG + p
                    cols.append(jnp.broadcast_to(gt[:, c:c + 1], (DEC_SEQ, KV_WIDTH)))
            return jnp.concatenate(cols, axis=0)

        comb = gate_rows(0) * ocmp_scr[...] + gate_rows(1) * o_slc + gate_rows(2) * owin_scr[...]
        comb = jnp.where(own, comb, 0.0)
        per_head = []
        for p in range(ATT_HPG):
            per_head.append(sum(comb[(g * ATT_HPG + p) * DEC_SEQ:(g * ATT_HPG + p + 1) * DEC_SEQ, :]
                                for g in range(ATT_GROUPS)))
        hi, lo = _split2(jnp.concatenate(per_head, axis=1))
        perm = perm_ref[...]
        o_ref[...] = (lax.dot_general(hi, perm, NT_DIMS, preferred_element_type=F32)
                      + lax.dot_general(lo, perm, NT_DIMS, preferred_element_type=F32))


def _nsa_sample(q, gates, kvc, cache_slc, page_table, slc_rows, state_win, win_rows, bias_cmp, bias_slc, bias_win):
    n_seq = page_table.shape[0]
    row0 = N_PROMPT // DEC_SEQ
    src = np.arange(ATT_WIDTH)
    g, p, d = src // (ATT_HPG * HEAD_DIM), (src // HEAD_DIM) % ATT_HPG, src % HEAD_DIM
    perm = np.zeros((ATT_WIDTH, ATT_WIDTH), np.float32)
    perm[src, p * KV_WIDTH + g * HEAD_DIM + d] = 1.0
    ov = _overlap_padded(SAMPLE_N_CMP, SAMPLE_N_BLK, PAST_LEN // CMP_STRIDE, SAMPLE_BLK_LANES)
    tok = lambda w: pl.BlockSpec((DEC_SEQ, w), lambda b, u, pt: (row0 + b, 0))
    const = lambda a: pl.BlockSpec(a.shape, lambda b, u, pt, nd=a.ndim: (0,) * nd)
    per_seq = lambda s: pl.BlockSpec((1,) + s, lambda b, u, pt, nd=len(s): (b,) + (0,) * nd)
    pages = [pl.BlockSpec((1, 2, ATT_GROUPS, HEAD_DIM, PAGE_SIZE),
                          lambda b, u, pt, k=k: (pt[b, u * SAMPLE_PAGES_PER_STEP + k], 0, 0, 0, 0))
             for k in range(SAMPLE_PAGES_PER_STEP)]
    consts = (bias_cmp, bias_slc, bias_win, jnp.asarray(perm, BF16), ov)
    grid_spec = pltpu.PrefetchScalarGridSpec(
        num_scalar_prefetch=1, grid=(n_seq, SAMPLE_STEPS),
        in_specs=[tok(ATT_WIDTH), tok(ATT_GROUPS * LANES), per_seq((PAST_LEN // CMP_STRIDE, KV_ROW))] + pages
        + [tok(KV_ROW), per_seq((2, ATT_GROUPS, HEAD_DIM, WINDOW)), tok(KV_ROW)] + [const(a) for a in consts],
        out_specs=pl.BlockSpec((DEC_SEQ, ATT_WIDTH), lambda b, u, pt: (b, 0)),
        scratch_shapes=[pltpu.VMEM((SAMPLE_ROWS, KV_WIDTH), BF16), pltpu.VMEM((SAMPLE_ROWS, SAMPLE_BLK_LANES), BF16),
                        pltpu.VMEM((SAMPLE_ROWS, LANES), F32), pltpu.VMEM((SAMPLE_ROWS, LANES), F32),
                        pltpu.VMEM((SAMPLE_ROWS, KV_WIDTH), F32), pltpu.VMEM((SAMPLE_ROWS, KV_WIDTH), F32),
                        pltpu.VMEM((SAMPLE_ROWS, KV_WIDTH), F32)])
    return pl.pallas_call(
        _nsa_sample_kernel, grid_spec=grid_spec,
        out_shape=jax.ShapeDtypeStruct((n_seq * DEC_SEQ, ATT_WIDTH), F32),
        compiler_params=_cparams("arbitrary", "arbitrary"),
        name="nsa_sample",
    )(page_table, q, gates, kvc, *([cache_slc] * SAMPLE_PAGES_PER_STEP), slc_rows, state_win, win_rows, *consts)


def kernel(x_prompt, x_sample, cache_cmp_kv, cache_slc_kv, state_win_kv, state_hgrn, page_table,
           rel_bias_table, hgrn_lower_bound, norm_ffn1, w_ffn1_gate_up, w_ffn1_down, norm_mix,
           w_in, q_norm, k_norm, w_cmp1, b_cmp1, w_cmp2, b_cmp2, attn_out_norm, hgrn_out_norm,
           w_out, norm_ffn2, w_ffn2_gate_up, w_ffn2_down):
    assert DEPTH == 1
    l = 0
    kv_shape = (2, ATT_GROUPS, HEAD_DIM)

    y1 = _ffn((x_prompt.reshape(N_PROMPT, D_MODEL), x_sample.reshape(N_SAMPLE, D_MODEL)), norm_ffn1[l],
              w_ffn1_gate_up[l].astype(BF16), w_ffn1_down[l].astype(BF16))

    (q, cmp_rows, slc_rows, slc_pack, win_rows, win_pack, hq, lf, hk, hv, hog, gates) = _project_all(
        y1, norm_mix[l], _permute_w_in(w_in[l]), q_norm[l], k_norm[l], hgrn_lower_bound)

    tbl = rel_bias_table.astype(F32)
    first_end = CMP_BLOCK - 1
    bias_cmp_p = _bias_table(tbl, SEQ, LANES, -first_end, 1, -CMP_STRIDE)
    bias_toep = _bias_table(tbl, TOEP_TILES * LANES, LANES, 0, 1, -1)
    bias_cmp_s = _bias_table(tbl, DEC_SEQ, PAST_LEN // CMP_STRIDE, PAST_LEN - first_end, 1, -CMP_STRIDE)
    bias_slc_s = _bias_table(tbl, DEC_SEQ, PAST_LEN + LANES, PAST_LEN, 1, -1)
    bias_win_s = _bias_table(tbl, DEC_SEQ, WINDOW + LANES, WINDOW, 1, -1)
    rows_ht = lambda a: a.reshape(SAMPLE_ROWS, a.shape[-1])

    cw = _compress_weights(w_cmp1[l], b_cmp1[l], w_cmp2[l], b_cmp2[l])
    kvc_p = _compress(cmp_rows, None, cw, k_norm[l][0], paged=False)
    kvc_s = _compress_paged(_cache_view(cache_cmp_kv[l]), page_table, w_cmp1[l], b_cmp1[l], w_cmp2[l], b_cmp2[l],
                            k_norm[l][0])

    o_att_p = _nsa_prompt(q, gates, kvc_p, slc_pack, win_pack, bias_cmp_p, bias_toep)
    o_att_s = _nsa_sample(q, gates, kvc_s, _cache_view(cache_slc_kv[l]), page_table, slc_rows,
                          _cache_view(state_win_kv[l]), win_rows,
                          rows_ht(bias_cmp_s), rows_ht(bias_slc_s), rows_ht(bias_win_s))

    o_hg_p, hg_p = _hgrn(hq, hk, hv, lf, None, n_seq=BATCH, t_len=SEQ, ch=HG_CHUNK_PROMPT, row0=0)
    o_hg_s, hg_s = _hgrn(hq, hk, hv, lf, state_hgrn[l].astype(F32), n_seq=DEC_BATCH, t_len=DEC_SEQ,
                         ch=HG_CHUNK_SAMPLE, row0=N_PROMPT)

    y2 = _mixer_out(y1, o_att_p, o_att_s, o_hg_p, o_hg_s, hog, attn_out_norm[l], hgrn_out_norm[l],
                    w_out[l].astype(BF16))
    y3_p, y3_s = _ffn(y2, norm_ffn2[l], w_ffn2_gate_up[l].astype(BF16), w_ffn2_down[l].astype(BF16), split_out=True)

    prompt_rows = lambda a: a[:N_PROMPT].reshape((1, BATCH, SEQ) + kv_shape)
    sample_rows = lambda a: a[N_PROMPT:].reshape((1, DEC_BATCH, DEC_SEQ) + kv_shape)
    win_p = prompt_rows(win_rows)[:, :, SEQ - min(WINDOW, SEQ):]
    win_s = jnp.concatenate([state_win_kv[l][:, DEC_SEQ:], sample_rows(win_rows)[0]], axis=1)[None]
    return (y3_p.reshape(BATCH, SEQ, D_MODEL), y3_s.reshape(DEC_BATCH, DEC_SEQ, D_MODEL),
            prompt_rows(cmp_rows), prompt_rows(slc_rows), win_p, hg_p[None],
            sample_rows(cmp_rows), sample_rows(slc_rows), win_s, hg_s[None].astype(state_hgrn.dtype))
```

```python
import functools
import math

import jax
import jax.numpy as jnp
import numpy as np
from jax import lax
from jax.experimental import pallas as pl
from jax.experimental.pallas import tpu as pltpu

D_MODEL = 2048
BATCH = 4
SEQ = 2048
DEPTH = 1
DEC_BATCH = 32
DEC_SEQ = 8
PAST_LEN = 8192
PAGE_SIZE = 128
HEAD_DIM = 64
ATT_HEADS = (D_MODEL // 2) // HEAD_DIM
ATT_GROUPS = ATT_HEADS // 4
ATT_HPG = ATT_HEADS // ATT_GROUPS
ATT_WIDTH = ATT_HEADS * HEAD_DIM
KV_WIDTH = ATT_GROUPS * HEAD_DIM
N_BRANCH = 3
CMP_BLOCK = 32
CMP_STRIDE = 16
CMP_RATIO = CMP_BLOCK // CMP_STRIDE
CMP_HIDDEN = 256
SEL_BLOCK = 64
N_SEL = 8
WINDOW = 512
Q_BLOCK = 128
HG_KEY = 128
HG_VAL = 128
HG_HEADS = (D_MODEL // 2) // HG_VAL
HG_WIDTH = HG_HEADS * HG_VAL
MIX_WIDTH = ATT_WIDTH + HG_WIDTH
D_FF = 256 * ((8 * D_MODEL // 3 + 255) // 256)
N_BUCKETS = 32
MAX_DISTANCE = 1024
EPS = 1e-6
IN_SPLITS = (ATT_WIDTH, 2 * N_BRANCH * KV_WIDTH, N_BRANCH * ATT_HEADS,
             HG_HEADS * HG_KEY, HG_HEADS * HG_KEY, HG_WIDTH, HG_WIDTH)
D_IN = sum(IN_SPLITS)

N_PROMPT = BATCH * SEQ
N_SAMPLE = DEC_BATCH * DEC_SEQ
N_TOKENS = N_PROMPT + N_SAMPLE
N_PAGES = PAST_LEN // PAGE_SIZE
KV_ROW = 2 * KV_WIDTH

LANES = 128
V7X_VMEM_BYTES = 64 * 1024 * 1024
VMEM_LIMIT_BYTES = 56 * 1024 * 1024

NEG = -1e30
F32 = jnp.float32
BF16 = jnp.bfloat16

FFN_ROW_TILE = 768
FFN_FF_TILE = 512
PROJ_ROW_TILE = 528
PROJ_COL_TILE = 512
OUT_ROW_TILE = 256
HG_CHUNK_PROMPT = 16
HG_CHUNK_SAMPLE = DEC_SEQ
CMP_UNIT_ROWS = 2048
CMP_UNITS = 4


def _cparams(*sem):
    return pltpu.CompilerParams(dimension_semantics=sem, vmem_limit_bytes=VMEM_LIMIT_BYTES)


def _t5_thresholds():
    n = np.arange(0, 2 * MAX_DISTANCE + 2)
    exact = N_BUCKETS // 2
    logn = np.log(np.maximum(n, 1).astype(np.float64) / exact)
    large = exact + (logn / math.log(MAX_DISTANCE / exact) * (N_BUCKETS - exact)).astype(np.int32)
    b = np.where(n < exact, n, np.minimum(large, N_BUCKETS - 1))
    return [int(n[b >= k][0]) for k in range(N_BUCKETS)]


T5_THRESHOLDS = _t5_thresholds()


def _block_diag_ones(n, blk):
    i = np.arange(n)
    return (i[:, None] // blk == i[None, :] // blk).astype(np.float32)


def _split2(x):
    hi = x.astype(BF16)
    lo = (x - hi.astype(F32)).astype(BF16)
    return hi, lo


def _group_rms(t, bd, gain, width):
    hi, lo = _split2(t * t)
    ss = (jnp.dot(hi, bd, preferred_element_type=F32) + jnp.dot(lo, bd, preferred_element_type=F32))
    return t * lax.rsqrt(ss * (1.0 / width) + EPS) * gain


def _ffn_kernel(*refs, split_in, split_out):
    n_in = 2 if split_in else 1
    x_ref = refs[0]
    gain_ref, wg_ref, wu_ref, wd_ref = refs[n_in:n_in + 4]
    o_ref = refs[n_in + 4]
    xn_ref, acc_ref = refs[-2:]
    j = pl.program_id(1)
    last_tile = pl.program_id(0) == pl.num_programs(0) - 1
    tm = x_ref.shape[0]

    def rows_in():
        x = x_ref[...]
        if split_in:
            xs = refs[1][...]
            x = jnp.where(last_tile, jnp.concatenate([x[:tm - xs.shape[0]], xs], axis=0), x)
        return x

    @pl.when(j == 0)
    def _():
        x = rows_in()
        y = x * lax.rsqrt(jnp.mean(x * x, axis=-1, keepdims=True) + EPS)
        xn_ref[...] = (y * gain_ref[...]).astype(BF16)
        acc_ref[...] = jnp.zeros_like(acc_ref)

    xn = xn_ref[...]
    g = jnp.dot(xn, wg_ref[...], preferred_element_type=F32)
    u = jnp.dot(xn, wu_ref[...], preferred_element_type=F32)
    a = (g * jax.nn.sigmoid(g) * u).astype(BF16)
    acc_ref[...] += jnp.dot(a, wd_ref[...], preferred_element_type=F32)

    @pl.when(j == pl.num_programs(1) - 1)
    def _():
        res = rows_in() + 0.5 * acc_ref[...]
        o_ref[...] = res
        if split_out:
            os_ref = refs[n_in + 5]

            @pl.when(last_tile)
            def _():
                os_ref[...] = res[tm - os_ref.shape[0]:]


def _ffn(x, gain, w_gu, w_down, split_out=False):
    split_in = isinstance(x, tuple)
    tm, tf = FFN_ROW_TILE, FFN_FF_TILE
    n_p, n_s = N_PROMPT, N_SAMPLE
    n, d = n_p + n_s, D_MODEL
    assert n % tm == 0 and D_FF % tf == 0 and n_p % tm == tm - n_s
    nj = D_FF // tf
    row = pl.BlockSpec((tm, d), lambda i, j: (i, 0))
    tail = pl.BlockSpec((n_s, d), lambda i, j: (0, 0))
    xs = x if split_in else (x,)
    return pl.pallas_call(
        functools.partial(_ffn_kernel, split_in=split_in, split_out=split_out),
        grid=(n // tm, nj),
        in_specs=([row, tail] if split_in else [row]) + [
            pl.BlockSpec((1, d), lambda i, j: (0, 0)),
            pl.BlockSpec((d, tf), lambda i, j: (0, j)),
            pl.BlockSpec((d, tf), lambda i, j: (0, j + nj)),
            pl.BlockSpec((tf, d), lambda i, j: (j, 0)),
        ],
        out_specs=(row, tail) if split_out else row,
        out_shape=((jax.ShapeDtypeStruct((n_p, d), F32), jax.ShapeDtypeStruct((n_s, d), F32)) if split_out
                   else jax.ShapeDtypeStruct((n, d), F32)),
        scratch_shapes=[pltpu.VMEM((tm, d), BF16), pltpu.VMEM((tm, d), F32)],
        compiler_params=_cparams("arbitrary", "arbitrary"),
        name="ffn",
    )(*xs, gain.reshape(1, d), w_gu, w_gu, w_down)


PB_Q, PB_CMP, PB_SLC, PB_WIN, PB_HQ, PB_HF, PB_HI, PB_HG, PB_GATE = 0, 2, 3, 4, 5, 7, 9, 11, 13
PROJ_N_BLOCKS = 14


def _permute_w_in(w_in):
    p = [int(v) for v in np.cumsum(IN_SPLITS)]
    a_g = w_in[:, p[1]:p[2]]
    src = np.zeros((PROJ_COL_TILE,), np.int32)
    valid = np.zeros((PROJ_COL_TILE,), bool)
    for g in range(ATT_GROUPS):
        for br in range(N_BRANCH):
            for hp in range(ATT_HPG):
                src[g * LANES + br * ATT_HPG + hp] = br * ATT_HEADS + g * ATT_HPG + hp
                valid[g * LANES + br * ATT_HPG + hp] = True
    gate = jnp.where(jnp.asarray(valid)[None, :], a_g[:, src], 0.0)
    return jnp.concatenate([w_in[:, :p[1]], w_in[:, p[2]:], gate], axis=1).astype(BF16)


def _pack_kv(k, v):
    parts = []
    for g in range(ATT_GROUPS):
        parts.append(k[:, g * HEAD_DIM:(g + 1) * HEAD_DIM])
        parts.append(v[:, g * HEAD_DIM:(g + 1) * HEAD_DIM])
    return jnp.concatenate(parts, axis=1).astype(BF16)


def _proj_kernel(x_ref, gain_ref, w_ref, bd_ref, qg_ref, kg_ref, lbp_ref,
                 q_ref, cmp_ref, slc_ref, slcp_ref, win_ref, winp_ref,
                 hq_ref, lf_ref, hk_ref, hv_ref, hog_ref, gate_ref, xn_ref):
    c = pl.program_id(1)

    @pl.when(c == 0)
    def _():
        x = x_ref[...]
        y = x * lax.rsqrt(jnp.mean(x * x, axis=-1, keepdims=True) + EPS)
        xn_ref[...] = (y * gain_ref[...]).astype(BF16)

    acc = jnp.dot(xn_ref[...], w_ref[...], preferred_element_type=F32)
    half = KV_WIDTH

    @pl.when(c < PB_CMP)
    def _():
        q_ref[...] = _group_rms(acc, bd_ref[...], qg_ref[...], HEAD_DIM)

    @pl.when(c == PB_CMP)
    def _():
        cmp_ref[...] = acc

    def kv_branch(rows_ref, pack_ref, br):
        k = _group_rms(acc[:, :half], bd_ref[:half, :half], kg_ref[br - 1:br, :], HEAD_DIM)
        v = acc[:, half:]
        rows_ref[:, :half] = k
        rows_ref[:, half:] = v
        pack_ref[...] = _pack_kv(k, v)

    @pl.when(c == PB_SLC)
    def _():
        kv_branch(slc_ref, slcp_ref, 1)

    @pl.when(c == PB_WIN)
    def _():
        kv_branch(win_ref, winp_ref, 2)

    @pl.when((c >= PB_HQ) & (c < PB_HF))
    def _():
        hq_ref[...] = acc

    @pl.when((c >= PB_HF) & (c < PB_HI))
    def _():
        p = lbp_ref[...]
        e = jnp.exp(p - jnp.max(p, axis=0, keepdims=True))
        lb = e[0:1, :] / jnp.sum(e, axis=0, keepdims=True)
        lf_ref[...] = jnp.log(lb + (1.0 - lb) * jax.nn.sigmoid(acc))
        hk_ref[...] = (1.0 - lb) * jax.nn.sigmoid(-acc)

    @pl.when((c >= PB_HI) & (c < PB_HG))
    def _():
        hv_ref[...] = acc

    @pl.when((c >= PB_HG) & (c < PB_GATE))
    def _():
        hog_ref[...] = acc

    @pl.when(c == PB_GATE)
    def _():
        gate_ref[...] = jax.nn.sigmoid(acc)


def _project_all(y, gain, w_perm, q_gain, k_gain, lb_logits):
    n, d = y.shape
    tm, tc = PROJ_ROW_TILE, PROJ_COL_TILE
    assert n % tm == 0 and DEPTH == 1
    bd = jnp.asarray(_block_diag_ones(tc, HEAD_DIM), BF16)

    def two(first):
        return lambda i, c: (i, jnp.clip(c - first, 0, 1))

    one = lambda i, c: (i, 0)
    wide = lambda dt: jax.ShapeDtypeStruct((n, 2 * tc), dt)
    narrow = lambda dt: jax.ShapeDtypeStruct((n, tc), dt)
    out_shape = (wide(F32), narrow(F32), narrow(F32), narrow(BF16), narrow(F32), narrow(BF16),
                 wide(F32), wide(F32), wide(F32), wide(F32), wide(F32), narrow(F32))
    blk = lambda f: pl.BlockSpec((tm, tc), f)
    out_specs = (blk(two(PB_Q)), blk(one), blk(one), blk(one), blk(one), blk(one),
                 blk(two(PB_HQ)), blk(two(PB_HF)), blk(two(PB_HF)), blk(two(PB_HI)), blk(two(PB_HG)), blk(one))
    return pl.pallas_call(
        _proj_kernel,
        grid=(n // tm, PROJ_N_BLOCKS),
        in_specs=[
            pl.BlockSpec((tm, d), lambda i, c: (i, 0)),
            pl.BlockSpec((1, d), lambda i, c: (0, 0)),
            pl.BlockSpec((d, tc), lambda i, c: (0, c)),
            pl.BlockSpec((tc, tc), lambda i, c: (0, 0)),
            pl.BlockSpec((1, tc), lambda i, c: (0, jnp.clip(c, 0, 1))),
            pl.BlockSpec((2, KV_WIDTH), lambda i, c: (0, 0)),
            pl.BlockSpec((DEPTH + 1, tc), lambda i, c: (0, jnp.clip(c - PB_HF, 0, 1))),
        ],
        out_specs=out_specs,
        out_shape=out_shape,
        scratch_shapes=[pltpu.VMEM((tm, d), BF16)],
        compiler_params=_cparams("arbitrary", "arbitrary"),
        name="proj",
    )(y, gain.reshape(1, d), w_perm, bd, q_gain.reshape(1, ATT_WIDTH),
      k_gain[1:].reshape(2, KV_WIDTH), lb_logits)


def _hgrn_kernel(q_ref, k_ref, v_ref, lf_ref, tri_ref, ones_ref, s0_ref, o_ref, st_ref,
                 qe_scr, kd_scr, vt_scr, dec_scr, oi_scr, *, t_len, ch, has_state):
    nj = t_len // ch
    shape3 = (nj, ch, HG_KEY)
    q3, k3, v3, lf3 = q_ref[...], k_ref[...], v_ref[...], lf_ref[...]
    tl = lax.broadcasted_iota(jnp.int32, shape3, 1)

    def row(x3, s):
        return jnp.broadcast_to(x3[:, s:s + 1, :], shape3)

    if t_len >= 256:
        lf2 = lf3.reshape(t_len, HG_KEY)
        parts = []
        for r0 in range(0, t_len, 256):
            x = lf2[r0:r0 + 256]
            hi = x.astype(BF16)
            r1 = x - hi.astype(F32)
            mid = r1.astype(BF16)
            lo = (r1 - mid.astype(F32)).astype(BF16)
            tri = tri_ref[...]
            parts.append(jnp.dot(tri, hi, preferred_element_type=F32)
                         + jnp.dot(tri, mid, preferred_element_type=F32)
                         + jnp.dot(tri, lo, preferred_element_type=F32))
        b3 = jnp.concatenate(parts, axis=0).reshape(shape3)
    else:
        b3 = jnp.zeros(shape3, F32)
        for s in range(ch):
            b3 = b3 + jnp.where(tl >= s, row(lf3, s), 0.0)

    bl3 = row(b3, ch - 1)
    qe3 = q3 * jnp.exp(b3)
    kd3 = k3 * jnp.exp(bl3 - b3)
    dec_scr[...] = jnp.exp(b3[:, ch - 1:ch, :])
    blocked = t_len % LANES == 0
    if blocked:
        nb = t_len // LANES
        qe2, kd2, v2 = (x.reshape(t_len, HG_KEY) for x in (qe3, kd3, v3))
        for m in range(nb):
            blk = slice(m * LANES, (m + 1) * LANES)
            qe_scr[m] = qe2[blk].T.astype(BF16)
            vt_scr[m] = v2[blk].T.astype(BF16)
            kd_scr[m] = kd2[blk].astype(BF16)
    else:
        qe_scr[...] = qe3.astype(BF16)
        kd_scr[...] = kd3.astype(BF16)

    ones = ones_ref[...]
    n8 = ch // 8
    shape8 = (nj, 8, HG_KEY)
    sub8 = lax.broadcasted_iota(jnp.int32, shape8, 1)
    q5, k5, v5, b5 = (x.reshape(nj, n8, 8, HG_KEY) for x in (q3, k3, v3, b3))
    od = [jnp.zeros(shape8, F32) for _ in range(n8)]
    for s in range(ch):
        hs, ss = divmod(s, 8)
        ks, bs, vs = (jnp.broadcast_to(x[:, hs, ss:ss + 1, :], shape8) for x in (k5, b5, v5))
        for hh in range(hs, n8):
            diff = b5[:, hh] - bs
            if hh == hs:
                diff = jnp.where(sub8 >= ss, diff, NEG)
            w = q5[:, hh] * ks * jnp.exp(diff)
            a = jnp.dot(w.reshape(nj * 8, HG_KEY).astype(BF16), ones, preferred_element_type=F32)
            od[hh] = od[hh] + a.reshape(shape8) * vs
    for hh in range(n8):
        o_ref[:, hh * 8:(hh + 1) * 8, :] = od[hh]

    if has_state:
        st0 = s0_ref[0, 0].T
    else:
        st0 = jnp.zeros((HG_VAL, HG_KEY), F32)

    if blocked:
        cpb = LANES // ch
        lane_chunk = lax.broadcasted_iota(jnp.int32, (HG_KEY, LANES), 1) >> (ch.bit_length() - 1)
        keep = [jnp.where(lane_chunk == r, 1.0, 0.0).astype(BF16) for r in range(cpb)]

        def body(m, st):
            qet, vt, kd = qe_scr[m], vt_scr[m], kd_scr[m]
            ot = jnp.zeros((HG_VAL, LANES), F32)
            for r in range(cpb):
                ot = ot + jnp.dot(st.astype(BF16), qet * keep[r], preferred_element_type=F32)
                ut = jnp.dot(vt * keep[r], kd, preferred_element_type=F32)
                st = st * dec_scr[m * cpb + r] + ut
            oi_scr[m] = ot
            return st

        st = lax.fori_loop(0, t_len // LANES, body, st0)
        for m in range(t_len // LANES):
            o_ref[m * cpb:(m + 1) * cpb] = o_ref[m * cpb:(m + 1) * cpb] + oi_scr[m].T.reshape(cpb, ch, HG_VAL)
    else:
        def body(j, st):
            oi_scr[j] = lax.dot_general(qe_scr[j], st.astype(BF16), NT_DIMS, preferred_element_type=F32)
            ut = lax.dot_general(v_ref[j].astype(BF16), kd_scr[j], (((0,), (0,)), ((), ())),
                                 preferred_element_type=F32)
            return st * dec_scr[j] + ut

        st = lax.fori_loop(0, nj, body, st0, unroll=True)
        o_ref[...] = o_ref[...] + oi_scr[...]
    st_ref[0, 0] = st.T


def _hgrn(hq, hk, hv, lf, s0, *, n_seq, t_len, ch, row0):
    n = hq.shape[0]
    nj = t_len // ch
    assert row0 % t_len == 0 and t_len % ch == 0 and ch % 8 == 0
    blk0 = row0 // t_len
    r3 = lambda a: a.reshape(n // ch, ch, HG_WIDTH)
    has_state = s0 is not None
    if not has_state:
        s0 = jnp.zeros((1, 1, HG_KEY, HG_VAL), F32)
    tri_n = 256 if t_len >= 256 else 8
    i = np.arange(tri_n)
    tri = jnp.asarray(((i[:, None] // ch == i[None, :] // ch) & (i[:, None] >= i[None, :])).astype(np.float32), BF16)
    ones = jnp.ones((HG_KEY, HG_KEY), BF16)
    blk_shape = (t_len // LANES, LANES, HG_KEY) if t_len % LANES == 0 else (nj, ch, HG_KEY)
    hps = HG_HEADS if nj == 1 else 1
    seq = pl.BlockSpec((nj, ch, hps * HG_KEY), lambda b, h: (blk0 + b, 0, h))
    state_in = pl.BlockSpec((1, hps, HG_KEY, HG_VAL), (lambda b, h: (b, h, 0, 0)) if has_state else (lambda b, h: (0, 0, 0, 0)))

    def body(q_ref, k_ref, v_ref, lf_ref, tri_ref, ones_ref, s0_ref, o_ref, st_ref, *scratch):
        for h in range(hps):
            ln = slice(h * HG_KEY, (h + 1) * HG_KEY)
            _hgrn_kernel(q_ref.at[:, :, ln], k_ref.at[:, :, ln], v_ref.at[:, :, ln], lf_ref.at[:, :, ln], tri_ref,
                         ones_ref, s0_ref.at[:, h:h + 1], o_ref.at[:, :, ln], st_ref.at[:, h:h + 1], *scratch,
                         t_len=t_len, ch=ch, has_state=has_state)

    o, st = pl.pallas_call(
        body,
        grid=(n_seq, HG_HEADS // hps),
        in_specs=[seq, seq, seq, seq,
                  pl.BlockSpec((tri_n, tri_n), lambda b, h: (0, 0)),
                  pl.BlockSpec((HG_KEY, HG_KEY), lambda b, h: (0, 0)),
                  state_in],
        out_specs=(pl.BlockSpec((nj, ch, hps * HG_VAL), lambda b, h: (b, 0, h)),
                   pl.BlockSpec((1, hps, HG_KEY, HG_VAL), lambda b, h: (b, h, 0, 0))),
        out_shape=(jax.ShapeDtypeStruct((n_seq * nj, ch, HG_WIDTH), F32),
                   jax.ShapeDtypeStruct((n_seq, HG_HEADS, HG_KEY, HG_VAL), F32)),
        scratch_shapes=[pltpu.VMEM(blk_shape, BF16), pltpu.VMEM(blk_shape, BF16), pltpu.VMEM(blk_shape, BF16),
                        pltpu.VMEM((nj, 1, HG_KEY), F32), pltpu.VMEM(blk_shape, F32)],
        compiler_params=_cparams("arbitrary", "arbitrary"),
        name="hgrn",
    )(r3(hq), r3(hk), r3(hv), r3(lf), tri, ones, s0)
    return o.reshape(n_seq * t_len, HG_WIDTH), st


def _mixout_kernel(y_ref, oap_ref, oas_ref, ohp_ref, ohs_ref, og_ref, ag_ref, hgain_ref, w_ref, o_ref):
    is_sample = pl.program_id(0) == pl.num_programs(0) - 1
    oa = jnp.where(is_sample, oas_ref[...], oap_ref[...])
    a = oa * lax.rsqrt(jnp.mean(oa * oa, axis=-1, keepdims=True) + EPS) * ag_ref[...]
    oh = jnp.where(is_sample, ohs_ref[...], ohp_ref[...])
    hs = []
    for h in range(HG_HEADS):
        x = oh[:, h * HG_VAL:(h + 1) * HG_VAL]
        hs.append(x * lax.rsqrt(jnp.mean(x * x, axis=-1, keepdims=True) + EPS))
    og = og_ref[...]
    hh = jnp.concatenate(hs, axis=1) * hgain_ref[...] * (og * jax.nn.sigmoid(og))
    m = (jnp.dot(a.astype(BF16), w_ref[:ATT_WIDTH, :], preferred_element_type=F32)
         + jnp.dot(hh.astype(BF16), w_ref[ATT_WIDTH:, :], preferred_element_type=F32))
    o_ref[...] = y_ref[...] + m


def _mixer_out(y, o_att_p, o_att_s, o_hg_p, o_hg_s, og, attn_gain, hg_gain, w_out):
    n, d = y.shape
    tm = OUT_ROW_TILE
    n_p = o_att_p.shape[0]
    assert n % tm == 0 and n_p % tm == 0 and o_att_s.shape[0] == tm and n == n_p + tm
    row = lambda w: pl.BlockSpec((tm, w), lambda i: (i, 0))
    prompt_row = lambda w: pl.BlockSpec((tm, w), lambda i: (jnp.minimum(i, n_p // tm - 1), 0))
    const = lambda s: pl.BlockSpec(s, lambda i: (0, 0))
    return pl.pallas_call(
        _mixout_kernel,
        grid=(n // tm,),
        in_specs=[row(d), prompt_row(ATT_WIDTH), const((tm, ATT_WIDTH)), prompt_row(HG_WIDTH), const((tm, HG_WIDTH)),
                  row(HG_WIDTH), const((1, ATT_WIDTH)), const((1, HG_WIDTH)), const((MIX_WIDTH, d))],
        out_specs=row(d),
        out_shape=jax.ShapeDtypeStruct((n, d), F32),
        compiler_params=_cparams("arbitrary"),
        name="mixout",
    )(y, o_att_p, o_att_s, o_hg_p, o_hg_s, og, attn_gain.reshape(1, ATT_WIDTH), hg_gain.reshape(1, HG_WIDTH), w_out)


def _bias_kernel(tbl_ref, o_ref, *, a0, ar, ac, rows_blk):
    h = pl.program_id(0)
    rb = pl.program_id(1)
    shape = o_ref.shape[1:]
    r = lax.broadcasted_iota(jnp.int32, shape, 0) + rb * rows_blk
    c = lax.broadcasted_iota(jnp.int32, shape, 1)
    n = a0 + ar * r + ac * c
    out = jnp.full(shape, tbl_ref[0, h], F32)
    for k in range(1, N_BUCKETS):
        out = jnp.where(n >= T5_THRESHOLDS[k], tbl_ref[k, h], out)
    o_ref[0] = out


def _bias_table(tbl, rows, cols, a0, ar, ac):
    rows_blk = max(r for r in range(8, min(rows, 512) + 1, 8) if rows % r == 0)
    assert cols % LANES == 0
    return pl.pallas_call(
        functools.partial(_bias_kernel, a0=a0, ar=ar, ac=ac, rows_blk=rows_blk),
        grid=(ATT_HEADS, rows // rows_blk),
        in_specs=[pl.BlockSpec(memory_space=pltpu.SMEM)],
        out_specs=pl.BlockSpec((1, rows_blk, cols), lambda h, rb: (h, rb, 0)),
        out_shape=jax.ShapeDtypeStruct((ATT_HEADS, rows, cols), F32),
        compiler_params=_cparams("arbitrary", "arbitrary"),
        name="t5_bias",
    )(tbl)


def _compress_weights(w1, b1, w2, b2):
    w = w1.reshape(2, CMP_RATIO, CMP_STRIDE, HEAD_DIM, CMP_HIDDEN)
    z = jnp.zeros_like(w)
    top = jnp.concatenate([w, z], axis=-1)
    bot = jnp.concatenate([z, w], axis=-1)
    wpad = jnp.stack([top, bot], axis=3)
    wpad = wpad.reshape(2 * CMP_RATIO, CMP_STRIDE * LANES, 2 * CMP_HIDDEN).astype(BF16)
    z2 = jnp.zeros_like(w2)
    w2pad = jnp.concatenate([jnp.concatenate([w2, z2], axis=-1), jnp.concatenate([z2, w2], axis=-1)], axis=1)
    return wpad, jnp.concatenate([b1, b1], axis=-1), w2pad.astype(BF16), jnp.concatenate([b2, b2], axis=-1)


def _compress_kernel(rows_ref, perm_ref, wpad_ref, b1_ref, w2pad_ref, b2_ref, kg_ref, bd_ref, out_ref, xs_ref):
    u = pl.program_id(1)
    n_chunk = CMP_UNITS * CMP_UNIT_ROWS // CMP_STRIDE
    perm = perm_ref[...]

    for t in range(CMP_UNIT_ROWS // 256):
        x = rows_ref[t * 256:(t + 1) * 256, :]
        y = jnp.dot(perm, x.astype(BF16), preferred_element_type=F32).astype(BF16)
        base = pl.multiple_of(u * (CMP_UNIT_ROWS // CMP_STRIDE) + t * 16, 16)
        for s in range(CMP_STRIDE):
            xs_ref[s, pl.ds(base, 16), :] = y[s * 16:(s + 1) * 16, :]

    @pl.when(u == CMP_UNITS - 1)
    def _():
        halves = []
        for kv in range(2):
            outs = []
            for j in range(2):
                col = (kv * 2 + j) * LANES
                lhs = jnp.concatenate([xs_ref[s, :, col:col + LANES] for s in range(CMP_STRIDE)], axis=1)
                h0 = jnp.dot(lhs, wpad_ref[kv * CMP_RATIO + 0], preferred_element_type=F32)
                h1 = jnp.dot(lhs, wpad_ref[kv * CMP_RATIO + 1], preferred_element_type=F32)
                h = b1_ref[kv:kv + 1, :] + h0 + pltpu.roll(h1, n_chunk - 1, axis=0)
                a = (h * jax.nn.sigmoid(h)).astype(BF16)
                outs.append(jnp.dot(a, w2pad_ref[kv], preferred_element_type=F32) + b2_ref[kv:kv + 1, :])
            halves.append(jnp.concatenate(outs, axis=1))
        pk = _pack_kv(_group_rms(halves[0], bd_ref[...], kg_ref[...], HEAD_DIM), halves[1])
        for g in range(ATT_GROUPS):
            out_ref[0, g] = pk[:, g * LANES:(g + 1) * LANES]


def _compress(rows, cw, k_gain0):
    wpad, b1, w2pad, b2 = cw
    n_chunk = CMP_UNITS * CMP_UNIT_ROWS // CMP_STRIDE
    i = np.arange(256)
    perm = np.zeros((256, 256), np.float32)
    perm[(i % 16) * 16 + i // 16, i] = 1.0
    consts = (jnp.asarray(perm, BF16), wpad, b1, w2pad, b2, k_gain0.reshape(1, KV_WIDTH),
              jnp.asarray(_block_diag_ones(KV_WIDTH, HEAD_DIM), BF16))
    const = lambda a: pl.BlockSpec(a.shape, lambda b, u, nd=a.ndim: (0,) * nd)
    return pl.pallas_call(
        _compress_kernel,
        grid=(1, CMP_UNITS),
        in_specs=[pl.BlockSpec((CMP_UNIT_ROWS, KV_ROW), lambda b, u: (u, 0))] + [const(a) for a in consts],
        out_specs=pl.BlockSpec((1, ATT_GROUPS, n_chunk, LANES), lambda b, u: (0, 0, 0, 0)),
        out_shape=jax.ShapeDtypeStruct((1, ATT_GROUPS, n_chunk, LANES), BF16),
        scratch_shapes=[pltpu.VMEM((CMP_STRIDE, n_chunk, KV_ROW), BF16)],
        compiler_params=_cparams("arbitrary", "arbitrary"),
        name="compress",
    )(rows, *consts)


PAGED_UNIT_ROWS = 4096
PAGED_UNITS = PAST_LEN // PAGED_UNIT_ROWS
PAGES_PER_STEP = PAGED_UNIT_ROWS // PAGE_SIZE


def _cache_view(cache):
    return jnp.transpose(cache, (0, 2, 3, 4, 1))


def _compress_paged_kernel(pt_ref, *refs):
    del pt_ref
    page_refs = refs[:PAGES_PER_STEP]
    pick_ref, w1_ref, b1_ref, w2_ref, b2_ref, kg_ref, bd_ref, out_ref, xs_ref = refs[PAGES_PER_STEP:]
    u = pl.program_id(1)
    n_chunk = PAST_LEN // CMP_STRIDE
    pick = pick_ref[...]
    n_kvg = 2 * ATT_GROUPS

    for t in range(PAGES_PER_STEP // 2):
        base = pl.multiple_of(u * (PAGED_UNIT_ROWS // CMP_STRIDE) + t * 16, 16)
        kt = jnp.concatenate([page_refs[2 * t][0].reshape(n_kvg * HEAD_DIM, PAGE_SIZE),
                              page_refs[2 * t + 1][0].reshape(n_kvg * HEAD_DIM, PAGE_SIZE)], axis=1).astype(BF16)
        z = jnp.dot(kt, pick, preferred_element_type=F32)
        for kvg in range(n_kvg):
            zk = z[kvg * HEAD_DIM:(kvg + 1) * HEAD_DIM]
            y = jnp.concatenate([zk[:, :LANES], zk[:, LANES:]], axis=0).T.astype(BF16)
            for s2 in range(CMP_STRIDE // 2):
                xs_ref[kvg, s2, pl.ds(base, 16), :] = y[s2 * 16:(s2 + 1) * 16, :]

    @pl.when(u == PAGED_UNITS - 1)
    def _():
        halves = []
        for kv in range(2):
            lhs = jnp.concatenate(
                [jnp.concatenate([xs_ref[kv * ATT_GROUPS + g, s2] for s2 in range(CMP_STRIDE // 2)], axis=1)
                 for g in range(ATT_GROUPS)], axis=0)
            h0 = jnp.dot(lhs, w1_ref[kv * CMP_RATIO + 0], preferred_element_type=F32)
            h1 = jnp.dot(lhs, w1_ref[kv * CMP_RATIO + 1], preferred_element_type=F32)
            h = b1_ref[kv:kv + 1, :] + h0 + pltpu.roll(h1, ATT_GROUPS * n_chunk - 1, axis=0)
            a = (h * jax.nn.sigmoid(h)).astype(BF16)
            o = b2_ref[kv:kv + 1, :]
            for g in range(ATT_GROUPS):
                o = o + jnp.dot(a[g * n_chunk:(g + 1) * n_chunk], w2_ref[kv * ATT_GROUPS + g], preferred_element_type=F32)
            halves.append(o)
        out_ref[0, :, :KV_WIDTH] = _group_rms(halves[0], bd_ref[...], kg_ref[...], HEAD_DIM).astype(BF16)
        out_ref[0, :, KV_WIDTH:] = halves[1].astype(BF16)


def _compress_paged(cache_t, page_table, w1, b1, w2, b2, k_gain0):
    n_seq = page_table.shape[0]
    n_chunk = PAST_LEN // CMP_STRIDE
    r = np.arange(LANES)
    s2, c = r // 16, r % 16
    pick = np.zeros((2 * PAGE_SIZE, 2, LANES), np.float32)
    for half in range(2):
        pick[CMP_STRIDE * c + 2 * s2 + half, half, r] = 1.0
    w2p = jnp.zeros((2, ATT_GROUPS, CMP_HIDDEN, KV_WIDTH), F32)
    for g in range(ATT_GROUPS):
        w2p = w2p.at[:, g, :, g * HEAD_DIM:(g + 1) * HEAD_DIM].set(w2)
    consts = (jnp.asarray(pick.reshape(2 * PAGE_SIZE, 2 * LANES), BF16),
              w1.reshape(2 * CMP_RATIO, CMP_STRIDE * HEAD_DIM, CMP_HIDDEN).astype(BF16), b1,
              w2p.reshape(2 * ATT_GROUPS, CMP_HIDDEN, KV_WIDTH).astype(BF16), jnp.tile(b2, (1, ATT_GROUPS)),
              k_gain0.reshape(1, KV_WIDTH), jnp.asarray(_block_diag_ones(KV_WIDTH, HEAD_DIM), BF16))
    const = lambda a: pl.BlockSpec(a.shape, lambda b, u, pt, nd=a.ndim: (0,) * nd)
    pages = [pl.BlockSpec((1, 2, ATT_GROUPS, HEAD_DIM, PAGE_SIZE),
                          lambda b, u, pt, k=k: (pt[b, u * PAGES_PER_STEP + k], 0, 0, 0, 0))
             for k in range(PAGES_PER_STEP)]
    grid_spec = pltpu.PrefetchScalarGridSpec(
        num_scalar_prefetch=1, grid=(n_seq, PAGED_UNITS),
        in_specs=pages + [const(a) for a in consts],
        out_specs=pl.BlockSpec((1, n_chunk, KV_ROW), lambda b, u, pt: (b, 0, 0)),
        scratch_shapes=[pltpu.VMEM((2 * ATT_GROUPS, CMP_STRIDE // 2, n_chunk, LANES), BF16)])
    return pl.pallas_call(
        _compress_paged_kernel, grid_spec=grid_spec,
        out_shape=jax.ShapeDtypeStruct((n_seq, n_chunk, KV_ROW), BF16),
        compiler_params=_cparams("arbitrary", "arbitrary"),
        name="compress_paged",
    )(page_table, *([cache_t] * PAGES_PER_STEP), *consts)


def _block_overlap(n_cmp, n_blk):
    cs = np.arange(n_cmp)[:, None] * CMP_STRIDE
    bs = np.arange(n_blk)[None, :] * SEL_BLOCK
    ov = np.minimum(cs + CMP_BLOCK, bs + SEL_BLOCK) - np.maximum(cs, bs)
    return (np.clip(ov, 0, None) / CMP_BLOCK).astype(np.float32)


def _overlap_padded(n_cmp, n_blk, rows, cols):
    ov = np.zeros((rows, cols), np.float32)
    ov[:n_cmp, :n_blk] = _block_overlap(n_cmp, n_blk)
    return jnp.asarray(ov, BF16)


def _softmax_rows(s, mask):
    s = jnp.where(mask, s, NEG)
    m = jnp.max(s, axis=1, keepdims=True)
    e = jnp.where(mask, jnp.exp(s - m), 0.0)
    d = jnp.sum(e, axis=1, keepdims=True)
    return e / jnp.where(d > 0, d, 1.0)


def _select_blocks(imp, qpos, n_blk, axis):
    blk = lax.broadcasted_iota(jnp.int32, imp.shape, axis)
    cur = qpos >> SEL_SHIFT
    forced = (blk == 0) | (blk == cur) | (blk == cur - 1)
    valid = blk * SEL_BLOCK <= qpos
    score = jnp.where(forced, ATT_HPG + 1.0, jnp.where(valid, imp, -1.0))
    ahead = jnp.zeros(imp.shape, F32)
    for m in range(n_blk):
        sm = jnp.broadcast_to(score[m:m + 1, :] if axis == 0 else score[:, m:m + 1], imp.shape)
        tie = jnp.where(blk > m, 1.0, 0.0)
        ahead = ahead + jnp.where(sm > score, 1.0, jnp.where(sm == score, tie, 0.0))
    return jnp.where(blk < n_blk, jnp.where(ahead < N_SEL, 1.0, 0.0), 0.0)


SEL_SHIFT = SEL_BLOCK.bit_length() - 1
NT_DIMS = (((1,), (1,)), ((), ()))


SLC_UNROLL = 4
TOEP_TILES = min(SEQ // LANES, -(-(T5_THRESHOLDS[-1] + LANES - 1) // LANES) + 1)


NSA_GROUPS_PER_STEP = 2


def _nsa_prompt_multi_kernel(q_ref, gate_ref, kvc_ref, slc_ref, win_ref, bcmp_ref, toep_ref, ov_ref, o_ref,
                             s_scr, m_scr, l_scr, acc_scr):
    i = pl.program_id(2)
    ngs = NSA_GROUPS_PER_STEP
    tile = (Q_BLOCK, LANES)
    rows = ATT_HPG * Q_BLOCK
    lane = lax.broadcasted_iota(jnp.int32, tile, 1)
    sub = lax.broadcasted_iota(jnp.int32, tile, 0)
    low = lane < HEAD_DIM
    qpos = i * Q_BLOCK + sub
    head = lambda x, p: x[p * Q_BLOCK:(p + 1) * Q_BLOCK]
    heads = range(ATT_HPG)
    n_blk = SEQ // SEL_BLOCK
    blk_row = lax.broadcasted_iota(jnp.int32, (n_blk, LANES), 0)
    blk_of_key = lax.broadcasted_iota(jnp.int32, (n_blk, LANES), 1) >> SEL_SHIFT
    qpos_t = i * Q_BLOCK + lax.broadcasted_iota(jnp.int32, (n_blk, Q_BLOCK), 1)
    last_tile = SEQ // LANES - 1
    ov = ov_ref[...]
    glanes = lambda gg: slice(gg * LANES, (gg + 1) * LANES)

    def key_tile(kv_ref, gg, j):
        jc = jnp.clip(j, 0, last_tile)
        return kv_ref[pl.ds(pl.multiple_of(jc * LANES, LANES), LANES), glanes(gg)]

    def masked_scores(qs, gg, kv, j, mask):
        sc = lax.dot_general(qs, kv, NT_DIMS, preferred_element_type=F32)
        row0 = pl.multiple_of(jnp.clip(i - j, 0, TOEP_TILES - 1) * LANES, LANES)
        return [jnp.where(mask, head(sc, p) + toep_ref[gg * ATT_HPG + p, pl.ds(row0, LANES), :], NEG) for p in heads]

    def weights_times_v(scores, m, kv):
        es = [jnp.exp(scores[p] - m[p]) for p in heads]
        return es, jnp.dot(jnp.concatenate(es, axis=0).astype(BF16), kv, preferred_element_type=F32)

    def front(gg):
        qa = q_ref[:, gg * ATT_HPG * HEAD_DIM:(gg + 1) * ATT_HPG * HEAD_DIM] * (HEAD_DIM ** -0.5)
        qp = []
        for pair in range(ATT_HPG // 2):
            x = qa[:, pair * LANES:(pair + 1) * LANES]
            qp.append(jnp.where(low, x, 0.0))
            qp.append(jnp.where(low, pltpu.roll(x, HEAD_DIM, axis=1), 0.0))
        qs = jnp.concatenate(qp, axis=0).astype(BF16)

        kvc = kvc_ref[0, gg]
        s = lax.dot_general(qs, kvc, NT_DIMS, preferred_element_type=F32)
        s = s + bcmp_ref[gg * ATT_HPG:(gg + 1) * ATT_HPG].reshape(rows, LANES)
        cmask1 = (lane * CMP_STRIDE + (CMP_BLOCK - 1)) <= qpos
        pr = _softmax_rows(s, jnp.concatenate([cmask1] * ATT_HPG, axis=0))
        o_cmp = jnp.dot(pr.astype(BF16), kvc, preferred_element_type=F32)
        hi, lo = _split2(sum(head(pr, p) for p in heads))
        imp = jnp.dot(hi, ov, preferred_element_type=F32) + jnp.dot(lo, ov, preferred_element_type=F32)
        sel_t = _select_blocks(imp.T[:n_blk, :], qpos_t, n_blk, axis=0).astype(BF16)

        n_win = WINDOW // LANES + 1
        win_kv, win_s = [], []
        for w in range(n_win):
            j = i - (n_win - 1) + w
            dist = (i - j) * LANES + sub - lane
            inside = jnp.where(dist >= 0, jnp.where(dist < WINDOW, 1.0, 0.0), 0.0)
            win_kv.append(key_tile(win_ref, gg, j))
            win_s.append(masked_scores(qs, gg, win_kv[w], j, jnp.where(j >= 0, inside, 0.0) > 0.5))
        m_w = [jnp.max(functools.reduce(jnp.maximum, [win_s[w][p] for w in range(n_win)]), axis=1, keepdims=True)
               for p in heads]
        acc_w = jnp.zeros((rows, LANES), F32)
        l_w = [jnp.zeros(tile, F32) for _ in heads]
        for w in range(n_win):
            es, pv = weights_times_v(win_s[w], m_w, win_kv[w])
            acc_w = acc_w + pv
            l_w = [l_w[p] + es[p] for p in heads]
        o_win = [head(acc_w, p) / jnp.sum(l_w[p], axis=1, keepdims=True) for p in heads]
        return qs, o_cmp, sel_t, o_win

    fronts = [front(gg) for gg in range(ngs)]

    n_steps = (i + SLC_UNROLL) >> (SLC_UNROLL.bit_length() - 1)
    m_scr[...] = jnp.full(m_scr.shape, NEG, F32)

    def slc_scores(jj, carry):
        for gg in range(ngs):
            qs, _, sel_t, _ = fronts[gg]
            sps = []
            for r in range(SLC_UNROLL):
                j = SLC_UNROLL * jj + r
                dist = (i - j) * LANES + sub - lane
                expand = jnp.where(blk_row == 2 * j + blk_of_key, 1.0, 0.0).astype(BF16)
                picked = lax.dot_general(sel_t, expand, (((0,), (0,)), ((), ())), preferred_element_type=F32)
                sp = masked_scores(qs, gg, key_tile(slc_ref, gg, j), j, jnp.where(dist >= 0, picked, 0.0) > 0.5)
                for p in heads:
                    s_scr[j, gg * ATT_HPG + p] = sp[p]
                sps.append(sp)
            for p in heads:
                hp = gg * ATT_HPG + p
                m_scr[hp] = jnp.maximum(m_scr[hp], functools.reduce(jnp.maximum, [sp[p] for sp in sps]))
        return carry

    lax.fori_loop(0, n_steps, slc_scores, 0)
    for hp in range(ngs * ATT_HPG):
        m_scr[hp] = jnp.broadcast_to(jnp.max(m_scr[hp], axis=1, keepdims=True), tile)
    l_scr[...] = jnp.zeros(l_scr.shape, F32)
    acc_scr[...] = jnp.zeros(acc_scr.shape, F32)

    def slc_weights(jj, carry):
        for gg in range(ngs):
            m = [m_scr[gg * ATT_HPG + p] for p in heads]
            pvs, ess = [], []
            for r in range(SLC_UNROLL):
                j = SLC_UNROLL * jj + r
                es, pv = weights_times_v([s_scr[j, gg * ATT_HPG + p] for p in heads], m, key_tile(slc_ref, gg, j))
                pvs.append(pv)
                ess.append(es)
            acc_scr[gg] += functools.reduce(jnp.add, pvs)
            for p in heads:
                hp = gg * ATT_HPG + p
                l_scr[hp] = l_scr[hp] + functools.reduce(jnp.add, [es[p] for es in ess])
        return carry

    lax.fori_loop(0, n_steps, slc_weights, 0)

    for gg in range(ngs):
        _, o_cmp, _, o_win = fronts[gg]
        g = gate_ref[:, glanes(gg)]
        comb = []
        for p in heads:
            o_slc = head(acc_scr[gg], p) / jnp.sum(l_scr[gg * ATT_HPG + p], axis=1, keepdims=True)
            col = lambda br: jnp.broadcast_to(g[:, br * ATT_HPG + p:br * ATT_HPG + p + 1], tile)
            comb.append(col(0) * head(o_cmp, p) + col(1) * o_slc + col(2) * o_win[p])
        for pair in range(ATT_HPG // 2):
            c0 = gg * ATT_HPG * HEAD_DIM + pair * LANES
            o_ref[:, c0:c0 + LANES] = jnp.where(low, pltpu.roll(comb[2 * pair], HEAD_DIM, axis=1), comb[2 * pair + 1])


def _nsa_prompt(q, gates, kvc, slc_pack, win_pack, bias_cmp, bias_toep, n_batch=BATCH):
    nqb = SEQ // Q_BLOCK
    ngs = NSA_GROUPS_PER_STEP
    ov = _overlap_padded(SEQ // CMP_STRIDE - 1, SEQ // SEL_BLOCK, LANES, LANES)
    gw = ngs * ATT_HPG * HEAD_DIM
    nh = ngs * ATT_HPG
    return pl.pallas_call(
        _nsa_prompt_multi_kernel,
        grid=(n_batch, ATT_GROUPS // ngs, nqb),
        in_specs=[
            pl.BlockSpec((Q_BLOCK, gw), lambda b, g, i: (b * nqb + i, g)),
            pl.BlockSpec((Q_BLOCK, ngs * LANES), lambda b, g, i: (b * nqb + i, g)),
            pl.BlockSpec((1, ngs, SEQ // CMP_STRIDE, LANES), lambda b, g, i: (0, g, b, 0)),
            pl.BlockSpec((SEQ, ngs * LANES), lambda b, g, i: (b, g)),
            pl.BlockSpec((SEQ, ngs * LANES), lambda b, g, i: (b, g)),
            pl.BlockSpec((nh, Q_BLOCK, LANES), lambda b, g, i: (g, i, 0)),
            pl.BlockSpec((nh, TOEP_TILES * LANES, LANES), lambda b, g, i: (g, 0, 0)),
            pl.BlockSpec((LANES, LANES), lambda b, g, i: (0, 0)),
        ],
        out_specs=pl.BlockSpec((Q_BLOCK, gw), lambda b, g, i: (b * nqb + i, g)),
        out_shape=jax.ShapeDtypeStruct((n_batch * SEQ, ATT_WIDTH), F32),
        scratch_shapes=[pltpu.VMEM((nqb, nh, Q_BLOCK, LANES), F32),
                        pltpu.VMEM((nh, Q_BLOCK, LANES), F32), pltpu.VMEM((nh, Q_BLOCK, LANES), F32),
                        pltpu.VMEM((ngs, ATT_HPG * Q_BLOCK, LANES), F32)],
        compiler_params=_cparams("arbitrary", "arbitrary", "arbitrary"),
        name="nsa_prompt",
    )(q, gates, kvc, slc_pack, win_pack, bias_cmp, bias_toep, ov)


SAMPLE_ROWS = ATT_HEADS * DEC_SEQ
SAMPLE_PAGES_PER_STEP = 32
SAMPLE_STEPS = N_PAGES // SAMPLE_PAGES_PER_STEP
SAMPLE_N_CMP = (PAST_LEN + DEC_SEQ - CMP_BLOCK) // CMP_STRIDE + 1
SAMPLE_N_BLK = -(-(PAST_LEN + DEC_SEQ) // SEL_BLOCK)
SAMPLE_BLK_LANES = 2 * LANES


def _nsa_sample_kernel(pt_ref, q_ref, gate_ref, kvc_ref, *rest):
    page_refs = rest[:SAMPLE_PAGES_PER_STEP]
    (slc_new_ref, win_state_ref, win_new_ref, bcmp_ref, bslc_ref, bwin_ref, perm_ref, ov_ref,
     o_ref, qbd_scr, sel_scr, m_scr, l_scr, acc_scr, ocmp_scr, owin_scr) = rest[SAMPLE_PAGES_PER_STEP:]
    del pt_ref
    u = pl.program_id(1)
    tile = (SAMPLE_ROWS, LANES)
    wide = (SAMPLE_ROWS, KV_WIDTH)
    lane = lax.broadcasted_iota(jnp.int32, tile, 1)
    t_row = lax.broadcasted_iota(jnp.int32, tile, 0) & (DEC_SEQ - 1)
    rows_per_group = ATT_HPG * DEC_SEQ
    own = ((lax.broadcasted_iota(jnp.int32, wide, 1) >> (HEAD_DIM.bit_length() - 1))
           == (lax.broadcasted_iota(jnp.int32, wide, 0) >> (rows_per_group.bit_length() - 1)))

    def reset():
        m_scr[...] = jnp.full(m_scr.shape, NEG, F32)
        l_scr[...] = jnp.zeros(l_scr.shape, F32)
        acc_scr[...] = jnp.zeros(acc_scr.shape, F32)

    def finish():
        l = jnp.sum(l_scr[...], axis=1, keepdims=True)
        return acc_scr[...] / jnp.where(l > 0, l, 1.0)

    def attend(tiles):
        qbd = qbd_scr[...]
        scores = []
        m_el = None
        for kt, _, bias, mask in tiles:
            s = jnp.where(mask, jnp.dot(qbd, kt.astype(BF16), preferred_element_type=F32) + bias, NEG)
            scores.append(s)
            m_el = s if m_el is None else jnp.maximum(m_el, s)
        m_prev = m_scr[...]
        m_new = jnp.maximum(m_prev, jnp.max(m_el, axis=1, keepdims=True))
        alpha = jnp.exp(m_prev - m_new)
        l_el = alpha * l_scr[...]
        acc = jnp.concatenate([alpha, alpha], axis=1) * acc_scr[...]
        for (_, vt, _, _), s in zip(tiles, scores):
            e = jnp.exp(s - m_new)
            l_el = l_el + e
            acc = acc + lax.dot_general(e.astype(BF16), vt.astype(BF16), NT_DIMS, preferred_element_type=F32)
        m_scr[...] = m_new
        l_scr[...] = l_el
        acc_scr[...] = acc

    def page_tile(kv4, bias, mask):
        return (kv4[0].reshape(KV_WIDTH, LANES), kv4[1].reshape(KV_WIDTH, LANES), bias, mask)

    def new_tile(ref, bias, mask):
        rows = jnp.concatenate([ref[...], jnp.zeros((LANES - DEC_SEQ, KV_ROW), F32)], axis=0)
        return (rows[:, :KV_WIDTH].T, rows[:, KV_WIDTH:].T, bias, mask)

    @pl.when(u == 0)
    def _():
        q = (q_ref[...] * (HEAD_DIM ** -0.5)).astype(BF16)
        qperm = jnp.dot(q, perm_ref[...], preferred_element_type=F32)
        qfull = jnp.concatenate([qperm[:, p * KV_WIDTH:(p + 1) * KV_WIDTH]
                                 for g in range(ATT_GROUPS) for p in range(ATT_HPG)], axis=0)
        qbd = jnp.where(own, qfull, 0.0).astype(BF16)
        qbd_scr[...] = qbd

        kvc = kvc_ref[0]
        s = lax.dot_general(qbd, kvc[:, :KV_WIDTH], NT_DIMS, preferred_element_type=F32) + bcmp_ref[...]
        cmask = lax.broadcasted_iota(jnp.int32, s.shape, 1) < SAMPLE_N_CMP
        pr = _softmax_rows(s, cmask)
        ocmp_scr[...] = jnp.dot(pr.astype(BF16), kvc[:, KV_WIDTH:], preferred_element_type=F32)
        ps = []
        for g in range(ATT_GROUPS):
            r0 = g * rows_per_group
            ps.append(sum(pr[r0 + p * DEC_SEQ:r0 + (p + 1) * DEC_SEQ, :] for p in range(ATT_HPG)))
        hi, lo = _split2(jnp.concatenate(ps, axis=0))
        ov = ov_ref[...]
        imp = jnp.dot(hi, ov, preferred_element_type=F32) + jnp.dot(lo, ov, preferred_element_type=F32)
        qpos = PAST_LEN + (lax.broadcasted_iota(jnp.int32, imp.shape, 0) & (DEC_SEQ - 1))
        sel = _select_blocks(imp, qpos, SAMPLE_N_BLK, axis=1)
        sel_scr[...] = jnp.concatenate([sel[g * DEC_SEQ:(g + 1) * DEC_SEQ, :]
                                        for g in range(ATT_GROUPS) for p in range(ATT_HPG)], axis=0).astype(BF16)

        reset()
        tiles = [page_tile(win_state_ref[0, :, :, :, w * LANES:(w + 1) * LANES],
                           bwin_ref[:, w * LANES:(w + 1) * LANES], (w * LANES + lane) > t_row)
                 for w in range(WINDOW // LANES)]
        tiles.append(new_tile(win_new_ref, bwin_ref[:, WINDOW:WINDOW + LANES], lane <= t_row))
        attend(tiles)
        owin_scr[...] = finish()
        reset()

    blk_of_key = lax.broadcasted_iota(jnp.int32, (SAMPLE_BLK_LANES, LANES), 1) >> SEL_SHIFT
    blk_row = lax.broadcasted_iota(jnp.int32, (SAMPLE_BLK_LANES, LANES), 0)
    sel = sel_scr[...]
    tiles = []
    for k in range(SAMPLE_PAGES_PER_STEP):
        pg = u * SAMPLE_PAGES_PER_STEP + k
        expand = jnp.where(blk_row == 2 * pg + blk_of_key, 1.0, 0.0).astype(BF16)
        picked = jnp.dot(sel, expand, preferred_element_type=F32) > 0.5
        bias = bslc_ref[:, pl.ds(pl.multiple_of(pg * LANES, LANES), LANES)]
        tiles.append(page_tile(page_refs[k][0], bias, picked))
    attend(tiles)

    @pl.when(u == SAMPLE_STEPS - 1)
    def _():
        attend([new_tile(slc_new_ref, bslc_ref[:, PAST_LEN:PAST_LEN + LANES], lane <= t_row)])
        o_slc = finish()
        gt = gate_ref[...]

        def gate_rows(br):
            cols = []
            for g in range(ATT_GROUPS):
                for p in range(ATT_HPG):
                    c = g * LANES + br * ATT_HPG + p
                    cols.append(jnp.broadcast_to(gt[:, c:c + 1], (DEC_SEQ, KV_WIDTH)))
            return jnp.concatenate(cols, axis=0)

        comb = gate_rows(0) * ocmp_scr[...] + gate_rows(1) * o_slc + gate_rows(2) * owin_scr[...]
        comb = jnp.where(own, comb, 0.0)
        per_head = []
        for p in range(ATT_HPG):
            per_head.append(sum(comb[(g * ATT_HPG + p) * DEC_SEQ:(g * ATT_HPG + p + 1) * DEC_SEQ, :]
                                for g in range(ATT_GROUPS)))
        hi, lo = _split2(jnp.concatenate(per_head, axis=1))
        perm = perm_ref[...]
        o_ref[...] = (lax.dot_general(hi, perm, NT_DIMS, preferred_element_type=F32)
                      + lax.dot_general(lo, perm, NT_DIMS, preferred_element_type=F32))


def _nsa_sample(q, gates, kvc, cache_slc, page_table, slc_rows, state_win, win_rows, bias_cmp, bias_slc, bias_win):
    n_seq = page_table.shape[0]
    row0 = N_PROMPT // DEC_SEQ
    src = np.arange(ATT_WIDTH)
    g, p, d = src // (ATT_HPG * HEAD_DIM), (src // HEAD_DIM) % ATT_HPG, src % HEAD_DIM
    perm = np.zeros((ATT_WIDTH, ATT_WIDTH), np.float32)
    perm[src, p * KV_WIDTH + g * HEAD_DIM + d] = 1.0
    ov = _overlap_padded(SAMPLE_N_CMP, SAMPLE_N_BLK, PAST_LEN // CMP_STRIDE, SAMPLE_BLK_LANES)
    tok = lambda w: pl.BlockSpec((DEC_SEQ, w), lambda b, u, pt: (row0 + b, 0))
    const = lambda a: pl.BlockSpec(a.shape, lambda b, u, pt, nd=a.ndim: (0,) * nd)
    per_seq = lambda s: pl.BlockSpec((1,) + s, lambda b, u, pt, nd=len(s): (b,) + (0,) * nd)
    pages = [pl.BlockSpec((1, 2, ATT_GROUPS, HEAD_DIM, PAGE_SIZE),
                          lambda b, u, pt, k=k: (pt[b, u * SAMPLE_PAGES_PER_STEP + k], 0, 0, 0, 0))
             for k in range(SAMPLE_PAGES_PER_STEP)]
    consts = (bias_cmp, bias_slc, bias_win, jnp.asarray(perm, BF16), ov)
    grid_spec = pltpu.PrefetchScalarGridSpec(
        num_scalar_prefetch=1, grid=(n_seq, SAMPLE_STEPS),
        in_specs=[tok(ATT_WIDTH), tok(ATT_GROUPS * LANES), per_seq((PAST_LEN // CMP_STRIDE, KV_ROW))] + pages
        + [tok(KV_ROW), per_seq((2, ATT_GROUPS, HEAD_DIM, WINDOW)), tok(KV_ROW)] + [const(a) for a in consts],
        out_specs=pl.BlockSpec((DEC_SEQ, ATT_WIDTH), lambda b, u, pt: (b, 0)),
        scratch_shapes=[pltpu.VMEM((SAMPLE_ROWS, KV_WIDTH), BF16), pltpu.VMEM((SAMPLE_ROWS, SAMPLE_BLK_LANES), BF16),
                        pltpu.VMEM((SAMPLE_ROWS, LANES), F32), pltpu.VMEM((SAMPLE_ROWS, LANES), F32),
                        pltpu.VMEM((SAMPLE_ROWS, KV_WIDTH), F32), pltpu.VMEM((SAMPLE_ROWS, KV_WIDTH), F32),
                        pltpu.VMEM((SAMPLE_ROWS, KV_WIDTH), F32)])
    return pl.pallas_call(
        _nsa_sample_kernel, grid_spec=grid_spec,
        out_shape=jax.ShapeDtypeStruct((n_seq * DEC_SEQ, ATT_WIDTH), F32),
        compiler_params=_cparams("arbitrary", "arbitrary"),
        name="nsa_sample",
    )(page_table, q, gates, kvc, *([cache_slc] * SAMPLE_PAGES_PER_STEP), slc_rows, state_win, win_rows, *consts)


def kernel(x_prompt, x_sample, cache_cmp_kv, cache_slc_kv, state_win_kv, state_hgrn, page_table,
           rel_bias_table, hgrn_lower_bound, norm_ffn1, w_ffn1_gate_up, w_ffn1_down, norm_mix,
           w_in, q_norm, k_norm, w_cmp1, b_cmp1, w_cmp2, b_cmp2, attn_out_norm, hgrn_out_norm,
           w_out, norm_ffn2, w_ffn2_gate_up, w_ffn2_down):
    assert DEPTH == 1
    l = 0
    kv_shape = (2, ATT_GROUPS, HEAD_DIM)

    y1 = _ffn((x_prompt.reshape(N_PROMPT, D_MODEL), x_sample.reshape(N_SAMPLE, D_MODEL)), norm_ffn1[l],
              w_ffn1_gate_up[l].astype(BF16), w_ffn1_down[l].astype(BF16))

    (q, cmp_rows, slc_rows, slc_pack, win_rows, win_pack, hq, lf, hk, hv, hog, gates) = _project_all(
        y1, norm_mix[l], _permute_w_in(w_in[l]), q_norm[l], k_norm[l], hgrn_lower_bound)

    tbl = rel_bias_table.astype(F32)
    first_end = CMP_BLOCK - 1
    bias_cmp_p = _bias_table(tbl, SEQ, LANES, -first_end, 1, -CMP_STRIDE)
    bias_toep = _bias_table(tbl, TOEP_TILES * LANES, LANES, 0, 1, -1)
    bias_cmp_s = _bias_table(tbl, DEC_SEQ, PAST_LEN // CMP_STRIDE, PAST_LEN - first_end, 1, -CMP_STRIDE)
    bias_slc_s = _bias_table(tbl, DEC_SEQ, PAST_LEN + LANES, PAST_LEN, 1, -1)
    bias_win_s = _bias_table(tbl, DEC_SEQ, WINDOW + LANES, WINDOW, 1, -1)
    rows_ht = lambda a: a.reshape(SAMPLE_ROWS, a.shape[-1])

    cw = _compress_weights(w_cmp1[l], b_cmp1[l], w_cmp2[l], b_cmp2[l])
    kvc_p = _compress(cmp_rows, cw, k_norm[l][0])
    kvc_s = _compress_paged(_cache_view(cache_cmp_kv[l]), page_table, w_cmp1[l], b_cmp1[l], w_cmp2[l], b_cmp2[l],
                            k_norm[l][0])

    o_att_p = _nsa_prompt(q, gates, kvc_p, slc_pack, win_pack, bias_cmp_p, bias_toep)
    o_att_s = _nsa_sample(q, gates, kvc_s, _cache_view(cache_slc_kv[l]), page_table, slc_rows,
                          _cache_view(state_win_kv[l]), win_rows,
                          rows_ht(bias_cmp_s), rows_ht(bias_slc_s), rows_ht(bias_win_s))

    o_hg_p, hg_p = _hgrn(hq, hk, hv, lf, None, n_seq=BATCH, t_len=SEQ, ch=HG_CHUNK_PROMPT, row0=0)
    o_hg_s, hg_s = _hgrn(hq, hk, hv, lf, state_hgrn[l].astype(F32), n_seq=DEC_BATCH, t_len=DEC_SEQ,
                         ch=HG_CHUNK_SAMPLE, row0=N_PROMPT)

    y2 = _mixer_out(y1, o_att_p, o_att_s, o_hg_p, o_hg_s, hog, attn_out_norm[l], hgrn_out_norm[l],
                    w_out[l].astype(BF16))
    y3_p, y3_s = _ffn(y2, norm_ffn2[l], w_ffn2_gate_up[l].astype(BF16), w_ffn2_down[l].astype(BF16), split_out=True)

    prompt_rows = lambda a: a[:N_PROMPT].reshape((1, BATCH, SEQ) + kv_shape)
    sample_rows = lambda a: a[N_PROMPT:].reshape((1, DEC_BATCH, DEC_SEQ) + kv_shape)
    win_p = prompt_rows(win_rows)[:, :, SEQ - min(WINDOW, SEQ):]
    win_s = jnp.concatenate([state_win_kv[l][:, DEC_SEQ:], sample_rows(win_rows)[0]], axis=1)[None]
    return (y3_p.reshape(BATCH, SEQ, D_MODEL), y3_s.reshape(DEC_BATCH, DEC_SEQ, D_MODEL),
            prompt_rows(cmp_rows), prompt_rows(slc_rows), win_p, hg_p[None],
            sample_rows(cmp_rows), sample_rows(slc_rows), win_s, hg_s[None].astype(state_hgrn.dtype))
```

```python
import functools
import math

import jax
import jax.numpy as jnp
import numpy as np
from jax import lax
from jax.experimental import pallas as pl
from jax.experimental.pallas import tpu as pltpu

D_MODEL = 2048
BATCH = 4
SEQ = 2048
DEPTH = 1
DEC_BATCH = 32
DEC_SEQ = 8
PAST_LEN = 8192
PAGE_SIZE = 128
HEAD_DIM = 64
ATT_HEADS = (D_MODEL // 2) // HEAD_DIM
ATT_GROUPS = ATT_HEADS // 4
ATT_HPG = ATT_HEADS // ATT_GROUPS
ATT_WIDTH = ATT_HEADS * HEAD_DIM
KV_WIDTH = ATT_GROUPS * HEAD_DIM
N_BRANCH = 3
CMP_BLOCK = 32
CMP_STRIDE = 16
CMP_RATIO = CMP_BLOCK // CMP_STRIDE
CMP_HIDDEN = 256
SEL_BLOCK = 64
N_SEL = 8
WINDOW = 512
Q_BLOCK = 128
HG_KEY = 128
HG_VAL = 128
HG_HEADS = (D_MODEL // 2) // HG_VAL
HG_WIDTH = HG_HEADS * HG_VAL
MIX_WIDTH = ATT_WIDTH + HG_WIDTH
D_FF = 256 * ((8 * D_MODEL // 3 + 255) // 256)
N_BUCKETS = 32
MAX_DISTANCE = 1024
EPS = 1e-6
IN_SPLITS = (ATT_WIDTH, 2 * N_BRANCH * KV_WIDTH, N_BRANCH * ATT_HEADS,
             HG_HEADS * HG_KEY, HG_HEADS * HG_KEY, HG_WIDTH, HG_WIDTH)
D_IN = sum(IN_SPLITS)

N_PROMPT = BATCH * SEQ
N_SAMPLE = DEC_BATCH * DEC_SEQ
N_TOKENS = N_PROMPT + N_SAMPLE
N_PAGES = PAST_LEN // PAGE_SIZE
KV_ROW = 2 * KV_WIDTH

LANES = 128
V7X_VMEM_BYTES = 64 * 1024 * 1024
VMEM_LIMIT_BYTES = 56 * 1024 * 1024

NEG = -1e30
F32 = jnp.float32
BF16 = jnp.bfloat16

FFN_ROW_TILE = 768
FFN_FF_TILE = 512
PROJ_ROW_TILE = 528
PROJ_COL_TILE = 512
OUT_ROW_TILE = 256
HG_CHUNK_PROMPT = 16
HG_CHUNK_SAMPLE = DEC_SEQ
CMP_UNIT_ROWS = 2048
CMP_UNITS = 4


def _cparams(*sem):
    return pltpu.CompilerParams(dimension_semantics=sem, vmem_limit_bytes=VMEM_LIMIT_BYTES)


def _t5_thresholds():
    n = np.arange(0, 2 * MAX_DISTANCE + 2)
    exact = N_BUCKETS // 2
    logn = np.log(np.maximum(n, 1).astype(np.float64) / exact)
    large = exact + (logn / math.log(MAX_DISTANCE / exact) * (N_BUCKETS - exact)).astype(np.int32)
    b = np.where(n < exact, n, np.minimum(large, N_BUCKETS - 1))
    return [int(n[b >= k][0]) for k in range(N_BUCKETS)]


T5_THRESHOLDS = _t5_thresholds()


def _block_diag_ones(n, blk):
    i = np.arange(n)
    return (i[:, None] // blk == i[None, :] // blk).astype(np.float32)


def _split2(x):
    hi = x.astype(BF16)
    lo = (x - hi.astype(F32)).astype(BF16)
    return hi, lo


def _group_rms(t, bd, gain, width):
    hi, lo = _split2(t * t)
    ss = (jnp.dot(hi, bd, preferred_element_type=F32) + jnp.dot(lo, bd, preferred_element_type=F32))
    return t * lax.rsqrt(ss * (1.0 / width) + EPS) * gain


def _ffn_kernel(*refs, split_in, split_out):
    n_in = 2 if split_in else 1
    x_ref = refs[0]
    gain_ref, wg_ref, wu_ref, wd_ref = refs[n_in:n_in + 4]
    o_ref = refs[n_in + 4]
    xn_ref, acc_ref = refs[-2:]
    j = pl.program_id(1)
    last_tile = pl.program_id(0) == pl.num_programs(0) - 1
    tm = x_ref.shape[0]

    def rows_in():
        x = x_ref[...]
        if split_in:
            xs = refs[1][...]
            x = jnp.where(last_tile, jnp.concatenate([x[:tm - xs.shape[0]], xs], axis=0), x)
        return x

    @pl.when(j == 0)
    def _():
        x = rows_in()
        y = x * lax.rsqrt(jnp.mean(x * x, axis=-1, keepdims=True) + EPS)
        xn_ref[...] = (y * gain_ref[...]).astype(BF16)
        acc_ref[...] = jnp.zeros_like(acc_ref)

    xn = xn_ref[...]
    g = jnp.dot(xn, wg_ref[...], preferred_element_type=F32)
    u = jnp.dot(xn, wu_ref[...], preferred_element_type=F32)
    a = (g * jax.nn.sigmoid(g) * u).astype(BF16)
    acc_ref[...] += jnp.dot(a, wd_ref[...], preferred_element_type=F32)

    @pl.when(j == pl.num_programs(1) - 1)
    def _():
        res = rows_in() + 0.5 * acc_ref[...]
        o_ref[...] = res
        if split_out:
            os_ref = refs[n_in + 5]

            @pl.when(last_tile)
            def _():
                os_ref[...] = res[tm - os_ref.shape[0]:]


def _ffn(x, gain, w_gu, w_down, split_out=False):
    split_in = isinstance(x, tuple)
    tm, tf = FFN_ROW_TILE, FFN_FF_TILE
    n_p, n_s = N_PROMPT, N_SAMPLE
    n, d = n_p + n_s, D_MODEL
    assert n % tm == 0 and D_FF % tf == 0 and n_p % tm == tm - n_s
    nj = D_FF // tf
    row = pl.BlockSpec((tm, d), lambda i, j: (i, 0))
    tail = pl.BlockSpec((n_s, d), lambda i, j: (0, 0))
    xs = x if split_in else (x,)
    return pl.pallas_call(
        functools.partial(_ffn_kernel, split_in=split_in, split_out=split_out),
        grid=(n // tm, nj),
        in_specs=([row, tail] if split_in else [row]) + [
            pl.BlockSpec((1, d), lambda i, j: (0, 0)),
            pl.BlockSpec((d, tf), lambda i, j: (0, j)),
            pl.BlockSpec((d, tf), lambda i, j: (0, j + nj)),
            pl.BlockSpec((tf, d), lambda i, j: (j, 0)),
        ],
        out_specs=(row, tail) if split_out else row,
        out_shape=((jax.ShapeDtypeStruct((n_p, d), F32), jax.ShapeDtypeStruct((n_s, d), F32)) if split_out
                   else jax.ShapeDtypeStruct((n, d), F32)),
        scratch_shapes=[pltpu.VMEM((tm, d), BF16), pltpu.VMEM((tm, d), F32)],
        compiler_params=_cparams("arbitrary", "arbitrary"),
        name="ffn",
    )(*xs, gain.reshape(1, d), w_gu, w_gu, w_down)


PB_Q, PB_CMP, PB_SLC, PB_WIN, PB_HQ, PB_HF, PB_HI, PB_HG, PB_GATE = 0, 2, 3, 4, 5, 7, 9, 11, 13
PROJ_N_BLOCKS = 14


def _permute_w_in(w_in):
    p = [int(v) for v in np.cumsum(IN_SPLITS)]
    a_g = w_in[:, p[1]:p[2]]
    src = np.zeros((PROJ_COL_TILE,), np.int32)
    valid = np.zeros((PROJ_COL_TILE,), bool)
    for g in range(ATT_GROUPS):
        for br in range(N_BRANCH):
            for hp in range(ATT_HPG):
                src[g * LANES + br * ATT_HPG + hp] = br * ATT_HEADS + g * ATT_HPG + hp
                valid[g * LANES + br * ATT_HPG + hp] = True
    gate = jnp.where(jnp.asarray(valid)[None, :], a_g[:, src], 0.0)
    return jnp.concatenate([w_in[:, :p[1]], w_in[:, p[2]:], gate], axis=1).astype(BF16)


def _pack_kv(k, v):
    parts = []
    for g in range(ATT_GROUPS):
        parts.append(k[:, g * HEAD_DIM:(g + 1) * HEAD_DIM])
        parts.append(v[:, g * HEAD_DIM:(g + 1) * HEAD_DIM])
    return jnp.concatenate(parts, axis=1).astype(BF16)


def _proj_kernel(x_ref, gain_ref, w_ref, bd_ref, qg_ref, kg_ref, lbp_ref,
                 q_ref, cmp_ref, slc_ref, slcp_ref, win_ref, winp_ref,
                 hq_ref, lf_ref, hk_ref, hv_ref, hog_ref, gate_ref, xn_ref):
    c = pl.program_id(1)

    @pl.when(c == 0)
    def _():
        x = x_ref[...]
        y = x * lax.rsqrt(jnp.mean(x * x, axis=-1, keepdims=True) + EPS)
        xn_ref[...] = (y * gain_ref[...]).astype(BF16)

    acc = jnp.dot(xn_ref[...], w_ref[...], preferred_element_type=F32)
    half = KV_WIDTH

    @pl.when(c < PB_CMP)
    def _():
        q_ref[...] = _group_rms(acc, bd_ref[...], qg_ref[...], HEAD_DIM)

    @pl.when(c == PB_CMP)
    def _():
        cmp_ref[...] = acc

    def kv_branch(rows_ref, pack_ref, br):
        k = _group_rms(acc[:, :half], bd_ref[:half, :half], kg_ref[br - 1:br, :], HEAD_DIM)
        v = acc[:, half:]
        rows_ref[:, :half] = k
        rows_ref[:, half:] = v
        pack_ref[...] = _pack_kv(k, v)

    @pl.when(c == PB_SLC)
    def _():
        kv_branch(slc_ref, slcp_ref, 1)

    @pl.when(c == PB_WIN)
    def _():
        kv_branch(win_ref, winp_ref, 2)

    @pl.when((c >= PB_HQ) & (c < PB_HF))
    def _():
        hq_ref[...] = acc

    @pl.when((c >= PB_HF) & (c < PB_HI))
    def _():
        p = lbp_ref[...]
        e = jnp.exp(p - jnp.max(p, axis=0, keepdims=True))
        lb = e[0:1, :] / jnp.sum(e, axis=0, keepdims=True)
        lf_ref[...] = jnp.log(lb + (1.0 - lb) * jax.nn.sigmoid(acc))
        hk_ref[...] = (1.0 - lb) * jax.nn.sigmoid(-acc)

    @pl.when((c >= PB_HI) & (c < PB_HG))
    def _():
        hv_ref[...] = acc

    @pl.when((c >= PB_HG) & (c < PB_GATE))
    def _():
        hog_ref[...] = acc

    @pl.when(c == PB_GATE)
    def _():
        gate_ref[...] = jax.nn.sigmoid(acc)


def _project_all(y, gain, w_perm, q_gain, k_gain, lb_logits):
    n, d = y.shape
    tm, tc = PROJ_ROW_TILE, PROJ_COL_TILE
    assert n % tm == 0 and DEPTH == 1
    bd = jnp.asarray(_block_diag_ones(tc, HEAD_DIM), BF16)

    def two(first):
        return lambda i, c: (i, jnp.clip(c - first, 0, 1))

    one = lambda i, c: (i, 0)
    wide = lambda dt: jax.ShapeDtypeStruct((n, 2 * tc), dt)
    narrow = lambda dt: jax.ShapeDtypeStruct((n, tc), dt)
    out_shape = (wide(F32), narrow(F32), narrow(F32), narrow(BF16), narrow(F32), narrow(BF16),
                 wide(F32), wide(F32), wide(F32), wide(F32), wide(F32), narrow(F32))
    blk = lambda f: pl.BlockSpec((tm, tc), f)
    out_specs = (blk(two(PB_Q)), blk(one), blk(one), blk(one), blk(one), blk(one),
                 blk(two(PB_HQ)), blk(two(PB_HF)), blk(two(PB_HF)), blk(two(PB_HI)), blk(two(PB_HG)), blk(one))
    return pl.pallas_call(
        _proj_kernel,
        grid=(n // tm, PROJ_N_BLOCKS),
        in_specs=[
            pl.BlockSpec((tm, d), lambda i, c: (i, 0)),
            pl.BlockSpec((1, d), lambda i, c: (0, 0)),
            pl.BlockSpec((d, tc), lambda i, c: (0, c)),
            pl.BlockSpec((tc, tc), lambda i, c: (0, 0)),
            pl.BlockSpec((1, tc), lambda i, c: (0, jnp.clip(c, 0, 1))),
            pl.BlockSpec((2, KV_WIDTH), lambda i, c: (0, 0)),
            pl.BlockSpec((DEPTH + 1, tc), lambda i, c: (0, jnp.clip(c - PB_HF, 0, 1))),
        ],
        out_specs=out_specs,
        out_shape=out_shape,
        scratch_shapes=[pltpu.VMEM((tm, d), BF16)],
        compiler_params=_cparams("arbitrary", "arbitrary"),
        name="proj",
    )(y, gain.reshape(1, d), w_perm, bd, q_gain.reshape(1, ATT_WIDTH),
      k_gain[1:].reshape(2, KV_WIDTH), lb_logits)


def _hgrn_kernel(q_ref, k_ref, v_ref, lf_ref, tri_ref, ones_ref, s0_ref, o_ref, st_ref,
                 qe_scr, kd_scr, vt_scr, dec_scr, oi_scr, *, t_len, ch, has_state):
    nj = t_len // ch
    shape3 = (nj, ch, HG_KEY)
    q3, k3, v3, lf3 = q_ref[...], k_ref[...], v_ref[...], lf_ref[...]
    tl = lax.broadcasted_iota(jnp.int32, shape3, 1)

    def row(x3, s):
        return jnp.broadcast_to(x3[:, s:s + 1, :], shape3)

    if t_len >= 256:
        lf2 = lf3.reshape(t_len, HG_KEY)
        parts = []
        for r0 in range(0, t_len, 256):
            x = lf2[r0:r0 + 256]
            hi = x.astype(BF16)
            r1 = x - hi.astype(F32)
            mid = r1.astype(BF16)
            lo = (r1 - mid.astype(F32)).astype(BF16)
            tri = tri_ref[...]
            parts.append(jnp.dot(tri, hi, preferred_element_type=F32)
                         + jnp.dot(tri, mid, preferred_element_type=F32)
                         + jnp.dot(tri, lo, preferred_element_type=F32))
        b3 = jnp.concatenate(parts, axis=0).reshape(shape3)
    else:
        b3 = jnp.zeros(shape3, F32)
        for s in range(ch):
            b3 = b3 + jnp.where(tl >= s, row(lf3, s), 0.0)

    bl3 = row(b3, ch - 1)
    qe3 = q3 * jnp.exp(b3)
    kd3 = k3 * jnp.exp(bl3 - b3)
    dec_scr[...] = jnp.exp(b3[:, ch - 1:ch, :])
    blocked = t_len % LANES == 0
    if blocked:
        nb = t_len // LANES
        qe2, kd2, v2 = (x.reshape(t_len, HG_KEY) for x in (qe3, kd3, v3))
        for m in range(nb):
            blk = slice(m * LANES, (m + 1) * LANES)
            qe_scr[m] = qe2[blk].T.astype(BF16)
            vt_scr[m] = v2[blk].T.astype(BF16)
            kd_scr[m] = kd2[blk].astype(BF16)
    else:
        qe_scr[...] = qe3.astype(BF16)
        kd_scr[...] = kd3.astype(BF16)

    ones = ones_ref[...]
    n8 = ch // 8
    shape8 = (nj, 8, HG_KEY)
    sub8 = lax.broadcasted_iota(jnp.int32, shape8, 1)
    q5, k5, v5, b5 = (x.reshape(nj, n8, 8, HG_KEY) for x in (q3, k3, v3, b3))
    od = [jnp.zeros(shape8, F32) for _ in range(n8)]
    for s in range(ch):
        hs, ss = divmod(s, 8)
        ks, bs, vs = (jnp.broadcast_to(x[:, hs, ss:ss + 1, :], shape8) for x in (k5, b5, v5))
        for hh in range(hs, n8):
            diff = b5[:, hh] - bs
            if hh == hs:
                diff = jnp.where(sub8 >= ss, diff, NEG)
            w = q5[:, hh] * ks * jnp.exp(diff)
            a = jnp.dot(w.reshape(nj * 8, HG_KEY).astype(BF16), ones, preferred_element_type=F32)
            od[hh] = od[hh] + a.reshape(shape8) * vs
    for hh in range(n8):
        o_ref[:, hh * 8:(hh + 1) * 8, :] = od[hh]

    if has_state:
        st0 = s0_ref[0, 0].T
    else:
        st0 = jnp.zeros((HG_VAL, HG_KEY), F32)

    if blocked:
        cpb = LANES // ch
        lane_chunk = lax.broadcasted_iota(jnp.int32, (HG_KEY, LANES), 1) >> (ch.bit_length() - 1)
        keep = [jnp.where(lane_chunk == r, 1.0, 0.0).astype(BF16) for r in range(cpb)]

        def body(m, st):
            qet, vt, kd = qe_scr[m], vt_scr[m], kd_scr[m]
            ot = jnp.zeros((HG_VAL, LANES), F32)
            for r in range(cpb):
                ot = ot + jnp.dot(st.astype(BF16), qet * keep[r], preferred_element_type=F32)
                ut = jnp.dot(vt * keep[r], kd, preferred_element_type=F32)
                st = st * dec_scr[m * cpb + r] + ut
            oi_scr[m] = ot
            return st

        st = lax.fori_loop(0, t_len // LANES, body, st0)
        for m in range(t_len // LANES):
            o_ref[m * cpb:(m + 1) * cpb] = o_ref[m * cpb:(m + 1) * cpb] + oi_scr[m].T.reshape(cpb, ch, HG_VAL)
    else:
        def body(j, st):
            oi_scr[j] = lax.dot_general(qe_scr[j], st.astype(BF16), NT_DIMS, preferred_element_type=F32)
            ut = lax.dot_general(v_ref[j].astype(BF16), kd_scr[j], (((0,), (0,)), ((), ())),
                                 preferred_element_type=F32)
            return st * dec_scr[j] + ut

        st = lax.fori_loop(0, nj, body, st0, unroll=True)
        o_ref[...] = o_ref[...] + oi_scr[...]
    st_ref[0, 0] = st.T


def _hgrn(hq, hk, hv, lf, s0, *, n_seq, t_len, ch, row0):
    n = hq.shape[0]
    nj = t_len // ch
    assert row0 % t_len == 0 and t_len % ch == 0 and ch % 8 == 0
    blk0 = row0 // t_len
    r3 = lambda a: a.reshape(n // ch, ch, HG_WIDTH)
    has_state = s0 is not None
    if not has_state:
        s0 = jnp.zeros((1, 1, HG_KEY, HG_VAL), F32)
    tri_n = 256 if t_len >= 256 else 8
    i = np.arange(tri_n)
    tri = jnp.asarray(((i[:, None] // ch == i[None, :] // ch) & (i[:, None] >= i[None, :])).astype(np.float32), BF16)
    ones = jnp.ones((HG_KEY, HG_KEY), BF16)
    blk_shape = (t_len // LANES, LANES, HG_KEY) if t_len % LANES == 0 else (nj, ch, HG_KEY)
    hps = HG_HEADS if nj == 1 else 1
    seq = pl.BlockSpec((nj, ch, hps * HG_KEY), lambda b, h: (blk0 + b, 0, h))
    state_in = pl.BlockSpec((1, hps, HG_KEY, HG_VAL), (lambda b, h: (b, h, 0, 0)) if has_state else (lambda b, h: (0, 0, 0, 0)))

    def body(q_ref, k_ref, v_ref, lf_ref, tri_ref, ones_ref, s0_ref, o_ref, st_ref, *scratch):
        for h in range(hps):
            ln = slice(h * HG_KEY, (h + 1) * HG_KEY)
            _hgrn_kernel(q_ref.at[:, :, ln], k_ref.at[:, :, ln], v_ref.at[:, :, ln], lf_ref.at[:, :, ln], tri_ref,
                         ones_ref, s0_ref.at[:, h:h + 1], o_ref.at[:, :, ln], st_ref.at[:, h:h + 1], *scratch,
                         t_len=t_len, ch=ch, has_state=has_state)

    o, st = pl.pallas_call(
        body,
        grid=(n_seq, HG_HEADS // hps),
        in_specs=[seq, seq, seq, seq,
                  pl.BlockSpec((tri_n, tri_n), lambda b, h: (0, 0)),
                  pl.BlockSpec((HG_KEY, HG_KEY), lambda b, h: (0, 0)),
                  state_in],
        out_specs=(pl.BlockSpec((nj, ch, hps * HG_VAL), lambda b, h: (b, 0, h)),
                   pl.BlockSpec((1, hps, HG_KEY, HG_VAL), lambda b, h: (b, h, 0, 0))),
        out_shape=(jax.ShapeDtypeStruct((n_seq * nj, ch, HG_WIDTH), F32),
                   jax.ShapeDtypeStruct((n_seq, HG_HEADS, HG_KEY, HG_VAL), F32)),
        scratch_shapes=[pltpu.VMEM(blk_shape, BF16), pltpu.VMEM(blk_shape, BF16), pltpu.VMEM(blk_shape, BF16),
                        pltpu.VMEM((nj, 1, HG_KEY), F32), pltpu.VMEM(blk_shape, F32)],
        compiler_params=_cparams("arbitrary", "arbitrary"),
        name="hgrn",
    )(r3(hq), r3(hk), r3(hv), r3(lf), tri, ones, s0)
    return o.reshape(n_seq * t_len, HG_WIDTH), st


def _mixout_kernel(y_ref, oap_ref, oas_ref, ohp_ref, ohs_ref, og_ref, ag_ref, hgain_ref, w_ref, o_ref):
    is_sample = pl.program_id(0) == pl.num_programs(0) - 1
    oa = jnp.where(is_sample, oas_ref[...], oap_ref[...])
    a = oa * lax.rsqrt(jnp.mean(oa * oa, axis=-1, keepdims=True) + EPS) * ag_ref[...]
    oh = jnp.where(is_sample, ohs_ref[...], ohp_ref[...])
    hs = []
    for h in range(HG_HEADS):
        x = oh[:, h * HG_VAL:(h + 1) * HG_VAL]
        hs.append(x * lax.rsqrt(jnp.mean(x * x, axis=-1, keepdims=True) + EPS))
    og = og_ref[...]
    hh = jnp.concatenate(hs, axis=1) * hgain_ref[...] * (og * jax.nn.sigmoid(og))
    m = (jnp.dot(a.astype(BF16), w_ref[:ATT_WIDTH, :], preferred_element_type=F32)
         + jnp.dot(hh.astype(BF16), w_ref[ATT_WIDTH:, :], preferred_element_type=F32))
    o_ref[...] = y_ref[...] + m


def _mixer_out(y, o_att_p, o_att_s, o_hg_p, o_hg_s, og, attn_gain, hg_gain, w_out):
    n, d = y.shape
    tm = OUT_ROW_TILE
    n_p = o_att_p.shape[0]
    assert n % tm == 0 and n_p % tm == 0 and o_att_s.shape[0] == tm and n == n_p + tm
    row = lambda w: pl.BlockSpec((tm, w), lambda i: (i, 0))
    prompt_row = lambda w: pl.BlockSpec((tm, w), lambda i: (jnp.minimum(i, n_p // tm - 1), 0))
    const = lambda s: pl.BlockSpec(s, lambda i: (0, 0))
    return pl.pallas_call(
        _mixout_kernel,
        grid=(n // tm,),
        in_specs=[row(d), prompt_row(ATT_WIDTH), const((tm, ATT_WIDTH)), prompt_row(HG_WIDTH), const((tm, HG_WIDTH)),
                  row(HG_WIDTH), const((1, ATT_WIDTH)), const((1, HG_WIDTH)), const((MIX_WIDTH, d))],
        out_specs=row(d),
        out_shape=jax.ShapeDtypeStruct((n, d), F32),
        compiler_params=_cparams("arbitrary"),
        name="mixout",
    )(y, o_att_p, o_att_s, o_hg_p, o_hg_s, og, attn_gain.reshape(1, ATT_WIDTH), hg_gain.reshape(1, HG_WIDTH), w_out)


def _bias_kernel(tbl_ref, o_ref, *, a0, ar, ac, rows_blk):
    h = pl.program_id(0)
    rb = pl.program_id(1)
    shape = o_ref.shape[1:]
    r = lax.broadcasted_iota(jnp.int32, shape, 0) + rb * rows_blk
    c = lax.broadcasted_iota(jnp.int32, shape, 1)
    n = a0 + ar * r + ac * c
    out = jnp.full(shape, tbl_ref[0, h], F32)
    for k in range(1, N_BUCKETS):
        out = jnp.where(n >= T5_THRESHOLDS[k], tbl_ref[k, h], out)
    o_ref[0] = out


def _bias_table(tbl, rows, cols, a0, ar, ac):
    rows_blk = max(r for r in range(8, min(rows, 512) + 1, 8) if rows % r == 0)
    assert cols % LANES == 0
    return pl.pallas_call(
        functools.partial(_bias_kernel, a0=a0, ar=ar, ac=ac, rows_blk=rows_blk),
        grid=(ATT_HEADS, rows // rows_blk),
        in_specs=[pl.BlockSpec(memory_space=pltpu.SMEM)],
        out_specs=pl.BlockSpec((1, rows_blk, cols), lambda h, rb: (h, rb, 0)),
        out_shape=jax.ShapeDtypeStruct((ATT_HEADS, rows, cols), F32),
        compiler_params=_cparams("arbitrary", "arbitrary"),
        name="t5_bias",
    )(tbl)


def _compress_weights(w1, b1, w2, b2):
    w = w1.reshape(2, CMP_RATIO, CMP_STRIDE, HEAD_DIM, CMP_HIDDEN)
    z = jnp.zeros_like(w)
    top = jnp.concatenate([w, z], axis=-1)
    bot = jnp.concatenate([z, w], axis=-1)
    wpad = jnp.stack([top, bot], axis=3)
    wpad = wpad.reshape(2 * CMP_RATIO, CMP_STRIDE * LANES, 2 * CMP_HIDDEN).astype(BF16)
    z2 = jnp.zeros_like(w2)
    w2pad = jnp.concatenate([jnp.concatenate([w2, z2], axis=-1), jnp.concatenate([z2, w2], axis=-1)], axis=1)
    return wpad, jnp.concatenate([b1, b1], axis=-1), w2pad.astype(BF16), jnp.concatenate([b2, b2], axis=-1)


def _compress_kernel(rows_ref, perm_ref, wpad_ref, b1_ref, w2pad_ref, b2_ref, kg_ref, bd_ref, out_ref, xs_ref):
    u = pl.program_id(1)
    n_chunk = CMP_UNITS * CMP_UNIT_ROWS // CMP_STRIDE
    perm = perm_ref[...]

    for t in range(CMP_UNIT_ROWS // 256):
        x = rows_ref[t * 256:(t + 1) * 256, :]
        y = jnp.dot(perm, x.astype(BF16), preferred_element_type=F32).astype(BF16)
        base = pl.multiple_of(u * (CMP_UNIT_ROWS // CMP_STRIDE) + t * 16, 16)
        for s in range(CMP_STRIDE):
            xs_ref[s, pl.ds(base, 16), :] = y[s * 16:(s + 1) * 16, :]

    @pl.when(u == CMP_UNITS - 1)
    def _():
        halves = []
        for kv in range(2):
            outs = []
            for j in range(2):
                col = (kv * 2 + j) * LANES
                lhs = jnp.concatenate([xs_ref[s, :, col:col + LANES] for s in range(CMP_STRIDE)], axis=1)
                h0 = jnp.dot(lhs, wpad_ref[kv * CMP_RATIO + 0], preferred_element_type=F32)
                h1 = jnp.dot(lhs, wpad_ref[kv * CMP_RATIO + 1], preferred_element_type=F32)
                h = b1_ref[kv:kv + 1, :] + h0 + pltpu.roll(h1, n_chunk - 1, axis=0)
                a = (h * jax.nn.sigmoid(h)).astype(BF16)
                outs.append(jnp.dot(a, w2pad_ref[kv], preferred_element_type=F32) + b2_ref[kv:kv + 1, :])
            halves.append(jnp.concatenate(outs, axis=1))
        pk = _pack_kv(_group_rms(halves[0], bd_ref[...], kg_ref[...], HEAD_DIM), halves[1])
        for g in range(ATT_GROUPS):
            out_ref[0, g] = pk[:, g * LANES:(g + 1) * LANES]


def _compress(rows, cw, k_gain0):
    wpad, b1, w2pad, b2 = cw
    n_chunk = CMP_UNITS * CMP_UNIT_ROWS // CMP_STRIDE
    i = np.arange(256)
    perm = np.zeros((256, 256), np.float32)
    perm[(i % 16) * 16 + i // 16, i] = 1.0
    consts = (jnp.asarray(perm, BF16), wpad, b1, w2pad, b2, k_gain0.reshape(1, KV_WIDTH),
              jnp.asarray(_block_diag_ones(KV_WIDTH, HEAD_DIM), BF16))
    const = lambda a: pl.BlockSpec(a.shape, lambda b, u, nd=a.ndim: (0,) * nd)
    return pl.pallas_call(
        _compress_kernel,
        grid=(1, CMP_UNITS),
        in_specs=[pl.BlockSpec((CMP_UNIT_ROWS, KV_ROW), lambda b, u: (u, 0))] + [const(a) for a in consts],
        out_specs=pl.BlockSpec((1, ATT_GROUPS, n_chunk, LANES), lambda b, u: (0, 0, 0, 0)),
        out_shape=jax.ShapeDtypeStruct((1, ATT_GROUPS, n_chunk, LANES), BF16),
        scratch_shapes=[pltpu.VMEM((CMP_STRIDE, n_chunk, KV_ROW), BF16)],
        compiler_params=_cparams("arbitrary", "arbitrary"),
        name="compress",
    )(rows, *consts)


PAGED_UNIT_ROWS = 4096
PAGED_UNITS = PAST_LEN // PAGED_UNIT_ROWS
PAGES_PER_STEP = PAGED_UNIT_ROWS // PAGE_SIZE


def _cache_view(cache):
    return jnp.transpose(cache, (0, 2, 3, 4, 1))


def _compress_paged_kernel(pt_ref, *refs):
    del pt_ref
    page_refs = refs[:PAGES_PER_STEP]
    pick_ref, w1_ref, b1_ref, w2_ref, b2_ref, kg_ref, bd_ref, out_ref, xs_ref = refs[PAGES_PER_STEP:]
    u = pl.program_id(1)
    n_chunk = PAST_LEN // CMP_STRIDE
    pick = pick_ref[...]
    n_kvg = 2 * ATT_GROUPS

    for t in range(PAGES_PER_STEP // 2):
        base = pl.multiple_of(u * (PAGED_UNIT_ROWS // CMP_STRIDE) + t * 16, 16)
        kt = jnp.concatenate([page_refs[2 * t][0].reshape(n_kvg * HEAD_DIM, PAGE_SIZE),
                              page_refs[2 * t + 1][0].reshape(n_kvg * HEAD_DIM, PAGE_SIZE)], axis=1).astype(BF16)
        z = jnp.dot(kt, pick, preferred_element_type=F32)
        for kvg in range(n_kvg):
            zk = z[kvg * HEAD_DIM:(kvg + 1) * HEAD_DIM]
            y = jnp.concatenate([zk[:, :LANES], zk[:, LANES:]], axis=0).T.astype(BF16)
            for s2 in range(CMP_STRIDE // 2):
                xs_ref[kvg, s2, pl.ds(base, 16), :] = y[s2 * 16:(s2 + 1) * 16, :]

    @pl.when(u == PAGED_UNITS - 1)
    def _():
        halves = []
        for kv in range(2):
            lhs = jnp.concatenate(
                [jnp.concatenate([xs_ref[kv * ATT_GROUPS + g, s2] for s2 in range(CMP_STRIDE // 2)], axis=1)
                 for g in range(ATT_GROUPS)], axis=0)
            h0 = jnp.dot(lhs, w1_ref[kv * CMP_RATIO + 0], preferred_element_type=F32)
            h1 = jnp.dot(lhs, w1_ref[kv * CMP_RATIO + 1], preferred_element_type=F32)
            h = b1_ref[kv:kv + 1, :] + h0 + pltpu.roll(h1, ATT_GROUPS * n_chunk - 1, axis=0)
            a = (h * jax.nn.sigmoid(h)).astype(BF16)
            o = b2_ref[kv:kv + 1, :]
            for g in range(ATT_GROUPS):
                o = o + jnp.dot(a[g * n_chunk:(g + 1) * n_chunk], w2_ref[kv * ATT_GROUPS + g], preferred_element_type=F32)
            halves.append(o)
        out_ref[0, :, :KV_WIDTH] = _group_rms(halves[0], bd_ref[...], kg_ref[...], HEAD_DIM).astype(BF16)
        out_ref[0, :, KV_WIDTH:] = halves[1].astype(BF16)


def _compress_paged(cache_t, page_table, w1, b1, w2, b2, k_gain0):
    n_seq = page_table.shape[0]
    n_chunk = PAST_LEN // CMP_STRIDE
    r = np.arange(LANES)
    s2, c = r // 16, r % 16
    pick = np.zeros((2 * PAGE_SIZE, 2, LANES), np.float32)
    for half in range(2):
        pick[CMP_STRIDE * c + 2 * s2 + half, half, r] = 1.0
    w2p = jnp.zeros((2, ATT_GROUPS, CMP_HIDDEN, KV_WIDTH), F32)
    for g in range(ATT_GROUPS):
        w2p = w2p.at[:, g, :, g * HEAD_DIM:(g + 1) * HEAD_DIM].set(w2)
    consts = (jnp.asarray(pick.reshape(2 * PAGE_SIZE, 2 * LANES), BF16),
              w1.reshape(2 * CMP_RATIO, CMP_STRIDE * HEAD_DIM, CMP_HIDDEN).astype(BF16), b1,
              w2p.reshape(2 * ATT_GROUPS, CMP_HIDDEN, KV_WIDTH).astype(BF16), jnp.tile(b2, (1, ATT_GROUPS)),
              k_gain0.reshape(1, KV_WIDTH), jnp.asarray(_block_diag_ones(KV_WIDTH, HEAD_DIM), BF16))
    const = lambda a: pl.BlockSpec(a.shape, lambda b, u, pt, nd=a.ndim: (0,) * nd)
    pages = [pl.BlockSpec((1, 2, ATT_GROUPS, HEAD_DIM, PAGE_SIZE),
                          lambda b, u, pt, k=k: (pt[b, u * PAGES_PER_STEP + k], 0, 0, 0, 0))
             for k in range(PAGES_PER_STEP)]
    grid_spec = pltpu.PrefetchScalarGridSpec(
        num_scalar_prefetch=1, grid=(n_seq, PAGED_UNITS),
        in_specs=pages + [const(a) for a in consts],
        out_specs=pl.BlockSpec((1, n_chunk, KV_ROW), lambda b, u, pt: (b, 0, 0)),
        scratch_shapes=[pltpu.VMEM((2 * ATT_GROUPS, CMP_STRIDE // 2, n_chunk, LANES), BF16)])
    return pl.pallas_call(
        _compress_paged_kernel, grid_spec=grid_spec,
        out_shape=jax.ShapeDtypeStruct((n_seq, n_chunk, KV_ROW), BF16),
        compiler_params=_cparams("arbitrary", "arbitrary"),
        name="compress_paged",
    )(page_table, *([cache_t] * PAGES_PER_STEP), *consts)


def _block_overlap(n_cmp, n_blk):
    cs = np.arange(n_cmp)[:, None] * CMP_STRIDE
    bs = np.arange(n_blk)[None, :] * SEL_BLOCK
    ov = np.minimum(cs + CMP_BLOCK, bs + SEL_BLOCK) - np.maximum(cs, bs)
    return (np.clip(ov, 0, None) / CMP_BLOCK).astype(np.float32)


def _overlap_padded(n_cmp, n_blk, rows, cols):
    ov = np.zeros((rows, cols), np.float32)
    ov[:n_cmp, :n_blk] = _block_overlap(n_cmp, n_blk)
    return jnp.asarray(ov, BF16)


def _softmax_rows(s, mask):
    s = jnp.where(mask, s, NEG)
    m = jnp.max(s, axis=1, keepdims=True)
    e = jnp.where(mask, jnp.exp(s - m), 0.0)
    d = jnp.sum(e, axis=1, keepdims=True)
    return e / jnp.where(d > 0, d, 1.0)


def _select_blocks(imp, qpos, n_blk, axis):
    blk = lax.broadcasted_iota(jnp.int32, imp.shape, axis)
    cur = qpos >> SEL_SHIFT
    forced = (blk == 0) | (blk == cur) | (blk == cur - 1)
    valid = blk * SEL_BLOCK <= qpos
    score = jnp.where(forced, ATT_HPG + 1.0, jnp.where(valid, imp, -1.0))
    ahead = jnp.zeros(imp.shape, F32)
    for m in range(n_blk):
        sm = jnp.broadcast_to(score[m:m + 1, :] if axis == 0 else score[:, m:m + 1], imp.shape)
        tie = jnp.where(blk > m, 1.0, 0.0)
        ahead = ahead + jnp.where(sm > score, 1.0, jnp.where(sm == score, tie, 0.0))
    return jnp.where(blk < n_blk, jnp.where(ahead < N_SEL, 1.0, 0.0), 0.0)


SEL_SHIFT = SEL_BLOCK.bit_length() - 1
NT_DIMS = (((1,), (1,)), ((), ()))


SLC_UNROLL = 4
TOEP_TILES = min(SEQ // LANES, -(-(T5_THRESHOLDS[-1] + LANES - 1) // LANES) + 1)


NSA_GROUPS_PER_STEP = 4


def _nsa_prompt_multi_kernel(q_ref, gate_ref, kvc_ref, slc_ref, win_ref, bcmp_ref, toep_ref, ov_ref, spread_ref,
                             o_ref, s_scr, m_scr, l_scr, acc_scr):
    i = pl.program_id(2)
    ngs = NSA_GROUPS_PER_STEP
    tile = (Q_BLOCK, LANES)
    rows = ATT_HPG * Q_BLOCK
    lane = lax.broadcasted_iota(jnp.int32, tile, 1)
    sub = lax.broadcasted_iota(jnp.int32, tile, 0)
    low = lane < HEAD_DIM
    qpos = i * Q_BLOCK + sub
    head = lambda x, p: x[p * Q_BLOCK:(p + 1) * Q_BLOCK]
    heads = range(ATT_HPG)
    n_blk = SEQ // SEL_BLOCK
    blk_row = lax.broadcasted_iota(jnp.int32, (n_blk, LANES), 0)
    blk_of_key = lax.broadcasted_iota(jnp.int32, (n_blk, LANES), 1) >> SEL_SHIFT
    qpos_t = i * Q_BLOCK + lax.broadcasted_iota(jnp.int32, (n_blk, Q_BLOCK), 1)
    last_tile = SEQ // LANES - 1
    ov = ov_ref[...]
    glanes = lambda gg: slice(gg * LANES, (gg + 1) * LANES)

    def key_tile(kv_ref, gg, j):
        jc = jnp.clip(j, 0, last_tile)
        return kv_ref[pl.ds(pl.multiple_of(jc * LANES, LANES), LANES), glanes(gg)]

    def masked_scores(qs, gg, kv, j, mask):
        sc = lax.dot_general(qs, kv, NT_DIMS, preferred_element_type=F32)
        row0 = pl.multiple_of(jnp.clip(i - j, 0, TOEP_TILES - 1) * LANES, LANES)
        return [jnp.where(mask, head(sc, p) + toep_ref[gg * ATT_HPG + p, pl.ds(row0, LANES), :], NEG) for p in heads]

    def weights_times_v(scores, m, kv):
        es = [jnp.exp(scores[p] - m[p]) for p in heads]
        return es, jnp.dot(jnp.concatenate(es, axis=0).astype(BF16), kv, preferred_element_type=F32)

    def front(gg):
        qa = q_ref[:, gg * ATT_HPG * HEAD_DIM:(gg + 1) * ATT_HPG * HEAD_DIM] * (HEAD_DIM ** -0.5)
        qp = []
        for pair in range(ATT_HPG // 2):
            x = qa[:, pair * LANES:(pair + 1) * LANES]
            qp.append(jnp.where(low, x, 0.0))
            qp.append(jnp.where(low, pltpu.roll(x, HEAD_DIM, axis=1), 0.0))
        qs = jnp.concatenate(qp, axis=0).astype(BF16)

        kvc = kvc_ref[0, gg]
        s = lax.dot_general(qs, kvc, NT_DIMS, preferred_element_type=F32)
        s = s + bcmp_ref[gg * ATT_HPG:(gg + 1) * ATT_HPG].reshape(rows, LANES)
        cmask1 = (lane * CMP_STRIDE + (CMP_BLOCK - 1)) <= qpos
        pr = _softmax_rows(s, jnp.concatenate([cmask1] * ATT_HPG, axis=0))
        o_cmp = jnp.dot(pr.astype(BF16), kvc, preferred_element_type=F32)
        hi, lo = _split2(sum(head(pr, p) for p in heads))
        imp = jnp.dot(hi, ov, preferred_element_type=F32) + jnp.dot(lo, ov, preferred_element_type=F32)
        sel_t = _select_blocks(imp.T[:n_blk, :], qpos_t, n_blk, axis=0).astype(BF16)

        n_win = WINDOW // LANES + 1
        win_kv, win_s = [], []
        for w in range(n_win):
            j = i - (n_win - 1) + w
            dist = (i - j) * LANES + sub - lane
            inside = jnp.where(dist >= 0, jnp.where(dist < WINDOW, 1.0, 0.0), 0.0)
            win_kv.append(key_tile(win_ref, gg, j))
            win_s.append(masked_scores(qs, gg, win_kv[w], j, jnp.where(j >= 0, inside, 0.0) > 0.5))
        m_w = [jnp.max(functools.reduce(jnp.maximum, [win_s[w][p] for w in range(n_win)]), axis=1, keepdims=True)
               for p in heads]
        acc_w = jnp.zeros((rows, LANES), F32)
        l_w = [jnp.zeros(tile, F32) for _ in heads]
        for w in range(n_win):
            es, pv = weights_times_v(win_s[w], m_w, win_kv[w])
            acc_w = acc_w + pv
            l_w = [l_w[p] + es[p] for p in heads]
        o_win = [head(acc_w, p) / jnp.sum(l_w[p], axis=1, keepdims=True) for p in heads]
        return qs, o_cmp, sel_t, o_win

    fronts = [front(gg) for gg in range(ngs)]

    n_steps = (i + SLC_UNROLL) >> (SLC_UNROLL.bit_length() - 1)
    m_scr[...] = jnp.full(m_scr.shape, NEG, F32)

    def slc_scores(jj, carry):
        for gg in range(ngs):
            qs, _, sel_t, _ = fronts[gg]
            sps = []
            for r in range(SLC_UNROLL):
                j = SLC_UNROLL * jj + r
                dist = (i - j) * LANES + sub - lane
                expand = jnp.where(blk_row == 2 * j + blk_of_key, 1.0, 0.0).astype(BF16)
                picked = lax.dot_general(sel_t, expand, (((0,), (0,)), ((), ())), preferred_element_type=F32)
                sp = masked_scores(qs, gg, key_tile(slc_ref, gg, j), j, jnp.where(dist >= 0, picked, 0.0) > 0.5)
                for p in heads:
                    s_scr[j, gg * ATT_HPG + p] = sp[p]
                sps.append(sp)
            for p in heads:
                hp = gg * ATT_HPG + p
                m_scr[hp] = jnp.maximum(m_scr[hp], functools.reduce(jnp.maximum, [sp[p] for sp in sps]))
        return carry

    lax.fori_loop(0, n_steps, slc_scores, 0)
    for hp in range(ngs * ATT_HPG):
        m_scr[hp] = jnp.zeros(tile, F32) + jnp.max(m_scr[hp], axis=1, keepdims=True)
    l_scr[...] = jnp.zeros(l_scr.shape, F32)
    acc_scr[...] = jnp.zeros(acc_scr.shape, F32)

    def slc_weights(jj, carry):
        for gg in range(ngs):
            m = [m_scr[gg * ATT_HPG + p] for p in heads]
            pvs, ess = [], []
            for r in range(SLC_UNROLL):
                j = SLC_UNROLL * jj + r
                es, pv = weights_times_v([s_scr[j, gg * ATT_HPG + p] for p in heads], m, key_tile(slc_ref, gg, j))
                pvs.append(pv)
                ess.append(es)
            acc_scr[gg] += functools.reduce(jnp.add, pvs)
            for p in heads:
                hp = gg * ATT_HPG + p
                l_scr[hp] = l_scr[hp] + functools.reduce(jnp.add, [es[p] for es in ess])
        return carry

    lax.fori_loop(0, n_steps, slc_weights, 0)

    for gg in range(ngs):
        _, o_cmp, _, o_win = fronts[gg]
        g_hi, g_lo = _split2(gate_ref[:, glanes(gg)])
        spread = spread_ref[...]
        gb = jnp.dot(g_hi, spread, preferred_element_type=F32) + jnp.dot(g_lo, spread, preferred_element_type=F32)
        comb = []
        for p in heads:
            o_slc = head(acc_scr[gg], p) / jnp.sum(l_scr[gg * ATT_HPG + p], axis=1, keepdims=True)
            col = lambda br: gb[:, (br * ATT_HPG + p) * LANES:(br * ATT_HPG + p + 1) * LANES]
            comb.append(col(0) * head(o_cmp, p) + col(1) * o_slc + col(2) * o_win[p])
        for pair in range(ATT_HPG // 2):
            c0 = gg * ATT_HPG * HEAD_DIM + pair * LANES
            o_ref[:, c0:c0 + LANES] = jnp.where(low, pltpu.roll(comb[2 * pair], HEAD_DIM, axis=1), comb[2 * pair + 1])


def _nsa_prompt(q, gates, kvc, slc_pack, win_pack, bias_cmp, bias_toep, n_batch=BATCH):
    nqb = SEQ // Q_BLOCK
    ngs = NSA_GROUPS_PER_STEP
    ov = _overlap_padded(SEQ // CMP_STRIDE - 1, SEQ // SEL_BLOCK, LANES, LANES)
    gw = ngs * ATT_HPG * HEAD_DIM
    nh = ngs * ATT_HPG
    n_gate = N_BRANCH * ATT_HPG
    spread = np.zeros((LANES, n_gate, LANES), np.float32)
    spread[np.arange(n_gate), np.arange(n_gate), :] = 1.0
    spread = jnp.asarray(spread.reshape(LANES, n_gate * LANES), BF16)
    return pl.pallas_call(
        _nsa_prompt_multi_kernel,
        grid=(n_batch, ATT_GROUPS // ngs, nqb),
        in_specs=[
            pl.BlockSpec((Q_BLOCK, gw), lambda b, g, i: (b * nqb + i, g)),
            pl.BlockSpec((Q_BLOCK, ngs * LANES), lambda b, g, i: (b * nqb + i, g)),
            pl.BlockSpec((1, ngs, SEQ // CMP_STRIDE, LANES), lambda b, g, i: (0, g, b, 0)),
            pl.BlockSpec((SEQ, ngs * LANES), lambda b, g, i: (b, g)),
            pl.BlockSpec((SEQ, ngs * LANES), lambda b, g, i: (b, g)),
            pl.BlockSpec((nh, Q_BLOCK, LANES), lambda b, g, i: (g, i, 0)),
            pl.BlockSpec((nh, TOEP_TILES * LANES, LANES), lambda b, g, i: (g, 0, 0)),
            pl.BlockSpec((LANES, LANES), lambda b, g, i: (0, 0)),
            pl.BlockSpec((LANES, n_gate * LANES), lambda b, g, i: (0, 0)),
        ],
        out_specs=pl.BlockSpec((Q_BLOCK, gw), lambda b, g, i: (b * nqb + i, g)),
        out_shape=jax.ShapeDtypeStruct((n_batch * SEQ, ATT_WIDTH), F32),
        scratch_shapes=[pltpu.VMEM((nqb, nh, Q_BLOCK, LANES), F32),
                        pltpu.VMEM((nh, Q_BLOCK, LANES), F32), pltpu.VMEM((nh, Q_BLOCK, LANES), F32),
                        pltpu.VMEM((ngs, ATT_HPG * Q_BLOCK, LANES), F32)],
        compiler_params=_cparams("arbitrary", "arbitrary", "arbitrary"),
        name="nsa_prompt",
    )(q, gates, kvc, slc_pack, win_pack, bias_cmp, bias_toep, ov, spread)


SAMPLE_ROWS = ATT_HEADS * DEC_SEQ
SAMPLE_PAGES_PER_STEP = 32
SAMPLE_STEPS = N_PAGES // SAMPLE_PAGES_PER_STEP
SAMPLE_N_CMP = (PAST_LEN + DEC_SEQ - CMP_BLOCK) // CMP_STRIDE + 1
SAMPLE_N_BLK = -(-(PAST_LEN + DEC_SEQ) // SEL_BLOCK)
SAMPLE_BLK_LANES = 2 * LANES


def _nsa_sample_kernel(pt_ref, q_ref, gate_ref, kvc_ref, *rest):
    page_refs = rest[:SAMPLE_PAGES_PER_STEP]
    (slc_new_ref, win_state_ref, win_new_ref, bcmp_ref, bslc_ref, bwin_ref, perm_ref, ov_ref,
     o_ref, qbd_scr, sel_scr, m_scr, l_scr, acc_scr, ocmp_scr, owin_scr) = rest[SAMPLE_PAGES_PER_STEP:]
    del pt_ref
    u = pl.program_id(1)
    tile = (SAMPLE_ROWS, LANES)
    wide = (SAMPLE_ROWS, KV_WIDTH)
    lane = lax.broadcasted_iota(jnp.int32, tile, 1)
    t_row = lax.broadcasted_iota(jnp.int32, tile, 0) & (DEC_SEQ - 1)
    rows_per_group = ATT_HPG * DEC_SEQ
    own = ((lax.broadcasted_iota(jnp.int32, wide, 1) >> (HEAD_DIM.bit_length() - 1))
           == (lax.broadcasted_iota(jnp.int32, wide, 0) >> (rows_per_group.bit_length() - 1)))

    def reset():
        m_scr[...] = jnp.full(m_scr.shape, NEG, F32)
        l_scr[...] = jnp.zeros(l_scr.shape, F32)
        acc_scr[...] = jnp.zeros(acc_scr.shape, F32)

    def finish():
        l = jnp.sum(l_scr[...], axis=1, keepdims=True)
        return acc_scr[...] / jnp.where(l > 0, l, 1.0)

    def attend(tiles):
        qbd = qbd_scr[...]
        scores = []
        m_el = None
        for kt, _, bias, mask in tiles:
            s = jnp.where(mask, jnp.dot(qbd, kt.astype(BF16), preferred_element_type=F32) + bias, NEG)
            scores.append(s)
            m_el = s if m_el is None else jnp.maximum(m_el, s)
        m_prev = m_scr[...]
        m_new = jnp.maximum(m_prev, jnp.max(m_el, axis=1, keepdims=True))
        alpha = jnp.exp(m_prev - m_new)
        l_el = alpha * l_scr[...]
        acc = jnp.concatenate([alpha, alpha], axis=1) * acc_scr[...]
        for (_, vt, _, _), s in zip(tiles, scores):
            e = jnp.exp(s - m_new)
            l_el = l_el + e
            acc = acc + lax.dot_general(e.astype(BF16), vt.astype(BF16), NT_DIMS, preferred_element_type=F32)
        m_scr[...] = m_new
        l_scr[...] = l_el
        acc_scr[...] = acc

    def page_tile(kv4, bias, mask):
        return (kv4[0].reshape(KV_WIDTH, LANES), kv4[1].reshape(KV_WIDTH, LANES), bias, mask)

    def new_tile(ref, bias, mask):
        rows = jnp.concatenate([ref[...], jnp.zeros((LANES - DEC_SEQ, KV_ROW), F32)], axis=0)
        return (rows[:, :KV_WIDTH].T, rows[:, KV_WIDTH:].T, bias, mask)

    @pl.when(u == 0)
    def _():
        q = (q_ref[...] * (HEAD_DIM ** -0.5)).astype(BF16)
        qperm = jnp.dot(q, perm_ref[...], preferred_element_type=F32)
        qfull = jnp.concatenate([qperm[:, p * KV_WIDTH:(p + 1) * KV_WIDTH]
                                 for g in range(ATT_GROUPS) for p in range(ATT_HPG)], axis=0)
        qbd = jnp.where(own, qfull, 0.0).astype(BF16)
        qbd_scr[...] = qbd

        kvc = kvc_ref[0]
        s = lax.dot_general(qbd, kvc[:, :KV_WIDTH], NT_DIMS, preferred_element_type=F32) + bcmp_ref[...]
        cmask = lax.broadcasted_iota(jnp.int32, s.shape, 1) < SAMPLE_N_CMP
        pr = _softmax_rows(s, cmask)
        ocmp_scr[...] = jnp.dot(pr.astype(BF16), kvc[:, KV_WIDTH:], preferred_element_type=F32)
        ps = []
        for g in range(ATT_GROUPS):
            r0 = g * rows_per_group
            ps.append(sum(pr[r0 + p * DEC_SEQ:r0 + (p + 1) * DEC_SEQ, :] for p in range(ATT_HPG)))
        hi, lo = _split2(jnp.concatenate(ps, axis=0))
        ov = ov_ref[...]
        imp = jnp.dot(hi, ov, preferred_element_type=F32) + jnp.dot(lo, ov, preferred_element_type=F32)
        qpos = PAST_LEN + (lax.broadcasted_iota(jnp.int32, imp.shape, 0) & (DEC_SEQ - 1))
        sel = _select_blocks(imp, qpos, SAMPLE_N_BLK, axis=1)
        sel_scr[...] = jnp.concatenate([sel[g * DEC_SEQ:(g + 1) * DEC_SEQ, :]
                                        for g in range(ATT_GROUPS) for p in range(ATT_HPG)], axis=0).astype(BF16)

        reset()
        tiles = [page_tile(win_state_ref[0, :, :, :, w * LANES:(w + 1) * LANES],
                           bwin_ref[:, w * LANES:(w + 1) * LANES], (w * LANES + lane) > t_row)
                 for w in range(WINDOW // LANES)]
        tiles.append(new_tile(win_new_ref, bwin_ref[:, WINDOW:WINDOW + LANES], lane <= t_row))
        attend(tiles)
        owin_scr[...] = finish()
        reset()

    blk_of_key = lax.broadcasted_iota(jnp.int32, (SAMPLE_BLK_LANES, LANES), 1) >> SEL_SHIFT
    blk_row = lax.broadcasted_iota(jnp.int32, (SAMPLE_BLK_LANES, LANES), 0)
    sel = sel_scr[...]
    tiles = []
    for k in range(SAMPLE_PAGES_PER_STEP):
        pg = u * SAMPLE_PAGES_PER_STEP + k
        expand = jnp.where(blk_row == 2 * pg + blk_of_key, 1.0, 0.0).astype(BF16)
        picked = jnp.dot(sel, expand, preferred_element_type=F32) > 0.5
        bias = bslc_ref[:, pl.ds(pl.multiple_of(pg * LANES, LANES), LANES)]
        tiles.append(page_tile(page_refs[k][0], bias, picked))
    attend(tiles)

    @pl.when(u == SAMPLE_STEPS - 1)
    def _():
        attend([new_tile(slc_new_ref, bslc_ref[:, PAST_LEN:PAST_LEN + LANES], lane <= t_row)])
        o_slc = finish()
        gt = gate_ref[...]

        def gate_rows(br):
            cols = []
            for g in range(ATT_GROUPS):
                for p in range(ATT_HPG):
                    c = g * LANES + br * ATT_HPG + p
                    cols.append(jnp.broadcast_to(gt[:, c:c + 1], (DEC_SEQ, KV_WIDTH)))
            return jnp.concatenate(cols, axis=0)

        comb = gate_rows(0) * ocmp_scr[...] + gate_rows(1) * o_slc + gate_rows(2) * owin_scr[...]
        comb = jnp.where(own, comb, 0.0)
        per_head = []
        for p in range(ATT_HPG):
            per_head.append(sum(comb[(g * ATT_HPG + p) * DEC_SEQ:(g * ATT_HPG + p + 1) * DEC_SEQ, :]
                                for g in range(ATT_GROUPS)))
        hi, lo = _split2(jnp.concatenate(per_head, axis=1))
        perm = perm_ref[...]
        o_ref[...] = (lax.dot_general(hi, perm, NT_DIMS, preferred_element_type=F32)
                      + lax.dot_general(lo, perm, NT_DIMS, preferred_element_type=F32))


def _nsa_sample(q, gates, kvc, cache_slc, page_table, slc_rows, state_win, win_rows, bias_cmp, bias_slc, bias_win):
    n_seq = page_table.shape[0]
    row0 = N_PROMPT // DEC_SEQ
    src = np.arange(ATT_WIDTH)
    g, p, d = src // (ATT_HPG * HEAD_DIM), (src // HEAD_DIM) % ATT_HPG, src % HEAD_DIM
    perm = np.zeros((ATT_WIDTH, ATT_WIDTH), np.float32)
    perm[src, p * KV_WIDTH + g * HEAD_DIM + d] = 1.0
    ov = _overlap_padded(SAMPLE_N_CMP, SAMPLE_N_BLK, PAST_LEN // CMP_STRIDE, SAMPLE_BLK_LANES)
    tok = lambda w: pl.BlockSpec((DEC_SEQ, w), lambda b, u, pt: (row0 + b, 0))
    const = lambda a: pl.BlockSpec(a.shape, lambda b, u, pt, nd=a.ndim: (0,) * nd)
    per_seq = lambda s: pl.BlockSpec((1,) + s, lambda b, u, pt, nd=len(s): (b,) + (0,) * nd)
    pages = [pl.BlockSpec((1, 2, ATT_GROUPS, HEAD_DIM, PAGE_SIZE),
                          lambda b, u, pt, k=k: (pt[b, u * SAMPLE_PAGES_PER_STEP + k], 0, 0, 0, 0))
             for k in range(SAMPLE_PAGES_PER_STEP)]
    consts = (bias_cmp, bias_slc, bias_win, jnp.asarray(perm, BF16), ov)
    grid_spec = pltpu.PrefetchScalarGridSpec(
        num_scalar_prefetch=1, grid=(n_seq, SAMPLE_STEPS),
        in_specs=[tok(ATT_WIDTH), tok(ATT_GROUPS * LANES), per_seq((PAST_LEN // CMP_STRIDE, KV_ROW))] + pages
        + [tok(KV_ROW), per_seq((2, ATT_GROUPS, HEAD_DIM, WINDOW)), tok(KV_ROW)] + [const(a) for a in consts],
        out_specs=pl.BlockSpec((DEC_SEQ, ATT_WIDTH), lambda b, u, pt: (b, 0)),
        scratch_shapes=[pltpu.VMEM((SAMPLE_ROWS, KV_WIDTH), BF16), pltpu.VMEM((SAMPLE_ROWS, SAMPLE_BLK_LANES), BF16),
                        pltpu.VMEM((SAMPLE_ROWS, LANES), F32), pltpu.VMEM((SAMPLE_ROWS, LANES), F32),
                        pltpu.VMEM((SAMPLE_ROWS, KV_WIDTH), F32), pltpu.VMEM((SAMPLE_ROWS, KV_WIDTH), F32),
                        pltpu.VMEM((SAMPLE_ROWS, KV_WIDTH), F32)])
    return pl.pallas_call(
        _nsa_sample_kernel, grid_spec=grid_spec,
        out_shape=jax.ShapeDtypeStruct((n_seq * DEC_SEQ, ATT_WIDTH), F32),
        compiler_params=_cparams("arbitrary", "arbitrary"),
        name="nsa_sample",
    )(page_table, q, gates, kvc, *([cache_slc] * SAMPLE_PAGES_PER_STEP), slc_rows, state_win, win_rows, *consts)


def kernel(x_prompt, x_sample, cache_cmp_kv, cache_slc_kv, state_win_kv, state_hgrn, page_table,
           rel_bias_table, hgrn_lower_bound, norm_ffn1, w_ffn1_gate_up, w_ffn1_down, norm_mix,
           w_in, q_norm, k_norm, w_cmp1, b_cmp1, w_cmp2, b_cmp2, attn_out_norm, hgrn_out_norm,
           w_out, norm_ffn2, w_ffn2_gate_up, w_ffn2_down):
    assert DEPTH == 1
    l = 0
    kv_shape = (2, ATT_GROUPS, HEAD_DIM)

    y1 = _ffn((x_prompt.reshape(N_PROMPT, D_MODEL), x_sample.reshape(N_SAMPLE, D_MODEL)), norm_ffn1[l],
              w_ffn1_gate_up[l].astype(BF16), w_ffn1_down[l].astype(BF16))

    (q, cmp_rows, slc_rows, slc_pack, win_rows, win_pack, hq, lf, hk, hv, hog, gates) = _project_all(
        y1, norm_mix[l], _permute_w_in(w_in[l]), q_norm[l], k_norm[l], hgrn_lower_bound)

    tbl = rel_bias_table.astype(F32)
    first_end = CMP_BLOCK - 1
    bias_cmp_p = _bias_table(tbl, SEQ, LANES, -first_end, 1, -CMP_STRIDE)
    bias_toep = _bias_table(tbl, TOEP_TILES * LANES, LANES, 0, 1, -1)
    bias_cmp_s = _bias_table(tbl, DEC_SEQ, PAST_LEN // CMP_STRIDE, PAST_LEN - first_end, 1, -CMP_STRIDE)
    bias_slc_s = _bias_table(tbl, DEC_SEQ, PAST_LEN + LANES, PAST_LEN, 1, -1)
    bias_win_s = _bias_table(tbl, DEC_SEQ, WINDOW + LANES, WINDOW, 1, -1)
    rows_ht = lambda a: a.reshape(SAMPLE_ROWS, a.shape[-1])

    cw = _compress_weights(w_cmp1[l], b_cmp1[l], w_cmp2[l], b_cmp2[l])
    kvc_p = _compress(cmp_rows, cw, k_norm[l][0])
    kvc_s = _compress_paged(_cache_view(cache_cmp_kv[l]), page_table, w_cmp1[l], b_cmp1[l], w_cmp2[l], b_cmp2[l],
                            k_norm[l][0])

    o_att_p = _nsa_prompt(q, gates, kvc_p, slc_pack, win_pack, bias_cmp_p, bias_toep)
    o_att_s = _nsa_sample(q, gates, kvc_s, _cache_view(cache_slc_kv[l]), page_table, slc_rows,
                          _cache_view(state_win_kv[l]), win_rows,
                          rows_ht(bias_cmp_s), rows_ht(bias_slc_s), rows_ht(bias_win_s))

    o_hg_p, hg_p = _hgrn(hq, hk, hv, lf, None, n_seq=BATCH, t_len=SEQ, ch=HG_CHUNK_PROMPT, row0=0)
    o_hg_s, hg_s = _hgrn(hq, hk, hv, lf, state_hgrn[l].astype(F32), n_seq=DEC_BATCH, t_len=DEC_SEQ,
                         ch=HG_CHUNK_SAMPLE, row0=N_PROMPT)

    y2 = _mixer_out(y1, o_att_p, o_att_s, o_hg_p, o_hg_s, hog, attn_out_norm[l], hgrn_out_norm[l],
                    w_out[l].astype(BF16))
    y3_p, y3_s = _ffn(y2, norm_ffn2[l], w_ffn2_gate_up[l].astype(BF16), w_ffn2_down[l].astype(BF16), split_out=True)

    prompt_rows = lambda a: a[:N_PROMPT].reshape((1, BATCH, SEQ) + kv_shape)
    sample_rows = lambda a: a[N_PROMPT:].reshape((1, DEC_BATCH, DEC_SEQ) + kv_shape)
    win_p = prompt_rows(win_rows)[:, :, SEQ - min(WINDOW, SEQ):]
    win_s = jnp.concatenate([state_win_kv[l][:, DEC_SEQ:], sample_rows(win_rows)[0]], axis=1)[None]
    return (y3_p.reshape(BATCH, SEQ, D_MODEL), y3_s.reshape(DEC_BATCH, DEC_SEQ, D_MODEL),
            prompt_rows(cmp_rows), prompt_rows(slc_rows), win_p, hg_p[None],
            sample_rows(cmp_rows), sample_rows(slc_rows), win_s, hg_s[None].astype(state_hgrn.dtype))
```

```python
import functools
import math

import jax
import jax.numpy as jnp
import numpy as np
from jax import lax
from jax.experimental import pallas as pl
from jax.experimental.pallas import tpu as pltpu

D_MODEL = 2048
BATCH = 4
SEQ = 2048
DEPTH = 1
DEC_BATCH = 32
DEC_SEQ = 8
PAST_LEN = 8192
PAGE_SIZE = 128
HEAD_DIM = 64
ATT_HEADS = (D_MODEL // 2) // HEAD_DIM
ATT_GROUPS = ATT_HEADS // 4
ATT_HPG = ATT_HEADS // ATT_GROUPS
ATT_WIDTH = ATT_HEADS * HEAD_DIM
KV_WIDTH = ATT_GROUPS * HEAD_DIM
N_BRANCH = 3
CMP_BLOCK = 32
CMP_STRIDE = 16
CMP_RATIO = CMP_BLOCK // CMP_STRIDE
CMP_HIDDEN = 256
SEL_BLOCK = 64
N_SEL = 8
WINDOW = 512
Q_BLOCK = 128
HG_KEY = 128
HG_VAL = 128
HG_HEADS = (D_MODEL // 2) // HG_VAL
HG_WIDTH = HG_HEADS * HG_VAL
MIX_WIDTH = ATT_WIDTH + HG_WIDTH
D_FF = 256 * ((8 * D_MODEL // 3 + 255) // 256)
N_BUCKETS = 32
MAX_DISTANCE = 1024
EPS = 1e-6
IN_SPLITS = (ATT_WIDTH, 2 * N_BRANCH * KV_WIDTH, N_BRANCH * ATT_HEADS,
             HG_HEADS * HG_KEY, HG_HEADS * HG_KEY, HG_WIDTH, HG_WIDTH)
D_IN = sum(IN_SPLITS)

N_PROMPT = BATCH * SEQ
N_SAMPLE = DEC_BATCH * DEC_SEQ
N_TOKENS = N_PROMPT + N_SAMPLE
N_PAGES = PAST_LEN // PAGE_SIZE
KV_ROW = 2 * KV_WIDTH

LANES = 128
V7X_VMEM_BYTES = 64 * 1024 * 1024
VMEM_LIMIT_BYTES = 56 * 1024 * 1024

NEG = -1e30
F32 = jnp.float32
BF16 = jnp.bfloat16

FFN_ROW_TILE = 768
FFN_FF_TILE = 512
PROJ_ROW_TILE = 528
PROJ_COL_TILE = 512
OUT_ROW_TILE = 256
HG_CHUNK_PROMPT = 16
HG_CHUNK_SAMPLE = DEC_SEQ
CMP_UNIT_ROWS = 2048
CMP_UNITS = 4


def _cparams(*sem):
    return pltpu.CompilerParams(dimension_semantics=sem, vmem_limit_bytes=VMEM_LIMIT_BYTES)


def _t5_thresholds():
    n = np.arange(0, 2 * MAX_DISTANCE + 2)
    exact = N_BUCKETS // 2
    logn = np.log(np.maximum(n, 1).astype(np.float64) / exact)
    large = exact + (logn / math.log(MAX_DISTANCE / exact) * (N_BUCKETS - exact)).astype(np.int32)
    b = np.where(n < exact, n, np.minimum(large, N_BUCKETS - 1))
    return [int(n[b >= k][0]) for k in range(N_BUCKETS)]


T5_THRESHOLDS = _t5_thresholds()


def _block_diag_ones(n, blk):
    i = np.arange(n)
    return (i[:, None] // blk == i[None, :] // blk).astype(np.float32)


def _split2(x):
    hi = x.astype(BF16)
    lo = (x - hi.astype(F32)).astype(BF16)
    return hi, lo


def _group_rms(t, bd, gain, width):
    hi, lo = _split2(t * t)
    ss = (jnp.dot(hi, bd, preferred_element_type=F32) + jnp.dot(lo, bd, preferred_element_type=F32))
    return t * lax.rsqrt(ss * (1.0 / width) + EPS) * gain


def _ffn_kernel(*refs, split_in, split_out):
    n_in = 2 if split_in else 1
    x_ref = refs[0]
    gain_ref, wg_ref, wu_ref, wd_ref = refs[n_in:n_in + 4]
    o_ref = refs[n_in + 4]
    xn_ref, acc_ref = refs[-2:]
    j = pl.program_id(1)
    last_tile = pl.program_id(0) == pl.num_programs(0) - 1
    tm = x_ref.shape[0]

    def rows_in():
        x = x_ref[...]
        if split_in:
            xs = refs[1][...]
            x = jnp.where(last_tile, jnp.concatenate([x[:tm - xs.shape[0]], xs], axis=0), x)
        return x

    @pl.when(j == 0)
    def _():
        x = rows_in()
        y = x * lax.rsqrt(jnp.mean(x * x, axis=-1, keepdims=True) + EPS)
        xn_ref[...] = (y * gain_ref[...]).astype(BF16)
        acc_ref[...] = jnp.zeros_like(acc_ref)

    xn = xn_ref[...]
    g = jnp.dot(xn, wg_ref[...], preferred_element_type=F32)
    u = jnp.dot(xn, wu_ref[...], preferred_element_type=F32)
    a = (g * jax.nn.sigmoid(g) * u).astype(BF16)
    acc_ref[...] += jnp.dot(a, wd_ref[...], preferred_element_type=F32)

    @pl.when(j == pl.num_programs(1) - 1)
    def _():
        res = rows_in() + 0.5 * acc_ref[...]
        o_ref[...] = res
        if split_out:
            os_ref = refs[n_in + 5]

            @pl.when(last_tile)
            def _():
                os_ref[...] = res[tm - os_ref.shape[0]:]


def _ffn(x, gain, w_gu, w_down, split_out=False):
    split_in = isinstance(x, tuple)
    tm, tf = FFN_ROW_TILE, FFN_FF_TILE
    n_p, n_s = N_PROMPT, N_SAMPLE
    n, d = n_p + n_s, D_MODEL
    assert n % tm == 0 and D_FF % tf == 0 and n_p % tm == tm - n_s
    nj = D_FF // tf
    row = pl.BlockSpec((tm, d), lambda i, j: (i, 0))
    tail = pl.BlockSpec((n_s, d), lambda i, j: (0, 0))
    xs = x if split_in else (x,)
    return pl.pallas_call(
        functools.partial(_ffn_kernel, split_in=split_in, split_out=split_out),
        grid=(n // tm, nj),
        in_specs=([row, tail] if split_in else [row]) + [
            pl.BlockSpec((1, d), lambda i, j: (0, 0)),
            pl.BlockSpec((d, tf), lambda i, j: (0, j)),
            pl.BlockSpec((d, tf), lambda i, j: (0, j + nj)),
            pl.BlockSpec((tf, d), lambda i, j: (j, 0)),
        ],
        out_specs=(row, tail) if split_out else row,
        out_shape=((jax.ShapeDtypeStruct((n_p, d), F32), jax.ShapeDtypeStruct((n_s, d), F32)) if split_out
                   else jax.ShapeDtypeStruct((n, d), F32)),
        scratch_shapes=[pltpu.VMEM((tm, d), BF16), pltpu.VMEM((tm, d), F32)],
        compiler_params=_cparams("arbitrary", "arbitrary"),
        name="ffn",
    )(*xs, gain.reshape(1, d), w_gu, w_gu, w_down)


PB_Q, PB_CMP, PB_SLC, PB_WIN, PB_HQ, PB_HF, PB_HI, PB_HG, PB_GATE = 0, 2, 3, 4, 5, 7, 9, 11, 13
PROJ_N_BLOCKS = 14


def _permute_w_in(w_in):
    p = [int(v) for v in np.cumsum(IN_SPLITS)]
    a_g = w_in[:, p[1]:p[2]]
    src = np.zeros((PROJ_COL_TILE,), np.int32)
    valid = np.zeros((PROJ_COL_TILE,), bool)
    for g in range(ATT_GROUPS):
        for br in range(N_BRANCH):
            for hp in range(ATT_HPG):
                src[g * LANES + br * ATT_HPG + hp] = br * ATT_HEADS + g * ATT_HPG + hp
                valid[g * LANES + br * ATT_HPG + hp] = True
    gate = jnp.where(jnp.asarray(valid)[None, :], a_g[:, src], 0.0)
    return jnp.concatenate([w_in[:, :p[1]], w_in[:, p[2]:], gate], axis=1).astype(BF16)


def _pack_kv(k, v):
    parts = []
    for g in range(ATT_GROUPS):
        parts.append(k[:, g * HEAD_DIM:(g + 1) * HEAD_DIM])
        parts.append(v[:, g * HEAD_DIM:(g + 1) * HEAD_DIM])
    return jnp.concatenate(parts, axis=1).astype(BF16)


def _proj_kernel(x_ref, gain_ref, w_ref, bd_ref, qg_ref, kg_ref, lbp_ref,
                 q_ref, cmp_ref, slc_ref, slcp_ref, win_ref, winp_ref,
                 hq_ref, lf_ref, hk_ref, hv_ref, hog_ref, gate_ref, cmps_ref, slcs_ref, wins_ref, xn_ref):
    c = pl.program_id(1)
    last_tile = pl.program_id(0) == pl.num_programs(0) - 1
    tail = x_ref.shape[0] - cmps_ref.shape[0]

    @pl.when(c == 0)
    def _():
        x = x_ref[...]
        y = x * lax.rsqrt(jnp.mean(x * x, axis=-1, keepdims=True) + EPS)
        xn_ref[...] = (y * gain_ref[...]).astype(BF16)

    acc = jnp.dot(xn_ref[...], w_ref[...], preferred_element_type=F32)
    half = KV_WIDTH

    @pl.when(c < PB_CMP)
    def _():
        q_ref[...] = _group_rms(acc, bd_ref[...], qg_ref[...], HEAD_DIM)

    @pl.when(c == PB_CMP)
    def _():
        cmp_ref[...] = acc

        @pl.when(last_tile)
        def _():
            cmps_ref[...] = acc[tail:]

    def kv_branch(rows_ref, sample_ref, pack_ref, br):
        k = _group_rms(acc[:, :half], bd_ref[:half, :half], kg_ref[br - 1:br, :], HEAD_DIM)
        v = acc[:, half:]
        rows_ref[:, :half] = k
        rows_ref[:, half:] = v
        pack_ref[...] = _pack_kv(k, v)

        @pl.when(last_tile)
        def _():
            sample_ref[:, :half] = k[tail:]
            sample_ref[:, half:] = v[tail:]

    @pl.when(c == PB_SLC)
    def _():
        kv_branch(slc_ref, slcs_ref, slcp_ref, 1)

    @pl.when(c == PB_WIN)
    def _():
        kv_branch(win_ref, wins_ref, winp_ref, 2)

    @pl.when((c >= PB_HQ) & (c < PB_HF))
    def _():
        hq_ref[...] = acc

    @pl.when((c >= PB_HF) & (c < PB_HI))
    def _():
        p = lbp_ref[...]
        e = jnp.exp(p - jnp.max(p, axis=0, keepdims=True))
        lb = e[0:1, :] / jnp.sum(e, axis=0, keepdims=True)
        lf_ref[...] = jnp.log(lb + (1.0 - lb) * jax.nn.sigmoid(acc))
        hk_ref[...] = (1.0 - lb) * jax.nn.sigmoid(-acc)

    @pl.when((c >= PB_HI) & (c < PB_HG))
    def _():
        hv_ref[...] = acc

    @pl.when((c >= PB_HG) & (c < PB_GATE))
    def _():
        hog_ref[...] = acc

    @pl.when(c == PB_GATE)
    def _():
        gate_ref[...] = jax.nn.sigmoid(acc)


def _project_all(y, gain, w_perm, q_gain, k_gain, lb_logits):
    n, d = y.shape
    tm, tc = PROJ_ROW_TILE, PROJ_COL_TILE
    assert n == N_TOKENS and n % tm == 0 and N_PROMPT % tm == tm - N_SAMPLE and DEPTH == 1
    bd = jnp.asarray(_block_diag_ones(tc, HEAD_DIM), BF16)

    def two(first):
        return lambda i, c: (i, jnp.clip(c - first, 0, 1))

    one = lambda i, c: (i, 0)
    wide = lambda dt: jax.ShapeDtypeStruct((n, 2 * tc), dt)
    narrow = lambda dt: jax.ShapeDtypeStruct((n, tc), dt)
    prompt = jax.ShapeDtypeStruct((N_PROMPT, tc), F32)
    sample = jax.ShapeDtypeStruct((N_SAMPLE, tc), F32)
    out_shape = (wide(F32), prompt, prompt, narrow(BF16), prompt, narrow(BF16),
                 wide(F32), wide(F32), wide(F32), wide(F32), wide(F32), narrow(F32), sample, sample, sample)
    blk = lambda f: pl.BlockSpec((tm, tc), f)
    tail = pl.BlockSpec((N_SAMPLE, tc), lambda i, c: (0, 0))
    out_specs = (blk(two(PB_Q)), blk(one), blk(one), blk(one), blk(one), blk(one),
                 blk(two(PB_HQ)), blk(two(PB_HF)), blk(two(PB_HF)), blk(two(PB_HI)), blk(two(PB_HG)), blk(one),
                 tail, tail, tail)
    return pl.pallas_call(
        _proj_kernel,
        grid=(n // tm, PROJ_N_BLOCKS),
        in_specs=[
            pl.BlockSpec((tm, d), lambda i, c: (i, 0)),
            pl.BlockSpec((1, d), lambda i, c: (0, 0)),
            pl.BlockSpec((d, tc), lambda i, c: (0, c)),
            pl.BlockSpec((tc, tc), lambda i, c: (0, 0)),
            pl.BlockSpec((1, tc), lambda i, c: (0, jnp.clip(c, 0, 1))),
            pl.BlockSpec((2, KV_WIDTH), lambda i, c: (0, 0)),
            pl.BlockSpec((DEPTH + 1, tc), lambda i, c: (0, jnp.clip(c - PB_HF, 0, 1))),
        ],
        out_specs=out_specs,
        out_shape=out_shape,
        scratch_shapes=[pltpu.VMEM((tm, d), BF16)],
        compiler_params=_cparams("arbitrary", "arbitrary"),
        name="proj",
    )(y, gain.reshape(1, d), w_perm, bd, q_gain.reshape(1, ATT_WIDTH),
      k_gain[1:].reshape(2, KV_WIDTH), lb_logits)


def _hgrn_kernel(q_ref, k_ref, v_ref, lf_ref, tri_ref, ones_ref, s0_ref, o_ref, st_ref,
                 qe_scr, kd_scr, vt_scr, dec_scr, oi_scr, b_scr, *, t_len, ch, has_state):
    nj = t_len // ch
    shape3 = (nj, ch, HG_KEY)
    q3, k3, v3, lf3 = q_ref[...], k_ref[...], v_ref[...], lf_ref[...]
    tl = lax.broadcasted_iota(jnp.int32, shape3, 1)

    def row(x3, s):
        return jnp.broadcast_to(x3[:, s:s + 1, :], shape3)

    if t_len >= 256:
        lf2 = lf3.reshape(t_len, HG_KEY)
        parts = []
        for r0 in range(0, t_len, 256):
            x = lf2[r0:r0 + 256]
            hi = x.astype(BF16)
            r1 = x - hi.astype(F32)
            mid = r1.astype(BF16)
            lo = (r1 - mid.astype(F32)).astype(BF16)
            tri = tri_ref[...]
            parts.append(jnp.dot(tri, hi, preferred_element_type=F32)
                         + jnp.dot(tri, mid, preferred_element_type=F32)
                         + jnp.dot(tri, lo, preferred_element_type=F32))
        b3 = jnp.concatenate(parts, axis=0).reshape(shape3)
    else:
        b3 = jnp.zeros(shape3, F32)
        for s in range(ch):
            b3 = b3 + jnp.where(tl >= s, row(lf3, s), 0.0)

    bl3 = row(b3, ch - 1)
    qe3 = q3 * jnp.exp(b3)
    kd3 = k3 * jnp.exp(bl3 - b3)
    dec_scr[...] = jnp.exp(b3[:, ch - 1:ch, :])
    blocked = t_len % LANES == 0
    if blocked:
        nb = t_len // LANES
        qe2, kd2, v2 = (x.reshape(t_len, HG_KEY) for x in (qe3, kd3, v3))
        for m in range(nb):
            blk = slice(m * LANES, (m + 1) * LANES)
            qe_scr[m] = qe2[blk].T.astype(BF16)
            vt_scr[m] = v2[blk].T.astype(BF16)
            kd_scr[m] = kd2[blk].astype(BF16)
    else:
        qe_scr[...] = qe3.astype(BF16)
        kd_scr[...] = kd3.astype(BF16)

    ones = ones_ref[...]
    n8 = ch // 8
    shape8 = (nj, 8, HG_KEY)
    sub8 = lax.broadcasted_iota(jnp.int32, shape8, 1)
    q5, b5 = (x.reshape(nj, n8, 8, HG_KEY) for x in (q3, b3))
    b_scr[...] = b3
    od = [jnp.zeros(shape8, F32) for _ in range(n8)]
    for s in range(ch):
        hs, ss = divmod(s, 8)
        ks, bs, vs = (jnp.broadcast_to(r[:, s:s + 1, :], shape8) for r in (k_ref, b_scr, v_ref))
        for hh in range(hs, n8):
            diff = b5[:, hh] - bs
            if hh == hs:
                diff = jnp.where(sub8 >= ss, diff, NEG)
            w = q5[:, hh] * ks * jnp.exp(diff)
            a = jnp.dot(w.reshape(nj * 8, HG_KEY).astype(BF16), ones, preferred_element_type=F32)
            od[hh] = od[hh] + a.reshape(shape8) * vs
    for hh in range(n8):
        o_ref[:, hh * 8:(hh + 1) * 8, :] = od[hh]

    if has_state:
        st0 = s0_ref[0, 0].T
    else:
        st0 = jnp.zeros((HG_VAL, HG_KEY), F32)

    if blocked:
        cpb = LANES // ch
        lane_chunk = lax.broadcasted_iota(jnp.int32, (HG_KEY, LANES), 1) >> (ch.bit_length() - 1)
        keep = [jnp.where(lane_chunk == r, 1.0, 0.0).astype(BF16) for r in range(cpb)]

        def body(m, st):
            qet, vt, kd = qe_scr[m], vt_scr[m], kd_scr[m]
            ot = jnp.zeros((HG_VAL, LANES), F32)
            for r in range(cpb):
                ot = ot + jnp.dot(st.astype(BF16), qet * keep[r], preferred_element_type=F32)
                ut = jnp.dot(vt * keep[r], kd, preferred_element_type=F32)
                st = st * dec_scr[m * cpb + r] + ut
            oi_scr[m] = ot
            return st

        st = lax.fori_loop(0, t_len // LANES, body, st0)
        for m in range(t_len // LANES):
            o_ref[m * cpb:(m + 1) * cpb] = o_ref[m * cpb:(m + 1) * cpb] + oi_scr[m].T.reshape(cpb, ch, HG_VAL)
    else:
        def body(j, st):
            oi_scr[j] = lax.dot_general(qe_scr[j], st.astype(BF16), NT_DIMS, preferred_element_type=F32)
            ut = lax.dot_general(v_ref[j].astype(BF16), kd_scr[j], (((0,), (0,)), ((), ())),
                                 preferred_element_type=F32)
            return st * dec_scr[j] + ut

        st = lax.fori_loop(0, nj, body, st0, unroll=True)
        o_ref[...] = o_ref[...] + oi_scr[...]
    st_ref[0, 0] = st.T


def _hgrn(hq, hk, hv, lf, s0, *, n_seq, t_len, ch, row0):
    n = hq.shape[0]
    nj = t_len // ch
    assert row0 % t_len == 0 and t_len % ch == 0 and ch % 8 == 0
    blk0 = row0 // t_len
    r3 = lambda a: a.reshape(n // ch, ch, HG_WIDTH)
    has_state = s0 is not None
    if not has_state:
        s0 = jnp.zeros((1, 1, HG_KEY, HG_VAL), F32)
    tri_n = 256 if t_len >= 256 else 8
    i = np.arange(tri_n)
    tri = jnp.asarray(((i[:, None] // ch == i[None, :] // ch) & (i[:, None] >= i[None, :])).astype(np.float32), BF16)
    ones = jnp.ones((HG_KEY, HG_KEY), BF16)
    blk_shape = (t_len // LANES, LANES, HG_KEY) if t_len % LANES == 0 else (nj, ch, HG_KEY)
    hps = HG_HEADS if nj == 1 else 1
    seq = pl.BlockSpec((nj, ch, hps * HG_KEY), lambda b, h: (blk0 + b, 0, h))
    state_in = pl.BlockSpec((1, hps, HG_KEY, HG_VAL), (lambda b, h: (b, h, 0, 0)) if has_state else (lambda b, h: (0, 0, 0, 0)))

    def body(q_ref, k_ref, v_ref, lf_ref, tri_ref, ones_ref, s0_ref, o_ref, st_ref, *scratch):
        for h in range(hps):
            ln = slice(h * HG_KEY, (h + 1) * HG_KEY)
            _hgrn_kernel(q_ref.at[:, :, ln], k_ref.at[:, :, ln], v_ref.at[:, :, ln], lf_ref.at[:, :, ln], tri_ref,
                         ones_ref, s0_ref.at[:, h:h + 1], o_ref.at[:, :, ln], st_ref.at[:, h:h + 1], *scratch,
                         t_len=t_len, ch=ch, has_state=has_state)

    o, st = pl.pallas_call(
        body,
        grid=(n_seq, HG_HEADS // hps),
        in_specs=[seq, seq, seq, seq,
                  pl.BlockSpec((tri_n, tri_n), lambda b, h: (0, 0)),
                  pl.BlockSpec((HG_KEY, HG_KEY), lambda b, h: (0, 0)),
                  state_in],
        out_specs=(pl.BlockSpec((nj, ch, hps * HG_VAL), lambda b, h: (b, 0, h)),
                   pl.BlockSpec((1, hps, HG_KEY, HG_VAL), lambda b, h: (b, h, 0, 0))),
        out_shape=(jax.ShapeDtypeStruct((n_seq * nj, ch, HG_WIDTH), F32),
                   jax.ShapeDtypeStruct((n_seq, HG_HEADS, HG_KEY, HG_VAL), F32)),
        scratch_shapes=[pltpu.VMEM(blk_shape, BF16), pltpu.VMEM(blk_shape, BF16), pltpu.VMEM(blk_shape, BF16),
                        pltpu.VMEM((nj, 1, HG_KEY), F32), pltpu.VMEM(blk_shape, F32),
                        pltpu.VMEM((nj, ch, HG_KEY), F32)],
        compiler_params=_cparams("arbitrary", "arbitrary"),
        name="hgrn",
    )(r3(hq), r3(hk), r3(hv), r3(lf), tri, ones, s0)
    return o.reshape(n_seq * t_len, HG_WIDTH), st


def _mixout_kernel(y_ref, oap_ref, oas_ref, ohp_ref, ohs_ref, og_ref, ag_ref, hgain_ref, w_ref, o_ref):
    is_sample = pl.program_id(0) == pl.num_programs(0) - 1
    oa = jnp.where(is_sample, oas_ref[...], oap_ref[...])
    a = oa * lax.rsqrt(jnp.mean(oa * oa, axis=-1, keepdims=True) + EPS) * ag_ref[...]
    oh = jnp.where(is_sample, ohs_ref[...], ohp_ref[...])
    hs = []
    for h in range(HG_HEADS):
        x = oh[:, h * HG_VAL:(h + 1) * HG_VAL]
        hs.append(x * lax.rsqrt(jnp.mean(x * x, axis=-1, keepdims=True) + EPS))
    og = og_ref[...]
    hh = jnp.concatenate(hs, axis=1) * hgain_ref[...] * (og * jax.nn.sigmoid(og))
    m = (jnp.dot(a.astype(BF16), w_ref[:ATT_WIDTH, :], preferred_element_type=F32)
         + jnp.dot(hh.astype(BF16), w_ref[ATT_WIDTH:, :], preferred_element_type=F32))
    o_ref[...] = y_ref[...] + m


def _mixer_out(y, o_att_p, o_att_s, o_hg_p, o_hg_s, og, attn_gain, hg_gain, w_out):
    n, d = y.shape
    tm = OUT_ROW_TILE
    n_p = o_att_p.shape[0]
    assert n % tm == 0 and n_p % tm == 0 and o_att_s.shape[0] == tm and n == n_p + tm
    row = lambda w: pl.BlockSpec((tm, w), lambda i: (i, 0))
    prompt_row = lambda w: pl.BlockSpec((tm, w), lambda i: (jnp.minimum(i, n_p // tm - 1), 0))
    const = lambda s: pl.BlockSpec(s, lambda i: (0, 0))
    return pl.pallas_call(
        _mixout_kernel,
        grid=(n // tm,),
        in_specs=[row(d), prompt_row(ATT_WIDTH), const((tm, ATT_WIDTH)), prompt_row(HG_WIDTH), const((tm, HG_WIDTH)),
                  row(HG_WIDTH), const((1, ATT_WIDTH)), const((1, HG_WIDTH)), const((MIX_WIDTH, d))],
        out_specs=row(d),
        out_shape=jax.ShapeDtypeStruct((n, d), F32),
        compiler_params=_cparams("arbitrary"),
        name="mixout",
    )(y, o_att_p, o_att_s, o_hg_p, o_hg_s, og, attn_gain.reshape(1, ATT_WIDTH), hg_gain.reshape(1, HG_WIDTH), w_out)


def _bias_kernel(tbl_ref, o_ref, *, a0, ar, ac, rows_blk):
    h = pl.program_id(0)
    rb = pl.program_id(1)
    shape = o_ref.shape[1:]
    r = lax.broadcasted_iota(jnp.int32, shape, 0) + rb * rows_blk
    c = lax.broadcasted_iota(jnp.int32, shape, 1)
    n = a0 + ar * r + ac * c
    out = jnp.full(shape, tbl_ref[0, h], F32)
    for k in range(1, N_BUCKETS):
        out = jnp.where(n >= T5_THRESHOLDS[k], tbl_ref[k, h], out)
    o_ref[0] = out


def _bias_table(tbl, rows, cols, a0, ar, ac):
    rows_blk = max(r for r in range(8, min(rows, 512) + 1, 8) if rows % r == 0)
    assert cols % LANES == 0
    return pl.pallas_call(
        functools.partial(_bias_kernel, a0=a0, ar=ar, ac=ac, rows_blk=rows_blk),
        grid=(ATT_HEADS, rows // rows_blk),
        in_specs=[pl.BlockSpec(memory_space=pltpu.SMEM)],
        out_specs=pl.BlockSpec((1, rows_blk, cols), lambda h, rb: (h, rb, 0)),
        out_shape=jax.ShapeDtypeStruct((ATT_HEADS, rows, cols), F32),
        compiler_params=_cparams("arbitrary", "arbitrary"),
        name="t5_bias",
    )(tbl)


def _compress_weights(w1, b1, w2, b2):
    w = w1.reshape(2, CMP_RATIO, CMP_STRIDE, HEAD_DIM, CMP_HIDDEN)
    z = jnp.zeros_like(w)
    top = jnp.concatenate([w, z], axis=-1)
    bot = jnp.concatenate([z, w], axis=-1)
    wpad = jnp.stack([top, bot], axis=3)
    wpad = wpad.reshape(2 * CMP_RATIO, CMP_STRIDE * LANES, 2 * CMP_HIDDEN).astype(BF16)
    z2 = jnp.zeros_like(w2)
    w2pad = jnp.concatenate([jnp.concatenate([w2, z2], axis=-1), jnp.concatenate([z2, w2], axis=-1)], axis=1)
    return wpad, jnp.concatenate([b1, b1], axis=-1), w2pad.astype(BF16), jnp.concatenate([b2, b2], axis=-1)


def _compress_kernel(rows_ref, perm_ref, wpad_ref, b1_ref, w2pad_ref, b2_ref, kg_ref, bd_ref, out_ref, xs_ref):
    u = pl.program_id(1)
    n_chunk = CMP_UNITS * CMP_UNIT_ROWS // CMP_STRIDE
    perm = perm_ref[...]

    for t in range(CMP_UNIT_ROWS // 256):
        x = rows_ref[t * 256:(t + 1) * 256, :]
        y = jnp.dot(perm, x.astype(BF16), preferred_element_type=F32).astype(BF16)
        base = pl.multiple_of(u * (CMP_UNIT_ROWS // CMP_STRIDE) + t * 16, 16)
        for s in range(CMP_STRIDE):
            xs_ref[s, pl.ds(base, 16), :] = y[s * 16:(s + 1) * 16, :]

    @pl.when(u == CMP_UNITS - 1)
    def _():
        halves = []
        for kv in range(2):
            outs = []
            for j in range(2):
                col = (kv * 2 + j) * LANES
                lhs = jnp.concatenate([xs_ref[s, :, col:col + LANES] for s in range(CMP_STRIDE)], axis=1)
                h0 = jnp.dot(lhs, wpad_ref[kv * CMP_RATIO + 0], preferred_element_type=F32)
                h1 = jnp.dot(lhs, wpad_ref[kv * CMP_RATIO + 1], preferred_element_type=F32)
                h = b1_ref[kv:kv + 1, :] + h0 + pltpu.roll(h1, n_chunk - 1, axis=0)
                a = (h * jax.nn.sigmoid(h)).astype(BF16)
                outs.append(jnp.dot(a, w2pad_ref[kv], preferred_element_type=F32) + b2_ref[kv:kv + 1, :])
            halves.append(jnp.concatenate(outs, axis=1))
        pk = _pack_kv(_group_rms(halves[0], bd_ref[...], kg_ref[...], HEAD_DIM), halves[1])
        for g in range(ATT_GROUPS):
            out_ref[0, g] = pk[:, g * LANES:(g + 1) * LANES]


def _compress(rows, cw, k_gain0):
    wpad, b1, w2pad, b2 = cw
    n_chunk = CMP_UNITS * CMP_UNIT_ROWS // CMP_STRIDE
    i = np.arange(256)
    perm = np.zeros((256, 256), np.float32)
    perm[(i % 16) * 16 + i // 16, i] = 1.0
    consts = (jnp.asarray(perm, BF16), wpad, b1, w2pad, b2, k_gain0.reshape(1, KV_WIDTH),
              jnp.asarray(_block_diag_ones(KV_WIDTH, HEAD_DIM), BF16))
    const = lambda a: pl.BlockSpec(a.shape, lambda b, u, nd=a.ndim: (0,) * nd)
    return pl.pallas_call(
        _compress_kernel,
        grid=(1, CMP_UNITS),
        in_specs=[pl.BlockSpec((CMP_UNIT_ROWS, KV_ROW), lambda b, u: (u, 0))] + [const(a) for a in consts],
        out_specs=pl.BlockSpec((1, ATT_GROUPS, n_chunk, LANES), lambda b, u: (0, 0, 0, 0)),
        out_shape=jax.ShapeDtypeStruct((1, ATT_GROUPS, n_chunk, LANES), BF16),
        scratch_shapes=[pltpu.VMEM((CMP_STRIDE, n_chunk, KV_ROW), BF16)],
        compiler_params=_cparams("arbitrary", "arbitrary"),
        name="compress",
    )(rows, *consts)


PAGED_UNIT_ROWS = 4096
PAGED_UNITS = PAST_LEN // PAGED_UNIT_ROWS
PAGES_PER_STEP = PAGED_UNIT_ROWS // PAGE_SIZE


def _cache_view(cache):
    return jnp.transpose(cache, (0, 2, 3, 4, 1))


def _compress_paged_kernel(pt_ref, *refs):
    del pt_ref
    page_refs = refs[:PAGES_PER_STEP]
    pick_ref, w1_ref, b1_ref, w2_ref, b2_ref, kg_ref, bd_ref, out_ref, xs_ref = refs[PAGES_PER_STEP:]
    u = pl.program_id(1)
    n_chunk = PAST_LEN // CMP_STRIDE
    pick = pick_ref[...]
    n_kvg = 2 * ATT_GROUPS

    for t in range(PAGES_PER_STEP // 2):
        base = pl.multiple_of(u * (PAGED_UNIT_ROWS // CMP_STRIDE) + t * 16, 16)
        kt = jnp.concatenate([page_refs[2 * t][0].reshape(n_kvg * HEAD_DIM, PAGE_SIZE),
                              page_refs[2 * t + 1][0].reshape(n_kvg * HEAD_DIM, PAGE_SIZE)], axis=1).astype(BF16)
        z = jnp.dot(kt, pick, preferred_element_type=F32)
        for kvg in range(n_kvg):
            zk = z[kvg * HEAD_DIM:(kvg + 1) * HEAD_DIM]
            y = jnp.concatenate([zk[:, :LANES], zk[:, LANES:]], axis=0).T.astype(BF16)
            for s2 in range(CMP_STRIDE // 2):
                xs_ref[kvg, s2, pl.ds(base, 16), :] = y[s2 * 16:(s2 + 1) * 16, :]

    @pl.when(u == PAGED_UNITS - 1)
    def _():
        halves = []
        for kv in range(2):
            lhs = jnp.concatenate(
                [jnp.concatenate([xs_ref[kv * ATT_GROUPS + g, s2] for s2 in range(CMP_STRIDE // 2)], axis=1)
                 for g in range(ATT_GROUPS)], axis=0)
            h0 = jnp.dot(lhs, w1_ref[kv * CMP_RATIO + 0], preferred_element_type=F32)
            h1 = jnp.dot(lhs, w1_ref[kv * CMP_RATIO + 1], preferred_element_type=F32)
            h = b1_ref[kv:kv + 1, :] + h0 + pltpu.roll(h1, ATT_GROUPS * n_chunk - 1, axis=0)
            a = (h * jax.nn.sigmoid(h)).astype(BF16)
            o = b2_ref[kv:kv + 1, :]
            for g in range(ATT_GROUPS):
                o = o + jnp.dot(a[g * n_chunk:(g + 1) * n_chunk], w2_ref[kv * ATT_GROUPS + g], preferred_element_type=F32)
            halves.append(o)
        out_ref[0, :, :KV_WIDTH] = _group_rms(halves[0], bd_ref[...], kg_ref[...], HEAD_DIM).astype(BF16)
        out_ref[0, :, KV_WIDTH:] = halves[1].astype(BF16)


def _compress_paged(cache_t, page_table, w1, b1, w2, b2, k_gain0):
    n_seq = page_table.shape[0]
    n_chunk = PAST_LEN // CMP_STRIDE
    r = np.arange(LANES)
    s2, c = r // 16, r % 16
    pick = np.zeros((2 * PAGE_SIZE, 2, LANES), np.float32)
    for half in range(2):
        pick[CMP_STRIDE * c + 2 * s2 + half, half, r] = 1.0
    w2p = jnp.zeros((2, ATT_GROUPS, CMP_HIDDEN, KV_WIDTH), F32)
    for g in range(ATT_GROUPS):
        w2p = w2p.at[:, g, :, g * HEAD_DIM:(g + 1) * HEAD_DIM].set(w2)
    consts = (jnp.asarray(pick.reshape(2 * PAGE_SIZE, 2 * LANES), BF16),
              w1.reshape(2 * CMP_RATIO, CMP_STRIDE * HEAD_DIM, CMP_HIDDEN).astype(BF16), b1,
              w2p.reshape(2 * ATT_GROUPS, CMP_HIDDEN, KV_WIDTH).astype(BF16), jnp.tile(b2, (1, ATT_GROUPS)),
              k_gain0.reshape(1, KV_WIDTH), jnp.asarray(_block_diag_ones(KV_WIDTH, HEAD_DIM), BF16))
    const = lambda a: pl.BlockSpec(a.shape, lambda b, u, pt, nd=a.ndim: (0,) * nd)
    pages = [pl.BlockSpec((1, 2, ATT_GROUPS, HEAD_DIM, PAGE_SIZE),
                          lambda b, u, pt, k=k: (pt[b, u * PAGES_PER_STEP + k], 0, 0, 0, 0))
             for k in range(PAGES_PER_STEP)]
    grid_spec = pltpu.PrefetchScalarGridSpec(
        num_scalar_prefetch=1, grid=(n_seq, PAGED_UNITS),
        in_specs=pages + [const(a) for a in consts],
        out_specs=pl.BlockSpec((1, n_chunk, KV_ROW), lambda b, u, pt: (b, 0, 0)),
        scratch_shapes=[pltpu.VMEM((2 * ATT_GROUPS, CMP_STRIDE // 2, n_chunk, LANES), BF16)])
    return pl.pallas_call(
        _compress_paged_kernel, grid_spec=grid_spec,
        out_shape=jax.ShapeDtypeStruct((n_seq, n_chunk, KV_ROW), BF16),
        compiler_params=_cparams("arbitrary", "arbitrary"),
        name="compress_paged",
    )(page_table, *([cache_t] * PAGES_PER_STEP), *consts)


def _block_overlap(n_cmp, n_blk):
    cs = np.arange(n_cmp)[:, None] * CMP_STRIDE
    bs = np.arange(n_blk)[None, :] * SEL_BLOCK
    ov = np.minimum(cs + CMP_BLOCK, bs + SEL_BLOCK) - np.maximum(cs, bs)
    return (np.clip(ov, 0, None) / CMP_BLOCK).astype(np.float32)


def _overlap_padded(n_cmp, n_blk, rows, cols):
    ov = np.zeros((rows, cols), np.float32)
    ov[:n_cmp, :n_blk] = _block_overlap(n_cmp, n_blk)
    return jnp.asarray(ov, BF16)


def _softmax_rows(s, mask):
    s = jnp.where(mask, s, NEG)
    m = jnp.max(s, axis=1, keepdims=True)
    e = jnp.where(mask, jnp.exp(s - m), 0.0)
    d = jnp.sum(e, axis=1, keepdims=True)
    return e / jnp.where(d > 0, d, 1.0)


def _select_blocks(imp, qpos, n_blk, axis):
    blk = lax.broadcasted_iota(jnp.int32, imp.shape, axis)
    cur = qpos >> SEL_SHIFT
    forced = (blk == 0) | (blk == cur) | (blk == cur - 1)
    valid = blk * SEL_BLOCK <= qpos
    score = jnp.where(forced, ATT_HPG + 1.0, jnp.where(valid, imp, -1.0))
    ahead = jnp.zeros(imp.shape, F32)
    for m in range(n_blk):
        sm = jnp.broadcast_to(score[m:m + 1, :] if axis == 0 else score[:, m:m + 1], imp.shape)
        tie = jnp.where(blk > m, 1.0, 0.0)
        ahead = ahead + jnp.where(sm > score, 1.0, jnp.where(sm == score, tie, 0.0))
    return jnp.where(blk < n_blk, jnp.where(ahead < N_SEL, 1.0, 0.0), 0.0)


SEL_SHIFT = SEL_BLOCK.bit_length() - 1
NT_DIMS = (((1,), (1,)), ((), ()))


SLC_UNROLL = 4
TOEP_TILES = min(SEQ // LANES, -(-(T5_THRESHOLDS[-1] + LANES - 1) // LANES) + 1)


NSA_GROUPS_PER_STEP = 4


def _nsa_prompt_multi_kernel(q_ref, gate_ref, kvc_ref, slc_ref, win_ref, bcmp_ref, toep_ref, ov_ref, spread_ref,
                             o_ref, s_scr, m_scr, l_scr, acc_scr):
    i = pl.program_id(2)
    ngs = NSA_GROUPS_PER_STEP
    tile = (Q_BLOCK, LANES)
    rows = ATT_HPG * Q_BLOCK
    lane = lax.broadcasted_iota(jnp.int32, tile, 1)
    sub = lax.broadcasted_iota(jnp.int32, tile, 0)
    low = lane < HEAD_DIM
    qpos = i * Q_BLOCK + sub
    head = lambda x, p: x[p * Q_BLOCK:(p + 1) * Q_BLOCK]
    heads = range(ATT_HPG)
    n_blk = SEQ // SEL_BLOCK
    blk_row = lax.broadcasted_iota(jnp.int32, (n_blk, LANES), 0)
    blk_of_key = lax.broadcasted_iota(jnp.int32, (n_blk, LANES), 1) >> SEL_SHIFT
    qpos_t = i * Q_BLOCK + lax.broadcasted_iota(jnp.int32, (n_blk, Q_BLOCK), 1)
    last_tile = SEQ // LANES - 1
    ov = ov_ref[...]
    glanes = lambda gg: slice(gg * LANES, (gg + 1) * LANES)

    def key_tile(kv_ref, gg, j):
        jc = jnp.clip(j, 0, last_tile)
        return kv_ref[pl.ds(pl.multiple_of(jc * LANES, LANES), LANES), glanes(gg)]

    def masked_scores(qs, gg, kv, j, mask):
        sc = lax.dot_general(qs, kv, NT_DIMS, preferred_element_type=F32)
        row0 = pl.multiple_of(jnp.clip(i - j, 0, TOEP_TILES - 1) * LANES, LANES)
        return [jnp.where(mask, head(sc, p) + toep_ref[gg * ATT_HPG + p, pl.ds(row0, LANES), :], NEG) for p in heads]

    def weights_times_v(scores, m, kv):
        es = [jnp.exp(scores[p] - m[p]) for p in heads]
        return es, jnp.dot(jnp.concatenate(es, axis=0).astype(BF16), kv, preferred_element_type=F32)

    def front(gg):
        qa = q_ref[:, gg * ATT_HPG * HEAD_DIM:(gg + 1) * ATT_HPG * HEAD_DIM] * (HEAD_DIM ** -0.5)
        qp = []
        for pair in range(ATT_HPG // 2):
            x = qa[:, pair * LANES:(pair + 1) * LANES]
            qp.append(jnp.where(low, x, 0.0))
            qp.append(jnp.where(low, pltpu.roll(x, HEAD_DIM, axis=1), 0.0))
        qs = jnp.concatenate(qp, axis=0).astype(BF16)

        kvc = kvc_ref[0, gg]
        s = lax.dot_general(qs, kvc, NT_DIMS, preferred_element_type=F32)
        s = s + bcmp_ref[gg * ATT_HPG:(gg + 1) * ATT_HPG].reshape(rows, LANES)
        cmask1 = (lane * CMP_STRIDE + (CMP_BLOCK - 1)) <= qpos
        pr = _softmax_rows(s, jnp.concatenate([cmask1] * ATT_HPG, axis=0))
        o_cmp = jnp.dot(pr.astype(BF16), kvc, preferred_element_type=F32)
        hi, lo = _split2(sum(head(pr, p) for p in heads))
        imp = jnp.dot(hi, ov, preferred_element_type=F32) + jnp.dot(lo, ov, preferred_element_type=F32)
        sel_t = _select_blocks(imp.T[:n_blk, :], qpos_t, n_blk, axis=0).astype(BF16)

        n_win = WINDOW // LANES + 1
        win_kv, win_s = [], []
        for w in range(n_win):
            j = i - (n_win - 1) + w
            dist = (i - j) * LANES + sub - lane
            inside = jnp.where(dist >= 0, jnp.where(dist < WINDOW, 1.0, 0.0), 0.0)
            win_kv.append(key_tile(win_ref, gg, j))
            win_s.append(masked_scores(qs, gg, win_kv[w], j, jnp.where(j >= 0, inside, 0.0) > 0.5))
        m_w = [jnp.max(functools.reduce(jnp.maximum, [win_s[w][p] for w in range(n_win)]), axis=1, keepdims=True)
               for p in heads]
        acc_w = jnp.zeros((rows, LANES), F32)
        l_w = [jnp.zeros(tile, F32) for _ in heads]
        for w in range(n_win):
            es, pv = weights_times_v(win_s[w], m_w, win_kv[w])
            acc_w = acc_w + pv
            l_w = [l_w[p] + es[p] for p in heads]
        o_win = [head(acc_w, p) / jnp.sum(l_w[p], axis=1, keepdims=True) for p in heads]
        return qs, o_cmp, sel_t, o_win

    fronts = [front(gg) for gg in range(ngs)]

    n_steps = (i + SLC_UNROLL) >> (SLC_UNROLL.bit_length() - 1)
    m_scr[...] = jnp.full(m_scr.shape, NEG, F32)

    def slc_scores(jj, carry):
        for gg in range(ngs):
            qs, _, sel_t, _ = fronts[gg]
            sps = []
            for r in range(SLC_UNROLL):
                j = SLC_UNROLL * jj + r
                dist = (i - j) * LANES + sub - lane
                expand = jnp.where(blk_row == 2 * j + blk_of_key, 1.0, 0.0).astype(BF16)
                picked = lax.dot_general(sel_t, expand, (((0,), (0,)), ((), ())), preferred_element_type=F32)
                sp = masked_scores(qs, gg, key_tile(slc_ref, gg, j), j, jnp.where(dist >= 0, picked, 0.0) > 0.5)
                for p in heads:
                    s_scr[j, gg * ATT_HPG + p] = sp[p]
                sps.append(sp)
            for p in heads:
                hp = gg * ATT_HPG + p
                m_scr[hp] = jnp.maximum(m_scr[hp], functools.reduce(jnp.maximum, [sp[p] for sp in sps]))
        return carry

    lax.fori_loop(0, n_steps, slc_scores, 0)
    for hp in range(ngs * ATT_HPG):
        m_scr[hp] = jnp.zeros(tile, F32) + jnp.max(m_scr[hp], axis=1, keepdims=True)
    l_scr[...] = jnp.zeros(l_scr.shape, F32)
    acc_scr[...] = jnp.zeros(acc_scr.shape, F32)

    def slc_weights(jj, carry):
        for gg in range(ngs):
            m = [m_scr[gg * ATT_HPG + p] for p in heads]
            pvs, ess = [], []
            for r in range(SLC_UNROLL):
                j = SLC_UNROLL * jj + r
                es, pv = weights_times_v([s_scr[j, gg * ATT_HPG + p] for p in heads], m, key_tile(slc_ref, gg, j))
                pvs.append(pv)
                ess.append(es)
            acc_scr[gg] += functools.reduce(jnp.add, pvs)
            for p in heads:
                hp = gg * ATT_HPG + p
                l_scr[hp] = l_scr[hp] + functools.reduce(jnp.add, [es[p] for es in ess])
        return carry

    lax.fori_loop(0, n_steps, slc_weights, 0)

    for gg in range(ngs):
        _, o_cmp, _, o_win = fronts[gg]
        g_hi, g_lo = _split2(gate_ref[:, glanes(gg)])
        spread = spread_ref[...]
        gb = jnp.dot(g_hi, spread, preferred_element_type=F32) + jnp.dot(g_lo, spread, preferred_element_type=F32)
        comb = []
        for p in heads:
            o_slc = head(acc_scr[gg], p) / jnp.sum(l_scr[gg * ATT_HPG + p], axis=1, keepdims=True)
            col = lambda br: gb[:, (br * ATT_HPG + p) * LANES:(br * ATT_HPG + p + 1) * LANES]
            comb.append(col(0) * head(o_cmp, p) + col(1) * o_slc + col(2) * o_win[p])
        for pair in range(ATT_HPG // 2):
            c0 = gg * ATT_HPG * HEAD_DIM + pair * LANES
            o_ref[:, c0:c0 + LANES] = jnp.where(low, pltpu.roll(comb[2 * pair], HEAD_DIM, axis=1), comb[2 * pair + 1])


def _nsa_prompt(q, gates, kvc, slc_pack, win_pack, bias_cmp, bias_toep, n_batch=BATCH):
    nqb = SEQ // Q_BLOCK
    ngs = NSA_GROUPS_PER_STEP
    ov = _overlap_padded(SEQ // CMP_STRIDE - 1, SEQ // SEL_BLOCK, LANES, LANES)
    gw = ngs * ATT_HPG * HEAD_DIM
    nh = ngs * ATT_HPG
    n_gate = N_BRANCH * ATT_HPG
    spread = np.zeros((LANES, n_gate, LANES), np.float32)
    spread[np.arange(n_gate), np.arange(n_gate), :] = 1.0
    spread = jnp.asarray(spread.reshape(LANES, n_gate * LANES), BF16)
    return pl.pallas_call(
        _nsa_prompt_multi_kernel,
        grid=(n_batch, ATT_GROUPS // ngs, nqb),
        in_specs=[
            pl.BlockSpec((Q_BLOCK, gw), lambda b, g, i: (b * nqb + i, g)),
            pl.BlockSpec((Q_BLOCK, ngs * LANES), lambda b, g, i: (b * nqb + i, g)),
            pl.BlockSpec((1, ngs, SEQ // CMP_STRIDE, LANES), lambda b, g, i: (0, g, b, 0)),
            pl.BlockSpec((SEQ, ngs * LANES), lambda b, g, i: (b, g)),
            pl.BlockSpec((SEQ, ngs * LANES), lambda b, g, i: (b, g)),
            pl.BlockSpec((nh, Q_BLOCK, LANES), lambda b, g, i: (g, i, 0)),
            pl.BlockSpec((nh, TOEP_TILES * LANES, LANES), lambda b, g, i: (g, 0, 0)),
            pl.BlockSpec((LANES, LANES), lambda b, g, i: (0, 0)),
            pl.BlockSpec((LANES, n_gate * LANES), lambda b, g, i: (0, 0)),
        ],
        out_specs=pl.BlockSpec((Q_BLOCK, gw), lambda b, g, i: (b * nqb + i, g)),
        out_shape=jax.ShapeDtypeStruct((n_batch * SEQ, ATT_WIDTH), F32),
        scratch_shapes=[pltpu.VMEM((nqb, nh, Q_BLOCK, LANES), F32),
                        pltpu.VMEM((nh, Q_BLOCK, LANES), F32), pltpu.VMEM((nh, Q_BLOCK, LANES), F32),
                        pltpu.VMEM((ngs, ATT_HPG * Q_BLOCK, LANES), F32)],
        compiler_params=_cparams("arbitrary", "arbitrary", "arbitrary"),
        name="nsa_prompt",
    )(q, gates, kvc, slc_pack, win_pack, bias_cmp, bias_toep, ov, spread)


SAMPLE_ROWS = ATT_HEADS * DEC_SEQ
SAMPLE_PAGES_PER_STEP = 32
SAMPLE_STEPS = N_PAGES // SAMPLE_PAGES_PER_STEP
SAMPLE_N_CMP = (PAST_LEN + DEC_SEQ - CMP_BLOCK) // CMP_STRIDE + 1
SAMPLE_N_BLK = -(-(PAST_LEN + DEC_SEQ) // SEL_BLOCK)
SAMPLE_BLK_LANES = 2 * LANES


def _nsa_sample_kernel(pt_ref, q_ref, gate_ref, kvc_ref, *rest):
    page_refs = rest[:SAMPLE_PAGES_PER_STEP]
    (slc_new_ref, win_state_ref, win_new_ref, bcmp_ref, bslc_ref, bwin_ref, perm_ref, ov_ref,
     o_ref, qbd_scr, sel_scr, m_scr, l_scr, acc_scr, ocmp_scr, owin_scr) = rest[SAMPLE_PAGES_PER_STEP:]
    del pt_ref
    u = pl.program_id(1)
    tile = (SAMPLE_ROWS, LANES)
    wide = (SAMPLE_ROWS, KV_WIDTH)
    lane = lax.broadcasted_iota(jnp.int32, tile, 1)
    t_row = lax.broadcasted_iota(jnp.int32, tile, 0) & (DEC_SEQ - 1)
    rows_per_group = ATT_HPG * DEC_SEQ
    own = ((lax.broadcasted_iota(jnp.int32, wide, 1) >> (HEAD_DIM.bit_length() - 1))
           == (lax.broadcasted_iota(jnp.int32, wide, 0) >> (rows_per_group.bit_length() - 1)))

    def reset():
        m_scr[...] = jnp.full(m_scr.shape, NEG, F32)
        l_scr[...] = jnp.zeros(l_scr.shape, F32)
        acc_scr[...] = jnp.zeros(acc_scr.shape, F32)

    def finish():
        l = jnp.sum(l_scr[...], axis=1, keepdims=True)
        return acc_scr[...] / jnp.where(l > 0, l, 1.0)

    def attend(tiles):
        qbd = qbd_scr[...]
        scores = []
        m_el = None
        for kt, _, bias, mask in tiles:
            s = jnp.where(mask, jnp.dot(qbd, kt.astype(BF16), preferred_element_type=F32) + bias, NEG)
            scores.append(s)
            m_el = s if m_el is None else jnp.maximum(m_el, s)
        m_prev = m_scr[...]
        m_new = jnp.maximum(m_prev, jnp.max(m_el, axis=1, keepdims=True))
        alpha = jnp.exp(m_prev - m_new)
        l_el = alpha * l_scr[...]
        acc = jnp.concatenate([alpha, alpha], axis=1) * acc_scr[...]
        for (_, vt, _, _), s in zip(tiles, scores):
            e = jnp.exp(s - m_new)
            l_el = l_el + e
            acc = acc + lax.dot_general(e.astype(BF16), vt.astype(BF16), NT_DIMS, preferred_element_type=F32)
        m_scr[...] = m_new
        l_scr[...] = l_el
        acc_scr[...] = acc

    def page_tile(kv4, bias, mask):
        return (kv4[0].reshape(KV_WIDTH, LANES), kv4[1].reshape(KV_WIDTH, LANES), bias, mask)

    def new_tile(ref, bias, mask):
        rows = jnp.concatenate([ref[...], jnp.zeros((LANES - DEC_SEQ, KV_ROW), F32)], axis=0)
        return (rows[:, :KV_WIDTH].T, rows[:, KV_WIDTH:].T, bias, mask)

    @pl.when(u == 0)
    def _():
        q = (q_ref[...] * (HEAD_DIM ** -0.5)).astype(BF16)
        qperm = jnp.dot(q, perm_ref[...], preferred_element_type=F32)
        qfull = jnp.concatenate([qperm[:, p * KV_WIDTH:(p + 1) * KV_WIDTH]
                                 for g in range(ATT_GROUPS) for p in range(ATT_HPG)], axis=0)
        qbd = jnp.where(own, qfull, 0.0).astype(BF16)
        qbd_scr[...] = qbd

        kvc = kvc_ref[0]
        s = lax.dot_general(qbd, kvc[:, :KV_WIDTH], NT_DIMS, preferred_element_type=F32) + bcmp_ref[...]
        cmask = lax.broadcasted_iota(jnp.int32, s.shape, 1) < SAMPLE_N_CMP
        pr = _softmax_rows(s, cmask)
        ocmp_scr[...] = jnp.dot(pr.astype(BF16), kvc[:, KV_WIDTH:], preferred_element_type=F32)
        ps = []
        for g in range(ATT_GROUPS):
            r0 = g * rows_per_group
            ps.append(sum(pr[r0 + p * DEC_SEQ:r0 + (p + 1) * DEC_SEQ, :] for p in range(ATT_HPG)))
        hi, lo = _split2(jnp.concatenate(ps, axis=0))
        ov = ov_ref[...]
        imp = jnp.dot(hi, ov, preferred_element_type=F32) + jnp.dot(lo, ov, preferred_element_type=F32)
        qpos = PAST_LEN + (lax.broadcasted_iota(jnp.int32, imp.shape, 0) & (DEC_SEQ - 1))
        sel = _select_blocks(imp, qpos, SAMPLE_N_BLK, axis=1)
        sel_scr[...] = jnp.concatenate([sel[g * DEC_SEQ:(g + 1) * DEC_SEQ, :]
                                        for g in range(ATT_GROUPS) for p in range(ATT_HPG)], axis=0).astype(BF16)

        reset()
        tiles = [page_tile(win_state_ref[0, :, :, :, w * LANES:(w + 1) * LANES],
                           bwin_ref[:, w * LANES:(w + 1) * LANES], (w * LANES + lane) > t_row)
                 for w in range(WINDOW // LANES)]
        tiles.append(new_tile(win_new_ref, bwin_ref[:, WINDOW:WINDOW + LANES], lane <= t_row))
        attend(tiles)
        owin_scr[...] = finish()
        reset()

    blk_of_key = lax.broadcasted_iota(jnp.int32, (SAMPLE_BLK_LANES, LANES), 1) >> SEL_SHIFT
    blk_row = lax.broadcasted_iota(jnp.int32, (SAMPLE_BLK_LANES, LANES), 0)
    sel = sel_scr[...]
    tiles = []
    for k in range(SAMPLE_PAGES_PER_STEP):
        pg = u * SAMPLE_PAGES_PER_STEP + k
        expand = jnp.where(blk_row == 2 * pg + blk_of_key, 1.0, 0.0).astype(BF16)
        picked = jnp.dot(sel, expand, preferred_element_type=F32) > 0.5
        bias = bslc_ref[:, pl.ds(pl.multiple_of(pg * LANES, LANES), LANES)]
        tiles.append(page_tile(page_refs[k][0], bias, picked))
    attend(tiles)

    @pl.when(u == SAMPLE_STEPS - 1)
    def _():
        attend([new_tile(slc_new_ref, bslc_ref[:, PAST_LEN:PAST_LEN + LANES], lane <= t_row)])
        o_slc = finish()
        gt = gate_ref[...]

        def gate_rows(br):
            cols = []
            for g in range(ATT_GROUPS):
                for p in range(ATT_HPG):
                    c = g * LANES + br * ATT_HPG + p
                    cols.append(jnp.broadcast_to(gt[:, c:c + 1], (DEC_SEQ, KV_WIDTH)))
            return jnp.concatenate(cols, axis=0)

        comb = gate_rows(0) * ocmp_scr[...] + gate_rows(1) * o_slc + gate_rows(2) * owin_scr[...]
        comb = jnp.where(own, comb, 0.0)
        per_head = []
        for p in range(ATT_HPG):
            per_head.append(sum(comb[(g * ATT_HPG + p) * DEC_SEQ:(g * ATT_HPG + p + 1) * DEC_SEQ, :]
                                for g in range(ATT_GROUPS)))
        hi, lo = _split2(jnp.concatenate(per_head, axis=1))
        perm = perm_ref[...]
        o_ref[...] = (lax.dot_general(hi, perm, NT_DIMS, preferred_element_type=F32)
                      + lax.dot_general(lo, perm, NT_DIMS, preferred_element_type=F32))


def _nsa_sample(q, gates, kvc, cache_slc, page_table, slc_rows, state_win, win_rows, bias_cmp, bias_slc, bias_win):
    n_seq = page_table.shape[0]
    row0 = N_PROMPT // DEC_SEQ
    new = lambda w: pl.BlockSpec((DEC_SEQ, w), lambda b, u, pt: (b, 0))
    src = np.arange(ATT_WIDTH)
    g, p, d = src // (ATT_HPG * HEAD_DIM), (src // HEAD_DIM) % ATT_HPG, src % HEAD_DIM
    perm = np.zeros((ATT_WIDTH, ATT_WIDTH), np.float32)
    perm[src, p * KV_WIDTH + g * HEAD_DIM + d] = 1.0
    ov = _overlap_padded(SAMPLE_N_CMP, SAMPLE_N_BLK, PAST_LEN // CMP_STRIDE, SAMPLE_BLK_LANES)
    tok = lambda w: pl.BlockSpec((DEC_SEQ, w), lambda b, u, pt: (row0 + b, 0))
    const = lambda a: pl.BlockSpec(a.shape, lambda b, u, pt, nd=a.ndim: (0,) * nd)
    per_seq = lambda s: pl.BlockSpec((1,) + s, lambda b, u, pt, nd=len(s): (b,) + (0,) * nd)
    pages = [pl.BlockSpec((1, 2, ATT_GROUPS, HEAD_DIM, PAGE_SIZE),
                          lambda b, u, pt, k=k: (pt[b, u * SAMPLE_PAGES_PER_STEP + k], 0, 0, 0, 0))
             for k in range(SAMPLE_PAGES_PER_STEP)]
    consts = (bias_cmp, bias_slc, bias_win, jnp.asarray(perm, BF16), ov)
    grid_spec = pltpu.PrefetchScalarGridSpec(
        num_scalar_prefetch=1, grid=(n_seq, SAMPLE_STEPS),
        in_specs=[tok(ATT_WIDTH), tok(ATT_GROUPS * LANES), per_seq((PAST_LEN // CMP_STRIDE, KV_ROW))] + pages
        + [new(KV_ROW), per_seq((2, ATT_GROUPS, HEAD_DIM, WINDOW)), new(KV_ROW)] + [const(a) for a in consts],
        out_specs=pl.BlockSpec((DEC_SEQ, ATT_WIDTH), lambda b, u, pt: (b, 0)),
        scratch_shapes=[pltpu.VMEM((SAMPLE_ROWS, KV_WIDTH), BF16), pltpu.VMEM((SAMPLE_ROWS, SAMPLE_BLK_LANES), BF16),
                        pltpu.VMEM((SAMPLE_ROWS, LANES), F32), pltpu.VMEM((SAMPLE_ROWS, LANES), F32),
                        pltpu.VMEM((SAMPLE_ROWS, KV_WIDTH), F32), pltpu.VMEM((SAMPLE_ROWS, KV_WIDTH), F32),
                        pltpu.VMEM((SAMPLE_ROWS, KV_WIDTH), F32)])
    return pl.pallas_call(
        _nsa_sample_kernel, grid_spec=grid_spec,
        out_shape=jax.ShapeDtypeStruct((n_seq * DEC_SEQ, ATT_WIDTH), F32),
        compiler_params=_cparams("arbitrary", "arbitrary"),
        name="nsa_sample",
    )(page_table, q, gates, kvc, *([cache_slc] * SAMPLE_PAGES_PER_STEP), slc_rows, state_win, win_rows, *consts)


def kernel(x_prompt, x_sample, cache_cmp_kv, cache_slc_kv, state_win_kv, state_hgrn, page_table,
           rel_bias_table, hgrn_lower_bound, norm_ffn1, w_ffn1_gate_up, w_ffn1_down, norm_mix,
           w_in, q_norm, k_norm, w_cmp1, b_cmp1, w_cmp2, b_cmp2, attn_out_norm, hgrn_out_norm,
           w_out, norm_ffn2, w_ffn2_gate_up, w_ffn2_down):
    assert DEPTH == 1
    l = 0
    kv_shape = (2, ATT_GROUPS, HEAD_DIM)

    y1 = _ffn((x_prompt.reshape(N_PROMPT, D_MODEL), x_sample.reshape(N_SAMPLE, D_MODEL)), norm_ffn1[l],
              w_ffn1_gate_up[l].astype(BF16), w_ffn1_down[l].astype(BF16))

    (q, cmp_rows, slc_rows, slc_pack, win_rows, win_pack, hq, lf, hk, hv, hog, gates,
     cmp_new, slc_new, win_new) = _project_all(
        y1, norm_mix[l], _permute_w_in(w_in[l]), q_norm[l], k_norm[l], hgrn_lower_bound)

    tbl = rel_bias_table.astype(F32)
    first_end = CMP_BLOCK - 1
    bias_cmp_p = _bias_table(tbl, SEQ, LANES, -first_end, 1, -CMP_STRIDE)
    bias_toep = _bias_table(tbl, TOEP_TILES * LANES, LANES, 0, 1, -1)
    bias_cmp_s = _bias_table(tbl, DEC_SEQ, PAST_LEN // CMP_STRIDE, PAST_LEN - first_end, 1, -CMP_STRIDE)
    bias_slc_s = _bias_table(tbl, DEC_SEQ, PAST_LEN + LANES, PAST_LEN, 1, -1)
    bias_win_s = _bias_table(tbl, DEC_SEQ, WINDOW + LANES, WINDOW, 1, -1)
    rows_ht = lambda a: a.reshape(SAMPLE_ROWS, a.shape[-1])

    cw = _compress_weights(w_cmp1[l], b_cmp1[l], w_cmp2[l], b_cmp2[l])
    kvc_p = _compress(cmp_rows, cw, k_norm[l][0])
    kvc_s = _compress_paged(_cache_view(cache_cmp_kv[l]), page_table, w_cmp1[l], b_cmp1[l], w_cmp2[l], b_cmp2[l],
                            k_norm[l][0])

    o_att_p = _nsa_prompt(q, gates, kvc_p, slc_pack, win_pack, bias_cmp_p, bias_toep)
    o_att_s = _nsa_sample(q, gates, kvc_s, _cache_view(cache_slc_kv[l]), page_table, slc_new,
                          _cache_view(state_win_kv[l]), win_new,
                          rows_ht(bias_cmp_s), rows_ht(bias_slc_s), rows_ht(bias_win_s))

    o_hg_p, hg_p = _hgrn(hq, hk, hv, lf, None, n_seq=BATCH, t_len=SEQ, ch=HG_CHUNK_PROMPT, row0=0)
    o_hg_s, hg_s = _hgrn(hq, hk, hv, lf, state_hgrn[l].astype(F32), n_seq=DEC_BATCH, t_len=DEC_SEQ,
                         ch=HG_CHUNK_SAMPLE, row0=N_PROMPT)

    y2 = _mixer_out(y1, o_att_p, o_att_s, o_hg_p, o_hg_s, hog, attn_out_norm[l], hgrn_out_norm[l],
                    w_out[l].astype(BF16))
    y3_p, y3_s = _ffn(y2, norm_ffn2[l], w_ffn2_gate_up[l].astype(BF16), w_ffn2_down[l].astype(BF16), split_out=True)

    prompt_rows = lambda a: a.reshape((1, BATCH, SEQ) + kv_shape)
    sample_rows = lambda a: a.reshape((1, DEC_BATCH, DEC_SEQ) + kv_shape)
    win_p = prompt_rows(win_rows)[:, :, SEQ - min(WINDOW, SEQ):]
    win_s = jnp.concatenate([state_win_kv[l][:, DEC_SEQ:], sample_rows(win_new)[0]], axis=1)[None]
    return (y3_p.reshape(BATCH, SEQ, D_MODEL), y3_s.reshape(DEC_BATCH, DEC_SEQ, D_MODEL),
            prompt_rows(cmp_rows), prompt_rows(slc_rows), win_p, hg_p[None],
            sample_rows(cmp_new), sample_rows(slc_new), win_s, hg_s[None].astype(state_hgrn.dtype))
```

```python
import functools
import math

import jax
import jax.numpy as jnp
import numpy as np
from jax import lax
from jax.experimental import pallas as pl
from jax.experimental.pallas import tpu as pltpu

D_MODEL = 2048
BATCH = 4
SEQ = 2048
DEPTH = 1
DEC_BATCH = 32
DEC_SEQ = 8
PAST_LEN = 8192
PAGE_SIZE = 128
HEAD_DIM = 64
ATT_HEADS = (D_MODEL // 2) // HEAD_DIM
ATT_GROUPS = ATT_HEADS // 4
ATT_HPG = ATT_HEADS // ATT_GROUPS
ATT_WIDTH = ATT_HEADS * HEAD_DIM
KV_WIDTH = ATT_GROUPS * HEAD_DIM
N_BRANCH = 3
CMP_BLOCK = 32
CMP_STRIDE = 16
CMP_RATIO = CMP_BLOCK // CMP_STRIDE
CMP_HIDDEN = 256
SEL_BLOCK = 64
N_SEL = 8
WINDOW = 512
Q_BLOCK = 128
HG_KEY = 128
HG_VAL = 128
HG_HEADS = (D_MODEL // 2) // HG_VAL
HG_WIDTH = HG_HEADS * HG_VAL
MIX_WIDTH = ATT_WIDTH + HG_WIDTH
D_FF = 256 * ((8 * D_MODEL // 3 + 255) // 256)
N_BUCKETS = 32
MAX_DISTANCE = 1024
EPS = 1e-6
IN_SPLITS = (ATT_WIDTH, 2 * N_BRANCH * KV_WIDTH, N_BRANCH * ATT_HEADS,
             HG_HEADS * HG_KEY, HG_HEADS * HG_KEY, HG_WIDTH, HG_WIDTH)
D_IN = sum(IN_SPLITS)

N_PROMPT = BATCH * SEQ
N_SAMPLE = DEC_BATCH * DEC_SEQ
N_TOKENS = N_PROMPT + N_SAMPLE
N_PAGES = PAST_LEN // PAGE_SIZE
KV_ROW = 2 * KV_WIDTH

LANES = 128
V7X_VMEM_BYTES = 64 * 1024 * 1024
VMEM_LIMIT_BYTES = 56 * 1024 * 1024

NEG = -1e30
F32 = jnp.float32
BF16 = jnp.bfloat16

FFN_ROW_TILE = 768
FFN_FF_TILE = 512
PROJ_ROW_TILE = 528
PROJ_COL_TILE = 512
OUT_ROW_TILE = 256
HG_CHUNK_PROMPT = 16
HG_CHUNK_SAMPLE = DEC_SEQ
CMP_UNIT_ROWS = 2048
CMP_UNITS = 4


def _cparams(*sem):
    return pltpu.CompilerParams(dimension_semantics=sem, vmem_limit_bytes=VMEM_LIMIT_BYTES)


def _t5_thresholds():
    n = np.arange(0, 2 * MAX_DISTANCE + 2)
    exact = N_BUCKETS // 2
    logn = np.log(np.maximum(n, 1).astype(np.float64) / exact)
    large = exact + (logn / math.log(MAX_DISTANCE / exact) * (N_BUCKETS - exact)).astype(np.int32)
    b = np.where(n < exact, n, np.minimum(large, N_BUCKETS - 1))
    return [int(n[b >= k][0]) for k in range(N_BUCKETS)]


T5_THRESHOLDS = _t5_thresholds()


def _block_diag_ones(n, blk):
    i = np.arange(n)
    return (i[:, None] // blk == i[None, :] // blk).astype(np.float32)


def _split2(x):
    hi = x.astype(BF16)
    lo = (x - hi.astype(F32)).astype(BF16)
    return hi, lo


def _group_rms(t, bd, gain, width):
    hi, lo = _split2(t * t)
    ss = (jnp.dot(hi, bd, preferred_element_type=F32) + jnp.dot(lo, bd, preferred_element_type=F32))
    return t * lax.rsqrt(ss * (1.0 / width) + EPS) * gain


def _ffn_kernel(*refs, split_in, split_out):
    n_in = 2 if split_in else 1
    x_ref = refs[0]
    gain_ref, wg_ref, wu_ref, wd_ref = refs[n_in:n_in + 4]
    o_ref = refs[n_in + 4]
    xn_ref, acc_ref = refs[-2:]
    j = pl.program_id(1)
    last_tile = pl.program_id(0) == pl.num_programs(0) - 1
    tm = x_ref.shape[0]

    def rows_in():
        x = x_ref[...]
        if split_in:
            xs = refs[1][...]
            x = jnp.where(last_tile, jnp.concatenate([x[:tm - xs.shape[0]], xs], axis=0), x)
        return x

    @pl.when(j == 0)
    def _():
        x = rows_in()
        y = x * lax.rsqrt(jnp.mean(x * x, axis=-1, keepdims=True) + EPS)
        xn_ref[...] = (y * gain_ref[...]).astype(BF16)
        acc_ref[...] = jnp.zeros_like(acc_ref)

    xn = xn_ref[...]
    g = jnp.dot(xn, wg_ref[...], preferred_element_type=F32)
    u = jnp.dot(xn, wu_ref[...], preferred_element_type=F32)
    a = (g * jax.nn.sigmoid(g) * u).astype(BF16)
    acc_ref[...] += jnp.dot(a, wd_ref[...], preferred_element_type=F32)

    @pl.when(j == pl.num_programs(1) - 1)
    def _():
        res = rows_in() + 0.5 * acc_ref[...]
        o_ref[...] = res
        if split_out:
            os_ref = refs[n_in + 5]

            @pl.when(last_tile)
            def _():
                os_ref[...] = res[tm - os_ref.shape[0]:]


def _ffn(x, gain, w_gu, w_down, split_out=False):
    split_in = isinstance(x, tuple)
    tm, tf = FFN_ROW_TILE, FFN_FF_TILE
    n_p, n_s = N_PROMPT, N_SAMPLE
    n, d = n_p + n_s, D_MODEL
    assert n % tm == 0 and D_FF % tf == 0 and n_p % tm == tm - n_s
    nj = D_FF // tf
    row = pl.BlockSpec((tm, d), lambda i, j: (i, 0))
    tail = pl.BlockSpec((n_s, d), lambda i, j: (0, 0))
    xs = x if split_in else (x,)
    return pl.pallas_call(
        functools.partial(_ffn_kernel, split_in=split_in, split_out=split_out),
        grid=(n // tm, nj),
        in_specs=([row, tail] if split_in else [row]) + [
            pl.BlockSpec((1, d), lambda i, j: (0, 0)),
            pl.BlockSpec((d, tf), lambda i, j: (0, j)),
            pl.BlockSpec((d, tf), lambda i, j: (0, j + nj)),
            pl.BlockSpec((tf, d), lambda i, j: (j, 0)),
        ],
        out_specs=(row, tail) if split_out else row,
        out_shape=((jax.ShapeDtypeStruct((n_p, d), F32), jax.ShapeDtypeStruct((n_s, d), F32)) if split_out
                   else jax.ShapeDtypeStruct((n, d), F32)),
        scratch_shapes=[pltpu.VMEM((tm, d), BF16), pltpu.VMEM((tm, d), F32)],
        compiler_params=_cparams("arbitrary", "arbitrary"),
        name="ffn",
    )(*xs, gain.reshape(1, d), w_gu, w_gu, w_down)


PB_Q, PB_CMP, PB_SLC, PB_WIN, PB_HQ, PB_HF, PB_HI, PB_HG, PB_GATE = 0, 2, 3, 4, 5, 7, 9, 11, 13
PROJ_N_BLOCKS = 14


def _permute_w_in(w_in):
    p = [int(v) for v in np.cumsum(IN_SPLITS)]
    a_g = w_in[:, p[1]:p[2]]
    src = np.zeros((PROJ_COL_TILE,), np.int32)
    valid = np.zeros((PROJ_COL_TILE,), bool)
    for g in range(ATT_GROUPS):
        for br in range(N_BRANCH):
            for hp in range(ATT_HPG):
                src[g * LANES + br * ATT_HPG + hp] = br * ATT_HEADS + g * ATT_HPG + hp
                valid[g * LANES + br * ATT_HPG + hp] = True
    gate = jnp.where(jnp.asarray(valid)[None, :], a_g[:, src], 0.0)
    return jnp.concatenate([w_in[:, :p[1]], w_in[:, p[2]:], gate], axis=1).astype(BF16)


def _pack_kv(k, v):
    parts = []
    for g in range(ATT_GROUPS):
        parts.append(k[:, g * HEAD_DIM:(g + 1) * HEAD_DIM])
        parts.append(v[:, g * HEAD_DIM:(g + 1) * HEAD_DIM])
    return jnp.concatenate(parts, axis=1).astype(BF16)


def _proj_kernel(x_ref, gain_ref, w_ref, bd_ref, qg_ref, kg_ref, lbp_ref,
                 q_ref, cmp_ref, slc_ref, slcp_ref, win_ref, winp_ref,
                 hq_ref, lf_ref, hk_ref, hv_ref, hog_ref, gate_ref, cmps_ref, slcs_ref, wins_ref, xn_ref):
    c = pl.program_id(1)
    last_tile = pl.program_id(0) == pl.num_programs(0) - 1
    tail = x_ref.shape[0] - cmps_ref.shape[0]

    @pl.when(c == 0)
    def _():
        x = x_ref[...]
        y = x * lax.rsqrt(jnp.mean(x * x, axis=-1, keepdims=True) + EPS)
        xn_ref[...] = (y * gain_ref[...]).astype(BF16)

    acc = jnp.dot(xn_ref[...], w_ref[...], preferred_element_type=F32)
    half = KV_WIDTH

    @pl.when(c < PB_CMP)
    def _():
        q_ref[...] = _group_rms(acc, bd_ref[...], qg_ref[...], HEAD_DIM)

    @pl.when(c == PB_CMP)
    def _():
        cmp_ref[...] = acc

        @pl.when(last_tile)
        def _():
            cmps_ref[...] = acc[tail:]

    def kv_branch(rows_ref, sample_ref, pack_ref, br):
        k = _group_rms(acc[:, :half], bd_ref[:half, :half], kg_ref[br - 1:br, :], HEAD_DIM)
        v = acc[:, half:]
        rows_ref[:, :half] = k
        rows_ref[:, half:] = v
        pack_ref[...] = _pack_kv(k, v)

        @pl.when(last_tile)
        def _():
            sample_ref[:, :half] = k[tail:]
            sample_ref[:, half:] = v[tail:]

    @pl.when(c == PB_SLC)
    def _():
        kv_branch(slc_ref, slcs_ref, slcp_ref, 1)

    @pl.when(c == PB_WIN)
    def _():
        kv_branch(win_ref, wins_ref, winp_ref, 2)

    @pl.when((c >= PB_HQ) & (c < PB_HF))
    def _():
        hq_ref[...] = acc

    @pl.when((c >= PB_HF) & (c < PB_HI))
    def _():
        p = lbp_ref[...]
        e = jnp.exp(p - jnp.max(p, axis=0, keepdims=True))
        lb = e[0:1, :] / jnp.sum(e, axis=0, keepdims=True)
        lf_ref[...] = jnp.log(lb + (1.0 - lb) * jax.nn.sigmoid(acc))
        hk_ref[...] = (1.0 - lb) * jax.nn.sigmoid(-acc)

    @pl.when((c >= PB_HI) & (c < PB_HG))
    def _():
        hv_ref[...] = acc

    @pl.when((c >= PB_HG) & (c < PB_GATE))
    def _():
        hog_ref[...] = acc

    @pl.when(c == PB_GATE)
    def _():
        gate_ref[...] = jax.nn.sigmoid(acc)


def _project_all(y, gain, w_perm, q_gain, k_gain, lb_logits):
    n, d = y.shape
    tm, tc = PROJ_ROW_TILE, PROJ_COL_TILE
    assert n == N_TOKENS and n % tm == 0 and N_PROMPT % tm == tm - N_SAMPLE and DEPTH == 1
    bd = jnp.asarray(_block_diag_ones(tc, HEAD_DIM), BF16)

    def two(first):
        return lambda i, c: (i, jnp.clip(c - first, 0, 1))

    one = lambda i, c: (i, 0)
    wide = lambda dt: jax.ShapeDtypeStruct((n, 2 * tc), dt)
    narrow = lambda dt: jax.ShapeDtypeStruct((n, tc), dt)
    prompt = jax.ShapeDtypeStruct((N_PROMPT, tc), F32)
    sample = jax.ShapeDtypeStruct((N_SAMPLE, tc), F32)
    out_shape = (wide(F32), prompt, prompt, narrow(BF16), prompt, narrow(BF16),
                 wide(F32), wide(F32), wide(F32), wide(F32), wide(F32), narrow(F32), sample, sample, sample)
    blk = lambda f: pl.BlockSpec((tm, tc), f)
    tail = pl.BlockSpec((N_SAMPLE, tc), lambda i, c: (0, 0))
    out_specs = (blk(two(PB_Q)), blk(one), blk(one), blk(one), blk(one), blk(one),
                 blk(two(PB_HQ)), blk(two(PB_HF)), blk(two(PB_HF)), blk(two(PB_HI)), blk(two(PB_HG)), blk(one),
                 tail, tail, tail)
    return pl.pallas_call(
        _proj_kernel,
        grid=(n // tm, PROJ_N_BLOCKS),
        in_specs=[
            pl.BlockSpec((tm, d), lambda i, c: (i, 0)),
            pl.BlockSpec((1, d), lambda i, c: (0, 0)),
            pl.BlockSpec((d, tc), lambda i, c: (0, c)),
            pl.BlockSpec((tc, tc), lambda i, c: (0, 0)),
            pl.BlockSpec((1, tc), lambda i, c: (0, jnp.clip(c, 0, 1))),
            pl.BlockSpec((2, KV_WIDTH), lambda i, c: (0, 0)),
            pl.BlockSpec((DEPTH + 1, tc), lambda i, c: (0, jnp.clip(c - PB_HF, 0, 1))),
        ],
        out_specs=out_specs,
        out_shape=out_shape,
        scratch_shapes=[pltpu.VMEM((tm, d), BF16)],
        compiler_params=_cparams("arbitrary", "arbitrary"),
        name="proj",
    )(y, gain.reshape(1, d), w_perm, bd, q_gain.reshape(1, ATT_WIDTH),
      k_gain[1:].reshape(2, KV_WIDTH), lb_logits)


def _hgrn_kernel(q_ref, k_ref, v_ref, lf_ref, tri_ref, ones_ref, s0_ref, o_ref, st_ref,
                 qe_scr, kd_scr, vt_scr, dec_scr, oi_scr, b_scr, *, t_len, ch, has_state):
    nj = t_len // ch
    shape3 = (nj, ch, HG_KEY)
    q3, k3, v3, lf3 = q_ref[...], k_ref[...], v_ref[...], lf_ref[...]
    tl = lax.broadcasted_iota(jnp.int32, shape3, 1)

    def row(x3, s):
        return jnp.broadcast_to(x3[:, s:s + 1, :], shape3)

    if t_len >= 256:
        lf2 = lf3.reshape(t_len, HG_KEY)
        parts = []
        for r0 in range(0, t_len, 256):
            x = lf2[r0:r0 + 256]
            hi = x.astype(BF16)
            r1 = x - hi.astype(F32)
            mid = r1.astype(BF16)
            lo = (r1 - mid.astype(F32)).astype(BF16)
            tri = tri_ref[...]
            parts.append(jnp.dot(tri, hi, preferred_element_type=F32)
                         + jnp.dot(tri, mid, preferred_element_type=F32)
                         + jnp.dot(tri, lo, preferred_element_type=F32))
        b3 = jnp.concatenate(parts, axis=0).reshape(shape3)
    else:
        b3 = jnp.zeros(shape3, F32)
        for s in range(ch):
            b3 = b3 + jnp.where(tl >= s, row(lf3, s), 0.0)

    bl3 = row(b3, ch - 1)
    qe3 = q3 * jnp.exp(b3)
    kd3 = k3 * jnp.exp(bl3 - b3)
    dec_scr[...] = jnp.exp(b3[:, ch - 1:ch, :])
    blocked = t_len % LANES == 0
    if blocked:
        nb = t_len // LANES
        qe2, kd2, v2 = (x.reshape(t_len, HG_KEY) for x in (qe3, kd3, v3))
        for m in range(nb):
            blk = slice(m * LANES, (m + 1) * LANES)
            qe_scr[m] = qe2[blk].T.astype(BF16)
            vt_scr[m] = v2[blk].T.astype(BF16)
            kd_scr[m] = kd2[blk].astype(BF16)
    else:
        qe_scr[...] = qe3.astype(BF16)
        kd_scr[...] = kd3.astype(BF16)

    ones = ones_ref[...]
    n8 = ch // 8
    shape8 = (nj, 8, HG_KEY)
    sub8 = lax.broadcasted_iota(jnp.int32, shape8, 1)
    q5, b5 = (x.reshape(nj, n8, 8, HG_KEY) for x in (q3, b3))
    b_scr[...] = b3
    od = [jnp.zeros(shape8, F32) for _ in range(n8)]
    for s in range(ch):
        hs, ss = divmod(s, 8)
        ks, bs, vs = (jnp.broadcast_to(r[:, s:s + 1, :], shape8) for r in (k_ref, b_scr, v_ref))
        for hh in range(hs, n8):
            diff = b5[:, hh] - bs
            if hh == hs:
                diff = jnp.where(sub8 >= ss, diff, NEG)
            w = q5[:, hh] * ks * jnp.exp(diff)
            a = jnp.dot(w.reshape(nj * 8, HG_KEY).astype(BF16), ones, preferred_element_type=F32)
            od[hh] = od[hh] + a.reshape(shape8) * vs
    for hh in range(n8):
        o_ref[:, hh * 8:(hh + 1) * 8, :] = od[hh]

    if has_state:
        st0 = s0_ref[0, 0].T
    else:
        st0 = jnp.zeros((HG_VAL, HG_KEY), F32)

    if blocked:
        cpb = LANES // ch
        lane_chunk = lax.broadcasted_iota(jnp.int32, (HG_KEY, LANES), 1) >> (ch.bit_length() - 1)
        keep = [jnp.where(lane_chunk == r, 1.0, 0.0).astype(BF16) for r in range(cpb)]

        def body(m, st):
            qet, vt, kd = qe_scr[m], vt_scr[m], kd_scr[m]
            ot = jnp.zeros((HG_VAL, LANES), F32)
            for r in range(cpb):
                ot = ot + jnp.dot(st.astype(BF16), qet * keep[r], preferred_element_type=F32)
                ut = jnp.dot(vt * keep[r], kd, preferred_element_type=F32)
                st = st * dec_scr[m * cpb + r] + ut
            oi_scr[m] = ot
            return st

        st = lax.fori_loop(0, t_len // LANES, body, st0)
        for m in range(t_len // LANES):
            o_ref[m * cpb:(m + 1) * cpb] = o_ref[m * cpb:(m + 1) * cpb] + oi_scr[m].T.reshape(cpb, ch, HG_VAL)
    else:
        def body(j, st):
            oi_scr[j] = lax.dot_general(qe_scr[j], st.astype(BF16), NT_DIMS, preferred_element_type=F32)
            ut = lax.dot_general(v_ref[j].astype(BF16), kd_scr[j], (((0,), (0,)), ((), ())),
                                 preferred_element_type=F32)
            return st * dec_scr[j] + ut

        st = lax.fori_loop(0, nj, body, st0, unroll=True)
        o_ref[...] = o_ref[...] + oi_scr[...]
    st_ref[0, 0] = st.T


def _hgrn(hq, hk, hv, lf, s0, *, n_seq, t_len, ch, row0):
    n = hq.shape[0]
    nj = t_len // ch
    assert row0 % t_len == 0 and t_len % ch == 0 and ch % 8 == 0
    blk0 = row0 // t_len
    r3 = lambda a: a.reshape(n // ch, ch, HG_WIDTH)
    has_state = s0 is not None
    if not has_state:
        s0 = jnp.zeros((1, 1, HG_KEY, HG_VAL), F32)
    tri_n = 256 if t_len >= 256 else 8
    i = np.arange(tri_n)
    tri = jnp.asarray(((i[:, None] // ch == i[None, :] // ch) & (i[:, None] >= i[None, :])).astype(np.float32), BF16)
    ones = jnp.ones((HG_KEY, HG_KEY), BF16)
    blk_shape = (t_len // LANES, LANES, HG_KEY) if t_len % LANES == 0 else (nj, ch, HG_KEY)
    hps = HG_HEADS if nj == 1 else 1
    seq = pl.BlockSpec((nj, ch, hps * HG_KEY), lambda b, h: (blk0 + b, 0, h))
    state_in = pl.BlockSpec((1, hps, HG_KEY, HG_VAL), (lambda b, h: (b, h, 0, 0)) if has_state else (lambda b, h: (0, 0, 0, 0)))

    def body(q_ref, k_ref, v_ref, lf_ref, tri_ref, ones_ref, s0_ref, o_ref, st_ref, *scratch):
        for h in range(hps):
            ln = slice(h * HG_KEY, (h + 1) * HG_KEY)
            _hgrn_kernel(q_ref.at[:, :, ln], k_ref.at[:, :, ln], v_ref.at[:, :, ln], lf_ref.at[:, :, ln], tri_ref,
                         ones_ref, s0_ref.at[:, h:h + 1], o_ref.at[:, :, ln], st_ref.at[:, h:h + 1], *scratch,
                         t_len=t_len, ch=ch, has_state=has_state)

    o, st = pl.pallas_call(
        body,
        grid=(n_seq, HG_HEADS // hps),
        in_specs=[seq, seq, seq, seq,
                  pl.BlockSpec((tri_n, tri_n), lambda b, h: (0, 0)),
                  pl.BlockSpec((HG_KEY, HG_KEY), lambda b, h: (0, 0)),
                  state_in],
        out_specs=(pl.BlockSpec((nj, ch, hps * HG_VAL), lambda b, h: (b, 0, h)),
                   pl.BlockSpec((1, hps, HG_KEY, HG_VAL), lambda b, h: (b, h, 0, 0))),
        out_shape=(jax.ShapeDtypeStruct((n_seq * nj, ch, HG_WIDTH), F32),
                   jax.ShapeDtypeStruct((n_seq, HG_HEADS, HG_KEY, HG_VAL), F32)),
        scratch_shapes=[pltpu.VMEM(blk_shape, BF16), pltpu.VMEM(blk_shape, BF16), pltpu.VMEM(blk_shape, BF16),
                        pltpu.VMEM((nj, 1, HG_KEY), F32), pltpu.VMEM(blk_shape, F32),
                        pltpu.VMEM((nj, ch, HG_KEY), F32)],
        compiler_params=_cparams("arbitrary", "arbitrary"),
        name="hgrn",
    )(r3(hq), r3(hk), r3(hv), r3(lf), tri, ones, s0)
    return o.reshape(n_seq * t_len, HG_WIDTH), st


def _mixout_kernel(y_ref, oap_ref, oas_ref, ohp_ref, ohs_ref, og_ref, ag_ref, hgain_ref, w_ref, o_ref):
    is_sample = pl.program_id(0) == pl.num_programs(0) - 1
    oa = jnp.where(is_sample, oas_ref[...], oap_ref[...])
    a = oa * lax.rsqrt(jnp.mean(oa * oa, axis=-1, keepdims=True) + EPS) * ag_ref[...]
    oh = jnp.where(is_sample, ohs_ref[...], ohp_ref[...])
    hs = []
    for h in range(HG_HEADS):
        x = oh[:, h * HG_VAL:(h + 1) * HG_VAL]
        hs.append(x * lax.rsqrt(jnp.mean(x * x, axis=-1, keepdims=True) + EPS))
    og = og_ref[...]
    hh = jnp.concatenate(hs, axis=1) * hgain_ref[...] * (og * jax.nn.sigmoid(og))
    m = (jnp.dot(a.astype(BF16), w_ref[:ATT_WIDTH, :], preferred_element_type=F32)
         + jnp.dot(hh.astype(BF16), w_ref[ATT_WIDTH:, :], preferred_element_type=F32))
    o_ref[...] = y_ref[...] + m


def _mixer_out(y, o_att_p, o_att_s, o_hg_p, o_hg_s, og, attn_gain, hg_gain, w_out):
    n, d = y.shape
    tm = OUT_ROW_TILE
    n_p = o_att_p.shape[0]
    assert n % tm == 0 and n_p % tm == 0 and o_att_s.shape[0] == tm and n == n_p + tm
    row = lambda w: pl.BlockSpec((tm, w), lambda i: (i, 0))
    prompt_row = lambda w: pl.BlockSpec((tm, w), lambda i: (jnp.minimum(i, n_p // tm - 1), 0))
    const = lambda s: pl.BlockSpec(s, lambda i: (0, 0))
    return pl.pallas_call(
        _mixout_kernel,
        grid=(n // tm,),
        in_specs=[row(d), prompt_row(ATT_WIDTH), const((tm, ATT_WIDTH)), prompt_row(HG_WIDTH), const((tm, HG_WIDTH)),
                  row(HG_WIDTH), const((1, ATT_WIDTH)), const((1, HG_WIDTH)), const((MIX_WIDTH, d))],
        out_specs=row(d),
        out_shape=jax.ShapeDtypeStruct((n, d), F32),
        compiler_params=_cparams("arbitrary"),
        name="mixout",
    )(y, o_att_p, o_att_s, o_hg_p, o_hg_s, og, attn_gain.reshape(1, ATT_WIDTH), hg_gain.reshape(1, HG_WIDTH), w_out)


def _bias_kernel(tbl_ref, o_ref, *, a0, ar, ac, rows_blk):
    h = pl.program_id(0)
    rb = pl.program_id(1)
    shape = o_ref.shape[1:]
    r = lax.broadcasted_iota(jnp.int32, shape, 0) + rb * rows_blk
    c = lax.broadcasted_iota(jnp.int32, shape, 1)
    n = a0 + ar * r + ac * c
    out = jnp.full(shape, tbl_ref[0, h], F32)
    for k in range(1, N_BUCKETS):
        out = jnp.where(n >= T5_THRESHOLDS[k], tbl_ref[k, h], out)
    o_ref[0] = out


def _bias_table(tbl, rows, cols, a0, ar, ac):
    rows_blk = max(r for r in range(8, min(rows, 512) + 1, 8) if rows % r == 0)
    assert cols % LANES == 0
    return pl.pallas_call(
        functools.partial(_bias_kernel, a0=a0, ar=ar, ac=ac, rows_blk=rows_blk),
        grid=(ATT_HEADS, rows // rows_blk),
        in_specs=[pl.BlockSpec(memory_space=pltpu.SMEM)],
        out_specs=pl.BlockSpec((1, rows_blk, cols), lambda h, rb: (h, rb, 0)),
        out_shape=jax.ShapeDtypeStruct((ATT_HEADS, rows, cols), F32),
        compiler_params=_cparams("arbitrary", "arbitrary"),
        name="t5_bias",
    )(tbl)


def _compress_weights(w1, b1, w2, b2):
    w = w1.reshape(2, CMP_RATIO, CMP_STRIDE, HEAD_DIM, CMP_HIDDEN)
    z = jnp.zeros_like(w)
    top = jnp.concatenate([w, z], axis=-1)
    bot = jnp.concatenate([z, w], axis=-1)
    wpad = jnp.stack([top, bot], axis=3)
    wpad = wpad.reshape(2 * CMP_RATIO, CMP_STRIDE * LANES, 2 * CMP_HIDDEN).astype(BF16)
    z2 = jnp.zeros_like(w2)
    w2pad = jnp.concatenate([jnp.concatenate([w2, z2], axis=-1), jnp.concatenate([z2, w2], axis=-1)], axis=1)
    return wpad, jnp.concatenate([b1, b1], axis=-1), w2pad.astype(BF16), jnp.concatenate([b2, b2], axis=-1)


def _compress_kernel(rows_ref, perm_ref, wpad_ref, b1_ref, w2pad_ref, b2_ref, kg_ref, bd_ref, out_ref, xs_ref):
    u = pl.program_id(1)
    n_chunk = CMP_UNITS * CMP_UNIT_ROWS // CMP_STRIDE
    perm = perm_ref[...]

    for t in range(CMP_UNIT_ROWS // 256):
        x = rows_ref[t * 256:(t + 1) * 256, :]
        y = jnp.dot(perm, x.astype(BF16), preferred_element_type=F32).astype(BF16)
        base = pl.multiple_of(u * (CMP_UNIT_ROWS // CMP_STRIDE) + t * 16, 16)
        for s in range(CMP_STRIDE):
            xs_ref[s, pl.ds(base, 16), :] = y[s * 16:(s + 1) * 16, :]

    @pl.when(u == CMP_UNITS - 1)
    def _():
        halves = []
        for kv in range(2):
            outs = []
            for j in range(2):
                col = (kv * 2 + j) * LANES
                lhs = jnp.concatenate([xs_ref[s, :, col:col + LANES] for s in range(CMP_STRIDE)], axis=1)
                h0 = jnp.dot(lhs, wpad_ref[kv * CMP_RATIO + 0], preferred_element_type=F32)
                h1 = jnp.dot(lhs, wpad_ref[kv * CMP_RATIO + 1], preferred_element_type=F32)
                h = b1_ref[kv:kv + 1, :] + h0 + pltpu.roll(h1, n_chunk - 1, axis=0)
                a = (h * jax.nn.sigmoid(h)).astype(BF16)
                outs.append(jnp.dot(a, w2pad_ref[kv], preferred_element_type=F32) + b2_ref[kv:kv + 1, :])
            halves.append(jnp.concatenate(outs, axis=1))
        pk = _pack_kv(_group_rms(halves[0], bd_ref[...], kg_ref[...], HEAD_DIM), halves[1])
        for g in range(ATT_GROUPS):
            out_ref[0, g] = pk[:, g * LANES:(g + 1) * LANES]


def _compress(rows, cw, k_gain0):
    wpad, b1, w2pad, b2 = cw
    n_chunk = CMP_UNITS * CMP_UNIT_ROWS // CMP_STRIDE
    i = np.arange(256)
    perm = np.zeros((256, 256), np.float32)
    perm[(i % 16) * 16 + i // 16, i] = 1.0
    consts = (jnp.asarray(perm, BF16), wpad, b1, w2pad, b2, k_gain0.reshape(1, KV_WIDTH),
              jnp.asarray(_block_diag_ones(KV_WIDTH, HEAD_DIM), BF16))
    const = lambda a: pl.BlockSpec(a.shape, lambda b, u, nd=a.ndim: (0,) * nd)
    return pl.pallas_call(
        _compress_kernel,
        grid=(1, CMP_UNITS),
        in_specs=[pl.BlockSpec((CMP_UNIT_ROWS, KV_ROW), lambda b, u: (u, 0))] + [const(a) for a in consts],
        out_specs=pl.BlockSpec((1, ATT_GROUPS, n_chunk, LANES), lambda b, u: (0, 0, 0, 0)),
        out_shape=jax.ShapeDtypeStruct((1, ATT_GROUPS, n_chunk, LANES), BF16),
        scratch_shapes=[pltpu.VMEM((CMP_STRIDE, n_chunk, KV_ROW), BF16)],
        compiler_params=_cparams("arbitrary", "arbitrary"),
        name="compress",
    )(rows, *consts)


PAGED_UNIT_ROWS = 4096
PAGED_UNITS = PAST_LEN // PAGED_UNIT_ROWS
PAGES_PER_STEP = PAGED_UNIT_ROWS // PAGE_SIZE


def _cache_view(cache):
    return jnp.transpose(cache, (0, 2, 3, 4, 1))


def _compress_paged_kernel(pt_ref, *refs):
    del pt_ref
    page_refs = refs[:PAGES_PER_STEP]
    pick_ref, w1_ref, b1_ref, w2_ref, b2_ref, kg_ref, bd_ref, out_ref, xs_ref = refs[PAGES_PER_STEP:]
    u = pl.program_id(1)
    n_chunk = PAST_LEN // CMP_STRIDE
    pick = pick_ref[...]
    n_kvg = 2 * ATT_GROUPS

    for t in range(PAGES_PER_STEP // 2):
        base = pl.multiple_of(u * (PAGED_UNIT_ROWS // CMP_STRIDE) + t * 16, 16)
        kt = jnp.concatenate([page_refs[2 * t][0].reshape(n_kvg * HEAD_DIM, PAGE_SIZE),
                              page_refs[2 * t + 1][0].reshape(n_kvg * HEAD_DIM, PAGE_SIZE)], axis=1).astype(BF16)
        z = jnp.dot(kt, pick, preferred_element_type=F32)
        for kvg in range(n_kvg):
            zk = z[kvg * HEAD_DIM:(kvg + 1) * HEAD_DIM]
            y = jnp.concatenate([zk[:, :LANES], zk[:, LANES:]], axis=0).T.astype(BF16)
            for s2 in range(CMP_STRIDE // 2):
                xs_ref[kvg, s2, pl.ds(base, 16), :] = y[s2 * 16:(s2 + 1) * 16, :]

    @pl.when(u == PAGED_UNITS - 1)
    def _():
        halves = []
        for kv in range(2):
            lhs = jnp.concatenate(
                [jnp.concatenate([xs_ref[kv * ATT_GROUPS + g, s2] for s2 in range(CMP_STRIDE // 2)], axis=1)
                 for g in range(ATT_GROUPS)], axis=0)
            h0 = jnp.dot(lhs, w1_ref[kv * CMP_RATIO + 0], preferred_element_type=F32)
            h1 = jnp.dot(lhs, w1_ref[kv * CMP_RATIO + 1], preferred_element_type=F32)
            h = b1_ref[kv:kv + 1, :] + h0 + pltpu.roll(h1, ATT_GROUPS * n_chunk - 1, axis=0)
            a = (h * jax.nn.sigmoid(h)).astype(BF16)
            o = b2_ref[kv:kv + 1, :]
            for g in range(ATT_GROUPS):
                o = o + jnp.dot(a[g * n_chunk:(g + 1) * n_chunk], w2_ref[kv * ATT_GROUPS + g], preferred_element_type=F32)
            halves.append(o)
        out_ref[0, :, :KV_WIDTH] = _group_rms(halves[0], bd_ref[...], kg_ref[...], HEAD_DIM).astype(BF16)
        out_ref[0, :, KV_WIDTH:] = halves[1].astype(BF16)


def _compress_paged(cache_t, page_table, w1, b1, w2, b2, k_gain0):
    n_seq = page_table.shape[0]
    n_chunk = PAST_LEN // CMP_STRIDE
    r = np.arange(LANES)
    s2, c = r // 16, r % 16
    pick = np.zeros((2 * PAGE_SIZE, 2, LANES), np.float32)
    for half in range(2):
        pick[CMP_STRIDE * c + 2 * s2 + half, half, r] = 1.0
    w2p = jnp.zeros((2, ATT_GROUPS, CMP_HIDDEN, KV_WIDTH), F32)
    for g in range(ATT_GROUPS):
        w2p = w2p.at[:, g, :, g * HEAD_DIM:(g + 1) * HEAD_DIM].set(w2)
    consts = (jnp.asarray(pick.reshape(2 * PAGE_SIZE, 2 * LANES), BF16),
              w1.reshape(2 * CMP_RATIO, CMP_STRIDE * HEAD_DIM, CMP_HIDDEN).astype(BF16), b1,
              w2p.reshape(2 * ATT_GROUPS, CMP_HIDDEN, KV_WIDTH).astype(BF16), jnp.tile(b2, (1, ATT_GROUPS)),
              k_gain0.reshape(1, KV_WIDTH), jnp.asarray(_block_diag_ones(KV_WIDTH, HEAD_DIM), BF16))
    const = lambda a: pl.BlockSpec(a.shape, lambda b, u, pt, nd=a.ndim: (0,) * nd)
    pages = [pl.BlockSpec((1, 2, ATT_GROUPS, HEAD_DIM, PAGE_SIZE),
                          lambda b, u, pt, k=k: (pt[b, u * PAGES_PER_STEP + k], 0, 0, 0, 0))
             for k in range(PAGES_PER_STEP)]
    grid_spec = pltpu.PrefetchScalarGridSpec(
        num_scalar_prefetch=1, grid=(n_seq, PAGED_UNITS),
        in_specs=pages + [const(a) for a in consts],
        out_specs=pl.BlockSpec((1, n_chunk, KV_ROW), lambda b, u, pt: (b, 0, 0)),
        scratch_shapes=[pltpu.VMEM((2 * ATT_GROUPS, CMP_STRIDE // 2, n_chunk, LANES), BF16)])
    return pl.pallas_call(
        _compress_paged_kernel, grid_spec=grid_spec,
        out_shape=jax.ShapeDtypeStruct((n_seq, n_chunk, KV_ROW), BF16),
        compiler_params=_cparams("arbitrary", "arbitrary"),
        name="compress_paged",
    )(page_table, *([cache_t] * PAGES_PER_STEP), *consts)


def _block_overlap(n_cmp, n_blk):
    cs = np.arange(n_cmp)[:, None] * CMP_STRIDE
    bs = np.arange(n_blk)[None, :] * SEL_BLOCK
    ov = np.minimum(cs + CMP_BLOCK, bs + SEL_BLOCK) - np.maximum(cs, bs)
    return (np.clip(ov, 0, None) / CMP_BLOCK).astype(np.float32)


def _overlap_padded(n_cmp, n_blk, rows, cols):
    ov = np.zeros((rows, cols), np.float32)
    ov[:n_cmp, :n_blk] = _block_overlap(n_cmp, n_blk)
    return jnp.asarray(ov, BF16)


def _softmax_rows(s, mask):
    s = jnp.where(mask, s, NEG)
    m = jnp.max(s, axis=1, keepdims=True)
    e = jnp.where(mask, jnp.exp(s - m), 0.0)
    d = jnp.sum(e, axis=1, keepdims=True)
    return e / jnp.where(d > 0, d, 1.0)


def _select_blocks(imp, qpos, n_blk, axis):
    blk = lax.broadcasted_iota(jnp.int32, imp.shape, axis)
    cur = qpos >> SEL_SHIFT
    forced = (blk == 0) | (blk == cur) | (blk == cur - 1)
    valid = blk * SEL_BLOCK <= qpos
    score = jnp.where(forced, ATT_HPG + 1.0, jnp.where(valid, imp, -1.0))
    ahead = jnp.zeros(imp.shape, F32)
    for m in range(n_blk):
        sm = jnp.broadcast_to(score[m:m + 1, :] if axis == 0 else score[:, m:m + 1], imp.shape)
        tie = jnp.where(blk > m, 1.0, 0.0)
        ahead = ahead + jnp.where(sm > score, 1.0, jnp.where(sm == score, tie, 0.0))
    return jnp.where(blk < n_blk, jnp.where(ahead < N_SEL, 1.0, 0.0), 0.0)


SEL_SHIFT = SEL_BLOCK.bit_length() - 1
NT_DIMS = (((1,), (1,)), ((), ()))


SLC_UNROLL = 4
TOEP_TILES = min(SEQ // LANES, -(-(T5_THRESHOLDS[-1] + LANES - 1) // LANES) + 1)


NSA_GROUPS_PER_STEP = 4


def _nsa_prompt_multi_kernel(q_ref, gate_ref, kvc_ref, slc_ref, win_ref, bcmp_ref, toep_ref, ov_ref, spread_ref,
                             o_ref, s_scr, m_scr, l_scr, acc_scr):
    i = pl.program_id(2)
    ngs = NSA_GROUPS_PER_STEP
    tile = (Q_BLOCK, LANES)
    rows = ATT_HPG * Q_BLOCK
    lane = lax.broadcasted_iota(jnp.int32, tile, 1)
    sub = lax.broadcasted_iota(jnp.int32, tile, 0)
    low = lane < HEAD_DIM
    qpos = i * Q_BLOCK + sub
    head = lambda x, p: x[p * Q_BLOCK:(p + 1) * Q_BLOCK]
    heads = range(ATT_HPG)
    n_blk = SEQ // SEL_BLOCK
    blk_row = lax.broadcasted_iota(jnp.int32, (n_blk, LANES), 0)
    blk_of_key = lax.broadcasted_iota(jnp.int32, (n_blk, LANES), 1) >> SEL_SHIFT
    qpos_t = i * Q_BLOCK + lax.broadcasted_iota(jnp.int32, (n_blk, Q_BLOCK), 1)
    last_tile = SEQ // LANES - 1
    ov = ov_ref[...]
    glanes = lambda gg: slice(gg * LANES, (gg + 1) * LANES)

    def key_tile(kv_ref, gg, j):
        jc = jnp.clip(j, 0, last_tile)
        return kv_ref[pl.ds(pl.multiple_of(jc * LANES, LANES), LANES), glanes(gg)]

    def masked_scores(qs, gg, kv, j, mask):
        sc = lax.dot_general(qs, kv, NT_DIMS, preferred_element_type=F32)
        row0 = pl.multiple_of(jnp.clip(i - j, 0, TOEP_TILES - 1) * LANES, LANES)
        return [jnp.where(mask, head(sc, p) + toep_ref[gg * ATT_HPG + p, pl.ds(row0, LANES), :], NEG) for p in heads]

    def weights_times_v(scores, m, kv):
        es = [jnp.exp(scores[p] - m[p]) for p in heads]
        return es, jnp.dot(jnp.concatenate(es, axis=0).astype(BF16), kv, preferred_element_type=F32)

    def front(gg):
        qa = q_ref[:, gg * ATT_HPG * HEAD_DIM:(gg + 1) * ATT_HPG * HEAD_DIM] * (HEAD_DIM ** -0.5)
        qp = []
        for pair in range(ATT_HPG // 2):
            x = qa[:, pair * LANES:(pair + 1) * LANES]
            qp.append(jnp.where(low, x, 0.0))
            qp.append(jnp.where(low, pltpu.roll(x, HEAD_DIM, axis=1), 0.0))
        qs = jnp.concatenate(qp, axis=0).astype(BF16)

        kvc = kvc_ref[0, gg]
        s = lax.dot_general(qs, kvc, NT_DIMS, preferred_element_type=F32)
        s = s + bcmp_ref[gg * ATT_HPG:(gg + 1) * ATT_HPG].reshape(rows, LANES)
        cmask1 = (lane * CMP_STRIDE + (CMP_BLOCK - 1)) <= qpos
        pr = _softmax_rows(s, jnp.concatenate([cmask1] * ATT_HPG, axis=0))
        o_cmp = jnp.dot(pr.astype(BF16), kvc, preferred_element_type=F32)
        hi, lo = _split2(sum(head(pr, p) for p in heads))
        imp = jnp.dot(hi, ov, preferred_element_type=F32) + jnp.dot(lo, ov, preferred_element_type=F32)
        sel_t = _select_blocks(imp.T[:n_blk, :], qpos_t, n_blk, axis=0).astype(BF16)

        n_win = WINDOW // LANES + 1
        win_kv, win_s = [], []
        for w in range(n_win):
            j = i - (n_win - 1) + w
            dist = (i - j) * LANES + sub - lane
            inside = jnp.where(dist >= 0, jnp.where(dist < WINDOW, 1.0, 0.0), 0.0)
            win_kv.append(key_tile(win_ref, gg, j))
            win_s.append(masked_scores(qs, gg, win_kv[w], j, jnp.where(j >= 0, inside, 0.0) > 0.5))
        m_w = [jnp.max(functools.reduce(jnp.maximum, [win_s[w][p] for w in range(n_win)]), axis=1, keepdims=True)
               for p in heads]
        acc_w = jnp.zeros((rows, LANES), F32)
        l_w = [jnp.zeros(tile, F32) for _ in heads]
        for w in range(n_win):
            es, pv = weights_times_v(win_s[w], m_w, win_kv[w])
            acc_w = acc_w + pv
            l_w = [l_w[p] + es[p] for p in heads]
        o_win = [head(acc_w, p) / jnp.sum(l_w[p], axis=1, keepdims=True) for p in heads]
        return qs, o_cmp, sel_t, o_win

    fronts = [front(gg) for gg in range(ngs)]

    n_steps = (i + SLC_UNROLL) >> (SLC_UNROLL.bit_length() - 1)
    m_scr[...] = jnp.full(m_scr.shape, NEG, F32)

    def slc_scores(jj, carry):
        for gg in range(ngs):
            qs, _, sel_t, _ = fronts[gg]
            sps = []
            for r in range(SLC_UNROLL):
                j = SLC_UNROLL * jj + r
                dist = (i - j) * LANES + sub - lane
                expand = jnp.where(blk_row == 2 * j + blk_of_key, 1.0, 0.0).astype(BF16)
                picked = lax.dot_general(sel_t, expand, (((0,), (0,)), ((), ())), preferred_element_type=F32)
                sp = masked_scores(qs, gg, key_tile(slc_ref, gg, j), j, jnp.where(dist >= 0, picked, 0.0) > 0.5)
                for p in heads:
                    s_scr[j, gg * ATT_HPG + p] = sp[p]
                sps.append(sp)
            for p in heads:
                hp = gg * ATT_HPG + p
                m_scr[hp] = jnp.maximum(m_scr[hp], functools.reduce(jnp.maximum, [sp[p] for sp in sps]))
        return carry

    lax.fori_loop(0, n_steps, slc_scores, 0)
    for hp in range(ngs * ATT_HPG):
        m_scr[hp] = jnp.zeros(tile, F32) + jnp.max(m_scr[hp], axis=1, keepdims=True)
    l_scr[...] = jnp.zeros(l_scr.shape, F32)
    acc_scr[...] = jnp.zeros(acc_scr.shape, F32)

    def slc_weights(jj, carry):
        for gg in range(ngs):
            m = [m_scr[gg * ATT_HPG + p] for p in heads]
            pvs, ess = [], []
            for r in range(SLC_UNROLL):
                j = SLC_UNROLL * jj + r
                es, pv = weights_times_v([s_scr[j, gg * ATT_HPG + p] for p in heads], m, key_tile(slc_ref, gg, j))
                pvs.append(pv)
                ess.append(es)
            acc_scr[gg] += functools.reduce(jnp.add, pvs)
            for p in heads:
                hp = gg * ATT_HPG + p
                l_scr[hp] = l_scr[hp] + functools.reduce(jnp.add, [es[p] for es in ess])
        return carry

    lax.fori_loop(0, n_steps, slc_weights, 0)

    for gg in range(ngs):
        _, o_cmp, _, o_win = fronts[gg]
        g_hi, g_lo = _split2(gate_ref[:, glanes(gg)])
        spread = spread_ref[...]
        gb = jnp.dot(g_hi, spread, preferred_element_type=F32) + jnp.dot(g_lo, spread, preferred_element_type=F32)
        comb = []
        for p in heads:
            o_slc = head(acc_scr[gg], p) / jnp.sum(l_scr[gg * ATT_HPG + p], axis=1, keepdims=True)
            col = lambda br: gb[:, (br * ATT_HPG + p) * LANES:(br * ATT_HPG + p + 1) * LANES]
            comb.append(col(0) * head(o_cmp, p) + col(1) * o_slc + col(2) * o_win[p])
        for pair in range(ATT_HPG // 2):
            c0 = gg * ATT_HPG * HEAD_DIM + pair * LANES
            o_ref[:, c0:c0 + LANES] = jnp.where(low, pltpu.roll(comb[2 * pair], HEAD_DIM, axis=1), comb[2 * pair + 1])


def _nsa_prompt(q, gates, kvc, slc_pack, win_pack, bias_cmp, bias_toep, n_batch=BATCH):
    nqb = SEQ // Q_BLOCK
    ngs = NSA_GROUPS_PER_STEP
    ov = _overlap_padded(SEQ // CMP_STRIDE - 1, SEQ // SEL_BLOCK, LANES, LANES)
    gw = ngs * ATT_HPG * HEAD_DIM
    nh = ngs * ATT_HPG
    n_gate = N_BRANCH * ATT_HPG
    spread = np.zeros((LANES, n_gate, LANES), np.float32)
    spread[np.arange(n_gate), np.arange(n_gate), :] = 1.0
    spread = jnp.asarray(spread.reshape(LANES, n_gate * LANES), BF16)
    return pl.pallas_call(
        _nsa_prompt_multi_kernel,
        grid=(n_batch, ATT_GROUPS // ngs, nqb),
        in_specs=[
            pl.BlockSpec((Q_BLOCK, gw), lambda b, g, i: (b * nqb + i, g)),
            pl.BlockSpec((Q_BLOCK, ngs * LANES), lambda b, g, i: (b * nqb + i, g)),
            pl.BlockSpec((1, ngs, SEQ // CMP_STRIDE, LANES), lambda b, g, i: (0, g, b, 0)),
            pl.BlockSpec((SEQ, ngs * LANES), lambda b, g, i: (b, g)),
            pl.BlockSpec((SEQ, ngs * LANES), lambda b, g, i: (b, g)),
            pl.BlockSpec((nh, Q_BLOCK, LANES), lambda b, g, i: (g, i, 0)),
            pl.BlockSpec((nh, TOEP_TILES * LANES, LANES), lambda b, g, i: (g, 0, 0)),
            pl.BlockSpec((LANES, LANES), lambda b, g, i: (0, 0)),
            pl.BlockSpec((LANES, n_gate * LANES), lambda b, g, i: (0, 0)),
        ],
        out_specs=pl.BlockSpec((Q_BLOCK, gw), lambda b, g, i: (b * nqb + i, g)),
        out_shape=jax.ShapeDtypeStruct((n_batch * SEQ, ATT_WIDTH), F32),
        scratch_shapes=[pltpu.VMEM((nqb, nh, Q_BLOCK, LANES), F32),
                        pltpu.VMEM((nh, Q_BLOCK, LANES), F32), pltpu.VMEM((nh, Q_BLOCK, LANES), F32),
                        pltpu.VMEM((ngs, ATT_HPG * Q_BLOCK, LANES), F32)],
        compiler_params=_cparams("arbitrary", "arbitrary", "arbitrary"),
        name="nsa_prompt",
    )(q, gates, kvc, slc_pack, win_pack, bias_cmp, bias_toep, ov, spread)


SAMPLE_ROWS = ATT_HEADS * DEC_SEQ
SAMPLE_PAGES_PER_STEP = 32
SAMPLE_STEPS = N_PAGES // SAMPLE_PAGES_PER_STEP
SAMPLE_N_CMP = (PAST_LEN + DEC_SEQ - CMP_BLOCK) // CMP_STRIDE + 1
SAMPLE_N_BLK = -(-(PAST_LEN + DEC_SEQ) // SEL_BLOCK)
SAMPLE_BLK_LANES = 2 * LANES


def _nsa_sample_kernel(pt_ref, q_ref, gate_ref, kvc_ref, *rest):
    page_refs = rest[:SAMPLE_PAGES_PER_STEP]
    (slc_new_ref, win_state_ref, win_new_ref, bcmp_ref, bslc_ref, bwin_ref, perm_ref, ov_ref,
     o_ref, qbd_scr, picked_scr, m_scr, l_scr, acc_scr, ocmp_scr, owin_scr) = rest[SAMPLE_PAGES_PER_STEP:]
    del pt_ref
    u = pl.program_id(1)
    tile = (SAMPLE_ROWS, LANES)
    wide = (SAMPLE_ROWS, KV_WIDTH)
    lane = lax.broadcasted_iota(jnp.int32, tile, 1)
    t_row = lax.broadcasted_iota(jnp.int32, tile, 0) & (DEC_SEQ - 1)
    rows_per_group = ATT_HPG * DEC_SEQ
    own = ((lax.broadcasted_iota(jnp.int32, wide, 1) >> (HEAD_DIM.bit_length() - 1))
           == (lax.broadcasted_iota(jnp.int32, wide, 0) >> (rows_per_group.bit_length() - 1)))

    def reset():
        m_scr[...] = jnp.full(m_scr.shape, NEG, F32)
        l_scr[...] = jnp.zeros(l_scr.shape, F32)
        acc_scr[...] = jnp.zeros(acc_scr.shape, F32)

    def finish():
        l = jnp.sum(l_scr[...], axis=1, keepdims=True)
        return acc_scr[...] / jnp.where(l > 0, l, 1.0)

    def attend(tiles):
        qbd = qbd_scr[...]
        pairs = [tiles[n:n + 2] for n in range(0, len(tiles), 2)]
        side_by_side = lambda xs: (xs[0] if len(xs) == 1 else jnp.concatenate(xs, axis=1)).astype(BF16)
        scores = []
        m_el = None
        for grp in pairs:
            sc = jnp.dot(qbd, side_by_side([t[0] for t in grp]), preferred_element_type=F32)
            for n, (_, _, bias, mask) in enumerate(grp):
                s = jnp.where(mask, sc[:, n * LANES:(n + 1) * LANES] + bias, NEG)
                scores.append(s)
                m_el = s if m_el is None else jnp.maximum(m_el, s)
        m_prev = m_scr[...]
        m_new = jnp.maximum(m_prev, jnp.max(m_el, axis=1, keepdims=True))
        alpha = jnp.exp(m_prev - m_new)
        l_el = alpha * l_scr[...]
        acc = jnp.concatenate([alpha, alpha], axis=1) * acc_scr[...]
        for k, grp in enumerate(pairs):
            es = [jnp.exp(s - m_new) for s in scores[2 * k:2 * k + len(grp)]]
            l_el = l_el + functools.reduce(jnp.add, es)
            acc = acc + lax.dot_general(side_by_side(es), side_by_side([t[1] for t in grp]), NT_DIMS,
                                        preferred_element_type=F32)
        m_scr[...] = m_new
        l_scr[...] = l_el
        acc_scr[...] = acc

    def page_tile(kv4, bias, mask):
        return (kv4[0].reshape(KV_WIDTH, LANES), kv4[1].reshape(KV_WIDTH, LANES), bias, mask)

    def new_tile(ref, bias, mask):
        rows = jnp.concatenate([ref[...], jnp.zeros((LANES - DEC_SEQ, KV_ROW), F32)], axis=0)
        return (rows[:, :KV_WIDTH].T, rows[:, KV_WIDTH:].T, bias, mask)

    @pl.when(u == 0)
    def _():
        q = (q_ref[...] * (HEAD_DIM ** -0.5)).astype(BF16)
        qperm = jnp.dot(q, perm_ref[...], preferred_element_type=F32)
        qfull = jnp.concatenate([qperm[:, p * KV_WIDTH:(p + 1) * KV_WIDTH]
                                 for g in range(ATT_GROUPS) for p in range(ATT_HPG)], axis=0)
        qbd = jnp.where(own, qfull, 0.0).astype(BF16)
        qbd_scr[...] = qbd

        kvc = kvc_ref[0]
        s = lax.dot_general(qbd, kvc[:, :KV_WIDTH], NT_DIMS, preferred_element_type=F32) + bcmp_ref[...]
        cmask = lax.broadcasted_iota(jnp.int32, s.shape, 1) < SAMPLE_N_CMP
        pr = _softmax_rows(s, cmask)
        ocmp_scr[...] = jnp.dot(pr.astype(BF16), kvc[:, KV_WIDTH:], preferred_element_type=F32)
        ps = []
        for g in range(ATT_GROUPS):
            r0 = g * rows_per_group
            ps.append(sum(pr[r0 + p * DEC_SEQ:r0 + (p + 1) * DEC_SEQ, :] for p in range(ATT_HPG)))
        hi, lo = _split2(jnp.concatenate(ps, axis=0))
        ov = ov_ref[...]
        imp = jnp.dot(hi, ov, preferred_element_type=F32) + jnp.dot(lo, ov, preferred_element_type=F32)
        qpos = PAST_LEN + (lax.broadcasted_iota(jnp.int32, imp.shape, 0) & (DEC_SEQ - 1))
        sel = _select_blocks(imp, qpos, SAMPLE_N_BLK, axis=1)
        sel_rows = jnp.concatenate([sel[g * DEC_SEQ:(g + 1) * DEC_SEQ, :]
                                    for g in range(ATT_GROUPS) for p in range(ATT_HPG)], axis=0).astype(BF16)
        span = 4 * LANES
        blk_row = lax.broadcasted_iota(jnp.int32, (SAMPLE_BLK_LANES, span), 0)
        key_blk = lax.broadcasted_iota(jnp.int32, (SAMPLE_BLK_LANES, span), 1) >> SEL_SHIFT
        for c in range(PAST_LEN // span):
            expand = jnp.where(blk_row == key_blk + c * (span // SEL_BLOCK), 1.0, 0.0).astype(BF16)
            picked_scr[:, c * span:(c + 1) * span] = jnp.dot(sel_rows, expand, preferred_element_type=F32)

        reset()
        tiles = [page_tile(win_state_ref[0, :, :, :, w * LANES:(w + 1) * LANES],
                           bwin_ref[:, w * LANES:(w + 1) * LANES], (w * LANES + lane) > t_row)
                 for w in range(WINDOW // LANES)]
        tiles.append(new_tile(win_new_ref, bwin_ref[:, WINDOW:WINDOW + LANES], lane <= t_row))
        attend(tiles)
        owin_scr[...] = finish()
        reset()

    tiles = []
    for k in range(SAMPLE_PAGES_PER_STEP):
        keys = pl.ds(pl.multiple_of((u * SAMPLE_PAGES_PER_STEP + k) * LANES, LANES), LANES)
        tiles.append(page_tile(page_refs[k][0], bslc_ref[:, keys], picked_scr[:, keys] > 0.5))
    attend(tiles)

    @pl.when(u == SAMPLE_STEPS - 1)
    def _():
        attend([new_tile(slc_new_ref, bslc_ref[:, PAST_LEN:PAST_LEN + LANES], lane <= t_row)])
        o_slc = finish()
        gt = gate_ref[...]

        def gate_rows(br):
            cols = []
            for g in range(ATT_GROUPS):
                for p in range(ATT_HPG):
                    c = g * LANES + br * ATT_HPG + p
                    cols.append(jnp.broadcast_to(gt[:, c:c + 1], (DEC_SEQ, KV_WIDTH)))
            return jnp.concatenate(cols, axis=0)

        comb = gate_rows(0) * ocmp_scr[...] + gate_rows(1) * o_slc + gate_rows(2) * owin_scr[...]
        comb = jnp.where(own, comb, 0.0)
        per_head = []
        for p in range(ATT_HPG):
            per_head.append(sum(comb[(g * ATT_HPG + p) * DEC_SEQ:(g * ATT_HPG + p + 1) * DEC_SEQ, :]
                                for g in range(ATT_GROUPS)))
        hi, lo = _split2(jnp.concatenate(per_head, axis=1))
        perm = perm_ref[...]
        o_ref[...] = (lax.dot_general(hi, perm, NT_DIMS, preferred_element_type=F32)
                      + lax.dot_general(lo, perm, NT_DIMS, preferred_element_type=F32))


def _nsa_sample(q, gates, kvc, cache_slc, page_table, slc_rows, state_win, win_rows, bias_cmp, bias_slc, bias_win):
    n_seq = page_table.shape[0]
    row0 = N_PROMPT // DEC_SEQ
    new = lambda w: pl.BlockSpec((DEC_SEQ, w), lambda b, u, pt: (b, 0))
    src = np.arange(ATT_WIDTH)
    g, p, d = src // (ATT_HPG * HEAD_DIM), (src // HEAD_DIM) % ATT_HPG, src % HEAD_DIM
    perm = np.zeros((ATT_WIDTH, ATT_WIDTH), np.float32)
    perm[src, p * KV_WIDTH + g * HEAD_DIM + d] = 1.0
    ov = _overlap_padded(SAMPLE_N_CMP, SAMPLE_N_BLK, PAST_LEN // CMP_STRIDE, SAMPLE_BLK_LANES)
    tok = lambda w: pl.BlockSpec((DEC_SEQ, w), lambda b, u, pt: (row0 + b, 0))
    const = lambda a: pl.BlockSpec(a.shape, lambda b, u, pt, nd=a.ndim: (0,) * nd)
    per_seq = lambda s: pl.BlockSpec((1,) + s, lambda b, u, pt, nd=len(s): (b,) + (0,) * nd)
    pages = [pl.BlockSpec((1, 2, ATT_GROUPS, HEAD_DIM, PAGE_SIZE),
                          lambda b, u, pt, k=k: (pt[b, u * SAMPLE_PAGES_PER_STEP + k], 0, 0, 0, 0))
             for k in range(SAMPLE_PAGES_PER_STEP)]
    consts = (bias_cmp, bias_slc, bias_win, jnp.asarray(perm, BF16), ov)
    grid_spec = pltpu.PrefetchScalarGridSpec(
        num_scalar_prefetch=1, grid=(n_seq, SAMPLE_STEPS),
        in_specs=[tok(ATT_WIDTH), tok(ATT_GROUPS * LANES), per_seq((PAST_LEN // CMP_STRIDE, KV_ROW))] + pages
        + [new(KV_ROW), per_seq((2, ATT_GROUPS, HEAD_DIM, WINDOW)), new(KV_ROW)] + [const(a) for a in consts],
        out_specs=pl.BlockSpec((DEC_SEQ, ATT_WIDTH), lambda b, u, pt: (b, 0)),
        scratch_shapes=[pltpu.VMEM((SAMPLE_ROWS, KV_WIDTH), BF16), pltpu.VMEM((SAMPLE_ROWS, PAST_LEN), F32),
                        pltpu.VMEM((SAMPLE_ROWS, LANES), F32), pltpu.VMEM((SAMPLE_ROWS, LANES), F32),
                        pltpu.VMEM((SAMPLE_ROWS, KV_WIDTH), F32), pltpu.VMEM((SAMPLE_ROWS, KV_WIDTH), F32),
                        pltpu.VMEM((SAMPLE_ROWS, KV_WIDTH), F32)])
    return pl.pallas_call(
        _nsa_sample_kernel, grid_spec=grid_spec,
        out_shape=jax.ShapeDtypeStruct((n_seq * DEC_SEQ, ATT_WIDTH), F32),
        compiler_params=_cparams("arbitrary", "arbitrary"),
        name="nsa_sample",
    )(page_table, q, gates, kvc, *([cache_slc] * SAMPLE_PAGES_PER_STEP), slc_rows, state_win, win_rows, *consts)


def kernel(x_prompt, x_sample, cache_cmp_kv, cache_slc_kv, state_win_kv, state_hgrn, page_table,
           rel_bias_table, hgrn_lower_bound, norm_ffn1, w_ffn1_gate_up, w_ffn1_down, norm_mix,
           w_in, q_norm, k_norm, w_cmp1, b_cmp1, w_cmp2, b_cmp2, attn_out_norm, hgrn_out_norm,
           w_out, norm_ffn2, w_ffn2_gate_up, w_ffn2_down):
    assert DEPTH == 1
    l = 0
    kv_shape = (2, ATT_GROUPS, HEAD_DIM)

    y1 = _ffn((x_prompt.reshape(N_PROMPT, D_MODEL), x_sample.reshape(N_SAMPLE, D_MODEL)), norm_ffn1[l],
              w_ffn1_gate_up[l].astype(BF16), w_ffn1_down[l].astype(BF16))

    (q, cmp_rows, slc_rows, slc_pack, win_rows, win_pack, hq, lf, hk, hv, hog, gates,
     cmp_new, slc_new, win_new) = _project_all(
        y1, norm_mix[l], _permute_w_in(w_in[l]), q_norm[l], k_norm[l], hgrn_lower_bound)

    tbl = rel_bias_table.astype(F32)
    first_end = CMP_BLOCK - 1
    bias_cmp_p = _bias_table(tbl, SEQ, LANES, -first_end, 1, -CMP_STRIDE)
    bias_toep = _bias_table(tbl, TOEP_TILES * LANES, LANES, 0, 1, -1)
    bias_cmp_s = _bias_table(tbl, DEC_SEQ, PAST_LEN // CMP_STRIDE, PAST_LEN - first_end, 1, -CMP_STRIDE)
    bias_slc_s = _bias_table(tbl, DEC_SEQ, PAST_LEN + LANES, PAST_LEN, 1, -1)
    bias_win_s = _bias_table(tbl, DEC_SEQ, WINDOW + LANES, WINDOW, 1, -1)
    rows_ht = lambda a: a.reshape(SAMPLE_ROWS, a.shape[-1])

    cw = _compress_weights(w_cmp1[l], b_cmp1[l], w_cmp2[l], b_cmp2[l])
    kvc_p = _compress(cmp_rows, cw, k_norm[l][0])
    kvc_s = _compress_paged(_cache_view(cache_cmp_kv[l]), page_table, w_cmp1[l], b_cmp1[l], w_cmp2[l], b_cmp2[l],
                            k_norm[l][0])

    o_att_p = _nsa_prompt(q, gates, kvc_p, slc_pack, win_pack, bias_cmp_p, bias_toep)
    o_att_s = _nsa_sample(q, gates, kvc_s, _cache_view(cache_slc_kv[l]), page_table, slc_new,
                          _cache_view(state_win_kv[l]), win_new,
                          rows_ht(bias_cmp_s), rows_ht(bias_slc_s), rows_ht(bias_win_s))

    o_hg_p, hg_p = _hgrn(hq, hk, hv, lf, None, n_seq=BATCH, t_len=SEQ, ch=HG_CHUNK_PROMPT, row0=0)
    o_hg_s, hg_s = _hgrn(hq, hk, hv, lf, state_hgrn[l].astype(F32), n_seq=DEC_BATCH, t_len=DEC_SEQ,
                         ch=HG_CHUNK_SAMPLE, row0=N_PROMPT)

    y2 = _mixer_out(y1, o_att_p, o_att_s, o_hg_p, o_hg_s, hog, attn_out_norm[l], hgrn_out_norm[l],
                    w_out[l].astype(BF16))
    y3_p, y3_s = _ffn(y2, norm_ffn2[l], w_ffn2_gate_up[l].astype(BF16), w_ffn2_down[l].astype(BF16), split_out=True)

    prompt_rows = lambda a: a.reshape((1, BATCH, SEQ) + kv_shape)
    sample_rows = lambda a: a.reshape((1, DEC_BATCH, DEC_SEQ) + kv_shape)
    win_p = prompt_rows(win_rows)[:, :, SEQ - min(WINDOW, SEQ):]
    win_s = jnp.concatenate([state_win_kv[l][:, DEC_SEQ:], sample_rows(win_new)[0]], axis=1)[None]
    return (y3_p.reshape(BATCH, SEQ, D_MODEL), y3_s.reshape(DEC_BATCH, DEC_SEQ, D_MODEL),
            prompt_rows(cmp_rows), prompt_rows(slc_rows), win_p, hg_p[None],
            sample_rows(cmp_new), sample_rows(slc_new), win_s, hg_s[None].astype(state_hgrn.dtype))
```

```python
import functools
import math

import jax
import jax.numpy as jnp
import numpy as np
from jax import lax
from jax.experimental import pallas as pl
from jax.experimental.pallas import tpu as pltpu

D_MODEL = 2048
BATCH = 4
SEQ = 2048
DEPTH = 1
DEC_BATCH = 32
DEC_SEQ = 8
PAST_LEN = 8192
PAGE_SIZE = 128
HEAD_DIM = 64
ATT_HEADS = (D_MODEL // 2) // HEAD_DIM
ATT_GROUPS = ATT_HEADS // 4
ATT_HPG = ATT_HEADS // ATT_GROUPS
ATT_WIDTH = ATT_HEADS * HEAD_DIM
KV_WIDTH = ATT_GROUPS * HEAD_DIM
N_BRANCH = 3
CMP_BLOCK = 32
CMP_STRIDE = 16
CMP_RATIO = CMP_BLOCK // CMP_STRIDE
CMP_HIDDEN = 256
SEL_BLOCK = 64
N_SEL = 8
WINDOW = 512
Q_BLOCK = 128
HG_KEY = 128
HG_VAL = 128
HG_HEADS = (D_MODEL // 2) // HG_VAL
HG_WIDTH = HG_HEADS * HG_VAL
MIX_WIDTH = ATT_WIDTH + HG_WIDTH
D_FF = 256 * ((8 * D_MODEL // 3 + 255) // 256)
N_BUCKETS = 32
MAX_DISTANCE = 1024
EPS = 1e-6
IN_SPLITS = (ATT_WIDTH, 2 * N_BRANCH * KV_WIDTH, N_BRANCH * ATT_HEADS,
             HG_HEADS * HG_KEY, HG_HEADS * HG_KEY, HG_WIDTH, HG_WIDTH)

N_PROMPT = BATCH * SEQ
N_SAMPLE = DEC_BATCH * DEC_SEQ
N_TOKENS = N_PROMPT + N_SAMPLE
N_PAGES = PAST_LEN // PAGE_SIZE
KV_ROW = 2 * KV_WIDTH

LANES = 128
V7X_VMEM_BYTES = 64 * 1024 * 1024
VMEM_LIMIT_BYTES = V7X_VMEM_BYTES * 7 // 8

NEG = -1e30
F32 = jnp.float32
BF16 = jnp.bfloat16

FFN_ROW_TILE = 768
FFN_FF_TILE = 512
PROJ_ROW_TILE = 528
PROJ_COL_TILE = 512
OUT_ROW_TILE = 256
HG_CHUNK_PROMPT = 16
HG_CHUNK_SAMPLE = DEC_SEQ
CMP_UNIT_ROWS = 2048
CMP_UNITS = 4


def _cparams(*sem):
    return pltpu.CompilerParams(dimension_semantics=sem, vmem_limit_bytes=VMEM_LIMIT_BYTES)


def _t5_thresholds():
    n = np.arange(0, 2 * MAX_DISTANCE + 2)
    exact = N_BUCKETS // 2
    logn = np.log(np.maximum(n, 1).astype(np.float64) / exact)
    large = exact + (logn / math.log(MAX_DISTANCE / exact) * (N_BUCKETS - exact)).astype(np.int32)
    b = np.where(n < exact, n, np.minimum(large, N_BUCKETS - 1))
    return [int(n[b >= k][0]) for k in range(N_BUCKETS)]


T5_THRESHOLDS = _t5_thresholds()


def _block_diag_ones(n, blk):
    i = np.arange(n)
    return (i[:, None] // blk == i[None, :] // blk).astype(np.float32)


def _split2(x):
    hi = x.astype(BF16)
    lo = (x - hi.astype(F32)).astype(BF16)
    return hi, lo


def _group_rms(t, bd, gain, width):
    hi, lo = _split2(t * t)
    ss = (jnp.dot(hi, bd, preferred_element_type=F32) + jnp.dot(lo, bd, preferred_element_type=F32))
    return t * lax.rsqrt(ss * (1.0 / width) + EPS) * gain


def _ffn_kernel(*refs, split_in, split_out):
    n_in = 2 if split_in else 1
    x_ref = refs[0]
    gain_ref, wg_ref, wu_ref, wd_ref = refs[n_in:n_in + 4]
    o_ref = refs[n_in + 4]
    xn_ref, acc_ref = refs[-2:]
    j = pl.program_id(1)
    last_tile = pl.program_id(0) == pl.num_programs(0) - 1
    tm = x_ref.shape[0]

    def rows_in():
        x = x_ref[...]
        if split_in:
            xs = refs[1][...]
            x = jnp.where(last_tile, jnp.concatenate([x[:tm - xs.shape[0]], xs], axis=0), x)
        return x

    @pl.when(j == 0)
    def _():
        x = rows_in()
        y = x * lax.rsqrt(jnp.mean(x * x, axis=-1, keepdims=True) + EPS)
        xn_ref[...] = (y * gain_ref[...]).astype(BF16)
        acc_ref[...] = jnp.zeros_like(acc_ref)

    xn = xn_ref[...]
    g = jnp.dot(xn, wg_ref[...], preferred_element_type=F32)
    u = jnp.dot(xn, wu_ref[...], preferred_element_type=F32)
    a = (g * jax.nn.sigmoid(g) * u).astype(BF16)
    acc_ref[...] += jnp.dot(a, wd_ref[...], preferred_element_type=F32)

    @pl.when(j == pl.num_programs(1) - 1)
    def _():
        res = rows_in() + 0.5 * acc_ref[...]
        o_ref[...] = res
        if split_out:
            os_ref = refs[n_in + 5]

            @pl.when(last_tile)
            def _():
                os_ref[...] = res[tm - os_ref.shape[0]:]


def _ffn(x, gain, w_gu, w_down, split_out=False):
    split_in = isinstance(x, tuple)
    tm, tf = FFN_ROW_TILE, FFN_FF_TILE
    n_p, n_s = N_PROMPT, N_SAMPLE
    n, d = n_p + n_s, D_MODEL
    assert n % tm == 0 and D_FF % tf == 0 and n_p % tm == tm - n_s
    nj = D_FF // tf
    row = pl.BlockSpec((tm, d), lambda i, j: (i, 0))
    tail = pl.BlockSpec((n_s, d), lambda i, j: (0, 0))
    xs = x if split_in else (x,)
    return pl.pallas_call(
        functools.partial(_ffn_kernel, split_in=split_in, split_out=split_out),
        grid=(n // tm, nj),
        in_specs=([row, tail] if split_in else [row]) + [
            pl.BlockSpec((1, d), lambda i, j: (0, 0)),
            pl.BlockSpec((d, tf), lambda i, j: (0, j)),
            pl.BlockSpec((d, tf), lambda i, j: (0, j + nj)),
            pl.BlockSpec((tf, d), lambda i, j: (j, 0)),
        ],
        out_specs=(row, tail) if split_out else row,
        out_shape=((jax.ShapeDtypeStruct((n_p, d), F32), jax.ShapeDtypeStruct((n_s, d), F32)) if split_out
                   else jax.ShapeDtypeStruct((n, d), F32)),
        scratch_shapes=[pltpu.VMEM((tm, d), BF16), pltpu.VMEM((tm, d), F32)],
        compiler_params=_cparams("arbitrary", "arbitrary"),
        name="ffn",
    )(*xs, gain.reshape(1, d), w_gu, w_gu, w_down)


PB_Q, PB_CMP, PB_SLC, PB_WIN, PB_HQ, PB_HF, PB_HI, PB_HG, PB_GATE = 0, 2, 3, 4, 5, 7, 9, 11, 13
PROJ_N_BLOCKS = 14


def _permute_w_in(w_in):
    p = [int(v) for v in np.cumsum(IN_SPLITS)]
    a_g = w_in[:, p[1]:p[2]]
    src = np.zeros((PROJ_COL_TILE,), np.int32)
    valid = np.zeros((PROJ_COL_TILE,), bool)
    for g in range(ATT_GROUPS):
        for br in range(N_BRANCH):
            for hp in range(ATT_HPG):
                src[g * LANES + br * ATT_HPG + hp] = br * ATT_HEADS + g * ATT_HPG + hp
                valid[g * LANES + br * ATT_HPG + hp] = True
    gate = jnp.where(jnp.asarray(valid)[None, :], a_g[:, src], 0.0)
    return jnp.concatenate([w_in[:, :p[1]], w_in[:, p[2]:], gate], axis=1).astype(BF16)


def _pack_kv(k, v):
    parts = []
    for g in range(ATT_GROUPS):
        parts.append(k[:, g * HEAD_DIM:(g + 1) * HEAD_DIM])
        parts.append(v[:, g * HEAD_DIM:(g + 1) * HEAD_DIM])
    return jnp.concatenate(parts, axis=1).astype(BF16)


def _proj_kernel(x_ref, gain_ref, w_ref, bd_ref, qg_ref, kg_ref, lbp_ref,
                 q_ref, cmp_ref, slc_ref, slcp_ref, win_ref, winp_ref,
                 hq_ref, lf_ref, hk_ref, hv_ref, hog_ref, gate_ref, cmps_ref, slcs_ref, wins_ref, xn_ref):
    c = pl.program_id(1)
    last_tile = pl.program_id(0) == pl.num_programs(0) - 1
    tail = x_ref.shape[0] - cmps_ref.shape[0]

    @pl.when(c == 0)
    def _():
        x = x_ref[...]
        y = x * lax.rsqrt(jnp.mean(x * x, axis=-1, keepdims=True) + EPS)
        xn_ref[...] = (y * gain_ref[...]).astype(BF16)

    acc = jnp.dot(xn_ref[...], w_ref[...], preferred_element_type=F32)
    half = KV_WIDTH

    @pl.when(c < PB_CMP)
    def _():
        q_ref[...] = _group_rms(acc, bd_ref[...], qg_ref[...], HEAD_DIM)

    @pl.when(c == PB_CMP)
    def _():
        cmp_ref[...] = acc

        @pl.when(last_tile)
        def _():
            cmps_ref[...] = acc[tail:]

    def kv_branch(rows_ref, sample_ref, pack_ref, br):
        k = _group_rms(acc[:, :half], bd_ref[:half, :half], kg_ref[br - 1:br, :], HEAD_DIM)
        v = acc[:, half:]
        rows_ref[:, :half] = k
        rows_ref[:, half:] = v
        pack_ref[...] = _pack_kv(k, v)

        @pl.when(last_tile)
        def _():
            sample_ref[:, :half] = k[tail:]
            sample_ref[:, half:] = v[tail:]

    @pl.when(c == PB_SLC)
    def _():
        kv_branch(slc_ref, slcs_ref, slcp_ref, 1)

    @pl.when(c == PB_WIN)
    def _():
        kv_branch(win_ref, wins_ref, winp_ref, 2)

    @pl.when((c >= PB_HQ) & (c < PB_HF))
    def _():
        hq_ref[...] = acc

    @pl.when((c >= PB_HF) & (c < PB_HI))
    def _():
        p = lbp_ref[...]
        e = jnp.exp(p - jnp.max(p, axis=0, keepdims=True))
        lb = e[0:1, :] / jnp.sum(e, axis=0, keepdims=True)
        lf_ref[...] = jnp.log(lb + (1.0 - lb) * jax.nn.sigmoid(acc))
        hk_ref[...] = (1.0 - lb) * jax.nn.sigmoid(-acc)

    @pl.when((c >= PB_HI) & (c < PB_HG))
    def _():
        hv_ref[...] = acc

    @pl.when((c >= PB_HG) & (c < PB_GATE))
    def _():
        hog_ref[...] = acc

    @pl.when(c == PB_GATE)
    def _():
        gate_ref[...] = jax.nn.sigmoid(acc)


def _project_all(y, gain, w_perm, q_gain, k_gain, lb_logits):
    n, d = y.shape
    tm, tc = PROJ_ROW_TILE, PROJ_COL_TILE
    assert n == N_TOKENS and n % tm == 0 and N_PROMPT % tm == tm - N_SAMPLE and DEPTH == 1
    bd = jnp.asarray(_block_diag_ones(tc, HEAD_DIM), BF16)

    def two(first):
        return lambda i, c: (i, jnp.clip(c - first, 0, 1))

    one = lambda i, c: (i, 0)
    wide = lambda dt: jax.ShapeDtypeStruct((n, 2 * tc), dt)
    narrow = lambda dt: jax.ShapeDtypeStruct((n, tc), dt)
    prompt = jax.ShapeDtypeStruct((N_PROMPT, tc), F32)
    sample = jax.ShapeDtypeStruct((N_SAMPLE, tc), F32)
    out_shape = (wide(F32), prompt, prompt, narrow(BF16), prompt, narrow(BF16),
                 wide(F32), wide(F32), wide(F32), wide(F32), wide(F32), narrow(F32), sample, sample, sample)
    blk = lambda f: pl.BlockSpec((tm, tc), f)
    tail = pl.BlockSpec((N_SAMPLE, tc), lambda i, c: (0, 0))
    out_specs = (blk(two(PB_Q)), blk(one), blk(one), blk(one), blk(one), blk(one),
                 blk(two(PB_HQ)), blk(two(PB_HF)), blk(two(PB_HF)), blk(two(PB_HI)), blk(two(PB_HG)), blk(one),
                 tail, tail, tail)
    return pl.pallas_call(
        _proj_kernel,
        grid=(n // tm, PROJ_N_BLOCKS),
        in_specs=[
            pl.BlockSpec((tm, d), lambda i, c: (i, 0)),
            pl.BlockSpec((1, d), lambda i, c: (0, 0)),
            pl.BlockSpec((d, tc), lambda i, c: (0, c)),
            pl.BlockSpec((tc, tc), lambda i, c: (0, 0)),
            pl.BlockSpec((1, tc), lambda i, c: (0, jnp.clip(c, 0, 1))),
            pl.BlockSpec((2, KV_WIDTH), lambda i, c: (0, 0)),
            pl.BlockSpec((DEPTH + 1, tc), lambda i, c: (0, jnp.clip(c - PB_HF, 0, 1))),
        ],
        out_specs=out_specs,
        out_shape=out_shape,
        scratch_shapes=[pltpu.VMEM((tm, d), BF16)],
        compiler_params=_cparams("arbitrary", "arbitrary"),
        name="proj",
    )(y, gain.reshape(1, d), w_perm, bd, q_gain.reshape(1, ATT_WIDTH),
      k_gain[1:].reshape(2, KV_WIDTH), lb_logits)


def _hgrn_kernel(q_ref, k_ref, v_ref, lf_ref, tri_ref, ones_ref, s0_ref, o_ref, st_ref,
                 qe_scr, kd_scr, vt_scr, dec_scr, oi_scr, b_scr, *, t_len, ch, has_state):
    nj = t_len // ch
    shape3 = (nj, ch, HG_KEY)
    q3, k3, v3, lf3 = q_ref[...], k_ref[...], v_ref[...], lf_ref[...]
    tl = lax.broadcasted_iota(jnp.int32, shape3, 1)

    def row(x3, s):
        return jnp.broadcast_to(x3[:, s:s + 1, :], shape3)

    if t_len >= 256:
        lf2 = lf3.reshape(t_len, HG_KEY)
        parts = []
        for r0 in range(0, t_len, 256):
            x = lf2[r0:r0 + 256]
            hi = x.astype(BF16)
            r1 = x - hi.astype(F32)
            mid = r1.astype(BF16)
            lo = (r1 - mid.astype(F32)).astype(BF16)
            tri = tri_ref[...]
            parts.append(jnp.dot(tri, hi, preferred_element_type=F32)
                         + jnp.dot(tri, mid, preferred_element_type=F32)
                         + jnp.dot(tri, lo, preferred_element_type=F32))
        b3 = jnp.concatenate(parts, axis=0).reshape(shape3)
    else:
        b3 = jnp.zeros(shape3, F32)
        for s in range(ch):
            b3 = b3 + jnp.where(tl >= s, row(lf3, s), 0.0)

    bl3 = row(b3, ch - 1)
    qe3 = q3 * jnp.exp(b3)
    kd3 = k3 * jnp.exp(bl3 - b3)
    dec_scr[...] = jnp.exp(b3[:, ch - 1:ch, :])
    blocked = t_len % LANES == 0
    if blocked:
        nb = t_len // LANES
        qe2, kd2, v2 = (x.reshape(t_len, HG_KEY) for x in (qe3, kd3, v3))
        for m in range(nb):
            blk = slice(m * LANES, (m + 1) * LANES)
            qe_scr[m] = qe2[blk].T.astype(BF16)
            vt_scr[m] = v2[blk].T.astype(BF16)
            kd_scr[m] = kd2[blk].astype(BF16)
    else:
        qe_scr[...] = qe3.astype(BF16)
        kd_scr[...] = kd3.astype(BF16)

    ones = ones_ref[...]
    n8 = ch // 8
    shape8 = (nj, 8, HG_KEY)
    sub8 = lax.broadcasted_iota(jnp.int32, shape8, 1)
    q5, b5 = (x.reshape(nj, n8, 8, HG_KEY) for x in (q3, b3))
    b_scr[...] = b3
    od = [jnp.zeros(shape8, F32) for _ in range(n8)]
    for s in range(ch):
        hs, ss = divmod(s, 8)
        ks, bs, vs = (jnp.broadcast_to(r[:, s:s + 1, :], shape8) for r in (k_ref, b_scr, v_ref))
        for hh in range(hs, n8):
            diff = b5[:, hh] - bs
            if hh == hs:
                diff = jnp.where(sub8 >= ss, diff, NEG)
            w = q5[:, hh] * ks * jnp.exp(diff)
            a = jnp.dot(w.reshape(nj * 8, HG_KEY).astype(BF16), ones, preferred_element_type=F32)
            od[hh] = od[hh] + a.reshape(shape8) * vs
    for hh in range(n8):
        o_ref[:, hh * 8:(hh + 1) * 8, :] = od[hh]

    if has_state:
        st0 = s0_ref[0, 0].T
    else:
        st0 = jnp.zeros((HG_VAL, HG_KEY), F32)

    if blocked:
        cpb = LANES // ch
        lane_chunk = lax.broadcasted_iota(jnp.int32, (HG_KEY, LANES), 1) >> (ch.bit_length() - 1)
        keep = [jnp.where(lane_chunk == r, 1.0, 0.0).astype(BF16) for r in range(cpb)]

        def body(m, st):
            qet, vt, kd = qe_scr[m], vt_scr[m], kd_scr[m]
            ot = jnp.zeros((HG_VAL, LANES), F32)
            for r in range(cpb):
                ot = ot + jnp.dot(st.astype(BF16), qet * keep[r], preferred_element_type=F32)
                ut = jnp.dot(vt * keep[r], kd, preferred_element_type=F32)
                st = st * dec_scr[m * cpb + r] + ut
            oi_scr[m] = ot
            return st

        st = lax.fori_loop(0, t_len // LANES, body, st0)
        for m in range(t_len // LANES):
            o_ref[m * cpb:(m + 1) * cpb] = o_ref[m * cpb:(m + 1) * cpb] + oi_scr[m].T.reshape(cpb, ch, HG_VAL)
    else:
        def body(j, st):
            oi_scr[j] = lax.dot_general(qe_scr[j], st.astype(BF16), NT_DIMS, preferred_element_type=F32)
            ut = lax.dot_general(v_ref[j].astype(BF16), kd_scr[j], (((0,), (0,)), ((), ())),
                                 preferred_element_type=F32)
            return st * dec_scr[j] + ut

        st = lax.fori_loop(0, nj, body, st0, unroll=True)
        o_ref[...] = o_ref[...] + oi_scr[...]
    st_ref[0, 0] = st.T


def _hgrn(hq, hk, hv, lf, s0, *, n_seq, t_len, ch, row0):
    n = hq.shape[0]
    nj = t_len // ch
    assert row0 % t_len == 0 and t_len % ch == 0 and ch % 8 == 0
    blk0 = row0 // t_len
    r3 = lambda a: a.reshape(n // ch, ch, HG_WIDTH)
    has_state = s0 is not None
    if not has_state:
        s0 = jnp.zeros((1, 1, HG_KEY, HG_VAL), F32)
    tri_n = 256 if t_len >= 256 else 8
    i = np.arange(tri_n)
    tri = jnp.asarray(((i[:, None] // ch == i[None, :] // ch) & (i[:, None] >= i[None, :])).astype(np.float32), BF16)
    ones = jnp.ones((HG_KEY, HG_KEY), BF16)
    blk_shape = (t_len // LANES, LANES, HG_KEY) if t_len % LANES == 0 else (nj, ch, HG_KEY)
    hps = HG_HEADS if nj == 1 else 1
    seq = pl.BlockSpec((nj, ch, hps * HG_KEY), lambda b, h: (blk0 + b, 0, h))
    state_in = pl.BlockSpec((1, hps, HG_KEY, HG_VAL), (lambda b, h: (b, h, 0, 0)) if has_state else (lambda b, h: (0, 0, 0, 0)))

    def body(q_ref, k_ref, v_ref, lf_ref, tri_ref, ones_ref, s0_ref, o_ref, st_ref, *scratch):
        for h in range(hps):
            ln = slice(h * HG_KEY, (h + 1) * HG_KEY)
            _hgrn_kernel(q_ref.at[:, :, ln], k_ref.at[:, :, ln], v_ref.at[:, :, ln], lf_ref.at[:, :, ln], tri_ref,
                         ones_ref, s0_ref.at[:, h:h + 1], o_ref.at[:, :, ln], st_ref.at[:, h:h + 1], *scratch,
                         t_len=t_len, ch=ch, has_state=has_state)

    o, st = pl.pallas_call(
        body,
        grid=(n_seq, HG_HEADS // hps),
        in_specs=[seq, seq, seq, seq,
                  pl.BlockSpec((tri_n, tri_n), lambda b, h: (0, 0)),
                  pl.BlockSpec((HG_KEY, HG_KEY), lambda b, h: (0, 0)),
                  state_in],
        out_specs=(pl.BlockSpec((nj, ch, hps * HG_VAL), lambda b, h: (b, 0, h)),
                   pl.BlockSpec((1, hps, HG_KEY, HG_VAL), lambda b, h: (b, h, 0, 0))),
        out_shape=(jax.ShapeDtypeStruct((n_seq * nj, ch, HG_WIDTH), F32),
                   jax.ShapeDtypeStruct((n_seq, HG_HEADS, HG_KEY, HG_VAL), F32)),
        scratch_shapes=[pltpu.VMEM(blk_shape, BF16), pltpu.VMEM(blk_shape, BF16), pltpu.VMEM(blk_shape, BF16),
                        pltpu.VMEM((nj, 1, HG_KEY), F32), pltpu.VMEM(blk_shape, F32),
                        pltpu.VMEM((nj, ch, HG_KEY), F32)],
        compiler_params=_cparams("arbitrary", "arbitrary"),
        name="hgrn",
    )(r3(hq), r3(hk), r3(hv), r3(lf), tri, ones, s0)
    return o.reshape(n_seq * t_len, HG_WIDTH), st


def _mixout_kernel(y_ref, oap_ref, oas_ref, ohp_ref, ohs_ref, og_ref, ag_ref, hgain_ref, w_ref, o_ref):
    is_sample = pl.program_id(0) == pl.num_programs(0) - 1
    oa = jnp.where(is_sample, oas_ref[...], oap_ref[...])
    a = oa * lax.rsqrt(jnp.mean(oa * oa, axis=-1, keepdims=True) + EPS) * ag_ref[...]
    oh = jnp.where(is_sample, ohs_ref[...], ohp_ref[...])
    hs = []
    for h in range(HG_HEADS):
        x = oh[:, h * HG_VAL:(h + 1) * HG_VAL]
        hs.append(x * lax.rsqrt(jnp.mean(x * x, axis=-1, keepdims=True) + EPS))
    og = og_ref[...]
    hh = jnp.concatenate(hs, axis=1) * hgain_ref[...] * (og * jax.nn.sigmoid(og))
    m = (jnp.dot(a.astype(BF16), w_ref[:ATT_WIDTH, :], preferred_element_type=F32)
         + jnp.dot(hh.astype(BF16), w_ref[ATT_WIDTH:, :], preferred_element_type=F32))
    o_ref[...] = y_ref[...] + m


def _mixer_out(y, o_att_p, o_att_s, o_hg_p, o_hg_s, og, attn_gain, hg_gain, w_out):
    n, d = y.shape
    tm = OUT_ROW_TILE
    n_p = o_att_p.shape[0]
    assert n % tm == 0 and n_p % tm == 0 and o_att_s.shape[0] == tm and n == n_p + tm
    row = lambda w: pl.BlockSpec((tm, w), lambda i: (i, 0))
    prompt_row = lambda w: pl.BlockSpec((tm, w), lambda i: (jnp.minimum(i, n_p // tm - 1), 0))
    const = lambda s: pl.BlockSpec(s, lambda i: (0, 0))
    return pl.pallas_call(
        _mixout_kernel,
        grid=(n // tm,),
        in_specs=[row(d), prompt_row(ATT_WIDTH), const((tm, ATT_WIDTH)), prompt_row(HG_WIDTH), const((tm, HG_WIDTH)),
                  row(HG_WIDTH), const((1, ATT_WIDTH)), const((1, HG_WIDTH)), const((MIX_WIDTH, d))],
        out_specs=row(d),
        out_shape=jax.ShapeDtypeStruct((n, d), F32),
        compiler_params=_cparams("arbitrary"),
        name="mixout",
    )(y, o_att_p, o_att_s, o_hg_p, o_hg_s, og, attn_gain.reshape(1, ATT_WIDTH), hg_gain.reshape(1, HG_WIDTH), w_out)


def _bias_kernel(tbl_ref, o_ref, *, a0, ar, ac, rows_blk):
    h = pl.program_id(0)
    rb = pl.program_id(1)
    shape = o_ref.shape[1:]
    r = lax.broadcasted_iota(jnp.int32, shape, 0) + rb * rows_blk
    c = lax.broadcasted_iota(jnp.int32, shape, 1)
    n = a0 + ar * r + ac * c
    out = jnp.full(shape, tbl_ref[0, h], F32)
    for k in range(1, N_BUCKETS):
        out = jnp.where(n >= T5_THRESHOLDS[k], tbl_ref[k, h], out)
    o_ref[0] = out


def _bias_table(tbl, rows, cols, a0, ar, ac):
    rows_blk = max(r for r in range(8, min(rows, 512) + 1, 8) if rows % r == 0)
    assert cols % LANES == 0
    return pl.pallas_call(
        functools.partial(_bias_kernel, a0=a0, ar=ar, ac=ac, rows_blk=rows_blk),
        grid=(ATT_HEADS, rows // rows_blk),
        in_specs=[pl.BlockSpec(memory_space=pltpu.SMEM)],
        out_specs=pl.BlockSpec((1, rows_blk, cols), lambda h, rb: (h, rb, 0)),
        out_shape=jax.ShapeDtypeStruct((ATT_HEADS, rows, cols), F32),
        compiler_params=_cparams("arbitrary", "arbitrary"),
        name="t5_bias",
    )(tbl)


def _compress_weights(w1, b1, w2, b2):
    w = w1.reshape(2, CMP_RATIO, CMP_STRIDE, HEAD_DIM, CMP_HIDDEN)
    z = jnp.zeros_like(w)
    top = jnp.concatenate([w, z], axis=-1)
    bot = jnp.concatenate([z, w], axis=-1)
    wpad = jnp.stack([top, bot], axis=3)
    wpad = wpad.reshape(2 * CMP_RATIO, CMP_STRIDE * LANES, 2 * CMP_HIDDEN).astype(BF16)
    z2 = jnp.zeros_like(w2)
    w2pad = jnp.concatenate([jnp.concatenate([w2, z2], axis=-1), jnp.concatenate([z2, w2], axis=-1)], axis=1)
    return wpad, jnp.concatenate([b1, b1], axis=-1), w2pad.astype(BF16), jnp.concatenate([b2, b2], axis=-1)


def _compress_kernel(rows_ref, perm_ref, wpad_ref, b1_ref, w2pad_ref, b2_ref, kg_ref, bd_ref, out_ref, xs_ref):
    u = pl.program_id(1)
    n_chunk = CMP_UNITS * CMP_UNIT_ROWS // CMP_STRIDE
    perm = perm_ref[...]

    for t in range(CMP_UNIT_ROWS // 256):
        x = rows_ref[t * 256:(t + 1) * 256, :]
        y = jnp.dot(perm, x.astype(BF16), preferred_element_type=F32).astype(BF16)
        base = pl.multiple_of(u * (CMP_UNIT_ROWS // CMP_STRIDE) + t * 16, 16)
        for s in range(CMP_STRIDE):
            xs_ref[s, pl.ds(base, 16), :] = y[s * 16:(s + 1) * 16, :]

    @pl.when(u == CMP_UNITS - 1)
    def _():
        halves = []
        for kv in range(2):
            outs = []
            for j in range(2):
                col = (kv * 2 + j) * LANES
                lhs = jnp.concatenate([xs_ref[s, :, col:col + LANES] for s in range(CMP_STRIDE)], axis=1)
                h0 = jnp.dot(lhs, wpad_ref[kv * CMP_RATIO + 0], preferred_element_type=F32)
                h1 = jnp.dot(lhs, wpad_ref[kv * CMP_RATIO + 1], preferred_element_type=F32)
                h = b1_ref[kv:kv + 1, :] + h0 + pltpu.roll(h1, n_chunk - 1, axis=0)
                a = (h * jax.nn.sigmoid(h)).astype(BF16)
                outs.append(jnp.dot(a, w2pad_ref[kv], preferred_element_type=F32) + b2_ref[kv:kv + 1, :])
            halves.append(jnp.concatenate(outs, axis=1))
        pk = _pack_kv(_group_rms(halves[0], bd_ref[...], kg_ref[...], HEAD_DIM), halves[1])
        for g in range(ATT_GROUPS):
            out_ref[0, g] = pk[:, g * LANES:(g + 1) * LANES]


def _compress(rows, cw, k_gain0):
    wpad, b1, w2pad, b2 = cw
    n_chunk = CMP_UNITS * CMP_UNIT_ROWS // CMP_STRIDE
    i = np.arange(256)
    perm = np.zeros((256, 256), np.float32)
    perm[(i % 16) * 16 + i // 16, i] = 1.0
    consts = (jnp.asarray(perm, BF16), wpad, b1, w2pad, b2, k_gain0.reshape(1, KV_WIDTH),
              jnp.asarray(_block_diag_ones(KV_WIDTH, HEAD_DIM), BF16))
    const = lambda a: pl.BlockSpec(a.shape, lambda b, u, nd=a.ndim: (0,) * nd)
    return pl.pallas_call(
        _compress_kernel,
        grid=(1, CMP_UNITS),
        in_specs=[pl.BlockSpec((CMP_UNIT_ROWS, KV_ROW), lambda b, u: (u, 0))] + [const(a) for a in consts],
        out_specs=pl.BlockSpec((1, ATT_GROUPS, n_chunk, LANES), lambda b, u: (0, 0, 0, 0)),
        out_shape=jax.ShapeDtypeStruct((1, ATT_GROUPS, n_chunk, LANES), BF16),
        scratch_shapes=[pltpu.VMEM((CMP_STRIDE, n_chunk, KV_ROW), BF16)],
        compiler_params=_cparams("arbitrary", "arbitrary"),
        name="compress",
    )(rows, *consts)


PAGED_UNIT_ROWS = 4096
PAGED_UNITS = PAST_LEN // PAGED_UNIT_ROWS
PAGES_PER_STEP = PAGED_UNIT_ROWS // PAGE_SIZE


def _cache_view(cache):
    return jnp.transpose(cache, (0, 2, 3, 4, 1))


def _compress_paged_kernel(pt_ref, *refs):
    del pt_ref
    page_refs = refs[:PAGES_PER_STEP]
    pick_ref, w1_ref, b1_ref, w2_ref, b2_ref, kg_ref, bd_ref, out_ref, xs_ref = refs[PAGES_PER_STEP:]
    u = pl.program_id(1)
    n_chunk = PAST_LEN // CMP_STRIDE
    pick = pick_ref[...]
    n_kvg = 2 * ATT_GROUPS

    for t in range(PAGES_PER_STEP // 2):
        base = pl.multiple_of(u * (PAGED_UNIT_ROWS // CMP_STRIDE) + t * 16, 16)
        kt = jnp.concatenate([page_refs[2 * t][0].reshape(n_kvg * HEAD_DIM, PAGE_SIZE),
                              page_refs[2 * t + 1][0].reshape(n_kvg * HEAD_DIM, PAGE_SIZE)], axis=1).astype(BF16)
        z = jnp.dot(kt, pick, preferred_element_type=F32)
        for kvg in range(n_kvg):
            zk = z[kvg * HEAD_DIM:(kvg + 1) * HEAD_DIM]
            y = jnp.concatenate([zk[:, :LANES], zk[:, LANES:]], axis=0).T.astype(BF16)
            for s2 in range(CMP_STRIDE // 2):
                xs_ref[kvg, s2, pl.ds(base, 16), :] = y[s2 * 16:(s2 + 1) * 16, :]

    @pl.when(u == PAGED_UNITS - 1)
    def _():
        halves = []
        for kv in range(2):
            lhs = jnp.concatenate(
                [jnp.concatenate([xs_ref[kv * ATT_GROUPS + g, s2] for s2 in range(CMP_STRIDE // 2)], axis=1)
                 for g in range(ATT_GROUPS)], axis=0)
            h0 = jnp.dot(lhs, w1_ref[kv * CMP_RATIO + 0], preferred_element_type=F32)
            h1 = jnp.dot(lhs, w1_ref[kv * CMP_RATIO + 1], preferred_element_type=F32)
            h = b1_ref[kv:kv + 1, :] + h0 + pltpu.roll(h1, ATT_GROUPS * n_chunk - 1, axis=0)
            a = (h * jax.nn.sigmoid(h)).astype(BF16)
            o = b2_ref[kv:kv + 1, :]
            for g in range(ATT_GROUPS):
                o = o + jnp.dot(a[g * n_chunk:(g + 1) * n_chunk], w2_ref[kv * ATT_GROUPS + g], preferred_element_type=F32)
            halves.append(o)
        out_ref[0, :, :KV_WIDTH] = _group_rms(halves[0], bd_ref[...], kg_ref[...], HEAD_DIM).astype(BF16)
        out_ref[0, :, KV_WIDTH:] = halves[1].astype(BF16)


def _compress_paged(cache_t, page_table, w1, b1, w2, b2, k_gain0):
    n_seq = page_table.shape[0]
    n_chunk = PAST_LEN // CMP_STRIDE
    r = np.arange(LANES)
    s2, c = r // 16, r % 16
    pick = np.zeros((2 * PAGE_SIZE, 2, LANES), np.float32)
    for half in range(2):
        pick[CMP_STRIDE * c + 2 * s2 + half, half, r] = 1.0
    w2p = jnp.zeros((2, ATT_GROUPS, CMP_HIDDEN, KV_WIDTH), F32)
    for g in range(ATT_GROUPS):
        w2p = w2p.at[:, g, :, g * HEAD_DIM:(g + 1) * HEAD_DIM].set(w2)
    consts = (jnp.asarray(pick.reshape(2 * PAGE_SIZE, 2 * LANES), BF16),
              w1.reshape(2 * CMP_RATIO, CMP_STRIDE * HEAD_DIM, CMP_HIDDEN).astype(BF16), b1,
              w2p.reshape(2 * ATT_GROUPS, CMP_HIDDEN, KV_WIDTH).astype(BF16), jnp.tile(b2, (1, ATT_GROUPS)),
              k_gain0.reshape(1, KV_WIDTH), jnp.asarray(_block_diag_ones(KV_WIDTH, HEAD_DIM), BF16))
    const = lambda a: pl.BlockSpec(a.shape, lambda b, u, pt, nd=a.ndim: (0,) * nd)
    pages = [pl.BlockSpec((1, 2, ATT_GROUPS, HEAD_DIM, PAGE_SIZE),
                          lambda b, u, pt, k=k: (pt[b, u * PAGES_PER_STEP + k], 0, 0, 0, 0))
             for k in range(PAGES_PER_STEP)]
    grid_spec = pltpu.PrefetchScalarGridSpec(
        num_scalar_prefetch=1, grid=(n_seq, PAGED_UNITS),
        in_specs=pages + [const(a) for a in consts],
        out_specs=pl.BlockSpec((1, n_chunk, KV_ROW), lambda b, u, pt: (b, 0, 0)),
        scratch_shapes=[pltpu.VMEM((2 * ATT_GROUPS, CMP_STRIDE // 2, n_chunk, LANES), BF16)])
    return pl.pallas_call(
        _compress_paged_kernel, grid_spec=grid_spec,
        out_shape=jax.ShapeDtypeStruct((n_seq, n_chunk, KV_ROW), BF16),
        compiler_params=_cparams("arbitrary", "arbitrary"),
        name="compress_paged",
    )(page_table, *([cache_t] * PAGES_PER_STEP), *consts)


def _block_overlap(n_cmp, n_blk):
    cs = np.arange(n_cmp)[:, None] * CMP_STRIDE
    bs = np.arange(n_blk)[None, :] * SEL_BLOCK
    ov = np.minimum(cs + CMP_BLOCK, bs + SEL_BLOCK) - np.maximum(cs, bs)
    return (np.clip(ov, 0, None) / CMP_BLOCK).astype(np.float32)


def _overlap_padded(n_cmp, n_blk, rows, cols):
    ov = np.zeros((rows, cols), np.float32)
    ov[:n_cmp, :n_blk] = _block_overlap(n_cmp, n_blk)
    return jnp.asarray(ov, BF16)


def _softmax_rows(s, mask):
    s = jnp.where(mask, s, NEG)
    m = jnp.max(s, axis=1, keepdims=True)
    e = jnp.where(mask, jnp.exp(s - m), 0.0)
    d = jnp.sum(e, axis=1, keepdims=True)
    return e / jnp.where(d > 0, d, 1.0)


def _select_blocks(imp, qpos, n_blk, axis):
    blk = lax.broadcasted_iota(jnp.int32, imp.shape, axis)
    cur = qpos >> SEL_SHIFT
    forced = (blk == 0) | (blk == cur) | (blk == cur - 1)
    valid = blk * SEL_BLOCK <= qpos
    score = jnp.where(forced, ATT_HPG + 1.0, jnp.where(valid, imp, -1.0))
    ahead = jnp.zeros(imp.shape, F32)
    for m in range(n_blk):
        sm = jnp.broadcast_to(score[m:m + 1, :] if axis == 0 else score[:, m:m + 1], imp.shape)
        tie = jnp.where(blk > m, 1.0, 0.0)
        ahead = ahead + jnp.where(sm > score, 1.0, jnp.where(sm == score, tie, 0.0))
    return jnp.where(blk < n_blk, jnp.where(ahead < N_SEL, 1.0, 0.0), 0.0)


SEL_SHIFT = SEL_BLOCK.bit_length() - 1
NT_DIMS = (((1,), (1,)), ((), ()))


SLC_UNROLL = 4
TOEP_TILES = min(SEQ // LANES, -(-(T5_THRESHOLDS[-1] + LANES - 1) // LANES) + 1)


NSA_GROUPS_PER_STEP = 4


def _nsa_prompt_multi_kernel(q_ref, gate_ref, kvc_ref, slc_ref, win_ref, bcmp_ref, toep_ref, ov_ref, spread_ref,
                             o_ref, s_scr, m_scr, l_scr, acc_scr):
    i = pl.program_id(2)
    ngs = NSA_GROUPS_PER_STEP
    tile = (Q_BLOCK, LANES)
    rows = ATT_HPG * Q_BLOCK
    lane = lax.broadcasted_iota(jnp.int32, tile, 1)
    sub = lax.broadcasted_iota(jnp.int32, tile, 0)
    low = lane < HEAD_DIM
    qpos = i * Q_BLOCK + sub
    head = lambda x, p: x[p * Q_BLOCK:(p + 1) * Q_BLOCK]
    heads = range(ATT_HPG)
    n_blk = SEQ // SEL_BLOCK
    blk_row = lax.broadcasted_iota(jnp.int32, (n_blk, LANES), 0)
    blk_of_key = lax.broadcasted_iota(jnp.int32, (n_blk, LANES), 1) >> SEL_SHIFT
    qpos_t = i * Q_BLOCK + lax.broadcasted_iota(jnp.int32, (n_blk, Q_BLOCK), 1)
    last_tile = SEQ // LANES - 1
    ov = ov_ref[...]
    glanes = lambda gg: slice(gg * LANES, (gg + 1) * LANES)

    def key_tile(kv_ref, gg, j):
        jc = jnp.clip(j, 0, last_tile)
        return kv_ref[pl.ds(pl.multiple_of(jc * LANES, LANES), LANES), glanes(gg)]

    def masked_scores(qs, gg, kv, j, mask):
        sc = lax.dot_general(qs, kv, NT_DIMS, preferred_element_type=F32)
        row0 = pl.multiple_of(jnp.clip(i - j, 0, TOEP_TILES - 1) * LANES, LANES)
        return [jnp.where(mask, head(sc, p) + toep_ref[gg * ATT_HPG + p, pl.ds(row0, LANES), :], NEG) for p in heads]

    def weights_times_v(scores, m, kv):
        es = [jnp.exp(scores[p] - m[p]) for p in heads]
        return es, jnp.dot(jnp.concatenate(es, axis=0).astype(BF16), kv, preferred_element_type=F32)

    def front(gg):
        qa = q_ref[:, gg * ATT_HPG * HEAD_DIM:(gg + 1) * ATT_HPG * HEAD_DIM] * (HEAD_DIM ** -0.5)
        qp = []
        for pair in range(ATT_HPG // 2):
            x = qa[:, pair * LANES:(pair + 1) * LANES]
            qp.append(jnp.where(low, x, 0.0))
            qp.append(jnp.where(low, pltpu.roll(x, HEAD_DIM, axis=1), 0.0))
        qs = jnp.concatenate(qp, axis=0).astype(BF16)

        kvc = kvc_ref[0, gg]
        s = lax.dot_general(qs, kvc, NT_DIMS, preferred_element_type=F32)
        s = s + bcmp_ref[gg * ATT_HPG:(gg + 1) * ATT_HPG].reshape(rows, LANES)
        cmask1 = (lane * CMP_STRIDE + (CMP_BLOCK - 1)) <= qpos
        pr = _softmax_rows(s, jnp.concatenate([cmask1] * ATT_HPG, axis=0))
        o_cmp = jnp.dot(pr.astype(BF16), kvc, preferred_element_type=F32)
        hi, lo = _split2(sum(head(pr, p) for p in heads))
        imp = jnp.dot(hi, ov, preferred_element_type=F32) + jnp.dot(lo, ov, preferred_element_type=F32)
        sel_t = _select_blocks(imp.T[:n_blk, :], qpos_t, n_blk, axis=0).astype(BF16)

        n_win = WINDOW // LANES + 1
        win_kv, win_s = [], []
        for w in range(n_win):
            j = i - (n_win - 1) + w
            dist = (i - j) * LANES + sub - lane
            inside = jnp.where(dist >= 0, jnp.where(dist < WINDOW, 1.0, 0.0), 0.0)
            win_kv.append(key_tile(win_ref, gg, j))
            win_s.append(masked_scores(qs, gg, win_kv[w], j, jnp.where(j >= 0, inside, 0.0) > 0.5))
        m_w = [jnp.max(functools.reduce(jnp.maximum, [win_s[w][p] for w in range(n_win)]), axis=1, keepdims=True)
               for p in heads]
        acc_w = jnp.zeros((rows, LANES), F32)
        l_w = [jnp.zeros(tile, F32) for _ in heads]
        for w in range(n_win):
            es, pv = weights_times_v(win_s[w], m_w, win_kv[w])
            acc_w = acc_w + pv
            l_w = [l_w[p] + es[p] for p in heads]
        o_win = [head(acc_w, p) / jnp.sum(l_w[p], axis=1, keepdims=True) for p in heads]
        return qs, o_cmp, sel_t, o_win

    fronts = [front(gg) for gg in range(ngs)]

    n_steps = (i + SLC_UNROLL) >> (SLC_UNROLL.bit_length() - 1)
    m_scr[...] = jnp.full(m_scr.shape, NEG, F32)

    def slc_scores(jj, carry):
        for gg in range(ngs):
            qs, _, sel_t, _ = fronts[gg]
            sps = []
            for r in range(SLC_UNROLL):
                j = SLC_UNROLL * jj + r
                dist = (i - j) * LANES + sub - lane
                expand = jnp.where(blk_row == 2 * j + blk_of_key, 1.0, 0.0).astype(BF16)
                picked = lax.dot_general(sel_t, expand, (((0,), (0,)), ((), ())), preferred_element_type=F32)
                sp = masked_scores(qs, gg, key_tile(slc_ref, gg, j), j, jnp.where(dist >= 0, picked, 0.0) > 0.5)
                for p in heads:
                    s_scr[j, gg * ATT_HPG + p] = sp[p]
                sps.append(sp)
            for p in heads:
                hp = gg * ATT_HPG + p
                m_scr[hp] = jnp.maximum(m_scr[hp], functools.reduce(jnp.maximum, [sp[p] for sp in sps]))
        return carry

    lax.fori_loop(0, n_steps, slc_scores, 0)
    for hp in range(ngs * ATT_HPG):
        m_scr[hp] = jnp.zeros(tile, F32) + jnp.max(m_scr[hp], axis=1, keepdims=True)
    l_scr[...] = jnp.zeros(l_scr.shape, F32)
    acc_scr[...] = jnp.zeros(acc_scr.shape, F32)

    def slc_weights(jj, carry):
        for gg in range(ngs):
            m = [m_scr[gg * ATT_HPG + p] for p in heads]
            pvs, ess = [], []
            for r in range(SLC_UNROLL):
                j = SLC_UNROLL * jj + r
                es, pv = weights_times_v([s_scr[j, gg * ATT_HPG + p] for p in heads], m, key_tile(slc_ref, gg, j))
                pvs.append(pv)
                ess.append(es)
            acc_scr[gg] += functools.reduce(jnp.add, pvs)
            for p in heads:
                hp = gg * ATT_HPG + p
                l_scr[hp] = l_scr[hp] + functools.reduce(jnp.add, [es[p] for es in ess])
        return carry

    lax.fori_loop(0, n_steps, slc_weights, 0)

    for gg in range(ngs):
        _, o_cmp, _, o_win = fronts[gg]
        g_hi, g_lo = _split2(gate_ref[:, glanes(gg)])
        spread = spread_ref[...]
        gb = jnp.dot(g_hi, spread, preferred_element_type=F32) + jnp.dot(g_lo, spread, preferred_element_type=F32)
        comb = []
        for p in heads:
            o_slc = head(acc_scr[gg], p) / jnp.sum(l_scr[gg * ATT_HPG + p], axis=1, keepdims=True)
            col = lambda br: gb[:, (br * ATT_HPG + p) * LANES:(br * ATT_HPG + p + 1) * LANES]
            comb.append(col(0) * head(o_cmp, p) + col(1) * o_slc + col(2) * o_win[p])
        for pair in range(ATT_HPG // 2):
            c0 = gg * ATT_HPG * HEAD_DIM + pair * LANES
            o_ref[:, c0:c0 + LANES] = jnp.where(low, pltpu.roll(comb[2 * pair], HEAD_DIM, axis=1), comb[2 * pair + 1])


def _nsa_prompt(q, gates, kvc, slc_pack, win_pack, bias_cmp, bias_toep, n_batch=BATCH):
    nqb = SEQ // Q_BLOCK
    ngs = NSA_GROUPS_PER_STEP
    ov = _overlap_padded(SEQ // CMP_STRIDE - 1, SEQ // SEL_BLOCK, LANES, LANES)
    gw = ngs * ATT_HPG * HEAD_DIM
    nh = ngs * ATT_HPG
    n_gate = N_BRANCH * ATT_HPG
    spread = np.zeros((LANES, n_gate, LANES), np.float32)
    spread[np.arange(n_gate), np.arange(n_gate), :] = 1.0
    spread = jnp.asarray(spread.reshape(LANES, n_gate * LANES), BF16)
    return pl.pallas_call(
        _nsa_prompt_multi_kernel,
        grid=(n_batch, ATT_GROUPS // ngs, nqb),
        in_specs=[
            pl.BlockSpec((Q_BLOCK, gw), lambda b, g, i: (b * nqb + i, g)),
            pl.BlockSpec((Q_BLOCK, ngs * LANES), lambda b, g, i: (b * nqb + i, g)),
            pl.BlockSpec((1, ngs, SEQ // CMP_STRIDE, LANES), lambda b, g, i: (0, g, b, 0)),
            pl.BlockSpec((SEQ, ngs * LANES), lambda b, g, i: (b, g)),
            pl.BlockSpec((SEQ, ngs * LANES), lambda b, g, i: (b, g)),
            pl.BlockSpec((nh, Q_BLOCK, LANES), lambda b, g, i: (g, i, 0)),
            pl.BlockSpec((nh, TOEP_TILES * LANES, LANES), lambda b, g, i: (g, 0, 0)),
            pl.BlockSpec((LANES, LANES), lambda b, g, i: (0, 0)),
            pl.BlockSpec((LANES, n_gate * LANES), lambda b, g, i: (0, 0)),
        ],
        out_specs=pl.BlockSpec((Q_BLOCK, gw), lambda b, g, i: (b * nqb + i, g)),
        out_shape=jax.ShapeDtypeStruct((n_batch * SEQ, ATT_WIDTH), F32),
        scratch_shapes=[pltpu.VMEM((nqb, nh, Q_BLOCK, LANES), F32),
                        pltpu.VMEM((nh, Q_BLOCK, LANES), F32), pltpu.VMEM((nh, Q_BLOCK, LANES), F32),
                        pltpu.VMEM((ngs, ATT_HPG * Q_BLOCK, LANES), F32)],
        compiler_params=_cparams("arbitrary", "arbitrary", "arbitrary"),
        name="nsa_prompt",
    )(q, gates, kvc, slc_pack, win_pack, bias_cmp, bias_toep, ov, spread)


SAMPLE_ROWS = ATT_HEADS * DEC_SEQ
SAMPLE_PAGES_PER_STEP = 32
SAMPLE_STEPS = N_PAGES // SAMPLE_PAGES_PER_STEP
SAMPLE_N_CMP = (PAST_LEN + DEC_SEQ - CMP_BLOCK) // CMP_STRIDE + 1
SAMPLE_N_BLK = -(-(PAST_LEN + DEC_SEQ) // SEL_BLOCK)
SAMPLE_BLK_LANES = 2 * LANES


def _nsa_sample_kernel(pt_ref, q_ref, gate_ref, kvc_ref, *rest):
    page_refs = rest[:SAMPLE_PAGES_PER_STEP]
    (slc_new_ref, win_state_ref, win_new_ref, bcmp_ref, bslc_ref, bwin_ref, perm_ref, ov_ref,
     o_ref, qbd_scr, picked_scr, m_scr, l_scr, acc_scr, ocmp_scr, owin_scr) = rest[SAMPLE_PAGES_PER_STEP:]
    del pt_ref
    u = pl.program_id(1)
    tile = (SAMPLE_ROWS, LANES)
    wide = (SAMPLE_ROWS, KV_WIDTH)
    lane = lax.broadcasted_iota(jnp.int32, tile, 1)
    t_row = lax.broadcasted_iota(jnp.int32, tile, 0) & (DEC_SEQ - 1)
    rows_per_group = ATT_HPG * DEC_SEQ
    own = ((lax.broadcasted_iota(jnp.int32, wide, 1) >> (HEAD_DIM.bit_length() - 1))
           == (lax.broadcasted_iota(jnp.int32, wide, 0) >> (rows_per_group.bit_length() - 1)))

    def reset():
        m_scr[...] = jnp.full(m_scr.shape, NEG, F32)
        l_scr[...] = jnp.zeros(l_scr.shape, F32)
        acc_scr[...] = jnp.zeros(acc_scr.shape, F32)

    def finish():
        l = jnp.sum(l_scr[...], axis=1, keepdims=True)
        return acc_scr[...] / jnp.where(l > 0, l, 1.0)

    def attend(tiles):
        qbd = qbd_scr[...]
        pairs = [tiles[n:n + 2] for n in range(0, len(tiles), 2)]
        side_by_side = lambda xs: (xs[0] if len(xs) == 1 else jnp.concatenate(xs, axis=1)).astype(BF16)
        scores = []
        m_el = None
        for grp in pairs:
            sc = jnp.dot(qbd, side_by_side([t[0] for t in grp]), preferred_element_type=F32)
            for n, (_, _, bias, mask) in enumerate(grp):
                s = jnp.where(mask, sc[:, n * LANES:(n + 1) * LANES] + bias, NEG)
                scores.append(s)
                m_el = s if m_el is None else jnp.maximum(m_el, s)
        m_prev = m_scr[...]
        m_new = jnp.maximum(m_prev, jnp.max(m_el, axis=1, keepdims=True))
        alpha = jnp.exp(m_prev - m_new)
        l_el = alpha * l_scr[...]
        acc = jnp.concatenate([alpha, alpha], axis=1) * acc_scr[...]
        for k, grp in enumerate(pairs):
            es = [jnp.exp(s - m_new) for s in scores[2 * k:2 * k + len(grp)]]
            l_el = l_el + functools.reduce(jnp.add, es)
            acc = acc + lax.dot_general(side_by_side(es), side_by_side([t[1] for t in grp]), NT_DIMS,
                                        preferred_element_type=F32)
        m_scr[...] = m_new
        l_scr[...] = l_el
        acc_scr[...] = acc

    def page_tile(kv4, bias, mask):
        return (kv4[0].reshape(KV_WIDTH, LANES), kv4[1].reshape(KV_WIDTH, LANES), bias, mask)

    def new_tile(ref, bias, mask):
        rows = jnp.concatenate([ref[...], jnp.zeros((LANES - DEC_SEQ, KV_ROW), F32)], axis=0)
        return (rows[:, :KV_WIDTH].T, rows[:, KV_WIDTH:].T, bias, mask)

    @pl.when(u == 0)
    def _():
        q = (q_ref[...] * (HEAD_DIM ** -0.5)).astype(BF16)
        qperm = jnp.dot(q, perm_ref[...], preferred_element_type=F32)
        qfull = jnp.concatenate([qperm[:, p * KV_WIDTH:(p + 1) * KV_WIDTH]
                                 for g in range(ATT_GROUPS) for p in range(ATT_HPG)], axis=0)
        qbd = jnp.where(own, qfull, 0.0).astype(BF16)
        qbd_scr[...] = qbd

        kvc = kvc_ref[0]
        s = lax.dot_general(qbd, kvc[:, :KV_WIDTH], NT_DIMS, preferred_element_type=F32) + bcmp_ref[...]
        cmask = lax.broadcasted_iota(jnp.int32, s.shape, 1) < SAMPLE_N_CMP
        pr = _softmax_rows(s, cmask)
        ocmp_scr[...] = jnp.dot(pr.astype(BF16), kvc[:, KV_WIDTH:], preferred_element_type=F32)
        ps = []
        for g in range(ATT_GROUPS):
            r0 = g * rows_per_group
            ps.append(sum(pr[r0 + p * DEC_SEQ:r0 + (p + 1) * DEC_SEQ, :] for p in range(ATT_HPG)))
        hi, lo = _split2(jnp.concatenate(ps, axis=0))
        ov = ov_ref[...]
        imp = jnp.dot(hi, ov, preferred_element_type=F32) + jnp.dot(lo, ov, preferred_element_type=F32)
        qpos = PAST_LEN + (lax.broadcasted_iota(jnp.int32, imp.shape, 0) & (DEC_SEQ - 1))
        sel = _select_blocks(imp, qpos, SAMPLE_N_BLK, axis=1)
        sel_rows = jnp.concatenate([sel[g * DEC_SEQ:(g + 1) * DEC_SEQ, :]
                                    for g in range(ATT_GROUPS) for p in range(ATT_HPG)], axis=0).astype(BF16)
        span = 4 * LANES
        blk_row = lax.broadcasted_iota(jnp.int32, (SAMPLE_BLK_LANES, span), 0)
        key_blk = lax.broadcasted_iota(jnp.int32, (SAMPLE_BLK_LANES, span), 1) >> SEL_SHIFT
        for c in range(PAST_LEN // span):
            expand = jnp.where(blk_row == key_blk + c * (span // SEL_BLOCK), 1.0, 0.0).astype(BF16)
            picked_scr[:, c * span:(c + 1) * span] = jnp.dot(sel_rows, expand, preferred_element_type=F32)

        reset()
        tiles = [page_tile(win_state_ref[0, :, :, :, w * LANES:(w + 1) * LANES],
                           bwin_ref[:, w * LANES:(w + 1) * LANES], (w * LANES + lane) > t_row)
                 for w in range(WINDOW // LANES)]
        tiles.append(new_tile(win_new_ref, bwin_ref[:, WINDOW:WINDOW + LANES], lane <= t_row))
        attend(tiles)
        owin_scr[...] = finish()
        reset()

    tiles = []
    for k in range(SAMPLE_PAGES_PER_STEP):
        keys = pl.ds(pl.multiple_of((u * SAMPLE_PAGES_PER_STEP + k) * LANES, LANES), LANES)
        tiles.append(page_tile(page_refs[k][0], bslc_ref[:, keys], picked_scr[:, keys] > 0.5))
    attend(tiles)

    @pl.when(u == SAMPLE_STEPS - 1)
    def _():
        attend([new_tile(slc_new_ref, bslc_ref[:, PAST_LEN:PAST_LEN + LANES], lane <= t_row)])
        o_slc = finish()
        gt = gate_ref[...]

        def gate_rows(br):
            cols = []
            for g in range(ATT_GROUPS):
                for p in range(ATT_HPG):
                    c = g * LANES + br * ATT_HPG + p
                    cols.append(jnp.broadcast_to(gt[:, c:c + 1], (DEC_SEQ, KV_WIDTH)))
            return jnp.concatenate(cols, axis=0)

        comb = gate_rows(0) * ocmp_scr[...] + gate_rows(1) * o_slc + gate_rows(2) * owin_scr[...]
        comb = jnp.where(own, comb, 0.0)
        per_head = []
        for p in range(ATT_HPG):
            per_head.append(sum(comb[(g * ATT_HPG + p) * DEC_SEQ:(g * ATT_HPG + p + 1) * DEC_SEQ, :]
                                for g in range(ATT_GROUPS)))
        hi, lo = _split2(jnp.concatenate(per_head, axis=1))
        perm = perm_ref[...]
        o_ref[...] = (lax.dot_general(hi, perm, NT_DIMS, preferred_element_type=F32)
                      + lax.dot_general(lo, perm, NT_DIMS, preferred_element_type=F32))


def _nsa_sample(q, gates, kvc, cache_slc, page_table, slc_rows, state_win, win_rows, bias_cmp, bias_slc, bias_win):
    n_seq = page_table.shape[0]
    row0 = N_PROMPT // DEC_SEQ
    new = lambda w: pl.BlockSpec((DEC_SEQ, w), lambda b, u, pt: (b, 0))
    src = np.arange(ATT_WIDTH)
    g, p, d = src // (ATT_HPG * HEAD_DIM), (src // HEAD_DIM) % ATT_HPG, src % HEAD_DIM
    perm = np.zeros((ATT_WIDTH, ATT_WIDTH), np.float32)
    perm[src, p * KV_WIDTH + g * HEAD_DIM + d] = 1.0
    ov = _overlap_padded(SAMPLE_N_CMP, SAMPLE_N_BLK, PAST_LEN // CMP_STRIDE, SAMPLE_BLK_LANES)
    tok = lambda w: pl.BlockSpec((DEC_SEQ, w), lambda b, u, pt: (row0 + b, 0))
    const = lambda a: pl.BlockSpec(a.shape, lambda b, u, pt, nd=a.ndim: (0,) * nd)
    per_seq = lambda s: pl.BlockSpec((1,) + s, lambda b, u, pt, nd=len(s): (b,) + (0,) * nd)
    pages = [pl.BlockSpec((1, 2, ATT_GROUPS, HEAD_DIM, PAGE_SIZE),
                          lambda b, u, pt, k=k: (pt[b, u * SAMPLE_PAGES_PER_STEP + k], 0, 0, 0, 0))
             for k in range(SAMPLE_PAGES_PER_STEP)]
    consts = (bias_cmp, bias_slc, bias_win, jnp.asarray(perm, BF16), ov)
    grid_spec = pltpu.PrefetchScalarGridSpec(
        num_scalar_prefetch=1, grid=(n_seq, SAMPLE_STEPS),
        in_specs=[tok(ATT_WIDTH), tok(ATT_GROUPS * LANES), per_seq((PAST_LEN // CMP_STRIDE, KV_ROW))] + pages
        + [new(KV_ROW), per_seq((2, ATT_GROUPS, HEAD_DIM, WINDOW)), new(KV_ROW)] + [const(a) for a in consts],
        out_specs=pl.BlockSpec((DEC_SEQ, ATT_WIDTH), lambda b, u, pt: (b, 0)),
        scratch_shapes=[pltpu.VMEM((SAMPLE_ROWS, KV_WIDTH), BF16), pltpu.VMEM((SAMPLE_ROWS, PAST_LEN), F32),
                        pltpu.VMEM((SAMPLE_ROWS, LANES), F32), pltpu.VMEM((SAMPLE_ROWS, LANES), F32),
                        pltpu.VMEM((SAMPLE_ROWS, KV_WIDTH), F32), pltpu.VMEM((SAMPLE_ROWS, KV_WIDTH), F32),
                        pltpu.VMEM((SAMPLE_ROWS, KV_WIDTH), F32)])
    return pl.pallas_call(
        _nsa_sample_kernel, grid_spec=grid_spec,
        out_shape=jax.ShapeDtypeStruct((n_seq * DEC_SEQ, ATT_WIDTH), F32),
        compiler_params=_cparams("arbitrary", "arbitrary"),
        name="nsa_sample",
    )(page_table, q, gates, kvc, *([cache_slc] * SAMPLE_PAGES_PER_STEP), slc_rows, state_win, win_rows, *consts)


def kernel(x_prompt, x_sample, cache_cmp_kv, cache_slc_kv, state_win_kv, state_hgrn, page_table,
           rel_bias_table, hgrn_lower_bound, norm_ffn1, w_ffn1_gate_up, w_ffn1_down, norm_mix,
           w_in, q_norm, k_norm, w_cmp1, b_cmp1, w_cmp2, b_cmp2, attn_out_norm, hgrn_out_norm,
           w_out, norm_ffn2, w_ffn2_gate_up, w_ffn2_down):
    assert DEPTH == 1
    l = 0
    kv_shape = (2, ATT_GROUPS, HEAD_DIM)

    y1 = _ffn((x_prompt.reshape(N_PROMPT, D_MODEL), x_sample.reshape(N_SAMPLE, D_MODEL)), norm_ffn1[l],
              w_ffn1_gate_up[l].astype(BF16), w_ffn1_down[l].astype(BF16))

    (q, cmp_rows, slc_rows, slc_pack, win_rows, win_pack, hq, lf, hk, hv, hog, gates,
     cmp_new, slc_new, win_new) = _project_all(
        y1, norm_mix[l], _permute_w_in(w_in[l]), q_norm[l], k_norm[l], hgrn_lower_bound)

    tbl = rel_bias_table.astype(F32)
    first_end = CMP_BLOCK - 1
    bias_cmp_p = _bias_table(tbl, SEQ, LANES, -first_end, 1, -CMP_STRIDE)
    bias_toep = _bias_table(tbl, TOEP_TILES * LANES, LANES, 0, 1, -1)
    bias_cmp_s = _bias_table(tbl, DEC_SEQ, PAST_LEN // CMP_STRIDE, PAST_LEN - first_end, 1, -CMP_STRIDE)
    bias_slc_s = _bias_table(tbl, DEC_SEQ, PAST_LEN + LANES, PAST_LEN, 1, -1)
    bias_win_s = _bias_table(tbl, DEC_SEQ, WINDOW + LANES, WINDOW, 1, -1)
    rows_ht = lambda a: a.reshape(SAMPLE_ROWS, a.shape[-1])

    cw = _compress_weights(w_cmp1[l], b_cmp1[l], w_cmp2[l], b_cmp2[l])
    kvc_p = _compress(cmp_rows, cw, k_norm[l][0])
    kvc_s = _compress_paged(_cache_view(cache_cmp_kv[l]), page_table, w_cmp1[l], b_cmp1[l], w_cmp2[l], b_cmp2[l],
                            k_norm[l][0])

    o_att_p = _nsa_prompt(q, gates, kvc_p, slc_pack, win_pack, bias_cmp_p, bias_toep)
    o_att_s = _nsa_sample(q, gates, kvc_s, _cache_view(cache_slc_kv[l]), page_table, slc_new,
                          _cache_view(state_win_kv[l]), win_new,
                          rows_ht(bias_cmp_s), rows_ht(bias_slc_s), rows_ht(bias_win_s))

    o_hg_p, hg_p = _hgrn(hq, hk, hv, lf, None, n_seq=BATCH, t_len=SEQ, ch=HG_CHUNK_PROMPT, row0=0)
    o_hg_s, hg_s = _hgrn(hq, hk, hv, lf, state_hgrn[l].astype(F32), n_seq=DEC_BATCH, t_len=DEC_SEQ,
                         ch=HG_CHUNK_SAMPLE, row0=N_PROMPT)

    y2 = _mixer_out(y1, o_att_p, o_att_s, o_hg_p, o_hg_s, hog, attn_out_norm[l], hgrn_out_norm[l],
                    w_out[l].astype(BF16))
    y3_p, y3_s = _ffn(y2, norm_ffn2[l], w_ffn2_gate_up[l].astype(BF16), w_ffn2_down[l].astype(BF16), split_out=True)

    prompt_rows = lambda a: a.reshape((1, BATCH, SEQ) + kv_shape)
    sample_rows = lambda a: a.reshape((1, DEC_BATCH, DEC_SEQ) + kv_shape)
    win_p = prompt_rows(win_rows)[:, :, SEQ - min(WINDOW, SEQ):]
    win_s = jnp.concatenate([state_win_kv[l][:, DEC_SEQ:], sample_rows(win_new)[0]], axis=1)[None]
    return (y3_p.reshape(BATCH, SEQ, D_MODEL), y3_s.reshape(DEC_BATCH, DEC_SEQ, D_MODEL),
            prompt_rows(cmp_rows), prompt_rows(slc_rows), win_p, hg_p[None],
            sample_rows(cmp_new), sample_rows(slc_new), win_s, hg_s[None].astype(state_hgrn.dtype))
```

```python
import functools
import math

import jax
import jax.numpy as jnp
import numpy as np
from jax import lax
from jax.experimental import pallas as pl
from jax.experimental.pallas import tpu as pltpu

D_MODEL = 2048
BATCH = 4
SEQ = 2048
DEPTH = 1
DEC_BATCH = 32
DEC_SEQ = 8
PAST_LEN = 8192
PAGE_SIZE = 128
HEAD_DIM = 64
ATT_HEADS = (D_MODEL // 2) // HEAD_DIM
ATT_GROUPS = ATT_HEADS // 4
ATT_HPG = ATT_HEADS // ATT_GROUPS
ATT_WIDTH = ATT_HEADS * HEAD_DIM
KV_WIDTH = ATT_GROUPS * HEAD_DIM
N_BRANCH = 3
CMP_BLOCK = 32
CMP_STRIDE = 16
CMP_RATIO = CMP_BLOCK // CMP_STRIDE
CMP_HIDDEN = 256
SEL_BLOCK = 64
N_SEL = 8
WINDOW = 512
Q_BLOCK = 128
HG_KEY = 128
HG_VAL = 128
HG_HEADS = (D_MODEL // 2) // HG_VAL
HG_WIDTH = HG_HEADS * HG_VAL
MIX_WIDTH = ATT_WIDTH + HG_WIDTH
D_FF = 256 * ((8 * D_MODEL // 3 + 255) // 256)
N_BUCKETS = 32
MAX_DISTANCE = 1024
EPS = 1e-6
IN_SPLITS = (ATT_WIDTH, 2 * N_BRANCH * KV_WIDTH, N_BRANCH * ATT_HEADS,
             HG_HEADS * HG_KEY, HG_HEADS * HG_KEY, HG_WIDTH, HG_WIDTH)

N_PROMPT = BATCH * SEQ
N_SAMPLE = DEC_BATCH * DEC_SEQ
N_TOKENS = N_PROMPT + N_SAMPLE
N_PAGES = PAST_LEN // PAGE_SIZE
KV_ROW = 2 * KV_WIDTH

LANES = 128
V7X_VMEM_BYTES = 64 * 1024 * 1024
VMEM_LIMIT_BYTES = V7X_VMEM_BYTES * 7 // 8

NEG = -1e30
F32 = jnp.float32
BF16 = jnp.bfloat16

FFN_ROW_TILE = 768
FFN_FF_TILE = 512
PROJ_ROW_TILE = 528
PROJ_COL_TILE = 512
OUT_ROW_TILE = 256
HG_CHUNK_PROMPT = 16
HG_CHUNK_SAMPLE = DEC_SEQ
CMP_UNIT_ROWS = 2048
CMP_UNITS = 4


def _cparams(*sem):
    return pltpu.CompilerParams(dimension_semantics=sem, vmem_limit_bytes=VMEM_LIMIT_BYTES)


def _t5_thresholds():
    n = np.arange(0, 2 * MAX_DISTANCE + 2)
    exact = N_BUCKETS // 2
    logn = np.log(np.maximum(n, 1).astype(np.float64) / exact)
    large = exact + (logn / math.log(MAX_DISTANCE / exact) * (N_BUCKETS - exact)).astype(np.int32)
    b = np.where(n < exact, n, np.minimum(large, N_BUCKETS - 1))
    return [int(n[b >= k][0]) for k in range(N_BUCKETS)]


T5_THRESHOLDS = _t5_thresholds()


def _block_diag_ones(n, blk):
    i = np.arange(n)
    return (i[:, None] // blk == i[None, :] // blk).astype(np.float32)


def _split2(x):
    hi = x.astype(BF16)
    lo = (x - hi.astype(F32)).astype(BF16)
    return hi, lo


def _group_rms(t, bd, gain, width):
    hi, lo = _split2(t * t)
    ss = (jnp.dot(hi, bd, preferred_element_type=F32) + jnp.dot(lo, bd, preferred_element_type=F32))
    return t * lax.rsqrt(ss * (1.0 / width) + EPS) * gain


def _ffn_kernel(*refs, split_in, split_out):
    n_in = 2 if split_in else 1
    x_ref = refs[0]
    gain_ref, wg_ref, wu_ref, wd_ref = refs[n_in:n_in + 4]
    o_ref = refs[n_in + 4]
    xn_ref, acc_ref = refs[-2:]
    j = pl.program_id(1)
    last_tile = pl.program_id(0) == pl.num_programs(0) - 1
    tm = x_ref.shape[0]

    def rows_in():
        x = x_ref[...]
        if split_in:
            xs = refs[1][...]
            x = jnp.where(last_tile, jnp.concatenate([x[:tm - xs.shape[0]], xs], axis=0), x)
        return x

    @pl.when(j == 0)
    def _():
        x = rows_in()
        y = x * lax.rsqrt(jnp.mean(x * x, axis=-1, keepdims=True) + EPS)
        xn_ref[...] = (y * gain_ref[...]).astype(BF16)
        acc_ref[...] = jnp.zeros_like(acc_ref)

    xn = xn_ref[...]
    g = jnp.dot(xn, wg_ref[...], preferred_element_type=F32)
    u = jnp.dot(xn, wu_ref[...], preferred_element_type=F32)
    a = (g * jax.nn.sigmoid(g) * u).astype(BF16)
    acc_ref[...] += jnp.dot(a, wd_ref[...], preferred_element_type=F32)

    @pl.when(j == pl.num_programs(1) - 1)
    def _():
        res = rows_in() + 0.5 * acc_ref[...]
        o_ref[...] = res
        if split_out:
            os_ref = refs[n_in + 5]

            @pl.when(last_tile)
            def _():
                os_ref[...] = res[tm - os_ref.shape[0]:]


def _ffn(x, gain, w_gu, w_down, split_out=False):
    split_in = isinstance(x, tuple)
    tm, tf = FFN_ROW_TILE, FFN_FF_TILE
    n_p, n_s = N_PROMPT, N_SAMPLE
    n, d = n_p + n_s, D_MODEL
    assert n % tm == 0 and D_FF % tf == 0 and n_p % tm == tm - n_s
    nj = D_FF // tf
    row = pl.BlockSpec((tm, d), lambda i, j: (i, 0))
    tail = pl.BlockSpec((n_s, d), lambda i, j: (0, 0))
    xs = x if split_in else (x,)
    return pl.pallas_call(
        functools.partial(_ffn_kernel, split_in=split_in, split_out=split_out),
        grid=(n // tm, nj),
        in_specs=([row, tail] if split_in else [row]) + [
            pl.BlockSpec((1, d), lambda i, j: (0, 0)),
            pl.BlockSpec((d, tf), lambda i, j: (0, j)),
            pl.BlockSpec((d, tf), lambda i, j: (0, j + nj)),
            pl.BlockSpec((tf, d), lambda i, j: (j, 0)),
        ],
        out_specs=(row, tail) if split_out else row,
        out_shape=((jax.ShapeDtypeStruct((n_p, d), F32), jax.ShapeDtypeStruct((n_s, d), F32)) if split_out
                   else jax.ShapeDtypeStruct((n, d), F32)),
        scratch_shapes=[pltpu.VMEM((tm, d), BF16), pltpu.VMEM((tm, d), F32)],
        compiler_params=_cparams("arbitrary", "arbitrary"),
        name="ffn",
    )(*xs, gain.reshape(1, d), w_gu, w_gu, w_down)


PB_Q, PB_CMP, PB_SLC, PB_WIN, PB_HQ, PB_HF, PB_HI, PB_HG, PB_GATE = 0, 2, 3, 4, 5, 7, 9, 11, 13
PROJ_N_BLOCKS = 14


def _permute_w_in(w_in):
    p = [int(v) for v in np.cumsum(IN_SPLITS)]
    a_g = w_in[:, p[1]:p[2]]
    src = np.zeros((PROJ_COL_TILE,), np.int32)
    valid = np.zeros((PROJ_COL_TILE,), bool)
    for g in range(ATT_GROUPS):
        for br in range(N_BRANCH):
            for hp in range(ATT_HPG):
                src[g * LANES + br * ATT_HPG + hp] = br * ATT_HEADS + g * ATT_HPG + hp
                valid[g * LANES + br * ATT_HPG + hp] = True
    gate = jnp.where(jnp.asarray(valid)[None, :], a_g[:, src], 0.0)
    return jnp.concatenate([w_in[:, :p[1]], w_in[:, p[2]:], gate], axis=1).astype(BF16)


def _pack_kv(k, v):
    parts = []
    for g in range(ATT_GROUPS):
        parts.append(k[:, g * HEAD_DIM:(g + 1) * HEAD_DIM])
        parts.append(v[:, g * HEAD_DIM:(g + 1) * HEAD_DIM])
    return jnp.concatenate(parts, axis=1).astype(BF16)


def _proj_kernel(x_ref, gain_ref, w_ref, bd_ref, qg_ref, kg_ref, lbp_ref,
                 q_ref, cmp_ref, slc_ref, slcp_ref, win_ref, winp_ref,
                 hq_ref, lf_ref, hk_ref, hv_ref, hog_ref, gate_ref, cmps_ref, slcs_ref, wins_ref, xn_ref):
    c = pl.program_id(1)
    last_tile = pl.program_id(0) == pl.num_programs(0) - 1
    tail = x_ref.shape[0] - cmps_ref.shape[0]

    @pl.when(c == 0)
    def _():
        x = x_ref[...]
        y = x * lax.rsqrt(jnp.mean(x * x, axis=-1, keepdims=True) + EPS)
        xn_ref[...] = (y * gain_ref[...]).astype(BF16)

    acc = jnp.dot(xn_ref[...], w_ref[...], preferred_element_type=F32)
    half = KV_WIDTH

    @pl.when(c < PB_CMP)
    def _():
        q_ref[...] = _group_rms(acc, bd_ref[...], qg_ref[...], HEAD_DIM)

    @pl.when(c == PB_CMP)
    def _():
        cmp_ref[...] = acc

        @pl.when(last_tile)
        def _():
            cmps_ref[...] = acc[tail:]

    def kv_branch(rows_ref, sample_ref, pack_ref, br):
        k = _group_rms(acc[:, :half], bd_ref[:half, :half], kg_ref[br - 1:br, :], HEAD_DIM)
        v = acc[:, half:]
        rows_ref[:, :half] = k
        rows_ref[:, half:] = v
        pack_ref[...] = _pack_kv(k, v)

        @pl.when(last_tile)
        def _():
            sample_ref[:, :half] = k[tail:]
            sample_ref[:, half:] = v[tail:]

    @pl.when(c == PB_SLC)
    def _():
        kv_branch(slc_ref, slcs_ref, slcp_ref, 1)

    @pl.when(c == PB_WIN)
    def _():
        kv_branch(win_ref, wins_ref, winp_ref, 2)

    @pl.when((c >= PB_HQ) & (c < PB_HF))
    def _():
        hq_ref[...] = acc

    @pl.when((c >= PB_HF) & (c < PB_HI))
    def _():
        p = lbp_ref[...]
        e = jnp.exp(p - jnp.max(p, axis=0, keepdims=True))
        lb = e[0:1, :] / jnp.sum(e, axis=0, keepdims=True)
        lf_ref[...] = jnp.log(lb + (1.0 - lb) * jax.nn.sigmoid(acc))
        hk_ref[...] = (1.0 - lb) * jax.nn.sigmoid(-acc)

    @pl.when((c >= PB_HI) & (c < PB_HG))
    def _():
        hv_ref[...] = acc

    @pl.when((c >= PB_HG) & (c < PB_GATE))
    def _():
        hog_ref[...] = acc

    @pl.when(c == PB_GATE)
    def _():
        gate_ref[...] = jax.nn.sigmoid(acc)


def _project_all(y, gain, w_perm, q_gain, k_gain, lb_logits):
    n, d = y.shape
    tm, tc = PROJ_ROW_TILE, PROJ_COL_TILE
    assert n == N_TOKENS and n % tm == 0 and N_PROMPT % tm == tm - N_SAMPLE and DEPTH == 1
    bd = jnp.asarray(_block_diag_ones(tc, HEAD_DIM), BF16)

    def two(first):
        return lambda i, c: (i, jnp.clip(c - first, 0, 1))

    one = lambda i, c: (i, 0)
    wide = lambda dt: jax.ShapeDtypeStruct((n, 2 * tc), dt)
    narrow = lambda dt: jax.ShapeDtypeStruct((n, tc), dt)
    prompt = jax.ShapeDtypeStruct((N_PROMPT, tc), F32)
    sample = jax.ShapeDtypeStruct((N_SAMPLE, tc), F32)
    out_shape = (wide(F32), prompt, prompt, narrow(BF16), prompt, narrow(BF16),
                 wide(F32), wide(F32), wide(F32), wide(F32), wide(F32), narrow(F32), sample, sample, sample)
    blk = lambda f: pl.BlockSpec((tm, tc), f)
    tail = pl.BlockSpec((N_SAMPLE, tc), lambda i, c: (0, 0))
    out_specs = (blk(two(PB_Q)), blk(one), blk(one), blk(one), blk(one), blk(one),
                 blk(two(PB_HQ)), blk(two(PB_HF)), blk(two(PB_HF)), blk(two(PB_HI)), blk(two(PB_HG)), blk(one),
                 tail, tail, tail)
    return pl.pallas_call(
        _proj_kernel,
        grid=(n // tm, PROJ_N_BLOCKS),
        in_specs=[
            pl.BlockSpec((tm, d), lambda i, c: (i, 0)),
            pl.BlockSpec((1, d), lambda i, c: (0, 0)),
            pl.BlockSpec((d, tc), lambda i, c: (0, c)),
            pl.BlockSpec((tc, tc), lambda i, c: (0, 0)),
            pl.BlockSpec((1, tc), lambda i, c: (0, jnp.clip(c, 0, 1))),
            pl.BlockSpec((2, KV_WIDTH), lambda i, c: (0, 0)),
            pl.BlockSpec((DEPTH + 1, tc), lambda i, c: (0, jnp.clip(c - PB_HF, 0, 1))),
        ],
        out_specs=out_specs,
        out_shape=out_shape,
        scratch_shapes=[pltpu.VMEM((tm, d), BF16)],
        compiler_params=_cparams("arbitrary", "arbitrary"),
        name="proj",
    )(y, gain.reshape(1, d), w_perm, bd, q_gain.reshape(1, ATT_WIDTH),
      k_gain[1:].reshape(2, KV_WIDTH), lb_logits)


def _hgrn_prepare(q_ref, k_ref, v_ref, lf_ref, tri_ref, ones_ref, s0_ref, o_ref,
                  qe_scr, kd_scr, vt_scr, dec_scr, b_scr, *, t_len, ch, has_state):
    nj = t_len // ch
    shape3 = (nj, ch, HG_KEY)
    q3, k3, v3, lf3 = q_ref[...], k_ref[...], v_ref[...], lf_ref[...]
    tl = lax.broadcasted_iota(jnp.int32, shape3, 1)

    def row(x3, s):
        return jnp.broadcast_to(x3[:, s:s + 1, :], shape3)

    if t_len >= 256:
        lf2 = lf3.reshape(t_len, HG_KEY)
        parts = []
        for r0 in range(0, t_len, 256):
            x = lf2[r0:r0 + 256]
            hi = x.astype(BF16)
            r1 = x - hi.astype(F32)
            mid = r1.astype(BF16)
            lo = (r1 - mid.astype(F32)).astype(BF16)
            tri = tri_ref[...]
            parts.append(jnp.dot(tri, hi, preferred_element_type=F32)
                         + jnp.dot(tri, mid, preferred_element_type=F32)
                         + jnp.dot(tri, lo, preferred_element_type=F32))
        b3 = jnp.concatenate(parts, axis=0).reshape(shape3)
    else:
        b3 = jnp.zeros(shape3, F32)
        for s in range(ch):
            b3 = b3 + jnp.where(tl >= s, row(lf3, s), 0.0)

    bl3 = row(b3, ch - 1)
    qe3 = q3 * jnp.exp(b3)
    kd3 = k3 * jnp.exp(bl3 - b3)
    dec_scr[...] = jnp.exp(b3[:, ch - 1:ch, :])
    blocked = t_len % LANES == 0
    if blocked:
        nb = t_len // LANES
        qe2, kd2, v2 = (x.reshape(t_len, HG_KEY) for x in (qe3, kd3, v3))
        for m in range(nb):
            blk = slice(m * LANES, (m + 1) * LANES)
            qe_scr[m] = qe2[blk].T.astype(BF16)
            vt_scr[m] = v2[blk].T.astype(BF16)
            kd_scr[m] = kd2[blk].astype(BF16)
    else:
        qe_scr[...] = qe3.astype(BF16)
        kd_scr[...] = kd3.astype(BF16)

    ones = ones_ref[...]
    n8 = ch // 8
    shape8 = (nj, 8, HG_KEY)
    sub8 = lax.broadcasted_iota(jnp.int32, shape8, 1)
    q5, b5 = (x.reshape(nj, n8, 8, HG_KEY) for x in (q3, b3))
    b_scr[...] = b3
    od = [jnp.zeros(shape8, F32) for _ in range(n8)]
    for s in range(ch):
        hs, ss = divmod(s, 8)
        ks, bs, vs = (jnp.broadcast_to(r[:, s:s + 1, :], shape8) for r in (k_ref, b_scr, v_ref))
        for hh in range(hs, n8):
            diff = b5[:, hh] - bs
            if hh == hs:
                diff = jnp.where(sub8 >= ss, diff, NEG)
            w = q5[:, hh] * ks * jnp.exp(diff)
            a = jnp.dot(w.reshape(nj * 8, HG_KEY).astype(BF16), ones, preferred_element_type=F32)
            od[hh] = od[hh] + a.reshape(shape8) * vs
    for hh in range(n8):
        o_ref[:, hh * 8:(hh + 1) * 8, :] = od[hh]

    return s0_ref[0, 0].T if has_state else jnp.zeros((HG_VAL, HG_KEY), F32)


def _hgrn_scan(heads, st0s, *, t_len, ch):
    nj = t_len // ch
    if t_len % LANES == 0:
        cpb = LANES // ch
        lane_chunk = lax.broadcasted_iota(jnp.int32, (HG_KEY, LANES), 1) >> (ch.bit_length() - 1)
        keep = [jnp.where(lane_chunk == r, 1.0, 0.0).astype(BF16) for r in range(cpb)]

        def body(m, sts):
            out = []
            for (_, _, _, qe_scr, kd_scr, vt_scr, dec_scr, oi_scr), st in zip(heads, sts):
                qet, vt, kd = qe_scr[m], vt_scr[m], kd_scr[m]
                ot = jnp.zeros((HG_VAL, LANES), F32)
                for r in range(cpb):
                    ot = ot + jnp.dot(st.astype(BF16), qet * keep[r], preferred_element_type=F32)
                    ut = jnp.dot(vt * keep[r], kd, preferred_element_type=F32)
                    st = st * dec_scr[m * cpb + r] + ut
                oi_scr[m] = ot
                out.append(st)
            return tuple(out)

        sts = lax.fori_loop(0, t_len // LANES, body, tuple(st0s))
        for (_, o_ref, _, _, _, _, _, oi_scr) in heads:
            for m in range(t_len // LANES):
                o_ref[m * cpb:(m + 1) * cpb] = o_ref[m * cpb:(m + 1) * cpb] + oi_scr[m].T.reshape(cpb, ch, HG_VAL)
    else:
        sts = []
        for (v_ref, o_ref, _, qe_scr, kd_scr, _, dec_scr, oi_scr), st in zip(heads, st0s):
            for j in range(nj):
                oi_scr[j] = lax.dot_general(qe_scr[j], st.astype(BF16), NT_DIMS, preferred_element_type=F32)
                ut = lax.dot_general(v_ref[j].astype(BF16), kd_scr[j], (((0,), (0,)), ((), ())),
                                     preferred_element_type=F32)
                st = st * dec_scr[j] + ut
            o_ref[...] = o_ref[...] + oi_scr[...]
            sts.append(st)
    for (_, _, st_ref, *_), st in zip(heads, sts):
        st_ref[0, 0] = st.T


def _hgrn(hq, hk, hv, lf, s0, *, n_seq, t_len, ch, row0):
    n = hq.shape[0]
    nj = t_len // ch
    assert row0 % t_len == 0 and t_len % ch == 0 and ch % 8 == 0
    blk0 = row0 // t_len
    r3 = lambda a: a.reshape(n // ch, ch, HG_WIDTH)
    has_state = s0 is not None
    if not has_state:
        s0 = jnp.zeros((1, HG_HEADS, HG_KEY, HG_VAL), F32)
    tri_n = 256 if t_len >= 256 else 8
    i = np.arange(tri_n)
    tri = jnp.asarray(((i[:, None] // ch == i[None, :] // ch) & (i[:, None] >= i[None, :])).astype(np.float32), BF16)
    ones = jnp.ones((HG_KEY, HG_KEY), BF16)
    blk_shape = (t_len // LANES, LANES, HG_KEY) if t_len % LANES == 0 else (nj, ch, HG_KEY)
    hps = HG_HEADS if nj == 1 else 2
    seq = pl.BlockSpec((nj, ch, hps * HG_KEY), lambda b, h: (blk0 + b, 0, h))
    state_in = pl.BlockSpec((1, hps, HG_KEY, HG_VAL), (lambda b, h: (b, h, 0, 0)) if has_state else (lambda b, h: (0, 0, 0, 0)))

    def body(q_ref, k_ref, v_ref, lf_ref, tri_ref, ones_ref, s0_ref, o_ref, st_ref,
             qe_scr, kd_scr, vt_scr, dec_scr, oi_scr, b_scr):
        heads, st0s = [], []
        for h in range(hps):
            ln = slice(h * HG_KEY, (h + 1) * HG_KEY)
            v_h, o_h = v_ref.at[:, :, ln], o_ref.at[:, :, ln]
            s0_h = s0_ref.at[:, h:h + 1] if has_state else s0_ref
            st0s.append(_hgrn_prepare(q_ref.at[:, :, ln], k_ref.at[:, :, ln], v_h, lf_ref.at[:, :, ln], tri_ref, ones_ref,
                                      s0_h, o_h, qe_scr.at[h], kd_scr.at[h], vt_scr.at[h], dec_scr.at[h], b_scr,
                                      t_len=t_len, ch=ch, has_state=has_state))
            heads.append((v_h, o_h, st_ref.at[:, h:h + 1], qe_scr.at[h], kd_scr.at[h], vt_scr.at[h], dec_scr.at[h],
                          oi_scr.at[h]))
        _hgrn_scan(heads, st0s, t_len=t_len, ch=ch)

    o, st = pl.pallas_call(
        body,
        grid=(n_seq, HG_HEADS // hps),
        in_specs=[seq, seq, seq, seq,
                  pl.BlockSpec((tri_n, tri_n), lambda b, h: (0, 0)),
                  pl.BlockSpec((HG_KEY, HG_KEY), lambda b, h: (0, 0)),
                  state_in],
        out_specs=(pl.BlockSpec((nj, ch, hps * HG_VAL), lambda b, h: (b, 0, h)),
                   pl.BlockSpec((1, hps, HG_KEY, HG_VAL), lambda b, h: (b, h, 0, 0))),
        out_shape=(jax.ShapeDtypeStruct((n_seq * nj, ch, HG_WIDTH), F32),
                   jax.ShapeDtypeStruct((n_seq, HG_HEADS, HG_KEY, HG_VAL), F32)),
        scratch_shapes=[pltpu.VMEM((hps,) + blk_shape, BF16), pltpu.VMEM((hps,) + blk_shape, BF16),
                        pltpu.VMEM((hps,) + blk_shape, BF16), pltpu.VMEM((hps, nj, 1, HG_KEY), F32),
                        pltpu.VMEM((hps,) + blk_shape, F32), pltpu.VMEM((nj, ch, HG_KEY), F32)],
        compiler_params=_cparams("arbitrary", "arbitrary"),
        name="hgrn",
    )(r3(hq), r3(hk), r3(hv), r3(lf), tri, ones, s0)
    return o.reshape(n_seq * t_len, HG_WIDTH), st


def _mixout_kernel(y_ref, oap_ref, oas_ref, ohp_ref, ohs_ref, og_ref, ag_ref, hgain_ref, w_ref, o_ref):
    is_sample = pl.program_id(0) == pl.num_programs(0) - 1
    oa = jnp.where(is_sample, oas_ref[...], oap_ref[...])
    a = oa * lax.rsqrt(jnp.mean(oa * oa, axis=-1, keepdims=True) + EPS) * ag_ref[...]
    oh = jnp.where(is_sample, ohs_ref[...], ohp_ref[...])
    hs = []
    for h in range(HG_HEADS):
        x = oh[:, h * HG_VAL:(h + 1) * HG_VAL]
        hs.append(x * lax.rsqrt(jnp.mean(x * x, axis=-1, keepdims=True) + EPS))
    og = og_ref[...]
    hh = jnp.concatenate(hs, axis=1) * hgain_ref[...] * (og * jax.nn.sigmoid(og))
    m = (jnp.dot(a.astype(BF16), w_ref[:ATT_WIDTH, :], preferred_element_type=F32)
         + jnp.dot(hh.astype(BF16), w_ref[ATT_WIDTH:, :], preferred_element_type=F32))
    o_ref[...] = y_ref[...] + m


def _mixer_out(y, o_att_p, o_att_s, o_hg_p, o_hg_s, og, attn_gain, hg_gain, w_out):
    n, d = y.shape
    tm = OUT_ROW_TILE
    n_p = o_att_p.shape[0]
    assert n % tm == 0 and n_p % tm == 0 and o_att_s.shape[0] == tm and n == n_p + tm
    row = lambda w: pl.BlockSpec((tm, w), lambda i: (i, 0))
    prompt_row = lambda w: pl.BlockSpec((tm, w), lambda i: (jnp.minimum(i, n_p // tm - 1), 0))
    const = lambda s: pl.BlockSpec(s, lambda i: (0, 0))
    return pl.pallas_call(
        _mixout_kernel,
        grid=(n // tm,),
        in_specs=[row(d), prompt_row(ATT_WIDTH), const((tm, ATT_WIDTH)), prompt_row(HG_WIDTH), const((tm, HG_WIDTH)),
                  row(HG_WIDTH), const((1, ATT_WIDTH)), const((1, HG_WIDTH)), const((MIX_WIDTH, d))],
        out_specs=row(d),
        out_shape=jax.ShapeDtypeStruct((n, d), F32),
        compiler_params=_cparams("arbitrary"),
        name="mixout",
    )(y, o_att_p, o_att_s, o_hg_p, o_hg_s, og, attn_gain.reshape(1, ATT_WIDTH), hg_gain.reshape(1, HG_WIDTH), w_out)


def _bias_kernel(tbl_ref, o_ref, *, a0, ar, ac, rows_blk):
    h = pl.program_id(0)
    rb = pl.program_id(1)
    shape = o_ref.shape[1:]
    r = lax.broadcasted_iota(jnp.int32, shape, 0) + rb * rows_blk
    c = lax.broadcasted_iota(jnp.int32, shape, 1)
    n = a0 + ar * r + ac * c
    out = jnp.full(shape, tbl_ref[0, h], F32)
    for k in range(1, N_BUCKETS):
        out = jnp.where(n >= T5_THRESHOLDS[k], tbl_ref[k, h], out)
    o_ref[0] = out


def _bias_table(tbl, rows, cols, a0, ar, ac):
    rows_blk = max(r for r in range(8, min(rows, 512) + 1, 8) if rows % r == 0)
    assert cols % LANES == 0
    return pl.pallas_call(
        functools.partial(_bias_kernel, a0=a0, ar=ar, ac=ac, rows_blk=rows_blk),
        grid=(ATT_HEADS, rows // rows_blk),
        in_specs=[pl.BlockSpec(memory_space=pltpu.SMEM)],
        out_specs=pl.BlockSpec((1, rows_blk, cols), lambda h, rb: (h, rb, 0)),
        out_shape=jax.ShapeDtypeStruct((ATT_HEADS, rows, cols), F32),
        compiler_params=_cparams("arbitrary", "arbitrary"),
        name="t5_bias",
    )(tbl)


def _compress_weights(w1, b1, w2, b2):
    w = w1.reshape(2, CMP_RATIO, CMP_STRIDE, HEAD_DIM, CMP_HIDDEN)
    z = jnp.zeros_like(w)
    top = jnp.concatenate([w, z], axis=-1)
    bot = jnp.concatenate([z, w], axis=-1)
    wpad = jnp.stack([top, bot], axis=3)
    wpad = wpad.reshape(2 * CMP_RATIO, CMP_STRIDE * LANES, 2 * CMP_HIDDEN).astype(BF16)
    z2 = jnp.zeros_like(w2)
    w2pad = jnp.concatenate([jnp.concatenate([w2, z2], axis=-1), jnp.concatenate([z2, w2], axis=-1)], axis=1)
    return wpad, jnp.concatenate([b1, b1], axis=-1), w2pad.astype(BF16), jnp.concatenate([b2, b2], axis=-1)


def _compress_kernel(rows_ref, perm_ref, wpad_ref, b1_ref, w2pad_ref, b2_ref, kg_ref, bd_ref, out_ref, xs_ref):
    u = pl.program_id(1)
    n_chunk = CMP_UNITS * CMP_UNIT_ROWS // CMP_STRIDE
    perm = perm_ref[...]

    for t in range(CMP_UNIT_ROWS // 256):
        x = rows_ref[t * 256:(t + 1) * 256, :]
        y = jnp.dot(perm, x.astype(BF16), preferred_element_type=F32).astype(BF16)
        base = pl.multiple_of(u * (CMP_UNIT_ROWS // CMP_STRIDE) + t * 16, 16)
        for s in range(CMP_STRIDE):
            xs_ref[s, pl.ds(base, 16), :] = y[s * 16:(s + 1) * 16, :]

    @pl.when(u == CMP_UNITS - 1)
    def _():
        halves = []
        for kv in range(2):
            outs = []
            for j in range(2):
                col = (kv * 2 + j) * LANES
                lhs = jnp.concatenate([xs_ref[s, :, col:col + LANES] for s in range(CMP_STRIDE)], axis=1)
                h0 = jnp.dot(lhs, wpad_ref[kv * CMP_RATIO + 0], preferred_element_type=F32)
                h1 = jnp.dot(lhs, wpad_ref[kv * CMP_RATIO + 1], preferred_element_type=F32)
                h = b1_ref[kv:kv + 1, :] + h0 + pltpu.roll(h1, n_chunk - 1, axis=0)
                a = (h * jax.nn.sigmoid(h)).astype(BF16)
                outs.append(jnp.dot(a, w2pad_ref[kv], preferred_element_type=F32) + b2_ref[kv:kv + 1, :])
            halves.append(jnp.concatenate(outs, axis=1))
        pk = _pack_kv(_group_rms(halves[0], bd_ref[...], kg_ref[...], HEAD_DIM), halves[1])
        for g in range(ATT_GROUPS):
            out_ref[0, g] = pk[:, g * LANES:(g + 1) * LANES]


def _compress(rows, cw, k_gain0):
    wpad, b1, w2pad, b2 = cw
    n_chunk = CMP_UNITS * CMP_UNIT_ROWS // CMP_STRIDE
    i = np.arange(256)
    perm = np.zeros((256, 256), np.float32)
    perm[(i % 16) * 16 + i // 16, i] = 1.0
    consts = (jnp.asarray(perm, BF16), wpad, b1, w2pad, b2, k_gain0.reshape(1, KV_WIDTH),
              jnp.asarray(_block_diag_ones(KV_WIDTH, HEAD_DIM), BF16))
    const = lambda a: pl.BlockSpec(a.shape, lambda b, u, nd=a.ndim: (0,) * nd)
    return pl.pallas_call(
        _compress_kernel,
        grid=(1, CMP_UNITS),
        in_specs=[pl.BlockSpec((CMP_UNIT_ROWS, KV_ROW), lambda b, u: (u, 0))] + [const(a) for a in consts],
        out_specs=pl.BlockSpec((1, ATT_GROUPS, n_chunk, LANES), lambda b, u: (0, 0, 0, 0)),
        out_shape=jax.ShapeDtypeStruct((1, ATT_GROUPS, n_chunk, LANES), BF16),
        scratch_shapes=[pltpu.VMEM((CMP_STRIDE, n_chunk, KV_ROW), BF16)],
        compiler_params=_cparams("arbitrary", "arbitrary"),
        name="compress",
    )(rows, *consts)


PAGED_UNIT_ROWS = 4096
PAGED_UNITS = PAST_LEN // PAGED_UNIT_ROWS
PAGES_PER_STEP = PAGED_UNIT_ROWS // PAGE_SIZE


def _cache_view(cache):
    return jnp.transpose(cache, (0, 2, 3, 4, 1))


def _compress_paged_kernel(pt_ref, *refs):
    del pt_ref
    page_refs = refs[:PAGES_PER_STEP]
    pick_ref, w1_ref, b1_ref, w2_ref, b2_ref, kg_ref, bd_ref, out_ref, xs_ref = refs[PAGES_PER_STEP:]
    u = pl.program_id(1)
    n_chunk = PAST_LEN // CMP_STRIDE
    pick = pick_ref[...]
    n_kvg = 2 * ATT_GROUPS

    for t in range(PAGES_PER_STEP // 2):
        base = pl.multiple_of(u * (PAGED_UNIT_ROWS // CMP_STRIDE) + t * 16, 16)
        kt = jnp.concatenate([page_refs[2 * t][0].reshape(n_kvg * HEAD_DIM, PAGE_SIZE),
                              page_refs[2 * t + 1][0].reshape(n_kvg * HEAD_DIM, PAGE_SIZE)], axis=1).astype(BF16)
        z = jnp.dot(kt, pick, preferred_element_type=F32)
        for kvg in range(n_kvg):
            zk = z[kvg * HEAD_DIM:(kvg + 1) * HEAD_DIM]
            y = jnp.concatenate([zk[:, :LANES], zk[:, LANES:]], axis=0).T.astype(BF16)
            for s2 in range(CMP_STRIDE // 2):
                xs_ref[kvg, s2, pl.ds(base, 16), :] = y[s2 * 16:(s2 + 1) * 16, :]

    @pl.when(u == PAGED_UNITS - 1)
    def _():
        halves = []
        for kv in range(2):
            lhs = jnp.concatenate(
                [jnp.concatenate([xs_ref[kv * ATT_GROUPS + g, s2] for s2 in range(CMP_STRIDE // 2)], axis=1)
                 for g in range(ATT_GROUPS)], axis=0)
            h0 = jnp.dot(lhs, w1_ref[kv * CMP_RATIO + 0], preferred_element_type=F32)
            h1 = jnp.dot(lhs, w1_ref[kv * CMP_RATIO + 1], preferred_element_type=F32)
            h = b1_ref[kv:kv + 1, :] + h0 + pltpu.roll(h1, ATT_GROUPS * n_chunk - 1, axis=0)
            a = (h * jax.nn.sigmoid(h)).astype(BF16)
            o = b2_ref[kv:kv + 1, :]
            for g in range(ATT_GROUPS):
                o = o + jnp.dot(a[g * n_chunk:(g + 1) * n_chunk], w2_ref[kv * ATT_GROUPS + g], preferred_element_type=F32)
            halves.append(o)
        out_ref[0, :, :KV_WIDTH] = _group_rms(halves[0], bd_ref[...], kg_ref[...], HEAD_DIM).astype(BF16)
        out_ref[0, :, KV_WIDTH:] = halves[1].astype(BF16)


def _compress_paged(cache_t, page_table, w1, b1, w2, b2, k_gain0):
    n_seq = page_table.shape[0]
    n_chunk = PAST_LEN // CMP_STRIDE
    r = np.arange(LANES)
    s2, c = r // 16, r % 16
    pick = np.zeros((2 * PAGE_SIZE, 2, LANES), np.float32)
    for half in range(2):
        pick[CMP_STRIDE * c + 2 * s2 + half, half, r] = 1.0
    w2p = jnp.zeros((2, ATT_GROUPS, CMP_HIDDEN, KV_WIDTH), F32)
    for g in range(ATT_GROUPS):
        w2p = w2p.at[:, g, :, g * HEAD_DIM:(g + 1) * HEAD_DIM].set(w2)
    consts = (jnp.asarray(pick.reshape(2 * PAGE_SIZE, 2 * LANES), BF16),
              w1.reshape(2 * CMP_RATIO, CMP_STRIDE * HEAD_DIM, CMP_HIDDEN).astype(BF16), b1,
              w2p.reshape(2 * ATT_GROUPS, CMP_HIDDEN, KV_WIDTH).astype(BF16), jnp.tile(b2, (1, ATT_GROUPS)),
              k_gain0.reshape(1, KV_WIDTH), jnp.asarray(_block_diag_ones(KV_WIDTH, HEAD_DIM), BF16))
    const = lambda a: pl.BlockSpec(a.shape, lambda b, u, pt, nd=a.ndim: (0,) * nd)
    pages = [pl.BlockSpec((1, 2, ATT_GROUPS, HEAD_DIM, PAGE_SIZE),
                          lambda b, u, pt, k=k: (pt[b, u * PAGES_PER_STEP + k], 0, 0, 0, 0))
             for k in range(PAGES_PER_STEP)]
    grid_spec = pltpu.PrefetchScalarGridSpec(
        num_scalar_prefetch=1, grid=(n_seq, PAGED_UNITS),
        in_specs=pages + [const(a) for a in consts],
        out_specs=pl.BlockSpec((1, n_chunk, KV_ROW), lambda b, u, pt: (b, 0, 0)),
        scratch_shapes=[pltpu.VMEM((2 * ATT_GROUPS, CMP_STRIDE // 2, n_chunk, LANES), BF16)])
    return pl.pallas_call(
        _compress_paged_kernel, grid_spec=grid_spec,
        out_shape=jax.ShapeDtypeStruct((n_seq, n_chunk, KV_ROW), BF16),
        compiler_params=_cparams("arbitrary", "arbitrary"),
        name="compress_paged",
    )(page_table, *([cache_t] * PAGES_PER_STEP), *consts)


def _block_overlap(n_cmp, n_blk):
    cs = np.arange(n_cmp)[:, None] * CMP_STRIDE
    bs = np.arange(n_blk)[None, :] * SEL_BLOCK
    ov = np.minimum(cs + CMP_BLOCK, bs + SEL_BLOCK) - np.maximum(cs, bs)
    return (np.clip(ov, 0, None) / CMP_BLOCK).astype(np.float32)


def _overlap_padded(n_cmp, n_blk, rows, cols):
    ov = np.zeros((rows, cols), np.float32)
    ov[:n_cmp, :n_blk] = _block_overlap(n_cmp, n_blk)
    return jnp.asarray(ov, BF16)


def _softmax_rows(s, mask):
    s = jnp.where(mask, s, NEG)
    m = jnp.max(s, axis=1, keepdims=True)
    e = jnp.where(mask, jnp.exp(s - m), 0.0)
    d = jnp.sum(e, axis=1, keepdims=True)
    return e / jnp.where(d > 0, d, 1.0)


def _select_blocks(imp, qpos, n_blk, axis):
    blk = lax.broadcasted_iota(jnp.int32, imp.shape, axis)
    cur = qpos >> SEL_SHIFT
    forced = (blk == 0) | (blk == cur) | (blk == cur - 1)
    valid = blk * SEL_BLOCK <= qpos
    score = jnp.where(forced, ATT_HPG + 1.0, jnp.where(valid, imp, -1.0))
    ahead = jnp.zeros(imp.shape, F32)
    for m in range(n_blk):
        sm = jnp.broadcast_to(score[m:m + 1, :] if axis == 0 else score[:, m:m + 1], imp.shape)
        tie = jnp.where(blk > m, 1.0, 0.0)
        ahead = ahead + jnp.where(sm > score, 1.0, jnp.where(sm == score, tie, 0.0))
    return jnp.where(blk < n_blk, jnp.where(ahead < N_SEL, 1.0, 0.0), 0.0)


SEL_SHIFT = SEL_BLOCK.bit_length() - 1
NT_DIMS = (((1,), (1,)), ((), ()))


SLC_UNROLL = 4
TOEP_TILES = min(SEQ // LANES, -(-(T5_THRESHOLDS[-1] + LANES - 1) // LANES) + 1)


NSA_GROUPS_PER_STEP = 4


def _nsa_prompt_multi_kernel(q_ref, gate_ref, kvc_ref, slc_ref, win_ref, bcmp_ref, toep_ref, ov_ref, spread_ref,
                             o_ref, s_scr, m_scr, l_scr, acc_scr):
    i = pl.program_id(2)
    ngs = NSA_GROUPS_PER_STEP
    tile = (Q_BLOCK, LANES)
    rows = ATT_HPG * Q_BLOCK
    lane = lax.broadcasted_iota(jnp.int32, tile, 1)
    sub = lax.broadcasted_iota(jnp.int32, tile, 0)
    low = lane < HEAD_DIM
    qpos = i * Q_BLOCK + sub
    head = lambda x, p: x[p * Q_BLOCK:(p + 1) * Q_BLOCK]
    heads = range(ATT_HPG)
    n_blk = SEQ // SEL_BLOCK
    blk_row = lax.broadcasted_iota(jnp.int32, (n_blk, LANES), 0)
    blk_of_key = lax.broadcasted_iota(jnp.int32, (n_blk, LANES), 1) >> SEL_SHIFT
    qpos_t = i * Q_BLOCK + lax.broadcasted_iota(jnp.int32, (n_blk, Q_BLOCK), 1)
    last_tile = SEQ // LANES - 1
    ov = ov_ref[...]
    glanes = lambda gg: slice(gg * LANES, (gg + 1) * LANES)

    def key_tile(kv_ref, gg, j):
        jc = jnp.clip(j, 0, last_tile)
        return kv_ref[pl.ds(pl.multiple_of(jc * LANES, LANES), LANES), glanes(gg)]

    def masked_scores(qs, gg, kv, j, mask):
        sc = lax.dot_general(qs, kv, NT_DIMS, preferred_element_type=F32)
        row0 = pl.multiple_of(jnp.clip(i - j, 0, TOEP_TILES - 1) * LANES, LANES)
        return [jnp.where(mask, head(sc, p) + toep_ref[gg * ATT_HPG + p, pl.ds(row0, LANES), :], NEG) for p in heads]

    def weights_times_v(scores, m, kv):
        es = [jnp.exp(scores[p] - m[p]) for p in heads]
        return es, jnp.dot(jnp.concatenate(es, axis=0).astype(BF16), kv, preferred_element_type=F32)

    def front(gg):
        qa = q_ref[:, gg * ATT_HPG * HEAD_DIM:(gg + 1) * ATT_HPG * HEAD_DIM] * (HEAD_DIM ** -0.5)
        qp = []
        for pair in range(ATT_HPG // 2):
            x = qa[:, pair * LANES:(pair + 1) * LANES]
            qp.append(jnp.where(low, x, 0.0))
            qp.append(jnp.where(low, pltpu.roll(x, HEAD_DIM, axis=1), 0.0))
        qs = jnp.concatenate(qp, axis=0).astype(BF16)

        kvc = kvc_ref[0, gg]
        s = lax.dot_general(qs, kvc, NT_DIMS, preferred_element_type=F32)
        s = s + bcmp_ref[gg * ATT_HPG:(gg + 1) * ATT_HPG].reshape(rows, LANES)
        cmask1 = (lane * CMP_STRIDE + (CMP_BLOCK - 1)) <= qpos
        pr = _softmax_rows(s, jnp.concatenate([cmask1] * ATT_HPG, axis=0))
        o_cmp = jnp.dot(pr.astype(BF16), kvc, preferred_element_type=F32)
        hi, lo = _split2(sum(head(pr, p) for p in heads))
        imp = jnp.dot(hi, ov, preferred_element_type=F32) + jnp.dot(lo, ov, preferred_element_type=F32)
        sel_t = _select_blocks(imp.T[:n_blk, :], qpos_t, n_blk, axis=0).astype(BF16)

        n_win = WINDOW // LANES + 1
        win_kv, win_s = [], []
        for w in range(n_win):
            j = i - (n_win - 1) + w
            dist = (i - j) * LANES + sub - lane
            inside = jnp.where(dist >= 0, jnp.where(dist < WINDOW, 1.0, 0.0), 0.0)
            win_kv.append(key_tile(win_ref, gg, j))
            win_s.append(masked_scores(qs, gg, win_kv[w], j, jnp.where(j >= 0, inside, 0.0) > 0.5))
        m_w = [jnp.max(functools.reduce(jnp.maximum, [win_s[w][p] for w in range(n_win)]), axis=1, keepdims=True)
               for p in heads]
        acc_w = jnp.zeros((rows, LANES), F32)
        l_w = [jnp.zeros(tile, F32) for _ in heads]
        for w in range(n_win):
            es, pv = weights_times_v(win_s[w], m_w, win_kv[w])
            acc_w = acc_w + pv
            l_w = [l_w[p] + es[p] for p in heads]
        o_win = [head(acc_w, p) / jnp.sum(l_w[p], axis=1, keepdims=True) for p in heads]
        return qs, o_cmp, sel_t, o_win

    fronts = [front(gg) for gg in range(ngs)]

    n_steps = (i + SLC_UNROLL) >> (SLC_UNROLL.bit_length() - 1)
    m_scr[...] = jnp.full(m_scr.shape, NEG, F32)

    def slc_scores(jj, carry):
        for gg in range(ngs):
            qs, _, sel_t, _ = fronts[gg]
            sps = []
            for r in range(SLC_UNROLL):
                j = SLC_UNROLL * jj + r
                dist = (i - j) * LANES + sub - lane
                expand = jnp.where(blk_row == 2 * j + blk_of_key, 1.0, 0.0).astype(BF16)
                picked = lax.dot_general(sel_t, expand, (((0,), (0,)), ((), ())), preferred_element_type=F32)
                sp = masked_scores(qs, gg, key_tile(slc_ref, gg, j), j, jnp.where(dist >= 0, picked, 0.0) > 0.5)
                for p in heads:
                    s_scr[j, gg * ATT_HPG + p] = sp[p]
                sps.append(sp)
            for p in heads:
                hp = gg * ATT_HPG + p
                m_scr[hp] = jnp.maximum(m_scr[hp], functools.reduce(jnp.maximum, [sp[p] for sp in sps]))
        return carry

    lax.fori_loop(0, n_steps, slc_scores, 0)
    for hp in range(ngs * ATT_HPG):
        m_scr[hp] = jnp.zeros(tile, F32) + jnp.max(m_scr[hp], axis=1, keepdims=True)
    l_scr[...] = jnp.zeros(l_scr.shape, F32)
    acc_scr[...] = jnp.zeros(acc_scr.shape, F32)

    def slc_weights(jj, carry):
        for gg in range(ngs):
            m = [m_scr[gg * ATT_HPG + p] for p in heads]
            pvs, ess = [], []
            for r in range(SLC_UNROLL):
                j = SLC_UNROLL * jj + r
                es, pv = weights_times_v([s_scr[j, gg * ATT_HPG + p] for p in heads], m, key_tile(slc_ref, gg, j))
                pvs.append(pv)
                ess.append(es)
            acc_scr[gg] += functools.reduce(jnp.add, pvs)
            for p in heads:
                hp = gg * ATT_HPG + p
                l_scr[hp] = l_scr[hp] + functools.reduce(jnp.add, [es[p] for es in ess])
        return carry

    lax.fori_loop(0, n_steps, slc_weights, 0)

    for gg in range(ngs):
        _, o_cmp, _, o_win = fronts[gg]
        g_hi, g_lo = _split2(gate_ref[:, glanes(gg)])
        spread = spread_ref[...]
        gb = jnp.dot(g_hi, spread, preferred_element_type=F32) + jnp.dot(g_lo, spread, preferred_element_type=F32)
        comb = []
        for p in heads:
            o_slc = head(acc_scr[gg], p) / jnp.sum(l_scr[gg * ATT_HPG + p], axis=1, keepdims=True)
            col = lambda br: gb[:, (br * ATT_HPG + p) * LANES:(br * ATT_HPG + p + 1) * LANES]
            comb.append(col(0) * head(o_cmp, p) + col(1) * o_slc + col(2) * o_win[p])
        for pair in range(ATT_HPG // 2):
            c0 = gg * ATT_HPG * HEAD_DIM + pair * LANES
            o_ref[:, c0:c0 + LANES] = jnp.where(low, pltpu.roll(comb[2 * pair], HEAD_DIM, axis=1), comb[2 * pair + 1])


def _nsa_prompt(q, gates, kvc, slc_pack, win_pack, bias_cmp, bias_toep, n_batch=BATCH):
    nqb = SEQ // Q_BLOCK
    ngs = NSA_GROUPS_PER_STEP
    ov = _overlap_padded(SEQ // CMP_STRIDE - 1, SEQ // SEL_BLOCK, LANES, LANES)
    gw = ngs * ATT_HPG * HEAD_DIM
    nh = ngs * ATT_HPG
    n_gate = N_BRANCH * ATT_HPG
    spread = np.zeros((LANES, n_gate, LANES), np.float32)
    spread[np.arange(n_gate), np.arange(n_gate), :] = 1.0
    spread = jnp.asarray(spread.reshape(LANES, n_gate * LANES), BF16)
    return pl.pallas_call(
        _nsa_prompt_multi_kernel,
        grid=(n_batch, ATT_GROUPS // ngs, nqb),
        in_specs=[
            pl.BlockSpec((Q_BLOCK, gw), lambda b, g, i: (b * nqb + i, g)),
            pl.BlockSpec((Q_BLOCK, ngs * LANES), lambda b, g, i: (b * nqb + i, g)),
            pl.BlockSpec((1, ngs, SEQ // CMP_STRIDE, LANES), lambda b, g, i: (0, g, b, 0)),
            pl.BlockSpec((SEQ, ngs * LANES), lambda b, g, i: (b, g)),
            pl.BlockSpec((SEQ, ngs * LANES), lambda b, g, i: (b, g)),
            pl.BlockSpec((nh, Q_BLOCK, LANES), lambda b, g, i: (g, i, 0)),
            pl.BlockSpec((nh, TOEP_TILES * LANES, LANES), lambda b, g, i: (g, 0, 0)),
            pl.BlockSpec((LANES, LANES), lambda b, g, i: (0, 0)),
            pl.BlockSpec((LANES, n_gate * LANES), lambda b, g, i: (0, 0)),
        ],
        out_specs=pl.BlockSpec((Q_BLOCK, gw), lambda b, g, i: (b * nqb + i, g)),
        out_shape=jax.ShapeDtypeStruct((n_batch * SEQ, ATT_WIDTH), F32),
        scratch_shapes=[pltpu.VMEM((nqb, nh, Q_BLOCK, LANES), F32),
                        pltpu.VMEM((nh, Q_BLOCK, LANES), F32), pltpu.VMEM((nh, Q_BLOCK, LANES), F32),
                        pltpu.VMEM((ngs, ATT_HPG * Q_BLOCK, LANES), F32)],
        compiler_params=_cparams("arbitrary", "arbitrary", "arbitrary"),
        name="nsa_prompt",
    )(q, gates, kvc, slc_pack, win_pack, bias_cmp, bias_toep, ov, spread)


SAMPLE_ROWS = ATT_HEADS * DEC_SEQ
SAMPLE_PAGES_PER_STEP = 32
SAMPLE_STEPS = N_PAGES // SAMPLE_PAGES_PER_STEP
SAMPLE_N_CMP = (PAST_LEN + DEC_SEQ - CMP_BLOCK) // CMP_STRIDE + 1
SAMPLE_N_BLK = -(-(PAST_LEN + DEC_SEQ) // SEL_BLOCK)
SAMPLE_BLK_LANES = 2 * LANES


def _nsa_sample_kernel(pt_ref, q_ref, gate_ref, kvc_ref, *rest):
    page_refs = rest[:SAMPLE_PAGES_PER_STEP]
    (slc_new_ref, win_state_ref, win_new_ref, bcmp_ref, bslc_ref, bwin_ref, perm_ref, ov_ref,
     o_ref, qbd_scr, picked_scr, m_scr, l_scr, acc_scr, ocmp_scr, owin_scr) = rest[SAMPLE_PAGES_PER_STEP:]
    del pt_ref
    u = pl.program_id(1)
    tile = (SAMPLE_ROWS, LANES)
    wide = (SAMPLE_ROWS, KV_WIDTH)
    lane = lax.broadcasted_iota(jnp.int32, tile, 1)
    t_row = lax.broadcasted_iota(jnp.int32, tile, 0) & (DEC_SEQ - 1)
    rows_per_group = ATT_HPG * DEC_SEQ
    own = ((lax.broadcasted_iota(jnp.int32, wide, 1) >> (HEAD_DIM.bit_length() - 1))
           == (lax.broadcasted_iota(jnp.int32, wide, 0) >> (rows_per_group.bit_length() - 1)))

    def reset():
        m_scr[...] = jnp.full(m_scr.shape, NEG, F32)
        l_scr[...] = jnp.zeros(l_scr.shape, F32)
        acc_scr[...] = jnp.zeros(acc_scr.shape, F32)

    def finish():
        l = jnp.sum(l_scr[...], axis=1, keepdims=True)
        return acc_scr[...] / jnp.where(l > 0, l, 1.0)

    def attend(tiles):
        qbd = qbd_scr[...]
        pairs = [tiles[n:n + 2] for n in range(0, len(tiles), 2)]
        side_by_side = lambda xs: (xs[0] if len(xs) == 1 else jnp.concatenate(xs, axis=1)).astype(BF16)
        scores = []
        m_el = None
        for grp in pairs:
            sc = jnp.dot(qbd, side_by_side([t[0] for t in grp]), preferred_element_type=F32)
            for n, (_, _, bias, mask) in enumerate(grp):
                s = jnp.where(mask, sc[:, n * LANES:(n + 1) * LANES] + bias, NEG)
                scores.append(s)
                m_el = s if m_el is None else jnp.maximum(m_el, s)
        m_prev = m_scr[...]
        m_new = jnp.maximum(m_prev, jnp.max(m_el, axis=1, keepdims=True))
        alpha = jnp.exp(m_prev - m_new)
        l_el = alpha * l_scr[...]
        acc = jnp.concatenate([alpha, alpha], axis=1) * acc_scr[...]
        for k, grp in enumerate(pairs):
            es = [jnp.exp(s - m_new) for s in scores[2 * k:2 * k + len(grp)]]
            l_el = l_el + functools.reduce(jnp.add, es)
            acc = acc + lax.dot_general(side_by_side(es), side_by_side([t[1] for t in grp]), NT_DIMS,
                                        preferred_element_type=F32)
        m_scr[...] = m_new
        l_scr[...] = l_el
        acc_scr[...] = acc

    def page_tile(kv4, bias, mask):
        return (kv4[0].reshape(KV_WIDTH, LANES), kv4[1].reshape(KV_WIDTH, LANES), bias, mask)

    def new_tile(ref, bias, mask):
        rows = jnp.concatenate([ref[...], jnp.zeros((LANES - DEC_SEQ, KV_ROW), F32)], axis=0)
        return (rows[:, :KV_WIDTH].T, rows[:, KV_WIDTH:].T, bias, mask)

    @pl.when(u == 0)
    def _():
        q = (q_ref[...] * (HEAD_DIM ** -0.5)).astype(BF16)
        qperm = jnp.dot(q, perm_ref[...], preferred_element_type=F32)
        qfull = jnp.concatenate([qperm[:, p * KV_WIDTH:(p + 1) * KV_WIDTH]
                                 for g in range(ATT_GROUPS) for p in range(ATT_HPG)], axis=0)
        qbd = jnp.where(own, qfull, 0.0).astype(BF16)
        qbd_scr[...] = qbd

        kvc = kvc_ref[0]
        s = lax.dot_general(qbd, kvc[:, :KV_WIDTH], NT_DIMS, preferred_element_type=F32) + bcmp_ref[...]
        cmask = lax.broadcasted_iota(jnp.int32, s.shape, 1) < SAMPLE_N_CMP
        pr = _softmax_rows(s, cmask)
        ocmp_scr[...] = jnp.dot(pr.astype(BF16), kvc[:, KV_WIDTH:], preferred_element_type=F32)
        ps = []
        for g in range(ATT_GROUPS):
            r0 = g * rows_per_group
            ps.append(sum(pr[r0 + p * DEC_SEQ:r0 + (p + 1) * DEC_SEQ, :] for p in range(ATT_HPG)))
        hi, lo = _split2(jnp.concatenate(ps, axis=0))
        ov = ov_ref[...]
        imp = jnp.dot(hi, ov, preferred_element_type=F32) + jnp.dot(lo, ov, preferred_element_type=F32)
        qpos = PAST_LEN + (lax.broadcasted_iota(jnp.int32, imp.shape, 0) & (DEC_SEQ - 1))
        sel = _select_blocks(imp, qpos, SAMPLE_N_BLK, axis=1)
        sel_rows = jnp.concatenate([sel[g * DEC_SEQ:(g + 1) * DEC_SEQ, :]
                                    for g in range(ATT_GROUPS) for p in range(ATT_HPG)], axis=0).astype(BF16)
        span = 4 * LANES
        blk_row = lax.broadcasted_iota(jnp.int32, (SAMPLE_BLK_LANES, span), 0)
        key_blk = lax.broadcasted_iota(jnp.int32, (SAMPLE_BLK_LANES, span), 1) >> SEL_SHIFT
        for c in range(PAST_LEN // span):
            expand = jnp.where(blk_row == key_blk + c * (span // SEL_BLOCK), 1.0, 0.0).astype(BF16)
            picked_scr[:, c * span:(c + 1) * span] = jnp.dot(sel_rows, expand, preferred_element_type=F32)

        reset()
        tiles = [page_tile(win_state_ref[0, :, :, :, w * LANES:(w + 1) * LANES],
                           bwin_ref[:, w * LANES:(w + 1) * LANES], (w * LANES + lane) > t_row)
                 for w in range(WINDOW // LANES)]
        tiles.append(new_tile(win_new_ref, bwin_ref[:, WINDOW:WINDOW + LANES], lane <= t_row))
        attend(tiles)
        owin_scr[...] = finish()
        reset()

    tiles = []
    for k in range(SAMPLE_PAGES_PER_STEP):
        keys = pl.ds(pl.multiple_of((u * SAMPLE_PAGES_PER_STEP + k) * LANES, LANES), LANES)
        tiles.append(page_tile(page_refs[k][0], bslc_ref[:, keys], picked_scr[:, keys] > 0.5))
    attend(tiles)

    @pl.when(u == SAMPLE_STEPS - 1)
    def _():
        attend([new_tile(slc_new_ref, bslc_ref[:, PAST_LEN:PAST_LEN + LANES], lane <= t_row)])
        o_slc = finish()
        gt = gate_ref[...]

        def gate_rows(br):
            cols = []
            for g in range(ATT_GROUPS):
                for p in range(ATT_HPG):
                    c = g * LANES + br * ATT_HPG + p
                    cols.append(jnp.broadcast_to(gt[:, c:c + 1], (DEC_SEQ, KV_WIDTH)))
            return jnp.concatenate(cols, axis=0)

        comb = gate_rows(0) * ocmp_scr[...] + gate_rows(1) * o_slc + gate_rows(2) * owin_scr[...]
        comb = jnp.where(own, comb, 0.0)
        per_head = []
        for p in range(ATT_HPG):
            per_head.append(sum(comb[(g * ATT_HPG + p) * DEC_SEQ:(g * ATT_HPG + p + 1) * DEC_SEQ, :]
                                for g in range(ATT_GROUPS)))
        hi, lo = _split2(jnp.concatenate(per_head, axis=1))
        perm = perm_ref[...]
        o_ref[...] = (lax.dot_general(hi, perm, NT_DIMS, preferred_element_type=F32)
                      + lax.dot_general(lo, perm, NT_DIMS, preferred_element_type=F32))


def _nsa_sample(q, gates, kvc, cache_slc, page_table, slc_rows, state_win, win_rows, bias_cmp, bias_slc, bias_win):
    n_seq = page_table.shape[0]
    row0 = N_PROMPT // DEC_SEQ
    new = lambda w: pl.BlockSpec((DEC_SEQ, w), lambda b, u, pt: (b, 0))
    src = np.arange(ATT_WIDTH)
    g, p, d = src // (ATT_HPG * HEAD_DIM), (src // HEAD_DIM) % ATT_HPG, src % HEAD_DIM
    perm = np.zeros((ATT_WIDTH, ATT_WIDTH), np.float32)
    perm[src, p * KV_WIDTH + g * HEAD_DIM + d] = 1.0
    ov = _overlap_padded(SAMPLE_N_CMP, SAMPLE_N_BLK, PAST_LEN // CMP_STRIDE, SAMPLE_BLK_LANES)
    tok = lambda w: pl.BlockSpec((DEC_SEQ, w), lambda b, u, pt: (row0 + b, 0))
    const = lambda a: pl.BlockSpec(a.shape, lambda b, u, pt, nd=a.ndim: (0,) * nd)
    per_seq = lambda s: pl.BlockSpec((1,) + s, lambda b, u, pt, nd=len(s): (b,) + (0,) * nd)
    pages = [pl.BlockSpec((1, 2, ATT_GROUPS, HEAD_DIM, PAGE_SIZE),
                          lambda b, u, pt, k=k: (pt[b, u * SAMPLE_PAGES_PER_STEP + k], 0, 0, 0, 0))
             for k in range(SAMPLE_PAGES_PER_STEP)]
    consts = (bias_cmp, bias_slc, bias_win, jnp.asarray(perm, BF16), ov)
    grid_spec = pltpu.PrefetchScalarGridSpec(
        num_scalar_prefetch=1, grid=(n_seq, SAMPLE_STEPS),
        in_specs=[tok(ATT_WIDTH), tok(ATT_GROUPS * LANES), per_seq((PAST_LEN // CMP_STRIDE, KV_ROW))] + pages
        + [new(KV_ROW), per_seq((2, ATT_GROUPS, HEAD_DIM, WINDOW)), new(KV_ROW)] + [const(a) for a in consts],
        out_specs=pl.BlockSpec((DEC_SEQ, ATT_WIDTH), lambda b, u, pt: (b, 0)),
        scratch_shapes=[pltpu.VMEM((SAMPLE_ROWS, KV_WIDTH), BF16), pltpu.VMEM((SAMPLE_ROWS, PAST_LEN), F32),
                        pltpu.VMEM((SAMPLE_ROWS, LANES), F32), pltpu.VMEM((SAMPLE_ROWS, LANES), F32),
                        pltpu.VMEM((SAMPLE_ROWS, KV_WIDTH), F32), pltpu.VMEM((SAMPLE_ROWS, KV_WIDTH), F32),
                        pltpu.VMEM((SAMPLE_ROWS, KV_WIDTH), F32)])
    return pl.pallas_call(
        _nsa_sample_kernel, grid_spec=grid_spec,
        out_shape=jax.ShapeDtypeStruct((n_seq * DEC_SEQ, ATT_WIDTH), F32),
        compiler_params=_cparams("arbitrary", "arbitrary"),
        name="nsa_sample",
    )(page_table, q, gates, kvc, *([cache_slc] * SAMPLE_PAGES_PER_STEP), slc_rows, state_win, win_rows, *consts)


def kernel(x_prompt, x_sample, cache_cmp_kv, cache_slc_kv, state_win_kv, state_hgrn, page_table,
           rel_bias_table, hgrn_lower_bound, norm_ffn1, w_ffn1_gate_up, w_ffn1_down, norm_mix,
           w_in, q_norm, k_norm, w_cmp1, b_cmp1, w_cmp2, b_cmp2, attn_out_norm, hgrn_out_norm,
           w_out, norm_ffn2, w_ffn2_gate_up, w_ffn2_down):
    assert DEPTH == 1
    l = 0
    kv_shape = (2, ATT_GROUPS, HEAD_DIM)

    y1 = _ffn((x_prompt.reshape(N_PROMPT, D_MODEL), x_sample.reshape(N_SAMPLE, D_MODEL)), norm_ffn1[l],
              w_ffn1_gate_up[l].astype(BF16), w_ffn1_down[l].astype(BF16))

    (q, cmp_rows, slc_rows, slc_pack, win_rows, win_pack, hq, lf, hk, hv, hog, gates,
     cmp_new, slc_new, win_new) = _project_all(
        y1, norm_mix[l], _permute_w_in(w_in[l]), q_norm[l], k_norm[l], hgrn_lower_bound)

    tbl = rel_bias_table.astype(F32)
    first_end = CMP_BLOCK - 1
    bias_cmp_p = _bias_table(tbl, SEQ, LANES, -first_end, 1, -CMP_STRIDE)
    bias_toep = _bias_table(tbl, TOEP_TILES * LANES, LANES, 0, 1, -1)
    bias_cmp_s = _bias_table(tbl, DEC_SEQ, PAST_LEN // CMP_STRIDE, PAST_LEN - first_end, 1, -CMP_STRIDE)
    bias_slc_s = _bias_table(tbl, DEC_SEQ, PAST_LEN + LANES, PAST_LEN, 1, -1)
    bias_win_s = _bias_table(tbl, DEC_SEQ, WINDOW + LANES, WINDOW, 1, -1)
    rows_ht = lambda a: a.reshape(SAMPLE_ROWS, a.shape[-1])

    cw = _compress_weights(w_cmp1[l], b_cmp1[l], w_cmp2[l], b_cmp2[l])
    kvc_p = _compress(cmp_rows, cw, k_norm[l][0])
    kvc_s = _compress_paged(_cache_view(cache_cmp_kv[l]), page_table, w_cmp1[l], b_cmp1[l], w_cmp2[l], b_cmp2[l],
                            k_norm[l][0])

    o_att_p = _nsa_prompt(q, gates, kvc_p, slc_pack, win_pack, bias_cmp_p, bias_toep)
    o_att_s = _nsa_sample(q, gates, kvc_s, _cache_view(cache_slc_kv[l]), page_table, slc_new,
                          _cache_view(state_win_kv[l]), win_new,
                          rows_ht(bias_cmp_s), rows_ht(bias_slc_s), rows_ht(bias_win_s))

    o_hg_p, hg_p = _hgrn(hq, hk, hv, lf, None, n_seq=BATCH, t_len=SEQ, ch=HG_CHUNK_PROMPT, row0=0)
    o_hg_s, hg_s = _hgrn(hq, hk, hv, lf, state_hgrn[l].astype(F32), n_seq=DEC_BATCH, t_len=DEC_SEQ,
                         ch=HG_CHUNK_SAMPLE, row0=N_PROMPT)

    y2 = _mixer_out(y1, o_att_p, o_att_s, o_hg_p, o_hg_s, hog, attn_out_norm[l], hgrn_out_norm[l],
                    w_out[l].astype(BF16))
    y3_p, y3_s = _ffn(y2, norm_ffn2[l], w_ffn2_gate_up[l].astype(BF16), w_ffn2_down[l].astype(BF16), split_out=True)

    prompt_rows = lambda a: a.reshape((1, BATCH, SEQ) + kv_shape)
    sample_rows = lambda a: a.reshape((1, DEC_BATCH, DEC_SEQ) + kv_shape)
    win_p = prompt_rows(win_rows)[:, :, SEQ - min(WINDOW, SEQ):]
    win_s = jnp.concatenate([state_win_kv[l][:, DEC_SEQ:], sample_rows(win_new)[0]], axis=1)[None]
    return (y3_p.reshape(BATCH, SEQ, D_MODEL), y3_s.reshape(DEC_BATCH, DEC_SEQ, D_MODEL),
            prompt_rows(cmp_rows), prompt_rows(slc_rows), win_p, hg_p[None],
            sample_rows(cmp_new), sample_rows(slc_new), win_s, hg_s[None].astype(state_hgrn.dtype))
```

```python
import functools
import math

import jax
import jax.numpy as jnp
import numpy as np
from jax import lax
from jax.experimental import pallas as pl
from jax.experimental.pallas import tpu as pltpu

D_MODEL = 2048
BATCH = 4
SEQ = 2048
DEPTH = 1
DEC_BATCH = 32
DEC_SEQ = 8
PAST_LEN = 8192
PAGE_SIZE = 128
HEAD_DIM = 64
ATT_HEADS = (D_MODEL // 2) // HEAD_DIM
ATT_GROUPS = ATT_HEADS // 4
ATT_HPG = ATT_HEADS // ATT_GROUPS
ATT_WIDTH = ATT_HEADS * HEAD_DIM
KV_WIDTH = ATT_GROUPS * HEAD_DIM
N_BRANCH = 3
CMP_BLOCK = 32
CMP_STRIDE = 16
CMP_RATIO = CMP_BLOCK // CMP_STRIDE
CMP_HIDDEN = 256
SEL_BLOCK = 64
N_SEL = 8
WINDOW = 512
Q_BLOCK = 128
HG_KEY = 128
HG_VAL = 128
HG_HEADS = (D_MODEL // 2) // HG_VAL
HG_WIDTH = HG_HEADS * HG_VAL
MIX_WIDTH = ATT_WIDTH + HG_WIDTH
D_FF = 256 * ((8 * D_MODEL // 3 + 255) // 256)
N_BUCKETS = 32
MAX_DISTANCE = 1024
EPS = 1e-6
IN_SPLITS = (ATT_WIDTH, 2 * N_BRANCH * KV_WIDTH, N_BRANCH * ATT_HEADS,
             HG_HEADS * HG_KEY, HG_HEADS * HG_KEY, HG_WIDTH, HG_WIDTH)

N_PROMPT = BATCH * SEQ
N_SAMPLE = DEC_BATCH * DEC_SEQ
N_TOKENS = N_PROMPT + N_SAMPLE
N_PAGES = PAST_LEN // PAGE_SIZE
KV_ROW = 2 * KV_WIDTH

LANES = 128
V7X_VMEM_BYTES = 64 * 1024 * 1024
VMEM_LIMIT_BYTES = V7X_VMEM_BYTES * 7 // 8

NEG = -1e30
F32 = jnp.float32
BF16 = jnp.bfloat16

FFN_ROW_TILE = 768
FFN_FF_TILE = 512
PROJ_ROW_TILE = 528
PROJ_COL_TILE = 512
OUT_ROW_TILE = 256
HG_CHUNK_PROMPT = 16
HG_CHUNK_SAMPLE = DEC_SEQ
CMP_UNIT_ROWS = 2048
CMP_UNITS = 4


def _cparams(*sem):
    return pltpu.CompilerParams(dimension_semantics=sem, vmem_limit_bytes=VMEM_LIMIT_BYTES)


def _t5_thresholds():
    n = np.arange(0, 2 * MAX_DISTANCE + 2)
    exact = N_BUCKETS // 2
    logn = np.log(np.maximum(n, 1).astype(np.float64) / exact)
    large = exact + (logn / math.log(MAX_DISTANCE / exact) * (N_BUCKETS - exact)).astype(np.int32)
    b = np.where(n < exact, n, np.minimum(large, N_BUCKETS - 1))
    return [int(n[b >= k][0]) for k in range(N_BUCKETS)]


T5_THRESHOLDS = _t5_thresholds()


def _block_diag_ones(n, blk):
    i = np.arange(n)
    return (i[:, None] // blk == i[None, :] // blk).astype(np.float32)


def _split2(x):
    hi = x.astype(BF16)
    lo = (x - hi.astype(F32)).astype(BF16)
    return hi, lo


def _group_rms(t, bd, gain, width):
    hi, lo = _split2(t * t)
    ss = (jnp.dot(hi, bd, preferred_element_type=F32) + jnp.dot(lo, bd, preferred_element_type=F32))
    return t * lax.rsqrt(ss * (1.0 / width) + EPS) * gain


def _ffn_kernel(*refs, split_in, split_out):
    n_in = 2 if split_in else 1
    x_ref = refs[0]
    gain_ref, wg_ref, wu_ref, wd_ref = refs[n_in:n_in + 4]
    o_ref = refs[n_in + 4]
    xn_ref, acc_ref = refs[-2:]
    j = pl.program_id(1)
    last_tile = pl.program_id(0) == pl.num_programs(0) - 1
    tm = x_ref.shape[0]

    def rows_in():
        x = x_ref[...]
        if split_in:
            xs = refs[1][...]
            x = jnp.where(last_tile, jnp.concatenate([x[:tm - xs.shape[0]], xs], axis=0), x)
        return x

    @pl.when(j == 0)
    def _():
        x = rows_in()
        y = x * lax.rsqrt(jnp.mean(x * x, axis=-1, keepdims=True) + EPS)
        xn_ref[...] = (y * gain_ref[...]).astype(BF16)
        acc_ref[...] = jnp.zeros_like(acc_ref)

    xn = xn_ref[...]
    g = jnp.dot(xn, wg_ref[...], preferred_element_type=F32)
    u = jnp.dot(xn, wu_ref[...], preferred_element_type=F32)
    a = (g * jax.nn.sigmoid(g) * u).astype(BF16)
    acc_ref[...] += jnp.dot(a, wd_ref[...], preferred_element_type=F32)

    @pl.when(j == pl.num_programs(1) - 1)
    def _():
        res = rows_in() + 0.5 * acc_ref[...]
        o_ref[...] = res
        if split_out:
            os_ref = refs[n_in + 5]

            @pl.when(last_tile)
            def _():
                os_ref[...] = res[tm - os_ref.shape[0]:]


def _ffn(x, gain, w_gu, w_down, split_out=False):
    split_in = isinstance(x, tuple)
    tm, tf = FFN_ROW_TILE, FFN_FF_TILE
    n_p, n_s = N_PROMPT, N_SAMPLE
    n, d = n_p + n_s, D_MODEL
    assert n % tm == 0 and D_FF % tf == 0 and n_p % tm == tm - n_s
    nj = D_FF // tf
    row = pl.BlockSpec((tm, d), lambda i, j: (i, 0))
    tail = pl.BlockSpec((n_s, d), lambda i, j: (0, 0))
    xs = x if split_in else (x,)
    return pl.pallas_call(
        functools.partial(_ffn_kernel, split_in=split_in, split_out=split_out),
        grid=(n // tm, nj),
        in_specs=([row, tail] if split_in else [row]) + [
            pl.BlockSpec((1, d), lambda i, j: (0, 0)),
            pl.BlockSpec((d, tf), lambda i, j: (0, j)),
            pl.BlockSpec((d, tf), lambda i, j: (0, j + nj)),
            pl.BlockSpec((tf, d), lambda i, j: (j, 0)),
        ],
        out_specs=(row, tail) if split_out else row,
        out_shape=((jax.ShapeDtypeStruct((n_p, d), F32), jax.ShapeDtypeStruct((n_s, d), F32)) if split_out
                   else jax.ShapeDtypeStruct((n, d), F32)),
        scratch_shapes=[pltpu.VMEM((tm, d), BF16), pltpu.VMEM((tm, d), F32)],
        compiler_params=_cparams("arbitrary", "arbitrary"),
        name="ffn",
    )(*xs, gain.reshape(1, d), w_gu, w_gu, w_down)


PB_Q, PB_CMP, PB_SLC, PB_WIN, PB_HQ, PB_HF, PB_HI, PB_HG, PB_GATE = 0, 2, 3, 4, 5, 7, 9, 11, 13
PROJ_N_BLOCKS = 14


def _permute_w_in(w_in):
    p = [int(v) for v in np.cumsum(IN_SPLITS)]
    a_g = w_in[:, p[1]:p[2]]
    src = np.zeros((PROJ_COL_TILE,), np.int32)
    valid = np.zeros((PROJ_COL_TILE,), bool)
    for g in range(ATT_GROUPS):
        for br in range(N_BRANCH):
            for hp in range(ATT_HPG):
                src[g * LANES + br * ATT_HPG + hp] = br * ATT_HEADS + g * ATT_HPG + hp
                valid[g * LANES + br * ATT_HPG + hp] = True
    gate = jnp.where(jnp.asarray(valid)[None, :], a_g[:, src], 0.0)
    return jnp.concatenate([w_in[:, :p[1]], w_in[:, p[2]:], gate], axis=1).astype(BF16)


def _pack_kv(k, v):
    parts = []
    for g in range(ATT_GROUPS):
        parts.append(k[:, g * HEAD_DIM:(g + 1) * HEAD_DIM])
        parts.append(v[:, g * HEAD_DIM:(g + 1) * HEAD_DIM])
    return jnp.concatenate(parts, axis=1).astype(BF16)


def _proj_kernel(x_ref, gain_ref, w_ref, bd_ref, qg_ref, kg_ref, lbp_ref,
                 q_ref, cmp_ref, slc_ref, slcp_ref, win_ref, winp_ref,
                 hq_ref, lf_ref, hk_ref, hv_ref, hog_ref, gate_ref, cmps_ref, slcs_ref, wins_ref, xn_ref):
    c = pl.program_id(1)
    last_tile = pl.program_id(0) == pl.num_programs(0) - 1
    tail = x_ref.shape[0] - cmps_ref.shape[0]

    @pl.when(c == 0)
    def _():
        x = x_ref[...]
        y = x * lax.rsqrt(jnp.mean(x * x, axis=-1, keepdims=True) + EPS)
        xn_ref[...] = (y * gain_ref[...]).astype(BF16)

    acc = jnp.dot(xn_ref[...], w_ref[...], preferred_element_type=F32)
    half = KV_WIDTH

    @pl.when(c < PB_CMP)
    def _():
        q_ref[...] = _group_rms(acc, bd_ref[...], qg_ref[...], HEAD_DIM)

    @pl.when(c == PB_CMP)
    def _():
        cmp_ref[...] = acc

        @pl.when(last_tile)
        def _():
            cmps_ref[...] = acc[tail:]

    def kv_branch(rows_ref, sample_ref, pack_ref, br):
        k = _group_rms(acc[:, :half], bd_ref[:half, :half], kg_ref[br - 1:br, :], HEAD_DIM)
        v = acc[:, half:]
        rows_ref[:, :half] = k
        rows_ref[:, half:] = v
        pack_ref[...] = _pack_kv(k, v)

        @pl.when(last_tile)
        def _():
            sample_ref[:, :half] = k[tail:]
            sample_ref[:, half:] = v[tail:]

    @pl.when(c == PB_SLC)
    def _():
        kv_branch(slc_ref, slcs_ref, slcp_ref, 1)

    @pl.when(c == PB_WIN)
    def _():
        kv_branch(win_ref, wins_ref, winp_ref, 2)

    @pl.when((c >= PB_HQ) & (c < PB_HF))
    def _():
        hq_ref[...] = acc

    @pl.when((c >= PB_HF) & (c < PB_HI))
    def _():
        p = lbp_ref[...]
        e = jnp.exp(p - jnp.max(p, axis=0, keepdims=True))
        lb = e[0:1, :] / jnp.sum(e, axis=0, keepdims=True)
        lf_ref[...] = jnp.log(lb + (1.0 - lb) * jax.nn.sigmoid(acc))
        hk_ref[...] = (1.0 - lb) * jax.nn.sigmoid(-acc)

    @pl.when((c >= PB_HI) & (c < PB_HG))
    def _():
        hv_ref[...] = acc

    @pl.when((c >= PB_HG) & (c < PB_GATE))
    def _():
        hog_ref[...] = acc

    @pl.when(c == PB_GATE)
    def _():
        gate_ref[...] = jax.nn.sigmoid(acc)


def _project_all(y, gain, w_perm, q_gain, k_gain, lb_logits):
    n, d = y.shape
    tm, tc = PROJ_ROW_TILE, PROJ_COL_TILE
    assert n == N_TOKENS and n % tm == 0 and N_PROMPT % tm == tm - N_SAMPLE and DEPTH == 1
    bd = jnp.asarray(_block_diag_ones(tc, HEAD_DIM), BF16)

    def two(first):
        return lambda i, c: (i, jnp.clip(c - first, 0, 1))

    one = lambda i, c: (i, 0)
    wide = lambda dt: jax.ShapeDtypeStruct((n, 2 * tc), dt)
    narrow = lambda dt: jax.ShapeDtypeStruct((n, tc), dt)
    prompt = jax.ShapeDtypeStruct((N_PROMPT, tc), F32)
    sample = jax.ShapeDtypeStruct((N_SAMPLE, tc), F32)
    out_shape = (wide(F32), prompt, prompt, narrow(BF16), prompt, narrow(BF16),
                 wide(F32), wide(F32), wide(F32), wide(F32), wide(F32), narrow(F32), sample, sample, sample)
    blk = lambda f: pl.BlockSpec((tm, tc), f)
    tail = pl.BlockSpec((N_SAMPLE, tc), lambda i, c: (0, 0))
    out_specs = (blk(two(PB_Q)), blk(one), blk(one), blk(one), blk(one), blk(one),
                 blk(two(PB_HQ)), blk(two(PB_HF)), blk(two(PB_HF)), blk(two(PB_HI)), blk(two(PB_HG)), blk(one),
                 tail, tail, tail)
    return pl.pallas_call(
        _proj_kernel,
        grid=(n // tm, PROJ_N_BLOCKS),
        in_specs=[
            pl.BlockSpec((tm, d), lambda i, c: (i, 0)),
            pl.BlockSpec((1, d), lambda i, c: (0, 0)),
            pl.BlockSpec((d, tc), lambda i, c: (0, c)),
            pl.BlockSpec((tc, tc), lambda i, c: (0, 0)),
            pl.BlockSpec((1, tc), lambda i, c: (0, jnp.clip(c, 0, 1))),
            pl.BlockSpec((2, KV_WIDTH), lambda i, c: (0, 0)),
            pl.BlockSpec((DEPTH + 1, tc), lambda i, c: (0, jnp.clip(c - PB_HF, 0, 1))),
        ],
        out_specs=out_specs,
        out_shape=out_shape,
        scratch_shapes=[pltpu.VMEM((tm, d), BF16)],
        compiler_params=_cparams("arbitrary", "arbitrary"),
        name="proj",
    )(y, gain.reshape(1, d), w_perm, bd, q_gain.reshape(1, ATT_WIDTH),
      k_gain[1:].reshape(2, KV_WIDTH), lb_logits)


def _hgrn_prepare(q_ref, k_ref, v_ref, lf_ref, tri_ref, ones_ref, s0_ref, o_ref,
                  qe_scr, kd_scr, vt_scr, dec_scr, b_scr, *, t_len, ch, has_state):
    nj = t_len // ch
    shape3 = (nj, ch, HG_KEY)
    q3, k3, v3, lf3 = q_ref[...], k_ref[...], v_ref[...], lf_ref[...]
    tl = lax.broadcasted_iota(jnp.int32, shape3, 1)

    def row(x3, s):
        return jnp.broadcast_to(x3[:, s:s + 1, :], shape3)

    if t_len >= 256:
        lf2 = lf3.reshape(t_len, HG_KEY)
        parts = []
        for r0 in range(0, t_len, 256):
            x = lf2[r0:r0 + 256]
            hi = x.astype(BF16)
            r1 = x - hi.astype(F32)
            mid = r1.astype(BF16)
            lo = (r1 - mid.astype(F32)).astype(BF16)
            tri = tri_ref[...]
            parts.append(jnp.dot(tri, hi, preferred_element_type=F32)
                         + jnp.dot(tri, mid, preferred_element_type=F32)
                         + jnp.dot(tri, lo, preferred_element_type=F32))
        b3 = jnp.concatenate(parts, axis=0).reshape(shape3)
    else:
        b3 = jnp.zeros(shape3, F32)
        for s in range(ch):
            b3 = b3 + jnp.where(tl >= s, row(lf3, s), 0.0)

    bl3 = row(b3, ch - 1)
    qe3 = q3 * jnp.exp(b3)
    kd3 = k3 * jnp.exp(bl3 - b3)
    dec_scr[...] = jnp.exp(b3[:, ch - 1:ch, :])
    blocked = t_len % LANES == 0
    if blocked:
        nb = t_len // LANES
        qe2, kd2, v2 = (x.reshape(t_len, HG_KEY) for x in (qe3, kd3, v3))
        for m in range(nb):
            blk = slice(m * LANES, (m + 1) * LANES)
            qe_scr[m] = qe2[blk].T.astype(BF16)
            vt_scr[m] = v2[blk].T.astype(BF16)
            kd_scr[m] = kd2[blk].astype(BF16)
    else:
        qe_scr[...] = qe3.astype(BF16)
        kd_scr[...] = kd3.astype(BF16)

    ones = ones_ref[...]
    n8 = ch // 8
    shape8 = (nj, 8, HG_KEY)
    sub8 = lax.broadcasted_iota(jnp.int32, shape8, 1)
    q5, b5 = (x.reshape(nj, n8, 8, HG_KEY) for x in (q3, b3))
    b_scr[...] = b3
    od = [jnp.zeros(shape8, F32) for _ in range(n8)]
    for s in range(ch):
        hs, ss = divmod(s, 8)
        ks, bs, vs = (jnp.broadcast_to(r[:, s:s + 1, :], shape8) for r in (k_ref, b_scr, v_ref))
        for hh in range(hs, n8):
            diff = b5[:, hh] - bs
            if hh == hs:
                diff = jnp.where(sub8 >= ss, diff, NEG)
            w = q5[:, hh] * ks * jnp.exp(diff)
            a = jnp.dot(w.reshape(nj * 8, HG_KEY).astype(BF16), ones, preferred_element_type=F32)
            od[hh] = od[hh] + a.reshape(shape8) * vs
    for hh in range(n8):
        o_ref[:, hh * 8:(hh + 1) * 8, :] = od[hh]

    return s0_ref[0, 0].T if has_state else jnp.zeros((HG_VAL, HG_KEY), F32)


def _hgrn_scan(heads, st0s, *, t_len, ch):
    nj = t_len // ch
    if t_len % LANES == 0:
        cpb = LANES // ch
        lane_chunk = lax.broadcasted_iota(jnp.int32, (HG_KEY, LANES), 1) >> (ch.bit_length() - 1)
        keep = [jnp.where(lane_chunk == r, 1.0, 0.0).astype(BF16) for r in range(cpb)]

        def body(m, sts):
            out = []
            for (_, _, _, qe_scr, kd_scr, vt_scr, dec_scr, oi_scr), st in zip(heads, sts):
                qet, vt, kd = qe_scr[m], vt_scr[m], kd_scr[m]
                ot = jnp.zeros((HG_VAL, LANES), F32)
                for r in range(cpb):
                    ot = ot + jnp.dot(st.astype(BF16), qet * keep[r], preferred_element_type=F32)
                    ut = jnp.dot(vt * keep[r], kd, preferred_element_type=F32)
                    st = st * dec_scr[m * cpb + r] + ut
                oi_scr[m] = ot
                out.append(st)
            return tuple(out)

        sts = lax.fori_loop(0, t_len // LANES, body, tuple(st0s))
        for (_, o_ref, _, _, _, _, _, oi_scr) in heads:
            for m in range(t_len // LANES):
                o_ref[m * cpb:(m + 1) * cpb] = o_ref[m * cpb:(m + 1) * cpb] + oi_scr[m].T.reshape(cpb, ch, HG_VAL)
    else:
        sts = []
        for (v_ref, o_ref, _, qe_scr, kd_scr, _, dec_scr, oi_scr), st in zip(heads, st0s):
            for j in range(nj):
                oi_scr[j] = lax.dot_general(qe_scr[j], st.astype(BF16), NT_DIMS, preferred_element_type=F32)
                ut = lax.dot_general(v_ref[j].astype(BF16), kd_scr[j], (((0,), (0,)), ((), ())),
                                     preferred_element_type=F32)
                st = st * dec_scr[j] + ut
            o_ref[...] = o_ref[...] + oi_scr[...]
            sts.append(st)
    for (_, _, st_ref, *_), st in zip(heads, sts):
        st_ref[0, 0] = st.T


def _hgrn(hq, hk, hv, lf, s0, *, n_seq, t_len, ch, row0):
    n = hq.shape[0]
    nj = t_len // ch
    assert row0 % t_len == 0 and t_len % ch == 0 and ch % 8 == 0
    blk0 = row0 // t_len
    r3 = lambda a: a.reshape(n // ch, ch, HG_WIDTH)
    has_state = s0 is not None
    if not has_state:
        s0 = jnp.zeros((1, HG_HEADS, HG_KEY, HG_VAL), F32)
    tri_n = 256 if t_len >= 256 else 8
    i = np.arange(tri_n)
    tri = jnp.asarray(((i[:, None] // ch == i[None, :] // ch) & (i[:, None] >= i[None, :])).astype(np.float32), BF16)
    ones = jnp.ones((HG_KEY, HG_KEY), BF16)
    blk_shape = (t_len // LANES, LANES, HG_KEY) if t_len % LANES == 0 else (nj, ch, HG_KEY)
    hps = HG_HEADS if nj == 1 else 4
    seq = pl.BlockSpec((nj, ch, hps * HG_KEY), lambda b, h: (blk0 + b, 0, h))
    state_in = pl.BlockSpec((1, hps, HG_KEY, HG_VAL), (lambda b, h: (b, h, 0, 0)) if has_state else (lambda b, h: (0, 0, 0, 0)))

    def body(q_ref, k_ref, v_ref, lf_ref, tri_ref, ones_ref, s0_ref, o_ref, st_ref,
             qe_scr, kd_scr, vt_scr, dec_scr, oi_scr, b_scr):
        heads, st0s = [], []
        for h in range(hps):
            ln = slice(h * HG_KEY, (h + 1) * HG_KEY)
            v_h, o_h = v_ref.at[:, :, ln], o_ref.at[:, :, ln]
            s0_h = s0_ref.at[:, h:h + 1] if has_state else s0_ref
            st0s.append(_hgrn_prepare(q_ref.at[:, :, ln], k_ref.at[:, :, ln], v_h, lf_ref.at[:, :, ln], tri_ref, ones_ref,
                                      s0_h, o_h, qe_scr.at[h], kd_scr.at[h], vt_scr.at[h], dec_scr.at[h], b_scr,
                                      t_len=t_len, ch=ch, has_state=has_state))
            heads.append((v_h, o_h, st_ref.at[:, h:h + 1], qe_scr.at[h], kd_scr.at[h], vt_scr.at[h], dec_scr.at[h],
                          oi_scr.at[h]))
        _hgrn_scan(heads, st0s, t_len=t_len, ch=ch)

    o, st = pl.pallas_call(
        body,
        grid=(n_seq, HG_HEADS // hps),
        in_specs=[seq, seq, seq, seq,
                  pl.BlockSpec((tri_n, tri_n), lambda b, h: (0, 0)),
                  pl.BlockSpec((HG_KEY, HG_KEY), lambda b, h: (0, 0)),
                  state_in],
        out_specs=(pl.BlockSpec((nj, ch, hps * HG_VAL), lambda b, h: (b, 0, h)),
                   pl.BlockSpec((1, hps, HG_KEY, HG_VAL), lambda b, h: (b, h, 0, 0))),
        out_shape=(jax.ShapeDtypeStruct((n_seq * nj, ch, HG_WIDTH), F32),
                   jax.ShapeDtypeStruct((n_seq, HG_HEADS, HG_KEY, HG_VAL), F32)),
        scratch_shapes=[pltpu.VMEM((hps,) + blk_shape, BF16), pltpu.VMEM((hps,) + blk_shape, BF16),
                        pltpu.VMEM((hps,) + blk_shape, BF16), pltpu.VMEM((hps, nj, 1, HG_KEY), F32),
                        pltpu.VMEM((hps,) + blk_shape, F32), pltpu.VMEM((nj, ch, HG_KEY), F32)],
        compiler_params=_cparams("arbitrary", "arbitrary"),
        name="hgrn",
    )(r3(hq), r3(hk), r3(hv), r3(lf), tri, ones, s0)
    return o.reshape(n_seq * t_len, HG_WIDTH), st


def _mixout_kernel(y_ref, oap_ref, oas_ref, ohp_ref, ohs_ref, og_ref, ag_ref, hgain_ref, w_ref, o_ref):
    is_sample = pl.program_id(0) == pl.num_programs(0) - 1
    oa = jnp.where(is_sample, oas_ref[...], oap_ref[...])
    a = oa * lax.rsqrt(jnp.mean(oa * oa, axis=-1, keepdims=True) + EPS) * ag_ref[...]
    oh = jnp.where(is_sample, ohs_ref[...], ohp_ref[...])
    hs = []
    for h in range(HG_HEADS):
        x = oh[:, h * HG_VAL:(h + 1) * HG_VAL]
        hs.append(x * lax.rsqrt(jnp.mean(x * x, axis=-1, keepdims=True) + EPS))
    og = og_ref[...]
    hh = jnp.concatenate(hs, axis=1) * hgain_ref[...] * (og * jax.nn.sigmoid(og))
    m = (jnp.dot(a.astype(BF16), w_ref[:ATT_WIDTH, :], preferred_element_type=F32)
         + jnp.dot(hh.astype(BF16), w_ref[ATT_WIDTH:, :], preferred_element_type=F32))
    o_ref[...] = y_ref[...] + m


def _mixer_out(y, o_att_p, o_att_s, o_hg_p, o_hg_s, og, attn_gain, hg_gain, w_out):
    n, d = y.shape
    tm = OUT_ROW_TILE
    n_p = o_att_p.shape[0]
    assert n % tm == 0 and n_p % tm == 0 and o_att_s.shape[0] == tm and n == n_p + tm
    row = lambda w: pl.BlockSpec((tm, w), lambda i: (i, 0))
    prompt_row = lambda w: pl.BlockSpec((tm, w), lambda i: (jnp.minimum(i, n_p // tm - 1), 0))
    const = lambda s: pl.BlockSpec(s, lambda i: (0, 0))
    return pl.pallas_call(
        _mixout_kernel,
        grid=(n // tm,),
        in_specs=[row(d), prompt_row(ATT_WIDTH), const((tm, ATT_WIDTH)), prompt_row(HG_WIDTH), const((tm, HG_WIDTH)),
                  row(HG_WIDTH), const((1, ATT_WIDTH)), const((1, HG_WIDTH)), const((MIX_WIDTH, d))],
        out_specs=row(d),
        out_shape=jax.ShapeDtypeStruct((n, d), F32),
        compiler_params=_cparams("arbitrary"),
        name="mixout",
    )(y, o_att_p, o_att_s, o_hg_p, o_hg_s, og, attn_gain.reshape(1, ATT_WIDTH), hg_gain.reshape(1, HG_WIDTH), w_out)


def _bias_kernel(tbl_ref, o_ref, *, a0, ar, ac, rows_blk):
    h = pl.program_id(0)
    rb = pl.program_id(1)
    shape = o_ref.shape[1:]
    r = lax.broadcasted_iota(jnp.int32, shape, 0) + rb * rows_blk
    c = lax.broadcasted_iota(jnp.int32, shape, 1)
    n = a0 + ar * r + ac * c
    out = jnp.full(shape, tbl_ref[0, h], F32)
    for k in range(1, N_BUCKETS):
        out = jnp.where(n >= T5_THRESHOLDS[k], tbl_ref[k, h], out)
    o_ref[0] = out


def _bias_table(tbl, rows, cols, a0, ar, ac):
    rows_blk = max(r for r in range(8, min(rows, 512) + 1, 8) if rows % r == 0)
    assert cols % LANES == 0
    return pl.pallas_call(
        functools.partial(_bias_kernel, a0=a0, ar=ar, ac=ac, rows_blk=rows_blk),
        grid=(ATT_HEADS, rows // rows_blk),
        in_specs=[pl.BlockSpec(memory_space=pltpu.SMEM)],
        out_specs=pl.BlockSpec((1, rows_blk, cols), lambda h, rb: (h, rb, 0)),
        out_shape=jax.ShapeDtypeStruct((ATT_HEADS, rows, cols), F32),
        compiler_params=_cparams("arbitrary", "arbitrary"),
        name="t5_bias",
    )(tbl)


def _compress_weights(w1, b1, w2, b2):
    w = w1.reshape(2, CMP_RATIO, CMP_STRIDE, HEAD_DIM, CMP_HIDDEN)
    z = jnp.zeros_like(w)
    top = jnp.concatenate([w, z], axis=-1)
    bot = jnp.concatenate([z, w], axis=-1)
    wpad = jnp.stack([top, bot], axis=3)
    wpad = wpad.reshape(2 * CMP_RATIO, CMP_STRIDE * LANES, 2 * CMP_HIDDEN).astype(BF16)
    z2 = jnp.zeros_like(w2)
    w2pad = jnp.concatenate([jnp.concatenate([w2, z2], axis=-1), jnp.concatenate([z2, w2], axis=-1)], axis=1)
    return wpad, jnp.concatenate([b1, b1], axis=-1), w2pad.astype(BF16), jnp.concatenate([b2, b2], axis=-1)


def _compress_kernel(rows_ref, perm_ref, wpad_ref, b1_ref, w2pad_ref, b2_ref, kg_ref, bd_ref, out_ref, xs_ref):
    u = pl.program_id(1)
    n_chunk = CMP_UNITS * CMP_UNIT_ROWS // CMP_STRIDE
    perm = perm_ref[...]

    for t in range(CMP_UNIT_ROWS // 256):
        x = rows_ref[t * 256:(t + 1) * 256, :]
        y = jnp.dot(perm, x.astype(BF16), preferred_element_type=F32).astype(BF16)
        base = pl.multiple_of(u * (CMP_UNIT_ROWS // CMP_STRIDE) + t * 16, 16)
        for s in range(CMP_STRIDE):
            xs_ref[s, pl.ds(base, 16), :] = y[s * 16:(s + 1) * 16, :]

    @pl.when(u == CMP_UNITS - 1)
    def _():
        halves = []
        for kv in range(2):
            outs = []
            for j in range(2):
                col = (kv * 2 + j) * LANES
                lhs = jnp.concatenate([xs_ref[s, :, col:col + LANES] for s in range(CMP_STRIDE)], axis=1)
                h0 = jnp.dot(lhs, wpad_ref[kv * CMP_RATIO + 0], preferred_element_type=F32)
                h1 = jnp.dot(lhs, wpad_ref[kv * CMP_RATIO + 1], preferred_element_type=F32)
                h = b1_ref[kv:kv + 1, :] + h0 + pltpu.roll(h1, n_chunk - 1, axis=0)
                a = (h * jax.nn.sigmoid(h)).astype(BF16)
                outs.append(jnp.dot(a, w2pad_ref[kv], preferred_element_type=F32) + b2_ref[kv:kv + 1, :])
            halves.append(jnp.concatenate(outs, axis=1))
        pk = _pack_kv(_group_rms(halves[0], bd_ref[...], kg_ref[...], HEAD_DIM), halves[1])
        for g in range(ATT_GROUPS):
            out_ref[0, g] = pk[:, g * LANES:(g + 1) * LANES]


def _compress(rows, cw, k_gain0):
    wpad, b1, w2pad, b2 = cw
    n_chunk = CMP_UNITS * CMP_UNIT_ROWS // CMP_STRIDE
    i = np.arange(256)
    perm = np.zeros((256, 256), np.float32)
    perm[(i % 16) * 16 + i // 16, i] = 1.0
    consts = (jnp.asarray(perm, BF16), wpad, b1, w2pad, b2, k_gain0.reshape(1, KV_WIDTH),
              jnp.asarray(_block_diag_ones(KV_WIDTH, HEAD_DIM), BF16))
    const = lambda a: pl.BlockSpec(a.shape, lambda b, u, nd=a.ndim: (0,) * nd)
    return pl.pallas_call(
        _compress_kernel,
        grid=(1, CMP_UNITS),
        in_specs=[pl.BlockSpec((CMP_UNIT_ROWS, KV_ROW), lambda b, u: (u, 0))] + [const(a) for a in consts],
        out_specs=pl.BlockSpec((1, ATT_GROUPS, n_chunk, LANES), lambda b, u: (0, 0, 0, 0)),
        out_shape=jax.ShapeDtypeStruct((1, ATT_GROUPS, n_chunk, LANES), BF16),
        scratch_shapes=[pltpu.VMEM((CMP_STRIDE, n_chunk, KV_ROW), BF16)],
        compiler_params=_cparams("arbitrary", "arbitrary"),
        name="compress",
    )(rows, *consts)


PAGED_UNIT_ROWS = 4096
PAGED_UNITS = PAST_LEN // PAGED_UNIT_ROWS
PAGES_PER_STEP = PAGED_UNIT_ROWS // PAGE_SIZE


def _cache_view(cache):
    return jnp.transpose(cache, (0, 2, 3, 4, 1))


def _compress_paged_kernel(pt_ref, *refs):
    del pt_ref
    page_refs = refs[:PAGES_PER_STEP]
    pick_ref, w1_ref, b1_ref, w2_ref, b2_ref, kg_ref, bd_ref, out_ref, xs_ref = refs[PAGES_PER_STEP:]
    u = pl.program_id(1)
    n_chunk = PAST_LEN // CMP_STRIDE
    pick = pick_ref[...]
    n_kvg = 2 * ATT_GROUPS

    for t in range(PAGES_PER_STEP // 2):
        base = pl.multiple_of(u * (PAGED_UNIT_ROWS // CMP_STRIDE) + t * 16, 16)
        kt = jnp.concatenate([page_refs[2 * t][0].reshape(n_kvg * HEAD_DIM, PAGE_SIZE),
                              page_refs[2 * t + 1][0].reshape(n_kvg * HEAD_DIM, PAGE_SIZE)], axis=1).astype(BF16)
        z = jnp.dot(kt, pick, preferred_element_type=F32)
        for kvg in range(n_kvg):
            zk = z[kvg * HEAD_DIM:(kvg + 1) * HEAD_DIM]
            y = jnp.concatenate([zk[:, :LANES], zk[:, LANES:]], axis=0).T.astype(BF16)
            for s2 in range(CMP_STRIDE // 2):
                xs_ref[kvg, s2, pl.ds(base, 16), :] = y[s2 * 16:(s2 + 1) * 16, :]

    @pl.when(u == PAGED_UNITS - 1)
    def _():
        halves = []
        for kv in range(2):
            lhs = jnp.concatenate(
                [jnp.concatenate([xs_ref[kv * ATT_GROUPS + g, s2] for s2 in range(CMP_STRIDE // 2)], axis=1)
                 for g in range(ATT_GROUPS)], axis=0)
            h0 = jnp.dot(lhs, w1_ref[kv * CMP_RATIO + 0], preferred_element_type=F32)
            h1 = jnp.dot(lhs, w1_ref[kv * CMP_RATIO + 1], preferred_element_type=F32)
            h = b1_ref[kv:kv + 1, :] + h0 + pltpu.roll(h1, ATT_GROUPS * n_chunk - 1, axis=0)
            a = (h * jax.nn.sigmoid(h)).astype(BF16)
            o = b2_ref[kv:kv + 1, :]
            for g in range(ATT_GROUPS):
                o = o + jnp.dot(a[g * n_chunk:(g + 1) * n_chunk], w2_ref[kv * ATT_GROUPS + g], preferred_element_type=F32)
            halves.append(o)
        out_ref[0, :, :KV_WIDTH] = _group_rms(halves[0], bd_ref[...], kg_ref[...], HEAD_DIM).astype(BF16)
        out_ref[0, :, KV_WIDTH:] = halves[1].astype(BF16)


def _compress_paged(cache_t, page_table, w1, b1, w2, b2, k_gain0):
    n_seq = page_table.shape[0]
    n_chunk = PAST_LEN // CMP_STRIDE
    r = np.arange(LANES)
    s2, c = r // 16, r % 16
    pick = np.zeros((2 * PAGE_SIZE, 2, LANES), np.float32)
    for half in range(2):
        pick[CMP_STRIDE * c + 2 * s2 + half, half, r] = 1.0
    w2p = jnp.zeros((2, ATT_GROUPS, CMP_HIDDEN, KV_WIDTH), F32)
    for g in range(ATT_GROUPS):
        w2p = w2p.at[:, g, :, g * HEAD_DIM:(g + 1) * HEAD_DIM].set(w2)
    consts = (jnp.asarray(pick.reshape(2 * PAGE_SIZE, 2 * LANES), BF16),
              w1.reshape(2 * CMP_RATIO, CMP_STRIDE * HEAD_DIM, CMP_HIDDEN).astype(BF16), b1,
              w2p.reshape(2 * ATT_GROUPS, CMP_HIDDEN, KV_WIDTH).astype(BF16), jnp.tile(b2, (1, ATT_GROUPS)),
              k_gain0.reshape(1, KV_WIDTH), jnp.asarray(_block_diag_ones(KV_WIDTH, HEAD_DIM), BF16))
    const = lambda a: pl.BlockSpec(a.shape, lambda b, u, pt, nd=a.ndim: (0,) * nd)
    pages = [pl.BlockSpec((1, 2, ATT_GROUPS, HEAD_DIM, PAGE_SIZE),
                          lambda b, u, pt, k=k: (pt[b, u * PAGES_PER_STEP + k], 0, 0, 0, 0))
             for k in range(PAGES_PER_STEP)]
    grid_spec = pltpu.PrefetchScalarGridSpec(
        num_scalar_prefetch=1, grid=(n_seq, PAGED_UNITS),
        in_specs=pages + [const(a) for a in consts],
        out_specs=pl.BlockSpec((1, n_chunk, KV_ROW), lambda b, u, pt: (b, 0, 0)),
        scratch_shapes=[pltpu.VMEM((2 * ATT_GROUPS, CMP_STRIDE // 2, n_chunk, LANES), BF16)])
    return pl.pallas_call(
        _compress_paged_kernel, grid_spec=grid_spec,
        out_shape=jax.ShapeDtypeStruct((n_seq, n_chunk, KV_ROW), BF16),
        compiler_params=_cparams("arbitrary", "arbitrary"),
        name="compress_paged",
    )(page_table, *([cache_t] * PAGES_PER_STEP), *consts)


def _block_overlap(n_cmp, n_blk):
    cs = np.arange(n_cmp)[:, None] * CMP_STRIDE
    bs = np.arange(n_blk)[None, :] * SEL_BLOCK
    ov = np.minimum(cs + CMP_BLOCK, bs + SEL_BLOCK) - np.maximum(cs, bs)
    return (np.clip(ov, 0, None) / CMP_BLOCK).astype(np.float32)


def _overlap_padded(n_cmp, n_blk, rows, cols):
    ov = np.zeros((rows, cols), np.float32)
    ov[:n_cmp, :n_blk] = _block_overlap(n_cmp, n_blk)
    return jnp.asarray(ov, BF16)


def _softmax_rows(s, mask):
    s = jnp.where(mask, s, NEG)
    m = jnp.max(s, axis=1, keepdims=True)
    e = jnp.where(mask, jnp.exp(s - m), 0.0)
    d = jnp.sum(e, axis=1, keepdims=True)
    return e / jnp.where(d > 0, d, 1.0)


def _select_blocks(imp, qpos, n_blk, axis):
    blk = lax.broadcasted_iota(jnp.int32, imp.shape, axis)
    cur = qpos >> SEL_SHIFT
    forced = (blk == 0) | (blk == cur) | (blk == cur - 1)
    valid = blk * SEL_BLOCK <= qpos
    score = jnp.where(forced, ATT_HPG + 1.0, jnp.where(valid, imp, -1.0))
    ahead = jnp.zeros(imp.shape, F32)
    for m in range(n_blk):
        sm = jnp.broadcast_to(score[m:m + 1, :] if axis == 0 else score[:, m:m + 1], imp.shape)
        tie = jnp.where(blk > m, 1.0, 0.0)
        ahead = ahead + jnp.where(sm > score, 1.0, jnp.where(sm == score, tie, 0.0))
    return jnp.where(blk < n_blk, jnp.where(ahead < N_SEL, 1.0, 0.0), 0.0)


SEL_SHIFT = SEL_BLOCK.bit_length() - 1
NT_DIMS = (((1,), (1,)), ((), ()))


SLC_UNROLL = 4
TOEP_TILES = min(SEQ // LANES, -(-(T5_THRESHOLDS[-1] + LANES - 1) // LANES) + 1)


NSA_GROUPS_PER_STEP = 4


def _nsa_prompt_multi_kernel(q_ref, gate_ref, kvc_ref, slc_ref, win_ref, bcmp_ref, toep_ref, ov_ref, spread_ref,
                             o_ref, s_scr, m_scr, l_scr, acc_scr):
    i = pl.program_id(2)
    ngs = NSA_GROUPS_PER_STEP
    tile = (Q_BLOCK, LANES)
    rows = ATT_HPG * Q_BLOCK
    lane = lax.broadcasted_iota(jnp.int32, tile, 1)
    sub = lax.broadcasted_iota(jnp.int32, tile, 0)
    low = lane < HEAD_DIM
    qpos = i * Q_BLOCK + sub
    head = lambda x, p: x[p * Q_BLOCK:(p + 1) * Q_BLOCK]
    heads = range(ATT_HPG)
    n_blk = SEQ // SEL_BLOCK
    blk_row = lax.broadcasted_iota(jnp.int32, (n_blk, LANES), 0)
    blk_of_key = lax.broadcasted_iota(jnp.int32, (n_blk, LANES), 1) >> SEL_SHIFT
    qpos_t = i * Q_BLOCK + lax.broadcasted_iota(jnp.int32, (n_blk, Q_BLOCK), 1)
    last_tile = SEQ // LANES - 1
    ov = ov_ref[...]
    glanes = lambda gg: slice(gg * LANES, (gg + 1) * LANES)

    def key_tile(kv_ref, gg, j):
        jc = jnp.clip(j, 0, last_tile)
        return kv_ref[pl.ds(pl.multiple_of(jc * LANES, LANES), LANES), glanes(gg)]

    def masked_scores(qs, gg, kv, j, mask):
        sc = lax.dot_general(qs, kv, NT_DIMS, preferred_element_type=F32)
        row0 = pl.multiple_of(jnp.clip(i - j, 0, TOEP_TILES - 1) * LANES, LANES)
        return [jnp.where(mask, head(sc, p) + toep_ref[gg * ATT_HPG + p, pl.ds(row0, LANES), :], NEG) for p in heads]

    def weights_times_v(scores, m, kv):
        es = [jnp.exp(scores[p] - m[p]) for p in heads]
        return es, jnp.dot(jnp.concatenate(es, axis=0).astype(BF16), kv, preferred_element_type=F32)

    def front(gg):
        qa = q_ref[:, gg * ATT_HPG * HEAD_DIM:(gg + 1) * ATT_HPG * HEAD_DIM] * (HEAD_DIM ** -0.5)
        qp = []
        for pair in range(ATT_HPG // 2):
            x = qa[:, pair * LANES:(pair + 1) * LANES]
            qp.append(jnp.where(low, x, 0.0))
            qp.append(jnp.where(low, pltpu.roll(x, HEAD_DIM, axis=1), 0.0))
        qs = jnp.concatenate(qp, axis=0).astype(BF16)

        kvc = kvc_ref[0, gg]
        s = lax.dot_general(qs, kvc, NT_DIMS, preferred_element_type=F32)
        s = s + bcmp_ref[gg * ATT_HPG:(gg + 1) * ATT_HPG].reshape(rows, LANES)
        cmask1 = (lane * CMP_STRIDE + (CMP_BLOCK - 1)) <= qpos
        pr = _softmax_rows(s, jnp.concatenate([cmask1] * ATT_HPG, axis=0))
        o_cmp = jnp.dot(pr.astype(BF16), kvc, preferred_element_type=F32)
        hi, lo = _split2(sum(head(pr, p) for p in heads))
        imp = jnp.dot(hi, ov, preferred_element_type=F32) + jnp.dot(lo, ov, preferred_element_type=F32)
        sel_t = _select_blocks(imp.T[:n_blk, :], qpos_t, n_blk, axis=0).astype(BF16)

        n_win = WINDOW // LANES + 1
        win_kv, win_s = [], []
        for w in range(n_win):
            j = i - (n_win - 1) + w
            dist = (i - j) * LANES + sub - lane
            inside = jnp.where(dist >= 0, jnp.where(dist < WINDOW, 1.0, 0.0), 0.0)
            win_kv.append(key_tile(win_ref, gg, j))
            win_s.append(masked_scores(qs, gg, win_kv[w], j, jnp.where(j >= 0, inside, 0.0) > 0.5))
        m_w = [jnp.max(functools.reduce(jnp.maximum, [win_s[w][p] for w in range(n_win)]), axis=1, keepdims=True)
               for p in heads]
        acc_w = jnp.zeros((rows, LANES), F32)
        l_w = [jnp.zeros(tile, F32) for _ in heads]
        for w in range(n_win):
            es, pv = weights_times_v(win_s[w], m_w, win_kv[w])
            acc_w = acc_w + pv
            l_w = [l_w[p] + es[p] for p in heads]
        o_win = [head(acc_w, p) / jnp.sum(l_w[p], axis=1, keepdims=True) for p in heads]
        return qs, o_cmp, sel_t, o_win

    fronts = [front(gg) for gg in range(ngs)]

    n_steps = (i + SLC_UNROLL) >> (SLC_UNROLL.bit_length() - 1)
    m_scr[...] = jnp.full(m_scr.shape, NEG, F32)

    def slc_scores(jj, carry):
        for gg in range(ngs):
            qs, _, sel_t, _ = fronts[gg]
            sps = []
            for r in range(SLC_UNROLL):
                j = SLC_UNROLL * jj + r
                dist = (i - j) * LANES + sub - lane
                expand = jnp.where(blk_row == 2 * j + blk_of_key, 1.0, 0.0).astype(BF16)
                picked = lax.dot_general(sel_t, expand, (((0,), (0,)), ((), ())), preferred_element_type=F32)
                sp = masked_scores(qs, gg, key_tile(slc_ref, gg, j), j, jnp.where(dist >= 0, picked, 0.0) > 0.5)
                for p in heads:
                    s_scr[j, gg * ATT_HPG + p] = sp[p]
                sps.append(sp)
            for p in heads:
                hp = gg * ATT_HPG + p
                m_scr[hp] = jnp.maximum(m_scr[hp], functools.reduce(jnp.maximum, [sp[p] for sp in sps]))
        return carry

    lax.fori_loop(0, n_steps, slc_scores, 0)
    for hp in range(ngs * ATT_HPG):
        m_scr[hp] = jnp.zeros(tile, F32) + jnp.max(m_scr[hp], axis=1, keepdims=True)
    l_scr[...] = jnp.zeros(l_scr.shape, F32)
    acc_scr[...] = jnp.zeros(acc_scr.shape, F32)

    def slc_weights(jj, carry):
        for gg in range(ngs):
            m = [m_scr[gg * ATT_HPG + p] for p in heads]
            pvs, ess = [], []
            for r in range(SLC_UNROLL):
                j = SLC_UNROLL * jj + r
                es, pv = weights_times_v([s_scr[j, gg * ATT_HPG + p] for p in heads], m, key_tile(slc_ref, gg, j))
                pvs.append(pv)
                ess.append(es)
            acc_scr[gg] += functools.reduce(jnp.add, pvs)
            for p in heads:
                hp = gg * ATT_HPG + p
                l_scr[hp] = l_scr[hp] + functools.reduce(jnp.add, [es[p] for es in ess])
        return carry

    lax.fori_loop(0, n_steps, slc_weights, 0)

    for gg in range(ngs):
        _, o_cmp, _, o_win = fronts[gg]
        g_hi, g_lo = _split2(gate_ref[:, glanes(gg)])
        spread = spread_ref[...]
        gb = jnp.dot(g_hi, spread, preferred_element_type=F32) + jnp.dot(g_lo, spread, preferred_element_type=F32)
        comb = []
        for p in heads:
            o_slc = head(acc_scr[gg], p) / jnp.sum(l_scr[gg * ATT_HPG + p], axis=1, keepdims=True)
            col = lambda br: gb[:, (br * ATT_HPG + p) * LANES:(br * ATT_HPG + p + 1) * LANES]
            comb.append(col(0) * head(o_cmp, p) + col(1) * o_slc + col(2) * o_win[p])
        for pair in range(ATT_HPG // 2):
            c0 = gg * ATT_HPG * HEAD_DIM + pair * LANES
            o_ref[:, c0:c0 + LANES] = jnp.where(low, pltpu.roll(comb[2 * pair], HEAD_DIM, axis=1), comb[2 * pair + 1])


def _nsa_prompt(q, gates, kvc, slc_pack, win_pack, bias_cmp, bias_toep, n_batch=BATCH):
    nqb = SEQ // Q_BLOCK
    ngs = NSA_GROUPS_PER_STEP
    ov = _overlap_padded(SEQ // CMP_STRIDE - 1, SEQ // SEL_BLOCK, LANES, LANES)
    gw = ngs * ATT_HPG * HEAD_DIM
    nh = ngs * ATT_HPG
    n_gate = N_BRANCH * ATT_HPG
    spread = np.zeros((LANES, n_gate, LANES), np.float32)
    spread[np.arange(n_gate), np.arange(n_gate), :] = 1.0
    spread = jnp.asarray(spread.reshape(LANES, n_gate * LANES), BF16)
    return pl.pallas_call(
        _nsa_prompt_multi_kernel,
        grid=(n_batch, ATT_GROUPS // ngs, nqb),
        in_specs=[
            pl.BlockSpec((Q_BLOCK, gw), lambda b, g, i: (b * nqb + i, g)),
            pl.BlockSpec((Q_BLOCK, ngs * LANES), lambda b, g, i: (b * nqb + i, g)),
            pl.BlockSpec((1, ngs, SEQ // CMP_STRIDE, LANES), lambda b, g, i: (0, g, b, 0)),
            pl.BlockSpec((SEQ, ngs * LANES), lambda b, g, i: (b, g)),
            pl.BlockSpec((SEQ, ngs * LANES), lambda b, g, i: (b, g)),
            pl.BlockSpec((nh, Q_BLOCK, LANES), lambda b, g, i: (g, i, 0)),
            pl.BlockSpec((nh, TOEP_TILES * LANES, LANES), lambda b, g, i: (g, 0, 0)),
            pl.BlockSpec((LANES, LANES), lambda b, g, i: (0, 0)),
            pl.BlockSpec((LANES, n_gate * LANES), lambda b, g, i: (0, 0)),
        ],
        out_specs=pl.BlockSpec((Q_BLOCK, gw), lambda b, g, i: (b * nqb + i, g)),
        out_shape=jax.ShapeDtypeStruct((n_batch * SEQ, ATT_WIDTH), F32),
        scratch_shapes=[pltpu.VMEM((nqb, nh, Q_BLOCK, LANES), F32),
                        pltpu.VMEM((nh, Q_BLOCK, LANES), F32), pltpu.VMEM((nh, Q_BLOCK, LANES), F32),
                        pltpu.VMEM((ngs, ATT_HPG * Q_BLOCK, LANES), F32)],
        compiler_params=_cparams("arbitrary", "arbitrary", "arbitrary"),
        name="nsa_prompt",
    )(q, gates, kvc, slc_pack, win_pack, bias_cmp, bias_toep, ov, spread)


SAMPLE_ROWS = ATT_HEADS * DEC_SEQ
SAMPLE_PAGES_PER_STEP = 32
SAMPLE_STEPS = N_PAGES // SAMPLE_PAGES_PER_STEP
SAMPLE_N_CMP = (PAST_LEN + DEC_SEQ - CMP_BLOCK) // CMP_STRIDE + 1
SAMPLE_N_BLK = -(-(PAST_LEN + DEC_SEQ) // SEL_BLOCK)
SAMPLE_BLK_LANES = 2 * LANES


def _nsa_sample_kernel(pt_ref, q_ref, gate_ref, kvc_ref, *rest):
    page_refs = rest[:SAMPLE_PAGES_PER_STEP]
    (slc_new_ref, win_state_ref, win_new_ref, bcmp_ref, bslc_ref, bwin_ref, perm_ref, ov_ref,
     o_ref, qbd_scr, picked_scr, m_scr, l_scr, acc_scr, ocmp_scr, owin_scr) = rest[SAMPLE_PAGES_PER_STEP:]
    del pt_ref
    u = pl.program_id(1)
    tile = (SAMPLE_ROWS, LANES)
    wide = (SAMPLE_ROWS, KV_WIDTH)
    lane = lax.broadcasted_iota(jnp.int32, tile, 1)
    t_row = lax.broadcasted_iota(jnp.int32, tile, 0) & (DEC_SEQ - 1)
    rows_per_group = ATT_HPG * DEC_SEQ
    own = ((lax.broadcasted_iota(jnp.int32, wide, 1) >> (HEAD_DIM.bit_length() - 1))
           == (lax.broadcasted_iota(jnp.int32, wide, 0) >> (rows_per_group.bit_length() - 1)))

    def reset():
        m_scr[...] = jnp.full(m_scr.shape, NEG, F32)
        l_scr[...] = jnp.zeros(l_scr.shape, F32)
        acc_scr[...] = jnp.zeros(acc_scr.shape, F32)

    def finish():
        l = jnp.sum(l_scr[...], axis=1, keepdims=True)
        return acc_scr[...] / jnp.where(l > 0, l, 1.0)

    def attend(tiles):
        qbd = qbd_scr[...]
        pairs = [tiles[n:n + 2] for n in range(0, len(tiles), 2)]
        side_by_side = lambda xs: (xs[0] if len(xs) == 1 else jnp.concatenate(xs, axis=1)).astype(BF16)
        scores = []
        m_el = None
        for grp in pairs:
            sc = jnp.dot(qbd, side_by_side([t[0] for t in grp]), preferred_element_type=F32)
            for n, (_, _, bias, mask) in enumerate(grp):
                s = jnp.where(mask, sc[:, n * LANES:(n + 1) * LANES] + bias, NEG)
                scores.append(s)
                m_el = s if m_el is None else jnp.maximum(m_el, s)
        m_prev = m_scr[...]
        m_new = jnp.maximum(m_prev, jnp.max(m_el, axis=1, keepdims=True))
        alpha = jnp.exp(m_prev - m_new)
        l_el = alpha * l_scr[...]
        acc = jnp.concatenate([alpha, alpha], axis=1) * acc_scr[...]
        for k, grp in enumerate(pairs):
            es = [jnp.exp(s - m_new) for s in scores[2 * k:2 * k + len(grp)]]
            l_el = l_el + functools.reduce(jnp.add, es)
            acc = acc + lax.dot_general(side_by_side(es), side_by_side([t[1] for t in grp]), NT_DIMS,
                                        preferred_element_type=F32)
        m_scr[...] = m_new
        l_scr[...] = l_el
        acc_scr[...] = acc

    def page_tile(kv4, bias, mask):
        return (kv4[0].reshape(KV_WIDTH, LANES), kv4[1].reshape(KV_WIDTH, LANES), bias, mask)

    def new_tile(ref, bias, mask):
        rows = jnp.concatenate([ref[...], jnp.zeros((LANES - DEC_SEQ, KV_ROW), F32)], axis=0)
        return (rows[:, :KV_WIDTH].T, rows[:, KV_WIDTH:].T, bias, mask)

    @pl.when(u == 0)
    def _():
        q = (q_ref[...] * (HEAD_DIM ** -0.5)).astype(BF16)
        qperm = jnp.dot(q, perm_ref[...], preferred_element_type=F32)
        qfull = jnp.concatenate([qperm[:, p * KV_WIDTH:(p + 1) * KV_WIDTH]
                                 for g in range(ATT_GROUPS) for p in range(ATT_HPG)], axis=0)
        qbd = jnp.where(own, qfull, 0.0).astype(BF16)
        qbd_scr[...] = qbd

        kvc = kvc_ref[0]
        s = lax.dot_general(qbd, kvc[:, :KV_WIDTH], NT_DIMS, preferred_element_type=F32) + bcmp_ref[...]
        cmask = lax.broadcasted_iota(jnp.int32, s.shape, 1) < SAMPLE_N_CMP
        pr = _softmax_rows(s, cmask)
        ocmp_scr[...] = jnp.dot(pr.astype(BF16), kvc[:, KV_WIDTH:], preferred_element_type=F32)
        ps = []
        for g in range(ATT_GROUPS):
            r0 = g * rows_per_group
            ps.append(sum(pr[r0 + p * DEC_SEQ:r0 + (p + 1) * DEC_SEQ, :] for p in range(ATT_HPG)))
        hi, lo = _split2(jnp.concatenate(ps, axis=0))
        ov = ov_ref[...]
        imp = jnp.dot(hi, ov, preferred_element_type=F32) + jnp.dot(lo, ov, preferred_element_type=F32)
        qpos = PAST_LEN + (lax.broadcasted_iota(jnp.int32, imp.shape, 0) & (DEC_SEQ - 1))
        sel = _select_blocks(imp, qpos, SAMPLE_N_BLK, axis=1)
        sel_rows = jnp.concatenate([sel[g * DEC_SEQ:(g + 1) * DEC_SEQ, :]
                                    for g in range(ATT_GROUPS) for p in range(ATT_HPG)], axis=0).astype(BF16)
        span = 4 * LANES
        blk_row = lax.broadcasted_iota(jnp.int32, (SAMPLE_BLK_LANES, span), 0)
        key_blk = lax.broadcasted_iota(jnp.int32, (SAMPLE_BLK_LANES, span), 1) >> SEL_SHIFT
        for c in range(PAST_LEN // span):
            expand = jnp.where(blk_row == key_blk + c * (span // SEL_BLOCK), 1.0, 0.0).astype(BF16)
            picked_scr[:, c * span:(c + 1) * span] = jnp.dot(sel_rows, expand, preferred_element_type=F32)

        reset()
        tiles = [page_tile(win_state_ref[0, :, :, :, w * LANES:(w + 1) * LANES],
                           bwin_ref[:, w * LANES:(w + 1) * LANES], (w * LANES + lane) > t_row)
                 for w in range(WINDOW // LANES)]
        tiles.append(new_tile(win_new_ref, bwin_ref[:, WINDOW:WINDOW + LANES], lane <= t_row))
        attend(tiles)
        owin_scr[...] = finish()
        reset()

    tiles = []
    for k in range(SAMPLE_PAGES_PER_STEP):
        keys = pl.ds(pl.multiple_of((u * SAMPLE_PAGES_PER_STEP + k) * LANES, LANES), LANES)
        tiles.append(page_tile(page_refs[k][0], bslc_ref[:, keys], picked_scr[:, keys] > 0.5))
    attend(tiles)

    @pl.when(u == SAMPLE_STEPS - 1)
    def _():
        attend([new_tile(slc_new_ref, bslc_ref[:, PAST_LEN:PAST_LEN + LANES], lane <= t_row)])
        o_slc = finish()
        gt = gate_ref[...]

        def gate_rows(br):
            cols = []
            for g in range(ATT_GROUPS):
                for p in range(ATT_HPG):
                    c = g * LANES + br * ATT_HPG + p
                    cols.append(jnp.broadcast_to(gt[:, c:c + 1], (DEC_SEQ, KV_WIDTH)))
            return jnp.concatenate(cols, axis=0)

        comb = gate_rows(0) * ocmp_scr[...] + gate_rows(1) * o_slc + gate_rows(2) * owin_scr[...]
        comb = jnp.where(own, comb, 0.0)
        per_head = []
        for p in range(ATT_HPG):
            per_head.append(sum(comb[(g * ATT_HPG + p) * DEC_SEQ:(g * ATT_HPG + p + 1) * DEC_SEQ, :]
                                for g in range(ATT_GROUPS)))
        hi, lo = _split2(jnp.concatenate(per_head, axis=1))
        perm = perm_ref[...]
        o_ref[...] = (lax.dot_general(hi, perm, NT_DIMS, preferred_element_type=F32)
                      + lax.dot_general(lo, perm, NT_DIMS, preferred_element_type=F32))


def _nsa_sample(q, gates, kvc, cache_slc, page_table, slc_rows, state_win, win_rows, bias_cmp, bias_slc, bias_win):
    n_seq = page_table.shape[0]
    row0 = N_PROMPT // DEC_SEQ
    new = lambda w: pl.BlockSpec((DEC_SEQ, w), lambda b, u, pt: (b, 0))
    src = np.arange(ATT_WIDTH)
    g, p, d = src // (ATT_HPG * HEAD_DIM), (src // HEAD_DIM) % ATT_HPG, src % HEAD_DIM
    perm = np.zeros((ATT_WIDTH, ATT_WIDTH), np.float32)
    perm[src, p * KV_WIDTH + g * HEAD_DIM + d] = 1.0
    ov = _overlap_padded(SAMPLE_N_CMP, SAMPLE_N_BLK, PAST_LEN // CMP_STRIDE, SAMPLE_BLK_LANES)
    tok = lambda w: pl.BlockSpec((DEC_SEQ, w), lambda b, u, pt: (row0 + b, 0))
    const = lambda a: pl.BlockSpec(a.shape, lambda b, u, pt, nd=a.ndim: (0,) * nd)
    per_seq = lambda s: pl.BlockSpec((1,) + s, lambda b, u, pt, nd=len(s): (b,) + (0,) * nd)
    pages = [pl.BlockSpec((1, 2, ATT_GROUPS, HEAD_DIM, PAGE_SIZE),
                          lambda b, u, pt, k=k: (pt[b, u * SAMPLE_PAGES_PER_STEP + k], 0, 0, 0, 0))
             for k in range(SAMPLE_PAGES_PER_STEP)]
    consts = (bias_cmp, bias_slc, bias_win, jnp.asarray(perm, BF16), ov)
    grid_spec = pltpu.PrefetchScalarGridSpec(
        num_scalar_prefetch=1, grid=(n_seq, SAMPLE_STEPS),
        in_specs=[tok(ATT_WIDTH), tok(ATT_GROUPS * LANES), per_seq((PAST_LEN // CMP_STRIDE, KV_ROW))] + pages
        + [new(KV_ROW), per_seq((2, ATT_GROUPS, HEAD_DIM, WINDOW)), new(KV_ROW)] + [const(a) for a in consts],
        out_specs=pl.BlockSpec((DEC_SEQ, ATT_WIDTH), lambda b, u, pt: (b, 0)),
        scratch_shapes=[pltpu.VMEM((SAMPLE_ROWS, KV_WIDTH), BF16), pltpu.VMEM((SAMPLE_ROWS, PAST_LEN), F32),
                        pltpu.VMEM((SAMPLE_ROWS, LANES), F32), pltpu.VMEM((SAMPLE_ROWS, LANES), F32),
                        pltpu.VMEM((SAMPLE_ROWS, KV_WIDTH), F32), pltpu.VMEM((SAMPLE_ROWS, KV_WIDTH), F32),
                        pltpu.VMEM((SAMPLE_ROWS, KV_WIDTH), F32)])
    return pl.pallas_call(
        _nsa_sample_kernel, grid_spec=grid_spec,
        out_shape=jax.ShapeDtypeStruct((n_seq * DEC_SEQ, ATT_WIDTH), F32),
        compiler_params=_cparams("arbitrary", "arbitrary"),
        name="nsa_sample",
    )(page_table, q, gates, kvc, *([cache_slc] * SAMPLE_PAGES_PER_STEP), slc_rows, state_win, win_rows, *consts)


def kernel(x_prompt, x_sample, cache_cmp_kv, cache_slc_kv, state_win_kv, state_hgrn, page_table,
           rel_bias_table, hgrn_lower_bound, norm_ffn1, w_ffn1_gate_up, w_ffn1_down, norm_mix,
           w_in, q_norm, k_norm, w_cmp1, b_cmp1, w_cmp2, b_cmp2, attn_out_norm, hgrn_out_norm,
           w_out, norm_ffn2, w_ffn2_gate_up, w_ffn2_down):
    assert DEPTH == 1
    l = 0
    kv_shape = (2, ATT_GROUPS, HEAD_DIM)

    y1 = _ffn((x_prompt.reshape(N_PROMPT, D_MODEL), x_sample.reshape(N_SAMPLE, D_MODEL)), norm_ffn1[l],
              w_ffn1_gate_up[l].astype(BF16), w_ffn1_down[l].astype(BF16))

    (q, cmp_rows, slc_rows, slc_pack, win_rows, win_pack, hq, lf, hk, hv, hog, gates,
     cmp_new, slc_new, win_new) = _project_all(
        y1, norm_mix[l], _permute_w_in(w_in[l]), q_norm[l], k_norm[l], hgrn_lower_bound)

    tbl = rel_bias_table.astype(F32)
    first_end = CMP_BLOCK - 1
    bias_cmp_p = _bias_table(tbl, SEQ, LANES, -first_end, 1, -CMP_STRIDE)
    bias_toep = _bias_table(tbl, TOEP_TILES * LANES, LANES, 0, 1, -1)
    bias_cmp_s = _bias_table(tbl, DEC_SEQ, PAST_LEN // CMP_STRIDE, PAST_LEN - first_end, 1, -CMP_STRIDE)
    bias_slc_s = _bias_table(tbl, DEC_SEQ, PAST_LEN + LANES, PAST_LEN, 1, -1)
    bias_win_s = _bias_table(tbl, DEC_SEQ, WINDOW + LANES, WINDOW, 1, -1)
    rows_ht = lambda a: a.reshape(SAMPLE_ROWS, a.shape[-1])

    cw = _compress_weights(w_cmp1[l], b_cmp1[l], w_cmp2[l], b_cmp2[l])
    kvc_p = _compress(cmp_rows, cw, k_norm[l][0])
    kvc_s = _compress_paged(_cache_view(cache_cmp_kv[l]), page_table, w_cmp1[l], b_cmp1[l], w_cmp2[l], b_cmp2[l],
                            k_norm[l][0])

    o_att_p = _nsa_prompt(q, gates, kvc_p, slc_pack, win_pack, bias_cmp_p, bias_toep)
    o_att_s = _nsa_sample(q, gates, kvc_s, _cache_view(cache_slc_kv[l]), page_table, slc_new,
                          _cache_view(state_win_kv[l]), win_new,
                          rows_ht(bias_cmp_s), rows_ht(bias_slc_s), rows_ht(bias_win_s))

    o_hg_p, hg_p = _hgrn(hq, hk, hv, lf, None, n_seq=BATCH, t_len=SEQ, ch=HG_CHUNK_PROMPT, row0=0)
    o_hg_s, hg_s = _hgrn(hq, hk, hv, lf, state_hgrn[l].astype(F32), n_seq=DEC_BATCH, t_len=DEC_SEQ,
                         ch=HG_CHUNK_SAMPLE, row0=N_PROMPT)

    y2 = _mixer_out(y1, o_att_p, o_att_s, o_hg_p, o_hg_s, hog, attn_out_norm[l], hgrn_out_norm[l],
                    w_out[l].astype(BF16))
    y3_p, y3_s = _ffn(y2, norm_ffn2[l], w_ffn2_gate_up[l].astype(BF16), w_ffn2_down[l].astype(BF16), split_out=True)

    prompt_rows = lambda a: a.reshape((1, BATCH, SEQ) + kv_shape)
    sample_rows = lambda a: a.reshape((1, DEC_BATCH, DEC_SEQ) + kv_shape)
    win_p = prompt_rows(win_rows)[:, :, SEQ - min(WINDOW, SEQ):]
    win_s = jnp.concatenate([state_win_kv[l][:, DEC_SEQ:], sample_rows(win_new)[0]], axis=1)[None]
    return (y3_p.reshape(BATCH, SEQ, D_MODEL), y3_s.reshape(DEC_BATCH, DEC_SEQ, D_MODEL),
            prompt_rows(cmp_rows), prompt_rows(slc_rows), win_p, hg_p[None],
            sample_rows(cmp_new), sample_rows(slc_new), win_s, hg_s[None].astype(state_hgrn.dtype))
```
